```python
import jax, jax.numpy as jnp
from jax import lax
import numpy as np

D_MODEL = 1024
BATCH = 2
SEQ = 16384
DEPTH = 4

N_MEM = 256
ROPE_THETA = 500000.0
EPS = 1e-6
BLOCK = 128
NEG = -1e30

MLA_HEADS = 8
MLA_NOPE = 64
MLA_ROPE = 32
MLA_V = 64
MLA_Q_LORA = 256
MLA_KV_LORA = 128

ML_HEADS = 4
ML_DK = 64
ML_DV = 64
ML_CONV = 4
ML_CHUNK = 128

SW_HEADS = 4
SW_HD = 64
SW_ROT = SW_HD // 4
SW_CONFIGS = ((128, 1), (512, 4), (2048, 16))

X_HEADS = 4
X_HD = D_MODEL // X_HEADS

D_FF = 256 * (-(-(8 * D_MODEL) // (3 * 256)))

MLA_W = MLA_HEADS * MLA_V
ML_W = ML_HEADS * ML_DV
SW_W = SW_HEADS * SW_HD
D_MIX = MLA_W + ML_W + SW_W

IN_SIZES = (MLA_Q_LORA, MLA_KV_LORA, MLA_ROPE,
            ML_HEADS * ML_DK, ML_HEADS * ML_DK, ML_W, ML_W, ML_HEADS, ML_HEADS,
            SW_W, SW_W, SW_W)
D_IN = sum(IN_SIZES)

kernel_name = "hybrid_mla_mlstm_dilated_swa_trunk"


def rms_norm(x, g):
    xf = x.astype(jnp.float32)
    y = xf * lax.rsqrt(jnp.mean(xf * xf, axis=-1, keepdims=True) + EPS)
    return (y * g.astype(jnp.float32)).astype(x.dtype)


def rope_tables(seq, d_rot):
    inv = ROPE_THETA ** (-jnp.arange(0, d_rot, 2, dtype=jnp.float32) / d_rot)
    ang = jnp.arange(seq, dtype=jnp.float32)[:, None] * inv[None, :]
    return jnp.cos(ang), jnp.sin(ang)


def apply_rope(x, cos, sin):
    xf = x.astype(jnp.float32)
    half = xf.shape[-1] // 2
    x1, x2 = xf[..., :half], xf[..., half:]
    c = cos[None, :, None, :]
    s = sin[None, :, None, :]
    return jnp.concatenate([x1 * c - x2 * s, x2 * c + x1 * s], axis=-1).astype(x.dtype)


def partial_rope(x, cos, sin):
    return jnp.concatenate([apply_rope(x[..., :SW_ROT], cos, sin), x[..., SW_ROT:]], axis=-1)


def causal_block_attention(q, k, v, scale):
    B, S, H, Dk = q.shape
    Dv = v.shape[-1]
    nb = S // BLOCK
    qb = q.reshape(B, nb, BLOCK, H, Dk).transpose(1, 0, 2, 3, 4)
    kpos = jnp.arange(S)

    def one(args):
        q_blk, blk = args
        s = jnp.einsum('bqhd,bkhd->bhqk', q_blk, k, preferred_element_type=jnp.float32) * scale
        qpos = blk * BLOCK + jnp.arange(BLOCK)
        s = jnp.where(kpos[None, :] <= qpos[:, None], s, NEG)
        p = jax.nn.softmax(s, axis=-1)
        return jnp.einsum('bhqk,bkhd->bqhd', p.astype(v.dtype), v)

    out = lax.map(one, (qb, jnp.arange(nb)))
    return out.transpose(1, 0, 2, 3, 4).reshape(B, S, H, Dv)


def mla_mixer(c_q, c_kv, k_rope, q_norm_g, w_uq, kv_norm_g, w_ukv, cos, sin):
    B, S, _ = c_q.shape
    q = (rms_norm(c_q, q_norm_g) @ w_uq).reshape(B, S, MLA_HEADS, MLA_NOPE + MLA_ROPE)
    q = jnp.concatenate([q[..., :MLA_NOPE], apply_rope(q[..., MLA_NOPE:], cos, sin)], axis=-1)
    kv = (rms_norm(c_kv, kv_norm_g) @ w_ukv).reshape(B, S, MLA_HEADS, MLA_NOPE + MLA_V)
    k_pe = apply_rope(k_rope[:, :, None, :], cos, sin)
    k = jnp.concatenate([kv[..., :MLA_NOPE], jnp.broadcast_to(k_pe, (B, S, MLA_HEADS, MLA_ROPE))], axis=-1)
    v = kv[..., MLA_NOPE:]
    out = causal_block_attention(q, k, v, (MLA_NOPE + MLA_ROPE) ** -0.5)
    return out.reshape(B, S, MLA_W)


def causal_depthwise_conv(x, w, b):
    K = w.shape[0]
    y = lax.conv_general_dilated(x, w[:, None, :].astype(x.dtype), window_strides=(1,),
                                 padding=[(K - 1, 0)], dimension_numbers=('NWC', 'WIO', 'NWC'),
                                 feature_group_count=x.shape[-1])
    return y + b


def mlstm_mixer(q_pre, k_pre, v_in, o_pre, i_pre, f_pre, conv_w, conv_b, b_i, b_f):
    B, S, _ = v_in.shape
    f32 = jnp.float32
    qk = jax.nn.silu(causal_depthwise_conv(jnp.concatenate([q_pre, k_pre], axis=-1), conv_w, conv_b))
    q = qk[..., :ML_HEADS * ML_DK].reshape(B, S, ML_HEADS, ML_DK).astype(f32)
    k = qk[..., ML_HEADS * ML_DK:].reshape(B, S, ML_HEADS, ML_DK).astype(f32) * ML_DK ** -0.5
    v = v_in.reshape(B, S, ML_HEADS, ML_DV).astype(f32)
    log_i = (i_pre + b_i).astype(f32)
    log_f = jax.nn.log_sigmoid((f_pre + b_f).astype(f32))
    L = ML_CHUNK
    nc = S // L

    def to_chunks(a):
        return a.reshape(B, nc, L, ML_HEADS, -1).transpose(1, 0, 3, 2, 4)

    def gate_chunks(a):
        return a.reshape(B, nc, L, ML_HEADS).transpose(1, 0, 3, 2)

    causal = jnp.tril(jnp.ones((L, L), dtype=bool))

    def step(carry, inp):
        C, n, m = carry
        qc, kc, vc, li, lf = inp
        b = jnp.cumsum(lf, axis=-1)
        D = jnp.where(causal, b[..., :, None] - b[..., None, :] + li[..., None, :], NEG)
        inter = b + m[..., None]
        m_t = jnp.maximum(inter, jnp.max(D, axis=-1))
        w_intra = jnp.exp(D - m_t[..., None])
        w_inter = jnp.exp(inter - m_t)
        s = jnp.einsum('bhtd,bhsd->bhts', qc, kc) * w_intra
        num = jnp.einsum('bhts,bhsv->bhtv', s, vc) + w_inter[..., None] * jnp.einsum('bhtk,bhkv->bhtv', qc, C)
        nq = jnp.sum(s, axis=-1) + w_inter * jnp.einsum('bhtk,bhk->bht', qc, n)
        h = num / jnp.maximum(jnp.abs(nq), jnp.exp(-m_t))[..., None]
        bL = b[..., -1]
        g = bL[..., None] - b + li
        m_new = jnp.maximum(bL + m, jnp.max(g, axis=-1))
        decay = jnp.exp(bL + m - m_new)
        ws = jnp.exp(g - m_new[..., None])
        C = decay[..., None, None] * C + jnp.einsum('bhs,bhsk,bhsv->bhkv', ws, kc, vc)
        n = decay[..., None] * n + jnp.einsum('bhs,bhsk->bhk', ws, kc)
        return (C, n, m_new), h

    init = (jnp.zeros((B, ML_HEADS, ML_DK, ML_DV), f32),
            jnp.zeros((B, ML_HEADS, ML_DK), f32),
            jnp.full((B, ML_HEADS), NEG, f32))
    _, h = lax.scan(step, init, (to_chunks(q), to_chunks(k), to_chunks(v), gate_chunks(log_i), gate_chunks(log_f)))
    h = h.transpose(1, 0, 3, 2, 4).reshape(B, S, ML_W)
    return (jax.nn.sigmoid(o_pre.astype(f32)) * h).astype(v_in.dtype)


def dilated_branch(q, k, v, dil, n_back, scale):
    B, S, H, D = q.shape
    Ls = S // dil

    def sub(a):
        return a.reshape(B, Ls, dil, H, D).transpose(0, 2, 3, 1, 4)

    nb = -(-Ls // BLOCK)
    Lp = nb * BLOCK
    qs = jnp.pad(sub(q), ((0, 0), (0, 0), (0, 0), (0, Lp - Ls), (0, 0))).reshape(B, dil, H, nb, BLOCK, D)
    kpad = ((0, 0), (0, 0), (0, 0), (BLOCK, Lp - Ls), (0, 0))
    kp = jnp.pad(sub(k), kpad).reshape(B, dil, H, nb + 1, BLOCK, D)
    vp = jnp.pad(sub(v), kpad).reshape(B, dil, H, nb + 1, BLOCK, D)
    kw = jnp.concatenate([kp[:, :, :, :-1], kp[:, :, :, 1:]], axis=4)
    vw = jnp.concatenate([vp[:, :, :, :-1], vp[:, :, :, 1:]], axis=4)
    s = jnp.einsum('brhnqd,brhnkd->brhnqk', qs, kw, preferred_element_type=jnp.float32) * scale
    qi = jnp.arange(BLOCK)[:, None]
    kj = jnp.arange(2 * BLOCK)[None, :] - BLOCK
    dist = qi - kj
    kidx = jnp.arange(nb)[:, None, None] * BLOCK + kj[None]
    valid = ((dist >= 0) & (dist <= n_back))[None] & (kidx >= 0)
    s = jnp.where(valid, s, NEG)
    mx = jnp.max(s, axis=-1, keepdims=True)
    e = jnp.exp(s - mx)
    den = jnp.sum(e, axis=-1)
    o = jnp.einsum('brhnqk,brhnkd->brhnqd', e, vw.astype(jnp.float32)) / den[..., None]
    lse = mx[..., 0] + jnp.log(den)
    o = o.reshape(B, dil, H, Lp, D)[:, :, :, :Ls].transpose(0, 3, 1, 2, 4).reshape(B, S, H, D)
    lse = lse.reshape(B, dil, H, Lp)[..., :Ls].transpose(0, 3, 1, 2).reshape(B, S, H)
    return o, lse


def dilated_swa_mixer(q_in, k_in, v_in, cos, sin):
    B, S, _ = q_in.shape
    q = partial_rope(q_in.reshape(B, S, SW_HEADS, SW_HD), cos, sin)
    k = partial_rope(k_in.reshape(B, S, SW_HEADS, SW_HD), cos, sin)
    v = v_in.reshape(B, S, SW_HEADS, SW_HD)
    outs, lses = [], []
    for window, dil in SW_CONFIGS:
        o, lse = dilated_branch(q, k, v, dil, window // dil, SW_HD ** -0.5)
        outs.append(o)
        lses.append(lse)
    wts = jax.nn.softmax(jnp.stack(lses, axis=0), axis=0)
    out = jnp.einsum('gbsh,gbshd->bshd', wts, jnp.stack(outs, axis=0))
    return out.reshape(B, S, SW_W).astype(q_in.dtype)


def memory_cross_attention(h, mem_n, w_q, w_kv, w_o):
    B, S, _ = h.shape
    q = (h @ w_q).reshape(B, S, X_HEADS, X_HD)
    kv = (mem_n @ w_kv).reshape(B, mem_n.shape[1], 2, X_HEADS, X_HD)
    k, v = kv[:, :, 0], kv[:, :, 1]
    s = jnp.einsum('bshd,bmhd->bhsm', q, k, preferred_element_type=jnp.float32) * X_HD ** -0.5
    p = jax.nn.softmax(s, axis=-1)
    o = jnp.einsum('bhsm,bmhd->bshd', p.astype(v.dtype), v).reshape(B, S, D_MODEL)
    return o @ w_o


def swiglu(h, w_gate_up, w_down):
    g, u = jnp.split(h @ w_gate_up, 2, axis=-1)
    return (jax.nn.silu(g) * u) @ w_down


def setup_inputs(seed: int = 0) -> dict:
    key = jax.random.key(seed)
    ks = list(jax.random.split(key, 32))
    f32 = jnp.float32

    def w(shape, fan_in):
        return jax.random.normal(ks.pop(), shape, f32) * fan_in ** -0.5

    def gain(shape):
        return 1.0 + 0.02 * jax.random.normal(ks.pop(), shape, f32)

    def bias(shape, scale=0.01):
        return scale * jax.random.normal(ks.pop(), shape, f32)

    return {
        "x": jax.random.normal(ks.pop(), (BATCH, SEQ, D_MODEL), f32),
        "mem": jax.random.normal(ks.pop(), (BATCH, N_MEM, D_MODEL), f32),
        "w_in": w((DEPTH, D_MODEL, D_IN), D_MODEL),
        "mla_q_norm": gain((DEPTH, MLA_Q_LORA)),
        "mla_w_uq": w((DEPTH, MLA_Q_LORA, MLA_HEADS * (MLA_NOPE + MLA_ROPE)), MLA_Q_LORA),
        "mla_kv_norm": gain((DEPTH, MLA_KV_LORA)),
        "mla_w_ukv": w((DEPTH, MLA_KV_LORA, MLA_HEADS * (MLA_NOPE + MLA_V)), MLA_KV_LORA),
        "ml_conv_w": w((DEPTH, ML_CONV, 2 * ML_HEADS * ML_DK), ML_CONV),
        "ml_conv_b": bias((DEPTH, 2 * ML_HEADS * ML_DK)),
        "ml_b_i": bias((DEPTH, ML_HEADS), 0.1),
        "ml_b_f": jnp.linspace(3.0, 6.0, ML_HEADS, dtype=f32)[None, :] + bias((DEPTH, ML_HEADS)),
        "w_out": w((DEPTH, D_MIX, D_MODEL), D_MIX),
        "x_w_q": w((DEPTH, D_MODEL, D_MODEL), D_MODEL),
        "x_w_kv": w((DEPTH, D_MODEL, 2 * D_MODEL), D_MODEL),
        "x_w_o": w((DEPTH, D_MODEL, D_MODEL), D_MODEL),
        "w_gate_up": w((DEPTH, D_MODEL, 2 * D_FF), D_MODEL),
        "w_down": w((DEPTH, D_FF, D_MODEL), D_FF),
        "norm_mix_pre": gain((DEPTH, D_MODEL)),
        "norm_mix_post": gain((DEPTH, D_MODEL)),
        "norm_mem": gain((DEPTH, D_MODEL)),
        "norm_x_pre": gain((DEPTH, D_MODEL)),
        "norm_x_post": gain((DEPTH, D_MODEL)),
        "norm_ffn_pre": gain((DEPTH, D_MODEL)),
        "norm_ffn_post": gain((DEPTH, D_MODEL)),
    }


def reference(x, mem, w_in, mla_q_norm, mla_w_uq, mla_kv_norm, mla_w_ukv, ml_conv_w, ml_conv_b,
              ml_b_i, ml_b_f, w_out, x_w_q, x_w_kv, x_w_o, w_gate_up, w_down,
              norm_mix_pre, norm_mix_post, norm_mem, norm_x_pre, norm_x_post, norm_ffn_pre, norm_ffn_post):
    S = x.shape[1]
    cos_mla, sin_mla = rope_tables(S, MLA_ROPE)
    cos_sw, sin_sw = rope_tables(S, SW_ROT)
    splits = np.cumsum(IN_SIZES)[:-1].tolist()
    for l in range(DEPTH):
        h = rms_norm(x, norm_mix_pre[l])
        (c_q, c_kv, k_rope, ml_q, ml_k, ml_v, ml_o, ml_i, ml_f,
         sw_q, sw_k, sw_v) = jnp.split(h @ w_in[l], splits, axis=-1)
        y_a = mla_mixer(c_q, c_kv, k_rope, mla_q_norm[l], mla_w_uq[l], mla_kv_norm[l], mla_w_ukv[l],
                        cos_mla, sin_mla)
        y_b = mlstm_mixer(ml_q, ml_k, ml_v, ml_o, ml_i, ml_f, ml_conv_w[l], ml_conv_b[l], ml_b_i[l], ml_b_f[l])
        y_c = dilated_swa_mixer(sw_q, sw_k, sw_v, cos_sw, sin_sw)
        y = jnp.concatenate([y_a, y_b, y_c], axis=-1) @ w_out[l]
        x = x + rms_norm(y, norm_mix_post[l])
        h = rms_norm(x, norm_x_pre[l])
        y = memory_cross_attention(h, rms_norm(mem, norm_mem[l]), x_w_q[l], x_w_kv[l], x_w_o[l])
        x = x + rms_norm(y, norm_x_post[l])
        h = rms_norm(x, norm_ffn_pre[l])
        y = swiglu(h, w_gate_up[l], w_down[l])
        x = x + rms_norm(y, norm_ffn_post[l])
    return x
```

```python
import functools

import jax
import jax.numpy as jnp
import numpy as np
from jax import lax
from jax.experimental import pallas as pl
from jax.experimental.pallas import tpu as pltpu

F32 = jnp.float32
BF16 = jnp.bfloat16

D_MODEL = 1024
N_MEM = 256
ROPE_THETA = 500000.0
EPS = 1e-6
NEG = -1e30
LANES = 128

MLA_HEADS, MLA_NOPE, MLA_ROPE, MLA_V = 8, 64, 32, 64
MLA_Q_LORA, MLA_KV_LORA = 256, 128
ML_HEADS, ML_DK, ML_DV, ML_CONV, ML_CHUNK = 4, 64, 64, 4, 128
SW_HEADS, SW_HD = 4, 64
SW_ROT = SW_HD // 4
SW_DILATIONS = (1, 4, 16)
SW_BLOCK = 128
X_HEADS = 4
X_HD = D_MODEL // X_HEADS
D_FF = 2816

MLA_W = MLA_HEADS * MLA_V
ML_W = ML_HEADS * ML_DV
SW_W = SW_HEADS * SW_HD
MLA_PAD_W = MLA_HEADS * LANES
ML_PAD_W = ML_HEADS * LANES

IN_SIZES = (MLA_Q_LORA, MLA_KV_LORA, MLA_ROPE, ML_HEADS * ML_DK, ML_HEADS * ML_DK, ML_W, ML_W,
            ML_HEADS, ML_HEADS, SW_W, SW_W, SW_W)
IN_OFFS = tuple(int(v) for v in np.cumsum((0,) + IN_SIZES))

P_CQ = 0
P_CKV = 256
P_KROPE = 384
P_MLQK = 512
P_MLV = 1024
P_MLO = 1536
P_SWQ = 1792
P_SWK = 2048
P_SWV = 2304
P_TOTAL = 2560

TOK_TILE = 512
FLASH_TILE = 512
SWA_QB = 4
VMEM_LIMIT = 56 * 1024 * 1024


def _rms(x, g):
    return x * lax.rsqrt(jnp.mean(x * x, axis=-1, keepdims=True) + EPS) * g


def _dot(a, b):
    return jnp.dot(a, b, preferred_element_type=F32)


def _dot_nt(a, b):
    return lax.dot_general(a, b, (((1,), (1,)), ((), ())), preferred_element_type=F32)


def _dot_tn(a, b):
    return lax.dot_general(a, b, (((0,), (0,)), ((), ())), preferred_element_type=F32)


def _params(*sem):
    return pltpu.CompilerParams(dimension_semantics=sem, vmem_limit_bytes=VMEM_LIMIT)


def _inproj_kernel(x_ref, g_ref, w_ref, wgt_ref, qg_ref, wuq_ref, kvg_ref, wk_ref, wv_ref,
                   cm_ref, sm_ref, cs_ref, ss_ref, one_ref,
                   q_ref, k_ref, v_ref, mqk_ref, mv_ref, mo_ref, gr_ref, sq_ref, sk_ref, sv_ref):
    tm = x_ref.shape[0]
    h = _rms(x_ref[...], g_ref[...]).astype(BF16)

    def proj(a, b):
        return _dot(h, w_ref[:, a:b])

    lane = lax.broadcasted_iota(jnp.int32, (tm, LANES), 1)
    cm, sm = cm_ref[...], sm_ref[...]
    mla_first = lane < MLA_NOPE + MLA_ROPE // 2

    def rope_mla(t):
        rot = jnp.where(mla_first, pltpu.roll(t, LANES - MLA_ROPE // 2, 1), pltpu.roll(t, MLA_ROPE // 2, 1))
        return t * cm + rot * sm

    cs, ss = cs_ref[...], ss_ref[...]
    sw_first = (lane % SW_HD) < SW_ROT // 2

    def rope_sw(t):
        rot = jnp.where(sw_first, pltpu.roll(t, LANES - SW_ROT // 2, 1), pltpu.roll(t, SW_ROT // 2, 1))
        return t * cs + rot * ss

    cqn = _rms(proj(P_CQ, P_CQ + MLA_Q_LORA), qg_ref[...]).astype(BF16)
    q_scale = (MLA_NOPE + MLA_ROPE) ** -0.5
    for hh in range(MLA_HEADS):
        sl = slice(hh * LANES, (hh + 1) * LANES)
        q_ref[:, sl] = (rope_mla(_dot(cqn, wuq_ref[:, sl])) * q_scale).astype(BF16)
    kvn = _rms(proj(P_CKV, P_CKV + MLA_KV_LORA), kvg_ref[...]).astype(BF16)
    kpe = rope_mla(proj(P_KROPE, P_KROPE + LANES))
    one = one_ref[...]
    for hh in range(MLA_HEADS):
        sl = slice(hh * LANES, (hh + 1) * LANES)
        k_ref[:, sl] = (_dot(kvn, wk_ref[:, sl]) + kpe).astype(BF16)
        v_ref[:, sl] = (_dot(kvn, wv_ref[:, sl]) + one).astype(BF16)

    mqk_ref[...] = proj(P_MLQK, P_MLQK + 2 * ML_HEADS * ML_DK)
    for hh in range(ML_HEADS):
        sl = slice(hh * LANES, (hh + 1) * LANES)
        mv_ref[:, sl] = (proj(P_MLV + hh * LANES, P_MLV + (hh + 1) * LANES) + one).astype(BF16)
    mo_ref[...] = proj(P_MLO, P_MLO + ML_W)
    gr_ref[...] = _dot_nt(wgt_ref[...], h)

    for half in range(SW_W // LANES):
        sl = slice(half * LANES, (half + 1) * LANES)
        sq_ref[:, sl] = (rope_sw(proj(P_SWQ + half * LANES, P_SWQ + (half + 1) * LANES)) * SW_HD ** -0.5).astype(BF16)
        sk_ref[:, sl] = rope_sw(proj(P_SWK + half * LANES, P_SWK + (half + 1) * LANES)).astype(BF16)
    sv_ref[...] = proj(P_SWV, P_SWV + SW_W).astype(BF16)


def _inproj(x, g, w, wgt, qg, wuq, kvg, wk, wv, cm, sm, cs, ss, one, seq):
    t_all = x.shape[0]
    tm = TOK_TILE
    pos_tiles = seq // tm
    tok = lambda w_: pl.BlockSpec((tm, w_), lambda i: (i, 0))
    full = lambda a: pl.BlockSpec(a.shape, lambda i: (0,) * a.ndim)
    pos = pl.BlockSpec((tm, LANES), lambda i: (i % pos_tiles, 0))
    out_shapes = (
        jax.ShapeDtypeStruct((t_all, MLA_PAD_W), BF16),
        jax.ShapeDtypeStruct((t_all, MLA_PAD_W), BF16),
        jax.ShapeDtypeStruct((t_all, MLA_PAD_W), BF16),
        jax.ShapeDtypeStruct((t_all, 2 * ML_HEADS * ML_DK), F32),
        jax.ShapeDtypeStruct((t_all, ML_PAD_W), BF16),
        jax.ShapeDtypeStruct((t_all, ML_W), F32),
        jax.ShapeDtypeStruct((2 * ML_HEADS, t_all), F32),
        jax.ShapeDtypeStruct((t_all, SW_W), BF16),
        jax.ShapeDtypeStruct((t_all, SW_W), BF16),
        jax.ShapeDtypeStruct((t_all, SW_W), BF16),
    )
    out_specs = (tok(MLA_PAD_W), tok(MLA_PAD_W), tok(MLA_PAD_W), tok(2 * ML_HEADS * ML_DK), tok(ML_PAD_W),
                 tok(ML_W), pl.BlockSpec((2 * ML_HEADS, tm), lambda i: (0, i)), tok(SW_W), tok(SW_W), tok(SW_W))
    return pl.pallas_call(
        _inproj_kernel,
        grid=(t_all // tm,),
        in_specs=[tok(D_MODEL), full(g), full(w), full(wgt), full(qg), full(wuq), full(kvg), full(wk), full(wv),
                  pos, pos, pos, pos, full(one)],
        out_specs=out_specs,
        out_shape=out_shapes,
        compiler_params=_params("parallel"),
        name="inproj",
    )(x, g, w, wgt, qg, wuq, kvg, wk, wv, cm, sm, cs, ss, one)


def _flash_kernel(q_ref, k_ref, v_ref, o_ref, m_scr, acc_scr):
    t = q_ref.shape[0]
    i = pl.program_id(2)
    q = q_ref[...]
    m_scr[...] = jnp.full(m_scr.shape, NEG, F32)
    acc_scr[...] = jnp.zeros(acc_scr.shape, F32)

    def tile(j, masked):
        start = pl.multiple_of(j * t, t)
        s = _dot_nt(q, k_ref[pl.ds(start, t), :])
        if masked:
            row = lax.broadcasted_iota(jnp.int32, (t, t), 0)
            col = lax.broadcasted_iota(jnp.int32, (t, t), 1)
            s = jnp.where(col <= row, s, NEG)
        m_old = m_scr[...]
        m_new = jnp.maximum(m_old, jnp.max(s, axis=1, keepdims=True))
        p = jnp.exp(s - m_new).astype(BF16)
        acc_scr[...] = acc_scr[...] * jnp.exp(m_old - m_new) + _dot(p, v_ref[pl.ds(start, t), :])
        m_scr[...] = m_new

    def body(j, carry):
        tile(j, False)
        return carry

    lax.fori_loop(0, i, body, 0)
    tile(i, True)
    acc = acc_scr[...]
    o_ref[...] = (acc / acc[:, MLA_V:MLA_V + 1]).astype(BF16)


def _flash(q, k, v):
    b, s, _ = q.shape
    t = FLASH_TILE
    return pl.pallas_call(
        _flash_kernel,
        grid=(b, MLA_HEADS, s // t),
        in_specs=[pl.BlockSpec((None, t, LANES), lambda bb, hh, i: (bb, i, hh)),
                  pl.BlockSpec((None, s, LANES), lambda bb, hh, i: (bb, 0, hh)),
                  pl.BlockSpec((None, s, LANES), lambda bb, hh, i: (bb, 0, hh))],
        out_specs=pl.BlockSpec((None, t, LANES), lambda bb, hh, i: (bb, i, hh)),
        out_shape=jax.ShapeDtypeStruct((b, s, MLA_PAD_W), BF16),
        scratch_shapes=[pltpu.VMEM((t, 1), F32), pltpu.VMEM((t, LANES), F32)],
        compiler_params=_params("parallel", "parallel", "arbitrary"),
        name="mla_flash",
    )(q, k, v)


def _log_sigmoid(z):
    return jnp.minimum(z, 0.0) - jnp.log1p(jnp.exp(-jnp.abs(z)))


def _mlstm_kernel(qkc_ref, qkp_ref, v_ref, o_ref, g_ref, cw_ref, cb_ref, bi_ref, bf_ref,
                  y_ref, c_scr, m_scr):
    L = ML_CHUNK
    c = pl.program_id(1)

    @pl.when(c == 0)
    def _():
        c_scr[...] = jnp.zeros(c_scr.shape, F32)
        m_scr[...] = jnp.full(m_scr.shape, NEG, F32)

    cur = qkc_ref[...]
    prev = jnp.where(c > 0, qkp_ref[...], 0.0)
    row_w = lax.broadcasted_iota(jnp.int32, cur.shape, 0)
    conv = cb_ref[...] + cw_ref[ML_CONV - 1:ML_CONV, :] * cur
    for sh in range(1, ML_CONV):
        shifted = jnp.where(row_w >= sh, pltpu.roll(cur, sh, 0), pltpu.roll(prev, sh, 0))
        conv = conv + cw_ref[ML_CONV - 1 - sh:ML_CONV - sh, :] * shifted
    qk = conv * jax.nn.sigmoid(conv)

    gates = g_ref[...]
    li_all = gates[0:ML_HEADS, :] + bi_ref[...]
    lf_all = _log_sigmoid(gates[ML_HEADS:, :] + bf_ref[...])
    lane_g = lax.broadcasted_iota(jnp.int32, lf_all.shape, 1)
    b_all = lf_all
    step = 1
    while step < L:
        b_all = b_all + jnp.where(lane_g >= step, pltpu.roll(b_all, step, 1), 0.0)
        step *= 2

    row = lax.broadcasted_iota(jnp.int32, (L, L), 0)
    col = lax.broadcasted_iota(jnp.int32, (L, L), 1)
    tri = col <= row
    eye = col == row

    def to_col(r):
        return jnp.sum(jnp.where(eye, r, 0.0), axis=1, keepdims=True)

    for hh in range(ML_HEADS):
        qh = qk[:, hh * ML_DK:(hh + 1) * ML_DK].astype(BF16)
        kf = qk[:, (ML_HEADS + hh) * ML_DK:(ML_HEADS + hh + 1) * ML_DK] * ML_DK ** -0.5
        vh = v_ref[:, hh * LANES:(hh + 1) * LANES]
        b_row = b_all[hh:hh + 1, :]
        li_row = li_all[hh:hh + 1, :]
        m_prev = m_scr[hh:hh + 1, 0:1]
        c_ext = c_scr[hh]

        b_col = to_col(b_row)
        d = jnp.where(tri, b_col - b_row + li_row, NEG)
        inter = b_col + m_prev
        m_t = jnp.maximum(inter, jnp.max(d, axis=1, keepdims=True))
        w_intra = jnp.exp(d - m_t)
        w_inter = jnp.exp(inter - m_t)
        s = _dot_nt(qh, kf.astype(BF16)) * w_intra
        intra = _dot(s.astype(BF16), vh)
        cross = _dot(qh, c_ext.astype(BF16))
        num = intra[:, :ML_DV] + w_inter * cross[:, :ML_DV]
        nq = jnp.sum(s, axis=1, keepdims=True) + w_inter * cross[:, ML_DV:ML_DV + 1]
        hout = num / jnp.maximum(jnp.abs(nq), jnp.exp(-m_t))
        og = jax.nn.sigmoid(o_ref[:, hh * ML_DV:(hh + 1) * ML_DV])
        y_ref[:, hh * ML_DV:(hh + 1) * ML_DV] = (og * hout).astype(BF16)

        b_last = b_row[:, L - 1:L]
        g = b_last - b_row + li_row
        m_new = jnp.maximum(b_last + m_prev, jnp.max(g, axis=1, keepdims=True))
        decay = jnp.exp(b_last + m_prev - m_new)
        ws_col = to_col(jnp.exp(g - m_new))
        c_scr[hh] = decay * c_ext + _dot_tn((kf * ws_col).astype(BF16), vh)
        m_scr[hh:hh + 1, :] = jnp.broadcast_to(m_new, (1, LANES))


def _mlstm(qk_pre, v, o_pre, gates_t, conv_w, conv_b, b_i, b_f, batch, seq):
    L = ML_CHUNK
    nc = seq // L
    wqk = 2 * ML_HEADS * ML_DK
    full = lambda a: pl.BlockSpec(a.shape, lambda bb, c: (0,) * a.ndim)
    return pl.pallas_call(
        _mlstm_kernel,
        grid=(batch, nc),
        in_specs=[pl.BlockSpec((L, wqk), lambda bb, c: (bb * nc + c, 0)),
                  pl.BlockSpec((L, wqk), lambda bb, c: (bb * nc + jnp.maximum(c - 1, 0), 0)),
                  pl.BlockSpec((L, ML_PAD_W), lambda bb, c: (bb * nc + c, 0)),
                  pl.BlockSpec((L, ML_W), lambda bb, c: (bb * nc + c, 0)),
                  pl.BlockSpec((2 * ML_HEADS, L), lambda bb, c: (0, bb * nc + c)),
                  full(conv_w), full(conv_b), full(b_i), full(b_f)],
        out_specs=pl.BlockSpec((L, ML_W), lambda bb, c: (bb * nc + c, 0)),
        out_shape=jax.ShapeDtypeStruct((batch * seq, ML_W), BF16),
        scratch_shapes=[pltpu.VMEM((ML_HEADS, ML_DK, LANES), F32), pltpu.VMEM((8, LANES), F32)],
        compiler_params=_params("parallel", "arbitrary"),
        name="mlstm",
    )(qk_pre, qk_pre, v, o_pre, gates_t, conv_w, conv_b, b_i, b_f)


def _swa_kernel(q_ref, kc_ref, kp_ref, vc_ref, vp_ref, o_ref, l_ref):
    blk = SW_BLOCK
    n = pl.program_id(2)
    row = lax.broadcasted_iota(jnp.int32, (blk, blk), 0)
    col = lax.broadcasted_iota(jnp.int32, (blk, blk), 1)
    cur_ok = col <= row
    for qb in range(q_ref.shape[0] // blk):
        rows = slice(qb * blk, (qb + 1) * blk)
        if qb == 0:
            k_prev, v_prev = kp_ref[...], vp_ref[...]
            prev_ok = (col >= row) & (n > 0)
        else:
            before = slice((qb - 1) * blk, qb * blk)
            k_prev, v_prev = kc_ref[before, :], vc_ref[before, :]
            prev_ok = col >= row
        q, k_cur, v_cur = q_ref[rows, :], kc_ref[rows, :], vc_ref[rows, :]
        for hh in range(SW_HEADS):
            hs = slice(hh * SW_HD, (hh + 1) * SW_HD)
            s_p = jnp.where(prev_ok, _dot_nt(q[:, hs], k_prev[:, hs]), NEG)
            s_c = jnp.where(cur_ok, _dot_nt(q[:, hs], k_cur[:, hs]), NEG)
            mx = jnp.maximum(jnp.max(s_p, axis=1, keepdims=True), jnp.max(s_c, axis=1, keepdims=True))
            e_p = jnp.exp(s_p - mx)
            e_c = jnp.exp(s_c - mx)
            den = jnp.sum(e_p, axis=1, keepdims=True) + jnp.sum(e_c, axis=1, keepdims=True)
            o = (_dot(e_p.astype(BF16), v_prev[:, hs]) + _dot(e_c.astype(BF16), v_cur[:, hs])) / den
            o_ref[rows, hs] = o
            l_ref[rows, hs] = jnp.broadcast_to(mx + jnp.log(den), (blk, SW_HD))


def _swa_branch(q, k, v, dil):
    b, ls, _ = q.shape
    rows = min(SWA_QB * SW_BLOCK, ls)
    qb = rows // SW_BLOCK
    cur = pl.BlockSpec((None, rows, SW_W), lambda bb, r, n: (bb, n, r))
    prev = pl.BlockSpec((None, SW_BLOCK, SW_W), lambda bb, r, n: (bb, jnp.maximum(n * qb - 1, 0), r))
    return pl.pallas_call(
        _swa_kernel,
        grid=(b, dil, ls // rows),
        in_specs=[cur, cur, prev, cur, prev],
        out_specs=(cur, cur),
        out_shape=(jax.ShapeDtypeStruct(q.shape, F32), jax.ShapeDtypeStruct(q.shape, F32)),
        compiler_params=_params("parallel", "parallel", "arbitrary"),
        name=f"swa_d{dil}",
    )(q, k, k, v, v)


def _outproj_kernel(x_ref, ya_ref, yb_ref, o1_ref, o4_ref, o16_ref, l1_ref, l4_ref, l16_ref,
                    wa_ref, wb_ref, wc_ref, g_ref, out_ref):
    l1, l4, l16 = l1_ref[...], l4_ref[...], l16_ref[...]
    mx = jnp.maximum(jnp.maximum(l1, l4), l16)
    w1, w4, w16 = jnp.exp(l1 - mx), jnp.exp(l4 - mx), jnp.exp(l16 - mx)
    yc = (w1 * o1_ref[...] + w4 * o4_ref[...] + w16 * o16_ref[...]) / (w1 + w4 + w16)
    y = _dot(ya_ref[...], wa_ref[...]) + _dot(yb_ref[...], wb_ref[...]) + _dot(yc.astype(BF16), wc_ref[...])
    out_ref[...] = x_ref[...] + _rms(y, g_ref[...])


def _outproj(x, ya, yb, o1, o4, o16, l1, l4, l16, wa, wb, wc, g):
    t_all = x.shape[0]
    tm = TOK_TILE
    tok = lambda w_: pl.BlockSpec((tm, w_), lambda i: (i, 0))
    full = lambda a: pl.BlockSpec(a.shape, lambda i: (0,) * a.ndim)
    return pl.pallas_call(
        _outproj_kernel,
        grid=(t_all // tm,),
        in_specs=[tok(D_MODEL), tok(MLA_PAD_W), tok(ML_W)] + [tok(SW_W)] * 6 + [full(wa), full(wb), full(wc), full(g)],
        out_specs=tok(D_MODEL),
        out_shape=jax.ShapeDtypeStruct((t_all, D_MODEL), F32),
        compiler_params=_params("parallel"),
        name="outproj",
    )(x, ya, yb, o1, o4, o16, l1, l4, l16, wa, wb, wc, g)


def _memkv_kernel(mem_ref, g_ref, w_ref, kv_ref):
    kv_ref[...] = _dot(_rms(mem_ref[...], g_ref[...]).astype(BF16), w_ref[...]).astype(BF16)


def _memkv(mem, g, w):
    n = mem.shape[0]
    full = lambda a: pl.BlockSpec(a.shape, lambda i: (0,) * a.ndim)
    return pl.pallas_call(
        _memkv_kernel,
        grid=(1,),
        in_specs=[full(mem), full(g), full(w)],
        out_specs=pl.BlockSpec((n, 2 * D_MODEL), lambda i: (0, 0)),
        out_shape=jax.ShapeDtypeStruct((n, 2 * D_MODEL), BF16),
        compiler_params=_params("arbitrary"),
        name="memkv",
    )(mem, g, w)


def _xattn_kernel(x_ref, gpre_ref, wq_ref, kv_ref, wo_ref, gpost_ref, out_ref, o_scr):
    x = x_ref[...]
    h = _rms(x, gpre_ref[...]).astype(BF16)
    q = (_dot(h, wq_ref[...]) * X_HD ** -0.5).astype(BF16)
    for hh in range(X_HEADS):
        hs = slice(hh * X_HD, (hh + 1) * X_HD)
        s = _dot_nt(q[:, hs], kv_ref[:, hs])
        e = jnp.exp(s - jnp.max(s, axis=1, keepdims=True))
        den = jnp.sum(e, axis=1, keepdims=True)
        o_scr[:, hs] = (_dot(e.astype(BF16), kv_ref[:, D_MODEL + hh * X_HD:D_MODEL + (hh + 1) * X_HD]) / den).astype(BF16)
    y = _dot(o_scr[...], wo_ref[...])
    out_ref[...] = x + _rms(y, gpost_ref[...])


def _xattn(x, gpre, wq, kv, wo, gpost, seq):
    t_all = x.shape[0]
    tm = TOK_TILE
    per_batch = seq // tm
    tok = pl.BlockSpec((tm, D_MODEL), lambda i: (i, 0))
    full = lambda a: pl.BlockSpec(a.shape, lambda i: (0,) * a.ndim)
    return pl.pallas_call(
        _xattn_kernel,
        grid=(t_all // tm,),
        in_specs=[tok, full(gpre), full(wq), pl.BlockSpec((N_MEM, 2 * D_MODEL), lambda i: (i // per_batch, 0)),
                  full(wo), full(gpost)],
        out_specs=tok,
        out_shape=jax.ShapeDtypeStruct((t_all, D_MODEL), F32),
        scratch_shapes=[pltpu.VMEM((tm, D_MODEL), BF16)],
        compiler_params=_params("parallel"),
        name="xattn",
    )(x, gpre, wq, kv, wo, gpost)


def _ffn_kernel(x_ref, gpre_ref, wg_ref, wu_ref, wd_ref, gpost_ref, out_ref, h_scr, acc_scr):
    f = pl.program_id(1)

    @pl.when(f == 0)
    def _():
        h_scr[...] = _rms(x_ref[...], gpre_ref[...]).astype(BF16)
        acc_scr[...] = jnp.zeros(acc_scr.shape, F32)

    h = h_scr[...]
    gate = _dot(h, wg_ref[...])
    up = _dot(h, wu_ref[...])
    acc_scr[...] += _dot((gate * jax.nn.sigmoid(gate) * up).astype(BF16), wd_ref[...])

    @pl.when(f == pl.num_programs(1) - 1)
    def _():
        out_ref[...] = x_ref[...] + _rms(acc_scr[...], gpost_ref[...])


def _ffn(x, gpre, w_gate_up, w_down, gpost):
    t_all = x.shape[0]
    tm = TOK_TILE
    nf = 2
    tf = D_FF // nf
    tok = pl.BlockSpec((tm, D_MODEL), lambda i, f: (i, 0))
    full = lambda a: pl.BlockSpec(a.shape, lambda i, f: (0,) * a.ndim)
    return pl.pallas_call(
        _ffn_kernel,
        grid=(t_all // tm, nf),
        in_specs=[tok, full(gpre),
                  pl.BlockSpec((D_MODEL, tf), lambda i, f: (0, f)),
                  pl.BlockSpec((D_MODEL, tf), lambda i, f: (0, f + nf)),
                  pl.BlockSpec((tf, D_MODEL), lambda i, f: (f, 0)),
                  full(gpost)],
        out_specs=tok,
        out_shape=jax.ShapeDtypeStruct((t_all, D_MODEL), F32),
        scratch_shapes=[pltpu.VMEM((tm, D_MODEL), BF16), pltpu.VMEM((tm, D_MODEL), F32)],
        compiler_params=_params("parallel", "arbitrary"),
        name="ffn",
    )(x, gpre, w_gate_up, w_gate_up, w_down, gpost)


def _rope_tables(seq):
    pos = jnp.arange(seq, dtype=F32)[:, None]

    def cos_sin(d_rot):
        inv = ROPE_THETA ** (-jnp.arange(0, d_rot, 2, dtype=F32) / d_rot)
        ang = pos * inv[None, :]
        return jnp.cos(ang), jnp.sin(ang)

    c, s = cos_sin(MLA_ROPE)
    ones = jnp.ones((seq, MLA_NOPE), F32)
    tail = jnp.ones((seq, LANES - MLA_NOPE - MLA_ROPE), F32)
    cm = jnp.concatenate([ones, c, c, tail], axis=1)
    sm = jnp.concatenate([0 * ones, -s, s, 0 * tail], axis=1)
    c, s = cos_sin(SW_ROT)
    rest = jnp.ones((seq, SW_HD - SW_ROT), F32)
    cs = jnp.tile(jnp.concatenate([c, c, rest], axis=1), (1, LANES // SW_HD))
    ss = jnp.tile(jnp.concatenate([-s, s, 0 * rest], axis=1), (1, LANES // SW_HD))
    return cm, sm, cs, ss


def _pad_heads(w, heads, width, offset=0):
    k = w.shape[0]
    w = w.reshape(k, heads, width)
    w = jnp.pad(w, ((0, 0), (0, 0), (offset, LANES - width - offset)))
    return w.reshape(k, heads * LANES)


def _prep_layer(w_in, w_uq, w_ukv, w_out):
    o = IN_OFFS
    col = lambda i: w_in[:, o[i]:o[i + 1]]
    z = lambda n: jnp.zeros((D_MODEL, n), F32)
    w_p = jnp.concatenate([
        col(0), col(1),
        z(MLA_NOPE), col(2), z(LANES - MLA_NOPE - MLA_ROPE),
        col(3), col(4), _pad_heads(col(5), ML_HEADS, ML_DV), col(6), col(9), col(10), col(11)], axis=1)
    assert w_p.shape[1] == P_TOTAL
    w_gates_t = jnp.concatenate([col(7), col(8)], axis=1).T
    uq = w_uq.reshape(MLA_Q_LORA, MLA_HEADS, MLA_NOPE + MLA_ROPE)
    uq = jnp.pad(uq, ((0, 0), (0, 0), (0, LANES - MLA_NOPE - MLA_ROPE))).reshape(MLA_Q_LORA, MLA_PAD_W)
    ukv = w_ukv.reshape(MLA_KV_LORA, MLA_HEADS, MLA_NOPE + MLA_V)
    pad_half = lambda a: jnp.pad(a, ((0, 0), (0, 0), (0, LANES - a.shape[2]))).reshape(MLA_KV_LORA, MLA_PAD_W)
    wk, wv = pad_half(ukv[:, :, :MLA_NOPE]), pad_half(ukv[:, :, MLA_NOPE:])
    wa = w_out[:MLA_W].reshape(MLA_HEADS, MLA_V, D_MODEL)
    wa = jnp.pad(wa, ((0, 0), (0, LANES - MLA_V), (0, 0))).reshape(MLA_PAD_W, D_MODEL)
    wb, wc = w_out[MLA_W:MLA_W + ML_W], w_out[MLA_W + ML_W:]
    bf = lambda a: a.astype(BF16)
    return bf(w_p), bf(w_gates_t), bf(uq), bf(wk), bf(wv), bf(wa), bf(wb), bf(wc)


def kernel(x, mem, w_in, mla_q_norm, mla_w_uq, mla_kv_norm, mla_w_ukv, ml_conv_w, ml_conv_b, ml_b_i, ml_b_f,
           w_out, x_w_q, x_w_kv, x_w_o, w_gate_up, w_down, norm_mix_pre, norm_mix_post, norm_mem,
           norm_x_pre, norm_x_post, norm_ffn_pre, norm_ffn_post):
    batch, seq, _ = x.shape
    depth = w_in.shape[0]
    t_all = batch * seq
    assert seq % (SW_DILATIONS[-1] * SW_BLOCK) == 0 and seq % TOK_TILE == 0
    cm, sm, cs, ss = _rope_tables(seq)
    one = jnp.zeros((1, LANES), F32).at[0, MLA_V].set(1.0)
    row = lambda a: a.reshape(1, -1)
    colv = lambda a: a.reshape(-1, 1)
    xt = x.reshape(t_all, D_MODEL)
    mem2 = mem.reshape(batch * N_MEM, D_MODEL)
    for l in range(depth):
        w_p, w_gt, uq, wk, wv, wa, wb, wc = _prep_layer(w_in[l], mla_w_uq[l], mla_w_ukv[l], w_out[l])
        q, k, v, mqk, mv, mo, gates_t, sq, sk, sv = _inproj(
            xt, row(norm_mix_pre[l]), w_p, w_gt, row(mla_q_norm[l]), uq, row(mla_kv_norm[l]), wk, wv,
            cm, sm, cs, ss, one, seq)
        b3 = lambda a: a.reshape(batch, seq, a.shape[-1])
        ya = _flash(b3(q), b3(k), b3(v)).reshape(t_all, MLA_PAD_W)
        yb = _mlstm(mqk, mv, mo, gates_t, ml_conv_w[l], row(ml_conv_b[l]), colv(ml_b_i[l]), colv(ml_b_f[l]),
                    batch, seq)
        outs = []
        for dil in SW_DILATIONS:
            view = lambda a: a.reshape(batch, seq // dil, dil * SW_W)
            o_d, l_d = _swa_branch(view(sq), view(sk), view(sv), dil)
            outs.append((o_d.reshape(t_all, SW_W), l_d.reshape(t_all, SW_W)))
        xt = _outproj(xt, ya, yb, outs[0][0], outs[1][0], outs[2][0], outs[0][1], outs[1][1], outs[2][1],
                      wa, wb, wc, row(norm_mix_post[l]))
        kv = _memkv(mem2, row(norm_mem[l]), x_w_kv[l].astype(BF16))
        xt = _xattn(xt, row(norm_x_pre[l]), x_w_q[l].astype(BF16), kv, x_w_o[l].astype(BF16),
                    row(norm_x_post[l]), seq)
        xt = _ffn(xt, row(norm_ffn_pre[l]), w_gate_up[l].astype(BF16), w_down[l].astype(BF16),
                  row(norm_ffn_post[l]))
    return xt.reshape(batch, seq, D_MODEL)
```

```python
import functools

import jax
import jax.numpy as jnp
import numpy as np
from jax import lax
from jax.experimental import pallas as pl
from jax.experimental.pallas import tpu as pltpu

F32 = jnp.float32
BF16 = jnp.bfloat16

D_MODEL = 1024
N_MEM = 256
ROPE_THETA = 500000.0
EPS = 1e-6
NEG = -1e30
LOG2_E = 1.4426950408889634
LANES = 128

MLA_HEADS, MLA_NOPE, MLA_ROPE, MLA_V = 8, 64, 32, 64
MLA_Q_LORA, MLA_KV_LORA = 256, 128
ML_HEADS, ML_DK, ML_DV, ML_CONV, ML_CHUNK = 4, 64, 64, 4, 128
SW_HEADS, SW_HD = 4, 64
SW_ROT = SW_HD // 4
SW_DILATIONS = (1, 4, 16)
SW_BLOCK = 128
X_HEADS = 4
X_HD = D_MODEL // X_HEADS
D_FF = 2816

MLA_W = MLA_HEADS * MLA_V
ML_W = ML_HEADS * ML_DV
SW_W = SW_HEADS * SW_HD
MLA_PAD_W = MLA_HEADS * LANES
ML_PAD_W = ML_HEADS * LANES

IN_SIZES = (MLA_Q_LORA, MLA_KV_LORA, MLA_ROPE, ML_HEADS * ML_DK, ML_HEADS * ML_DK, ML_W, ML_W,
            ML_HEADS, ML_HEADS, SW_W, SW_W, SW_W)
IN_OFFS = tuple(int(v) for v in np.cumsum((0,) + IN_SIZES))

P_CQ = 0
P_CKV = 256
P_KROPE = 384
P_MLQK = 512
P_MLV = 1024
P_MLO = 1536
P_SWQ = 1792
P_SWK = 2048
P_SWV = 2304
P_TOTAL = 2560

TOK_TILE = 512
FLASH_TILE = 512
SWA_QB = 4
VMEM_LIMIT = 56 * 1024 * 1024


def _rms(x, g):
    return x * lax.rsqrt(jnp.mean(x * x, axis=-1, keepdims=True) + EPS) * g


def _dot(a, b):
    return jnp.dot(a, b, preferred_element_type=F32)


def _dot_nt(a, b):
    return lax.dot_general(a, b, (((1,), (1,)), ((), ())), preferred_element_type=F32)


def _dot_tn(a, b):
    return lax.dot_general(a, b, (((0,), (0,)), ((), ())), preferred_element_type=F32)


def _params(*sem):
    return pltpu.CompilerParams(dimension_semantics=sem, vmem_limit_bytes=VMEM_LIMIT)


def _inproj_kernel(x_ref, g_ref, w_ref, wgt_ref, qg_ref, wuq_ref, kvg_ref, wk_ref, wvt_ref,
                   cm_ref, sm_ref, cs_ref, ss_ref, one_ref,
                   q_ref, k_ref, vt_ref, mqk_ref, mv_ref, mo_ref, gr_ref, sq_ref, sk_ref, sv_ref):
    tm = x_ref.shape[0]
    h = _rms(x_ref[...], g_ref[...]).astype(BF16)

    def proj(a, b):
        return _dot(h, w_ref[:, a:b])

    lane = lax.broadcasted_iota(jnp.int32, (tm, LANES), 1)
    cm, sm = cm_ref[...], sm_ref[...]
    mla_first = lane < MLA_NOPE + MLA_ROPE // 2

    def rope_mla(t):
        rot = jnp.where(mla_first, pltpu.roll(t, LANES - MLA_ROPE // 2, 1), pltpu.roll(t, MLA_ROPE // 2, 1))
        return t * cm + rot * sm

    cs, ss = cs_ref[...], ss_ref[...]
    sw_first = (lane % SW_HD) < SW_ROT // 2

    def rope_sw(t):
        rot = jnp.where(sw_first, pltpu.roll(t, LANES - SW_ROT // 2, 1), pltpu.roll(t, SW_ROT // 2, 1))
        return t * cs + rot * ss

    cqn = _rms(proj(P_CQ, P_CQ + MLA_Q_LORA), qg_ref[...]).astype(BF16)
    q_scale = (MLA_NOPE + MLA_ROPE) ** -0.5 * LOG2_E
    for hh in range(MLA_HEADS):
        sl = slice(hh * LANES, (hh + 1) * LANES)
        q_ref[:, sl] = (rope_mla(_dot(cqn, wuq_ref[:, sl])) * q_scale).astype(BF16)
    kvn = _rms(proj(P_CKV, P_CKV + MLA_KV_LORA), kvg_ref[...]).astype(BF16)
    kpe = rope_mla(proj(P_KROPE, P_KROPE + LANES))
    one = one_ref[...]
    one_col = (lax.broadcasted_iota(jnp.int32, (LANES, 1), 0) == MLA_V).astype(F32)
    for hh in range(MLA_HEADS):
        sl = slice(hh * LANES, (hh + 1) * LANES)
        k_ref[:, sl] = (_dot(kvn, wk_ref[:, sl]) + kpe).astype(BF16)
        vt_ref[sl, :] = (_dot_nt(wvt_ref[sl, :], kvn) + one_col).astype(BF16)

    mqk_ref[...] = proj(P_MLQK, P_MLQK + 2 * ML_HEADS * ML_DK)
    for hh in range(ML_HEADS):
        sl = slice(hh * LANES, (hh + 1) * LANES)
        mv_ref[:, sl] = (proj(P_MLV + hh * LANES, P_MLV + (hh + 1) * LANES) + one).astype(BF16)
    mo_ref[...] = proj(P_MLO, P_MLO + ML_W)
    gr_ref[...] = _dot_nt(wgt_ref[...], h)

    for half in range(SW_W // LANES):
        sl = slice(half * LANES, (half + 1) * LANES)
        sq_ref[:, sl] = (rope_sw(proj(P_SWQ + half * LANES, P_SWQ + (half + 1) * LANES)) * SW_HD ** -0.5).astype(BF16)
        sk_ref[:, sl] = rope_sw(proj(P_SWK + half * LANES, P_SWK + (half + 1) * LANES)).astype(BF16)
    sv_ref[...] = proj(P_SWV, P_SWV + SW_W).astype(BF16)


def _inproj(x, g, w, wgt, qg, wuq, kvg, wk, wv, cm, sm, cs, ss, one, seq):
    t_all = x.shape[0]
    tm = TOK_TILE
    pos_tiles = seq // tm
    tok = lambda w_: pl.BlockSpec((tm, w_), lambda i: (i, 0))
    full = lambda a: pl.BlockSpec(a.shape, lambda i: (0,) * a.ndim)
    pos = pl.BlockSpec((tm, LANES), lambda i: (i % pos_tiles, 0))
    out_shapes = (
        jax.ShapeDtypeStruct((t_all, MLA_PAD_W), BF16),
        jax.ShapeDtypeStruct((t_all, MLA_PAD_W), BF16),
        jax.ShapeDtypeStruct((t_all // tm, MLA_PAD_W, tm), BF16),
        jax.ShapeDtypeStruct((t_all, 2 * ML_HEADS * ML_DK), F32),
        jax.ShapeDtypeStruct((t_all, ML_PAD_W), BF16),
        jax.ShapeDtypeStruct((t_all, ML_W), F32),
        jax.ShapeDtypeStruct((2 * ML_HEADS, t_all), F32),
        jax.ShapeDtypeStruct((t_all, SW_W), BF16),
        jax.ShapeDtypeStruct((t_all, SW_W), BF16),
        jax.ShapeDtypeStruct((t_all, SW_W), BF16),
    )
    out_specs = (tok(MLA_PAD_W), tok(MLA_PAD_W), pl.BlockSpec((None, MLA_PAD_W, tm), lambda i: (i, 0, 0)),
                 tok(2 * ML_HEADS * ML_DK), tok(ML_PAD_W),
                 tok(ML_W), pl.BlockSpec((2 * ML_HEADS, tm), lambda i: (0, i)), tok(SW_W), tok(SW_W), tok(SW_W))
    return pl.pallas_call(
        _inproj_kernel,
        grid=(t_all // tm,),
        in_specs=[tok(D_MODEL), full(g), full(w), full(wgt), full(qg), full(wuq), full(kvg), full(wk), full(wv),
                  pos, pos, pos, pos, full(one)],
        out_specs=out_specs,
        out_shape=out_shapes,
        compiler_params=_params("parallel"),
        name="inproj",
    )(x, g, w, wgt, qg, wuq, kvg, wk, wv, cm, sm, cs, ss, one)


def _flash_kernel(q_ref, k_ref, vt_ref, o_ref, m_scr, acc_scr, s_scr, cmax_scr):
    t = q_ref.shape[0]
    i = pl.program_id(2)
    n_pairs = lax.shift_right_logical(i + 2, 1)
    qt = q_ref[...].astype(F32).T.astype(BF16)
    m_scr[...] = jnp.full(m_scr.shape, NEG, F32)
    acc_scr[...] = jnp.zeros(acc_scr.shape, F32)

    def scores(j, buf, masked):
        start = pl.multiple_of(j * t, t)
        st = _dot(k_ref[pl.ds(start, t), :], qt)
        if masked:
            key = lax.broadcasted_iota(jnp.int32, (t, t), 0) + j * t
            qry = lax.broadcasted_iota(jnp.int32, (t, t), 1) + i * t
            st = jnp.where(key <= qry, st, NEG)
        s_scr[buf] = st
        cmax_scr[buf] = jnp.max(st, axis=0, keepdims=True)

    def accumulate(j, buf):
        m_old = m_scr[...]
        m_new = jnp.maximum(m_old, cmax_scr[buf])
        p = jnp.exp2(s_scr[buf] - m_new).astype(BF16)
        acc_scr[...] = acc_scr[...] * jnp.exp2(m_old - m_new) + _dot(vt_ref[j], p)
        m_scr[...] = m_new

    def pair(pp, mask_second):
        scores(2 * pp + 1, 1, False)
        accumulate(2 * pp, 0)
        scores(2 * pp + 2, 0, mask_second)
        accumulate(2 * pp + 1, 1)

    scores(0, 0, True)

    def body(pp, carry):
        pair(pp, False)
        return carry

    lax.fori_loop(0, n_pairs - 2, body, 0)

    @pl.when(n_pairs >= 2)
    def _():
        pair(n_pairs - 2, True)

    last = 2 * n_pairs - 1
    scores(last, 1, True)
    accumulate(last - 1, 0)
    accumulate(last, 1)
    acc = acc_scr[...]
    o_ref[...] = (acc / acc[MLA_V:MLA_V + 1, :]).T.astype(BF16)


def _flash(q, k, vt):
    b, s, _ = q.shape
    t = FLASH_TILE
    nt = s // t
    assert nt % 2 == 0
    return pl.pallas_call(
        _flash_kernel,
        grid=(b, MLA_HEADS, nt),
        in_specs=[pl.BlockSpec((None, t, LANES), lambda bb, hh, i: (bb, i, hh)),
                  pl.BlockSpec((None, s, LANES), lambda bb, hh, i: (bb, 0, hh)),
                  pl.BlockSpec((nt, LANES, t), lambda bb, hh, i: (bb, hh, 0))],
        out_specs=pl.BlockSpec((None, t, LANES), lambda bb, hh, i: (bb, i, hh)),
        out_shape=jax.ShapeDtypeStruct((b, s, MLA_PAD_W), BF16),
        scratch_shapes=[pltpu.VMEM((1, t), F32), pltpu.VMEM((LANES, t), F32),
                        pltpu.VMEM((2, t, t), F32), pltpu.VMEM((2, 1, t), F32)],
        compiler_params=_params("parallel", "parallel", "arbitrary"),
        name="mla_flash",
    )(q, k, vt)


def _log_sigmoid(z):
    return jnp.minimum(z, 0.0) - jnp.log1p(jnp.exp(-jnp.abs(z)))


def _mlstm_kernel(qkc_ref, qkp_ref, v_ref, o_ref, g_ref, cw_ref, cb_ref, bi_ref, bf_ref,
                  y_ref, c_scr, m_scr):
    L = ML_CHUNK
    c = pl.program_id(1)

    @pl.when(c == 0)
    def _():
        c_scr[...] = jnp.zeros(c_scr.shape, F32)
        m_scr[...] = jnp.full(m_scr.shape, NEG, F32)

    cur = qkc_ref[...]
    prev = jnp.where(c > 0, qkp_ref[...], 0.0)
    row_w = lax.broadcasted_iota(jnp.int32, cur.shape, 0)
    conv = cb_ref[...] + cw_ref[ML_CONV - 1:ML_CONV, :] * cur
    for sh in range(1, ML_CONV):
        shifted = jnp.where(row_w >= sh, pltpu.roll(cur, sh, 0), pltpu.roll(prev, sh, 0))
        conv = conv + cw_ref[ML_CONV - 1 - sh:ML_CONV - sh, :] * shifted
    qk = conv * jax.nn.sigmoid(conv)

    gates = g_ref[...]
    li_all = gates[0:ML_HEADS, :] + bi_ref[...]
    lf_all = _log_sigmoid(gates[ML_HEADS:, :] + bf_ref[...])
    lane_g = lax.broadcasted_iota(jnp.int32, lf_all.shape, 1)
    b_all = lf_all
    step = 1
    while step < L:
        b_all = b_all + jnp.where(lane_g >= step, pltpu.roll(b_all, step, 1), 0.0)
        step *= 2

    row = lax.broadcasted_iota(jnp.int32, (L, L), 0)
    col = lax.broadcasted_iota(jnp.int32, (L, L), 1)
    tri = col <= row
    eye = col == row

    def to_col(r):
        return jnp.sum(jnp.where(eye, r, 0.0), axis=1, keepdims=True)

    for hh in range(ML_HEADS):
        qh = qk[:, hh * ML_DK:(hh + 1) * ML_DK].astype(BF16)
        kf = qk[:, (ML_HEADS + hh) * ML_DK:(ML_HEADS + hh + 1) * ML_DK] * ML_DK ** -0.5
        vh = v_ref[:, hh * LANES:(hh + 1) * LANES]
        b_row = b_all[hh:hh + 1, :]
        li_row = li_all[hh:hh + 1, :]
        m_prev = m_scr[hh:hh + 1, 0:1]
        c_ext = c_scr[hh]

        b_col = to_col(b_row)
        d = jnp.where(tri, b_col - b_row + li_row, NEG)
        inter = b_col + m_prev
        m_t = jnp.maximum(inter, jnp.max(d, axis=1, keepdims=True))
        w_intra = jnp.exp(d - m_t)
        w_inter = jnp.exp(inter - m_t)
        s = _dot_nt(qh, kf.astype(BF16)) * w_intra
        intra = _dot(s.astype(BF16), vh)
        cross = _dot(qh, c_ext.astype(BF16))
        num = intra[:, :ML_DV] + w_inter * cross[:, :ML_DV]
        nq = jnp.sum(s, axis=1, keepdims=True) + w_inter * cross[:, ML_DV:ML_DV + 1]
        hout = num / jnp.maximum(jnp.abs(nq), jnp.exp(-m_t))
        og = jax.nn.sigmoid(o_ref[:, hh * ML_DV:(hh + 1) * ML_DV])
        y_ref[:, hh * ML_DV:(hh + 1) * ML_DV] = (og * hout).astype(BF16)

        b_last = b_row[:, L - 1:L]
        g = b_last - b_row + li_row
        m_new = jnp.maximum(b_last + m_prev, jnp.max(g, axis=1, keepdims=True))
        decay = jnp.exp(b_last + m_prev - m_new)
        ws_col = to_col(jnp.exp(g - m_new))
        c_scr[hh] = decay * c_ext + _dot_tn((kf * ws_col).astype(BF16), vh)
        m_scr[hh:hh + 1, :] = jnp.broadcast_to(m_new, (1, LANES))


def _mlstm(qk_pre, v, o_pre, gates_t, conv_w, conv_b, b_i, b_f, batch, seq):
    L = ML_CHUNK
    nc = seq // L
    wqk = 2 * ML_HEADS * ML_DK
    full = lambda a: pl.BlockSpec(a.shape, lambda bb, c: (0,) * a.ndim)
    return pl.pallas_call(
        _mlstm_kernel,
        grid=(batch, nc),
        in_specs=[pl.BlockSpec((L, wqk), lambda bb, c: (bb * nc + c, 0)),
                  pl.BlockSpec((L, wqk), lambda bb, c: (bb * nc + jnp.maximum(c - 1, 0), 0)),
                  pl.BlockSpec((L, ML_PAD_W), lambda bb, c: (bb * nc + c, 0)),
                  pl.BlockSpec((L, ML_W), lambda bb, c: (bb * nc + c, 0)),
                  pl.BlockSpec((2 * ML_HEADS, L), lambda bb, c: (0, bb * nc + c)),
                  full(conv_w), full(conv_b), full(b_i), full(b_f)],
        out_specs=pl.BlockSpec((L, ML_W), lambda bb, c: (bb * nc + c, 0)),
        out_shape=jax.ShapeDtypeStruct((batch * seq, ML_W), BF16),
        scratch_shapes=[pltpu.VMEM((ML_HEADS, ML_DK, LANES), F32), pltpu.VMEM((8, LANES), F32)],
        compiler_params=_params("parallel", "arbitrary"),
        name="mlstm",
    )(qk_pre, qk_pre, v, o_pre, gates_t, conv_w, conv_b, b_i, b_f)


def _swa_kernel(q_ref, kc_ref, kp_ref, vc_ref, vp_ref, o_ref, l_ref):
    blk = SW_BLOCK
    n = pl.program_id(2)
    row = lax.broadcasted_iota(jnp.int32, (blk, blk), 0)
    col = lax.broadcasted_iota(jnp.int32, (blk, blk), 1)
    cur_ok = col <= row
    for qb in range(q_ref.shape[0] // blk):
        rows = slice(qb * blk, (qb + 1) * blk)
        if qb == 0:
            k_prev, v_prev = kp_ref[...], vp_ref[...]
            prev_ok = (col >= row) & (n > 0)
        else:
            before = slice((qb - 1) * blk, qb * blk)
            k_prev, v_prev = kc_ref[before, :], vc_ref[before, :]
            prev_ok = col >= row
        q, k_cur, v_cur = q_ref[rows, :], kc_ref[rows, :], vc_ref[rows, :]
        for hh in range(SW_HEADS):
            hs = slice(hh * SW_HD, (hh + 1) * SW_HD)
            s_p = jnp.where(prev_ok, _dot_nt(q[:, hs], k_prev[:, hs]), NEG)
            s_c = jnp.where(cur_ok, _dot_nt(q[:, hs], k_cur[:, hs]), NEG)
            mx = jnp.maximum(jnp.max(s_p, axis=1, keepdims=True), jnp.max(s_c, axis=1, keepdims=True))
            e_p = jnp.exp(s_p - mx)
            e_c = jnp.exp(s_c - mx)
            den = jnp.sum(e_p, axis=1, keepdims=True) + jnp.sum(e_c, axis=1, keepdims=True)
            o = (_dot(e_p.astype(BF16), v_prev[:, hs]) + _dot(e_c.astype(BF16), v_cur[:, hs])) / den
            o_ref[rows, hs] = o
            l_ref[rows, hs] = jnp.broadcast_to(mx + jnp.log(den), (blk, SW_HD))


def _swa_branch(q, k, v, dil):
    b, ls, _ = q.shape
    rows = min(SWA_QB * SW_BLOCK, ls)
    qb = rows // SW_BLOCK
    cur = pl.BlockSpec((None, rows, SW_W), lambda bb, r, n: (bb, n, r))
    prev = pl.BlockSpec((None, SW_BLOCK, SW_W), lambda bb, r, n: (bb, jnp.maximum(n * qb - 1, 0), r))
    return pl.pallas_call(
        _swa_kernel,
        grid=(b, dil, ls // rows),
        in_specs=[cur, cur, prev, cur, prev],
        out_specs=(cur, cur),
        out_shape=(jax.ShapeDtypeStruct(q.shape, F32), jax.ShapeDtypeStruct(q.shape, F32)),
        compiler_params=_params("parallel", "parallel", "arbitrary"),
        name=f"swa_d{dil}",
    )(q, k, k, v, v)


def _outproj_kernel(x_ref, ya_ref, yb_ref, o1_ref, o4_ref, o16_ref, l1_ref, l4_ref, l16_ref,
                    wa_ref, wb_ref, wc_ref, g_ref, out_ref):
    l1, l4, l16 = l1_ref[...], l4_ref[...], l16_ref[...]
    mx = jnp.maximum(jnp.maximum(l1, l4), l16)
    w1, w4, w16 = jnp.exp(l1 - mx), jnp.exp(l4 - mx), jnp.exp(l16 - mx)
    yc = (w1 * o1_ref[...] + w4 * o4_ref[...] + w16 * o16_ref[...]) / (w1 + w4 + w16)
    y = _dot(ya_ref[...], wa_ref[...]) + _dot(yb_ref[...], wb_ref[...]) + _dot(yc.astype(BF16), wc_ref[...])
    out_ref[...] = x_ref[...] + _rms(y, g_ref[...])


def _outproj(x, ya, yb, o1, o4, o16, l1, l4, l16, wa, wb, wc, g):
    t_all = x.shape[0]
    tm = TOK_TILE
    tok = lambda w_: pl.BlockSpec((tm, w_), lambda i: (i, 0))
    full = lambda a: pl.BlockSpec(a.shape, lambda i: (0,) * a.ndim)
    return pl.pallas_call(
        _outproj_kernel,
        grid=(t_all // tm,),
        in_specs=[tok(D_MODEL), tok(MLA_PAD_W), tok(ML_W)] + [tok(SW_W)] * 6 + [full(wa), full(wb), full(wc), full(g)],
        out_specs=tok(D_MODEL),
        out_shape=jax.ShapeDtypeStruct((t_all, D_MODEL), F32),
        compiler_params=_params("parallel"),
        name="outproj",
    )(x, ya, yb, o1, o4, o16, l1, l4, l16, wa, wb, wc, g)


def _memkv_kernel(mem_ref, g_ref, w_ref, kv_ref):
    kv_ref[...] = _dot(_rms(mem_ref[...], g_ref[...]).astype(BF16), w_ref[...]).astype(BF16)


def _memkv(mem, g, w):
    n = mem.shape[0]
    full = lambda a: pl.BlockSpec(a.shape, lambda i: (0,) * a.ndim)
    return pl.pallas_call(
        _memkv_kernel,
        grid=(1,),
        in_specs=[full(mem), full(g), full(w)],
        out_specs=pl.BlockSpec((n, 2 * D_MODEL), lambda i: (0, 0)),
        out_shape=jax.ShapeDtypeStruct((n, 2 * D_MODEL), BF16),
        compiler_params=_params("arbitrary"),
        name="memkv",
    )(mem, g, w)


def _xattn_kernel(x_ref, gpre_ref, wq_ref, kv_ref, wo_ref, gpost_ref, out_ref, o_scr):
    x = x_ref[...]
    h = _rms(x, gpre_ref[...]).astype(BF16)
    q = (_dot(h, wq_ref[...]) * X_HD ** -0.5).astype(BF16)
    for hh in range(X_HEADS):
        hs = slice(hh * X_HD, (hh + 1) * X_HD)
        s = _dot_nt(q[:, hs], kv_ref[:, hs])
        e = jnp.exp(s - jnp.max(s, axis=1, keepdims=True))
        den = jnp.sum(e, axis=1, keepdims=True)
        o_scr[:, hs] = (_dot(e.astype(BF16), kv_ref[:, D_MODEL + hh * X_HD:D_MODEL + (hh + 1) * X_HD]) / den).astype(BF16)
    y = _dot(o_scr[...], wo_ref[...])
    out_ref[...] = x + _rms(y, gpost_ref[...])


def _xattn(x, gpre, wq, kv, wo, gpost, seq):
    t_all = x.shape[0]
    tm = TOK_TILE
    per_batch = seq // tm
    tok = pl.BlockSpec((tm, D_MODEL), lambda i: (i, 0))
    full = lambda a: pl.BlockSpec(a.shape, lambda i: (0,) * a.ndim)
    return pl.pallas_call(
        _xattn_kernel,
        grid=(t_all // tm,),
        in_specs=[tok, full(gpre), full(wq), pl.BlockSpec((N_MEM, 2 * D_MODEL), lambda i: (i // per_batch, 0)),
                  full(wo), full(gpost)],
        out_specs=tok,
        out_shape=jax.ShapeDtypeStruct((t_all, D_MODEL), F32),
        scratch_shapes=[pltpu.VMEM((tm, D_MODEL), BF16)],
        compiler_params=_params("parallel"),
        name="xattn",
    )(x, gpre, wq, kv, wo, gpost)


def _ffn_kernel(x_ref, gpre_ref, wg_ref, wu_ref, wd_ref, gpost_ref, out_ref, h_scr, acc_scr):
    f = pl.program_id(1)

    @pl.when(f == 0)
    def _():
        h_scr[...] = _rms(x_ref[...], gpre_ref[...]).astype(BF16)
        acc_scr[...] = jnp.zeros(acc_scr.shape, F32)

    h = h_scr[...]
    gate = _dot(h, wg_ref[...])
    up = _dot(h, wu_ref[...])
    acc_scr[...] += _dot((gate * jax.nn.sigmoid(gate) * up).astype(BF16), wd_ref[...])

    @pl.when(f == pl.num_programs(1) - 1)
    def _():
        out_ref[...] = x_ref[...] + _rms(acc_scr[...], gpost_ref[...])


def _ffn(x, gpre, w_gate_up, w_down, gpost):
    t_all = x.shape[0]
    tm = TOK_TILE
    nf = 2
    tf = D_FF // nf
    tok = pl.BlockSpec((tm, D_MODEL), lambda i, f: (i, 0))
    full = lambda a: pl.BlockSpec(a.shape, lambda i, f: (0,) * a.ndim)
    return pl.pallas_call(
        _ffn_kernel,
        grid=(t_all // tm, nf),
        in_specs=[tok, full(gpre),
                  pl.BlockSpec((D_MODEL, tf), lambda i, f: (0, f)),
                  pl.BlockSpec((D_MODEL, tf), lambda i, f: (0, f + nf)),
                  pl.BlockSpec((tf, D_MODEL), lambda i, f: (f, 0)),
                  full(gpost)],
        out_specs=tok,
        out_shape=jax.ShapeDtypeStruct((t_all, D_MODEL), F32),
        scratch_shapes=[pltpu.VMEM((tm, D_MODEL), BF16), pltpu.VMEM((tm, D_MODEL), F32)],
        compiler_params=_params("parallel", "arbitrary"),
        name="ffn",
    )(x, gpre, w_gate_up, w_gate_up, w_down, gpost)


def _rope_tables(seq):
    pos = jnp.arange(seq, dtype=F32)[:, None]

    def cos_sin(d_rot):
        inv = ROPE_THETA ** (-jnp.arange(0, d_rot, 2, dtype=F32) / d_rot)
        ang = pos * inv[None, :]
        return jnp.cos(ang), jnp.sin(ang)

    c, s = cos_sin(MLA_ROPE)
    ones = jnp.ones((seq, MLA_NOPE), F32)
    tail = jnp.ones((seq, LANES - MLA_NOPE - MLA_ROPE), F32)
    cm = jnp.concatenate([ones, c, c, tail], axis=1)
    sm = jnp.concatenate([0 * ones, -s, s, 0 * tail], axis=1)
    c, s = cos_sin(SW_ROT)
    rest = jnp.ones((seq, SW_HD - SW_ROT), F32)
    cs = jnp.tile(jnp.concatenate([c, c, rest], axis=1), (1, LANES // SW_HD))
    ss = jnp.tile(jnp.concatenate([-s, s, 0 * rest], axis=1), (1, LANES // SW_HD))
    return cm, sm, cs, ss


def _pad_heads(w, heads, width, offset=0):
    k = w.shape[0]
    w = w.reshape(k, heads, width)
    w = jnp.pad(w, ((0, 0), (0, 0), (offset, LANES - width - offset)))
    return w.reshape(k, heads * LANES)


def _prep_layer(w_in, w_uq, w_ukv, w_out):
    o = IN_OFFS
    col = lambda i: w_in[:, o[i]:o[i + 1]]
    z = lambda n: jnp.zeros((D_MODEL, n), F32)
    w_p = jnp.concatenate([
        col(0), col(1),
        z(MLA_NOPE), col(2), z(LANES - MLA_NOPE - MLA_ROPE),
        col(3), col(4), _pad_heads(col(5), ML_HEADS, ML_DV), col(6), col(9), col(10), col(11)], axis=1)
    assert w_p.shape[1] == P_TOTAL
    w_gates_t = jnp.concatenate([col(7), col(8)], axis=1).T
    uq = w_uq.reshape(MLA_Q_LORA, MLA_HEADS, MLA_NOPE + MLA_ROPE)
    uq = jnp.pad(uq, ((0, 0), (0, 0), (0, LANES - MLA_NOPE - MLA_ROPE))).reshape(MLA_Q_LORA, MLA_PAD_W)
    ukv = w_ukv.reshape(MLA_KV_LORA, MLA_HEADS, MLA_NOPE + MLA_V)
    pad_half = lambda a: jnp.pad(a, ((0, 0), (0, 0), (0, LANES - a.shape[2]))).reshape(MLA_KV_LORA, MLA_PAD_W)
    wk, wv = pad_half(ukv[:, :, :MLA_NOPE]), pad_half(ukv[:, :, MLA_NOPE:]).T
    wa = w_out[:MLA_W].reshape(MLA_HEADS, MLA_V, D_MODEL)
    wa = jnp.pad(wa, ((0, 0), (0, LANES - MLA_V), (0, 0))).reshape(MLA_PAD_W, D_MODEL)
    wb, wc = w_out[MLA_W:MLA_W + ML_W], w_out[MLA_W + ML_W:]
    bf = lambda a: a.astype(BF16)
    return bf(w_p), bf(w_gates_t), bf(uq), bf(wk), bf(wv), bf(wa), bf(wb), bf(wc)


def kernel(x, mem, w_in, mla_q_norm, mla_w_uq, mla_kv_norm, mla_w_ukv, ml_conv_w, ml_conv_b, ml_b_i, ml_b_f,
           w_out, x_w_q, x_w_kv, x_w_o, w_gate_up, w_down, norm_mix_pre, norm_mix_post, norm_mem,
           norm_x_pre, norm_x_post, norm_ffn_pre, norm_ffn_post):
    batch, seq, _ = x.shape
    depth = w_in.shape[0]
    t_all = batch * seq
    assert seq % (SW_DILATIONS[-1] * SW_BLOCK) == 0 and seq % TOK_TILE == 0 and TOK_TILE == FLASH_TILE
    cm, sm, cs, ss = _rope_tables(seq)
    one = jnp.zeros((1, LANES), F32).at[0, MLA_V].set(1.0)
    row = lambda a: a.reshape(1, -1)
    colv = lambda a: a.reshape(-1, 1)
    xt = x.reshape(t_all, D_MODEL)
    mem2 = mem.reshape(batch * N_MEM, D_MODEL)
    for l in range(depth):
        w_p, w_gt, uq, wk, wv, wa, wb, wc = _prep_layer(w_in[l], mla_w_uq[l], mla_w_ukv[l], w_out[l])
        q, k, v, mqk, mv, mo, gates_t, sq, sk, sv = _inproj(
            xt, row(norm_mix_pre[l]), w_p, w_gt, row(mla_q_norm[l]), uq, row(mla_kv_norm[l]), wk, wv,
            cm, sm, cs, ss, one, seq)
        b3 = lambda a: a.reshape(batch, seq, a.shape[-1])
        ya = _flash(b3(q), b3(k), v).reshape(t_all, MLA_PAD_W)
        yb = _mlstm(mqk, mv, mo, gates_t, ml_conv_w[l], row(ml_conv_b[l]), colv(ml_b_i[l]), colv(ml_b_f[l]),
                    batch, seq)
        outs = []
        for dil in SW_DILATIONS:
            view = lambda a: a.reshape(batch, seq // dil, dil * SW_W)
            o_d, l_d = _swa_branch(view(sq), view(sk), view(sv), dil)
            outs.append((o_d.reshape(t_all, SW_W), l_d.reshape(t_all, SW_W)))
        xt = _outproj(xt, ya, yb, outs[0][0], outs[1][0], outs[2][0], outs[0][1], outs[1][1], outs[2][1],
                      wa, wb, wc, row(norm_mix_post[l]))
        kv = _memkv(mem2, row(norm_mem[l]), x_w_kv[l].astype(BF16))
        xt = _xattn(xt, row(norm_x_pre[l]), x_w_q[l].astype(BF16), kv, x_w_o[l].astype(BF16),
                    row(norm_x_post[l]), seq)
        xt = _ffn(xt, row(norm_ffn_pre[l]), w_gate_up[l].astype(BF16), w_down[l].astype(BF16),
                  row(norm_ffn_post[l]))
    return xt.reshape(batch, seq, D_MODEL)
```

```python
import functools

import jax
import jax.numpy as jnp
import numpy as np
from jax import lax
from jax.experimental import pallas as pl
from jax.experimental.pallas import tpu as pltpu

F32 = jnp.float32
BF16 = jnp.bfloat16

D_MODEL = 1024
N_MEM = 256
ROPE_THETA = 500000.0
EPS = 1e-6
NEG = -1e30
LOG2_E = 1.4426950408889634
LANES = 128

MLA_HEADS, MLA_NOPE, MLA_ROPE, MLA_V = 8, 64, 32, 64
MLA_Q_LORA, MLA_KV_LORA = 256, 128
ML_HEADS, ML_DK, ML_DV, ML_CONV, ML_CHUNK = 4, 64, 64, 4, 128
SW_HEADS, SW_HD = 4, 64
SW_ROT = SW_HD // 4
SW_DILATIONS = (1, 4, 16)
SW_BLOCK = 128
X_HEADS = 4
X_HD = D_MODEL // X_HEADS
D_FF = 2816

MLA_W = MLA_HEADS * MLA_V
ML_W = ML_HEADS * ML_DV
SW_W = SW_HEADS * SW_HD
MLA_PAD_W = MLA_HEADS * LANES
ML_PAD_W = ML_HEADS * LANES

IN_SIZES = (MLA_Q_LORA, MLA_KV_LORA, MLA_ROPE, ML_HEADS * ML_DK, ML_HEADS * ML_DK, ML_W, ML_W,
            ML_HEADS, ML_HEADS, SW_W, SW_W, SW_W)
IN_OFFS = tuple(int(v) for v in np.cumsum((0,) + IN_SIZES))

P_CQ = 0
P_CKV = 256
P_KROPE = 384
P_MLQK = 512
P_MLV = 1024
P_MLO = 1536
P_SWQ = 1792
P_SWK = 2048
P_SWV = 2304
P_TOTAL = 2560

TOK_TILE = 512
FLASH_TILE = 512
SWA_TILE = 2048
VMEM_LIMIT = 56 * 1024 * 1024


def _rms(x, g):
    return x * lax.rsqrt(jnp.mean(x * x, axis=-1, keepdims=True) + EPS) * g


def _dot(a, b):
    return jnp.dot(a, b, preferred_element_type=F32)


def _dot_nt(a, b):
    return lax.dot_general(a, b, (((1,), (1,)), ((), ())), preferred_element_type=F32)


def _dot_tn(a, b):
    return lax.dot_general(a, b, (((0,), (0,)), ((), ())), preferred_element_type=F32)


def _params(*sem):
    return pltpu.CompilerParams(dimension_semantics=sem, vmem_limit_bytes=VMEM_LIMIT)


def _inproj_kernel(x_ref, g_ref, w_ref, wgt_ref, qg_ref, wuq_ref, kvg_ref, wk_ref, wvt_ref,
                   cm_ref, sm_ref, cs_ref, ss_ref, one_ref,
                   q_ref, k_ref, vt_ref, mqk_ref, mv_ref, mo_ref, gr_ref, sq_ref, sk_ref, sv_ref):
    tm = x_ref.shape[0]
    h = _rms(x_ref[...], g_ref[...]).astype(BF16)

    def proj(a, b):
        return _dot(h, w_ref[:, a:b])

    lane = lax.broadcasted_iota(jnp.int32, (tm, LANES), 1)
    cm, sm = cm_ref[...], sm_ref[...]
    mla_first = lane < MLA_NOPE + MLA_ROPE // 2

    def rope_mla(t):
        rot = jnp.where(mla_first, pltpu.roll(t, LANES - MLA_ROPE // 2, 1), pltpu.roll(t, MLA_ROPE // 2, 1))
        return t * cm + rot * sm

    cs, ss = cs_ref[...], ss_ref[...]
    sw_first = (lane % SW_HD) < SW_ROT // 2

    def rope_sw(t):
        rot = jnp.where(sw_first, pltpu.roll(t, LANES - SW_ROT // 2, 1), pltpu.roll(t, SW_ROT // 2, 1))
        return t * cs + rot * ss

    cqn = _rms(proj(P_CQ, P_CQ + MLA_Q_LORA), qg_ref[...]).astype(BF16)
    q_scale = (MLA_NOPE + MLA_ROPE) ** -0.5 * LOG2_E
    for hh in range(MLA_HEADS):
        sl = slice(hh * LANES, (hh + 1) * LANES)
        q_ref[:, sl] = (rope_mla(_dot(cqn, wuq_ref[:, sl])) * q_scale).astype(BF16)
    kvn = _rms(proj(P_CKV, P_CKV + MLA_KV_LORA), kvg_ref[...]).astype(BF16)
    kpe = rope_mla(proj(P_KROPE, P_KROPE + LANES))
    one = one_ref[...]
    one_col = (lax.broadcasted_iota(jnp.int32, (LANES, 1), 0) == MLA_V).astype(F32)
    for hh in range(MLA_HEADS):
        sl = slice(hh * LANES, (hh + 1) * LANES)
        k_ref[:, sl] = (_dot(kvn, wk_ref[:, sl]) + kpe).astype(BF16)
        vt_ref[sl, :] = (_dot_nt(wvt_ref[sl, :], kvn) + one_col).astype(BF16)

    mqk_ref[...] = proj(P_MLQK, P_MLQK + 2 * ML_HEADS * ML_DK)
    for hh in range(ML_HEADS):
        sl = slice(hh * LANES, (hh + 1) * LANES)
        mv_ref[:, sl] = (proj(P_MLV + hh * LANES, P_MLV + (hh + 1) * LANES) + one).astype(BF16)
    mo_ref[...] = proj(P_MLO, P_MLO + ML_W)
    gr_ref[...] = _dot_nt(wgt_ref[...], h)

    for half in range(SW_W // LANES):
        sl = slice(half * LANES, (half + 1) * LANES)
        sq_ref[half] = rope_sw(proj(P_SWQ + half * LANES, P_SWQ + (half + 1) * LANES)) * (SW_HD ** -0.5 * LOG2_E)
        sk_ref[half] = rope_sw(proj(P_SWK + half * LANES, P_SWK + (half + 1) * LANES))
        sv_ref[half] = proj(P_SWV + half * LANES, P_SWV + (half + 1) * LANES)


def _inproj(x, g, w, wgt, qg, wuq, kvg, wk, wv, cm, sm, cs, ss, one, seq):
    t_all = x.shape[0]
    tm = TOK_TILE
    pos_tiles = seq // tm
    tok = lambda w_: pl.BlockSpec((tm, w_), lambda i: (i, 0))
    full = lambda a: pl.BlockSpec(a.shape, lambda i: (0,) * a.ndim)
    pos = pl.BlockSpec((tm, LANES), lambda i: (i % pos_tiles, 0))
    slabs = pl.BlockSpec((SW_W // LANES, tm, LANES), lambda i: (0, i, 0))
    out_shapes = (
        jax.ShapeDtypeStruct((t_all, MLA_PAD_W), BF16),
        jax.ShapeDtypeStruct((t_all, MLA_PAD_W), BF16),
        jax.ShapeDtypeStruct((t_all // tm, MLA_PAD_W, tm), BF16),
        jax.ShapeDtypeStruct((t_all, 2 * ML_HEADS * ML_DK), F32),
        jax.ShapeDtypeStruct((t_all, ML_PAD_W), BF16),
        jax.ShapeDtypeStruct((t_all, ML_W), F32),
        jax.ShapeDtypeStruct((t_all // seq, 2 * ML_HEADS, seq), F32),
        jax.ShapeDtypeStruct((SW_W // LANES, t_all, LANES), F32),
        jax.ShapeDtypeStruct((SW_W // LANES, t_all, LANES), F32),
        jax.ShapeDtypeStruct((SW_W // LANES, t_all, LANES), F32),
    )
    out_specs = (tok(MLA_PAD_W), tok(MLA_PAD_W), pl.BlockSpec((None, MLA_PAD_W, tm), lambda i: (i, 0, 0)),
                 tok(2 * ML_HEADS * ML_DK), tok(ML_PAD_W),
                 tok(ML_W), pl.BlockSpec((None, 2 * ML_HEADS, tm), lambda i: (i // pos_tiles, 0, i % pos_tiles)),
                 slabs, slabs, slabs)
    return pl.pallas_call(
        _inproj_kernel,
        grid=(t_all // tm,),
        in_specs=[tok(D_MODEL), full(g), full(w), full(wgt), full(qg), full(wuq), full(kvg), full(wk), full(wv),
                  pos, pos, pos, pos, full(one)],
        out_specs=out_specs,
        out_shape=out_shapes,
        compiler_params=_params("parallel"),
        name="inproj",
    )(x, g, w, wgt, qg, wuq, kvg, wk, wv, cm, sm, cs, ss, one)


def _flash_kernel(q_ref, k_ref, vt_ref, o_ref, m_scr, acc_scr, s_scr, cmax_scr):
    t = q_ref.shape[0]
    i = pl.program_id(2)
    n_pairs = lax.shift_right_logical(i + 2, 1)
    qt = q_ref[...].astype(F32).T.astype(BF16)
    m_scr[...] = jnp.full(m_scr.shape, NEG, F32)
    acc_scr[...] = jnp.zeros(acc_scr.shape, F32)

    def scores(j, buf, masked):
        start = pl.multiple_of(j * t, t)
        st = _dot(k_ref[pl.ds(start, t), :], qt)
        if masked:
            key = lax.broadcasted_iota(jnp.int32, (t, t), 0) + j * t
            qry = lax.broadcasted_iota(jnp.int32, (t, t), 1) + i * t
            st = jnp.where(key <= qry, st, NEG)
        s_scr[buf] = st
        cmax_scr[buf] = jnp.max(st, axis=0, keepdims=True)

    def accumulate(j, buf):
        m_old = m_scr[...]
        m_new = jnp.maximum(m_old, cmax_scr[buf])
        p = jnp.exp2(s_scr[buf] - m_new).astype(BF16)
        acc_scr[...] = acc_scr[...] * jnp.exp2(m_old - m_new) + _dot(vt_ref[j], p)
        m_scr[...] = m_new

    def pairs(pp, count, mask_last):
        for c in range(count):
            scores(2 * (pp + c) + 1, 1, False)
            accumulate(2 * (pp + c), 0)
            scores(2 * (pp + c) + 2, 0, mask_last and c == count - 1)
            accumulate(2 * (pp + c) + 1, 1)

    scores(0, 0, True)
    n_plain = jnp.maximum(n_pairs - 2, 0)
    n_quads = lax.shift_right_logical(n_plain, 1)

    def body(g, carry):
        pairs(2 * g, 2, False)
        return carry

    lax.fori_loop(0, n_quads, body, 0)

    @pl.when((n_plain & 1) == 1)
    def _():
        pairs(2 * n_quads, 1, False)

    @pl.when(n_pairs >= 2)
    def _():
        pairs(n_pairs - 2, 1, True)

    last = 2 * n_pairs - 1
    scores(last, 1, True)
    accumulate(last - 1, 0)
    accumulate(last, 1)
    acc = acc_scr[...]
    o_ref[...] = (acc / acc[MLA_V:MLA_V + 1, :]).T.astype(BF16)


def _flash(q, k, vt):
    b, s, _ = q.shape
    t = FLASH_TILE
    nt = s // t
    assert nt % 2 == 0
    return pl.pallas_call(
        _flash_kernel,
        grid=(b, MLA_HEADS, nt),
        in_specs=[pl.BlockSpec((None, t, LANES), lambda bb, hh, i: (bb, i, hh)),
                  pl.BlockSpec((None, s, LANES), lambda bb, hh, i: (bb, 0, hh)),
                  pl.BlockSpec((nt, LANES, t), lambda bb, hh, i: (bb, hh, 0))],
        out_specs=pl.BlockSpec((None, t, LANES), lambda bb, hh, i: (bb, i, hh)),
        out_shape=jax.ShapeDtypeStruct((b, s, MLA_PAD_W), BF16),
        scratch_shapes=[pltpu.VMEM((1, t), F32), pltpu.VMEM((LANES, t), F32),
                        pltpu.VMEM((2, t, t), F32), pltpu.VMEM((2, 1, t), F32)],
        compiler_params=_params("parallel", "parallel", "arbitrary"),
        name="mla_flash",
    )(q, k, vt)


def _log_sigmoid(z):
    return jnp.minimum(z, 0.0) - jnp.log1p(jnp.exp(-jnp.abs(z)))


def _mlstm_kernel(qkc_ref, qkp_ref, v_ref, o_ref, g_ref, cw_ref, cb_ref, bi_ref, bf_ref,
                  y_ref, c_scr, m_scr):
    L = ML_CHUNK
    nb = qkc_ref.shape[0]
    c = pl.program_id(0)

    @pl.when(c == 0)
    def _():
        c_scr[...] = jnp.zeros(c_scr.shape, F32)
        m_scr[...] = jnp.full(m_scr.shape, NEG, F32)

    row = lax.broadcasted_iota(jnp.int32, (L, L), 0)
    col = lax.broadcasted_iota(jnp.int32, (L, L), 1)
    tri = col <= row
    eye = col == row

    def to_col(r):
        return jnp.sum(jnp.where(eye, r, 0.0), axis=1, keepdims=True)

    streams = [(b, hh) for b in range(nb) for hh in range(ML_HEADS)]
    qk, b_all, li_all = [], [], []
    for b in range(nb):
        cur = qkc_ref[b]
        prev = jnp.where(c > 0, qkp_ref[b], 0.0)
        row_w = lax.broadcasted_iota(jnp.int32, cur.shape, 0)
        conv = cb_ref[...] + cw_ref[ML_CONV - 1:ML_CONV, :] * cur
        for sh in range(1, ML_CONV):
            shifted = jnp.where(row_w >= sh, pltpu.roll(cur, sh, 0), pltpu.roll(prev, sh, 0))
            conv = conv + cw_ref[ML_CONV - 1 - sh:ML_CONV - sh, :] * shifted
        qk.append(conv * jax.nn.sigmoid(conv))
        gates = g_ref[b]
        li_all.append(gates[0:ML_HEADS, :] + bi_ref[...])
        lf = _log_sigmoid(gates[ML_HEADS:, :] + bf_ref[...])
        lane_g = lax.broadcasted_iota(jnp.int32, lf.shape, 1)
        step = 1
        while step < L:
            lf = lf + jnp.where(lane_g >= step, pltpu.roll(lf, step, 1), 0.0)
            step *= 2
        b_all.append(lf)

    st = []
    for n, (b, hh) in enumerate(streams):
        b_row = b_all[b][hh:hh + 1, :]
        li_row = li_all[b][hh:hh + 1, :]
        m_prev = m_scr[n:n + 1, 0:1]
        b_col = to_col(b_row)
        d = jnp.where(tri, b_col - b_row + li_row, NEG)
        inter = b_col + m_prev
        m_t = jnp.maximum(inter, jnp.max(d, axis=1, keepdims=True))
        b_last = b_row[:, L - 1:L]
        g = b_last - b_row + li_row
        m_new = jnp.maximum(b_last + m_prev, jnp.max(g, axis=1, keepdims=True))
        kf = qk[b][:, (ML_HEADS + hh) * ML_DK:(ML_HEADS + hh + 1) * ML_DK] * ML_DK ** -0.5
        st.append(dict(
            qh=qk[b][:, hh * ML_DK:(hh + 1) * ML_DK].astype(BF16),
            kh=kf.astype(BF16),
            kw=(kf * to_col(jnp.exp(g - m_new))).astype(BF16),
            vh=v_ref[b, :, hh * LANES:(hh + 1) * LANES],
            c_ext=c_scr[n],
            w_intra=jnp.exp(d - m_t), w_inter=jnp.exp(inter - m_t), floor=jnp.exp(-m_t),
            decay=jnp.exp(b_last + m_prev - m_new), m_new=m_new))

    for s_ in st:
        s_["qk"] = _dot_nt(s_["qh"], s_["kh"])
        s_["cross"] = _dot(s_["qh"], s_["c_ext"].astype(BF16))
    for s_ in st:
        s_["s"] = s_["qk"] * s_["w_intra"]
    for s_ in st:
        s_["intra"] = _dot(s_["s"].astype(BF16), s_["vh"])
        s_["kv"] = _dot_tn(s_["kw"], s_["vh"])
    for n, ((b, hh), s_) in enumerate(zip(streams, st)):
        num = s_["intra"][:, :ML_DV] + s_["w_inter"] * s_["cross"][:, :ML_DV]
        nq = jnp.sum(s_["s"], axis=1, keepdims=True) + s_["w_inter"] * s_["cross"][:, ML_DV:ML_DV + 1]
        hout = num / jnp.maximum(jnp.abs(nq), s_["floor"])
        og = jax.nn.sigmoid(o_ref[b, :, hh * ML_DV:(hh + 1) * ML_DV])
        y_ref[b, :, hh * ML_DV:(hh + 1) * ML_DV] = (og * hout).astype(BF16)
        c_scr[n] = s_["decay"] * s_["c_ext"] + s_["kv"]
        m_scr[n:n + 1, :] = jnp.broadcast_to(s_["m_new"], (1, LANES))


def _mlstm(qk_pre, v, o_pre, gates_t, conv_w, conv_b, b_i, b_f):
    batch, seq, wqk = qk_pre.shape
    L = ML_CHUNK
    full = lambda a: pl.BlockSpec(a.shape, lambda c: (0,) * a.ndim)
    chunk = lambda w_: pl.BlockSpec((batch, L, w_), lambda c: (0, c, 0))
    return pl.pallas_call(
        _mlstm_kernel,
        grid=(seq // L,),
        in_specs=[chunk(wqk),
                  pl.BlockSpec((batch, L, wqk), lambda c: (0, jnp.maximum(c - 1, 0), 0)),
                  chunk(ML_PAD_W), chunk(ML_W),
                  pl.BlockSpec((batch, 2 * ML_HEADS, L), lambda c: (0, 0, c)),
                  full(conv_w), full(conv_b), full(b_i), full(b_f)],
        out_specs=chunk(ML_W),
        out_shape=jax.ShapeDtypeStruct((batch, seq, ML_W), BF16),
        scratch_shapes=[pltpu.VMEM((batch * ML_HEADS, ML_DK, LANES), F32),
                        pltpu.VMEM((batch * ML_HEADS, LANES), F32)],
        compiler_params=_params("arbitrary"),
        name="mlstm",
    )(qk_pre, qk_pre, v, o_pre, gates_t, conv_w, conv_b, b_i, b_f)


def _swa_kernel(q_ref, k_ref, v_ref, y_ref, kk_scr, vv_scr, o_scr, l_scr):
    blk, tile = SW_BLOCK, SWA_TILE
    n_slab = SW_W // LANES
    n = pl.program_id(1)

    @pl.when(n == 0)
    def _():
        kk_scr[:, 0:tile, :] = jnp.zeros((n_slab, tile, LANES), F32)
        vv_scr[:, 0:tile, :] = jnp.zeros((n_slab, tile, LANES), F32)

    @pl.when(n > 0)
    def _():
        kk_scr[:, 0:tile, :] = kk_scr[:, tile:2 * tile, :]
        vv_scr[:, 0:tile, :] = vv_scr[:, tile:2 * tile, :]

    kk_scr[:, tile:2 * tile, :] = k_ref[...]
    vv_scr[:, tile:2 * tile, :] = v_ref[...]

    key = lax.broadcasted_iota(jnp.int32, (2 * blk, blk), 0)
    qry = lax.broadcasted_iota(jnp.int32, (2 * blk, blk), 1)
    dim =lax.broadcasted_iota(jnp.int32, (LANES, blk), 0)
    head_rows = [dim < SW_HD, dim >= SW_HD]

    for g, dil in enumerate(SW_DILATIONS):
        per_res = tile // (blk * dil)
        shift = per_res.bit_length() - 1

        def block_pair(it, carry, g=g, dil=dil, per_res=per_res, shift=shift):
            def rows(start):
                return pl.ds(start, blk, stride=dil) if dil > 1 else pl.ds(pl.multiple_of(start, blk), blk)

            work = []
            for u in range(2):
                bi = 2 * it + u
                r = lax.shift_right_logical(bi, shift)
                j = bi & (per_res - 1)
                q0 = r + j * (blk * dil)
                lowest = jnp.where((n == 0) & (j == 0), blk, 0)
                ok = (key >= jnp.maximum(qry, lowest)) & (key <= qry + blk)
                for sl in range(n_slab):
                    prev, cur = rows(tile + q0 - blk * dil), rows(tile + q0)
                    kb = jnp.concatenate([kk_scr[sl, prev, :], kk_scr[sl, cur, :]], axis=0)
                    vb = jnp.concatenate([vv_scr[sl, prev, :], vv_scr[sl, cur, :]], axis=0)
                    work.append(dict(q0=q0, sl=sl, ok=ok, qt=q_ref[sl, rows(q0), :].T,
                                     kb=kb.astype(BF16), vt=vb.T.astype(BF16)))
            for w in work:
                w["s"] = [_dot(w["kb"], jnp.where(head_rows[hh], w["qt"], 0.0).astype(BF16)) for hh in range(2)]
            for w in work:
                w["e"], w["den"], w["lse"] = [], [], []
                for hh in range(2):
                    s = jnp.where(w["ok"], w["s"][hh], NEG)
                    cmax = jnp.max(s, axis=0, keepdims=True)
                    e = jnp.exp2(s - cmax)
                    den = jnp.sum(e, axis=0, keepdims=True)
                    w["e"].append(e.astype(BF16))
                    w["den"].append(den)
                    w["lse"].append(cmax + jnp.log2(den))
            for w in work:
                ot = [_dot(w["vt"][hh * SW_HD:(hh + 1) * SW_HD, :], w["e"][hh]) / w["den"][hh] for hh in range(2)]
                lt = [jnp.broadcast_to(w["lse"][hh], (SW_HD, blk)) for hh in range(2)]
                o_scr[g, w["sl"], rows(w["q0"]), :] = jnp.concatenate(ot, axis=0).T
                l_scr[g, w["sl"], rows(w["q0"]), :] = jnp.concatenate(lt, axis=0).T
            return carry

        lax.fori_loop(0, tile // blk // 2, block_pair, 0)

    for sl in range(n_slab):
        ls = [l_scr[g, sl] for g in range(len(SW_DILATIONS))]
        mx = functools.reduce(jnp.maximum, ls)
        ws = [jnp.exp2(l - mx) for l in ls]
        num = functools.reduce(lambda a, b: a + b, [w * o_scr[g, sl] for g, w in enumerate(ws)])
        y_ref[:, sl * LANES:(sl + 1) * LANES] = (num / functools.reduce(lambda a, b: a + b, ws)).astype(BF16)


def _swa(q, k, v, batch, seq):
    tile = SWA_TILE
    n_slab = SW_W // LANES
    per_batch = seq // tile
    spec = pl.BlockSpec((n_slab, tile, LANES), lambda bb, n: (0, bb * per_batch + n, 0))
    n_br = len(SW_DILATIONS)
    return pl.pallas_call(
        _swa_kernel,
        grid=(batch, per_batch),
        in_specs=[spec, spec, spec],
        out_specs=pl.BlockSpec((tile, SW_W), lambda bb, n: (bb * per_batch + n, 0)),
        out_shape=jax.ShapeDtypeStruct((batch * seq, SW_W), BF16),
        scratch_shapes=[pltpu.VMEM((n_slab, 2 * tile, LANES), F32), pltpu.VMEM((n_slab, 2 * tile, LANES), F32),
                        pltpu.VMEM((n_br, n_slab, tile, LANES), F32), pltpu.VMEM((n_br, n_slab, tile, LANES), F32)],
        compiler_params=_params("parallel", "arbitrary"),
        name="swa",
    )(q, k, v)


def _outproj_kernel(x_ref, ya_ref, yb_ref, yc_ref, wa_ref, wb_ref, wc_ref, g_ref, out_ref):
    y = _dot(ya_ref[...], wa_ref[...]) + _dot(yb_ref[...], wb_ref[...]) + _dot(yc_ref[...], wc_ref[...])
    out_ref[...] = x_ref[...] + _rms(y, g_ref[...])


def _outproj(x, ya, yb, yc, wa, wb, wc, g):
    t_all = x.shape[0]
    tm = TOK_TILE
    tok = lambda w_: pl.BlockSpec((tm, w_), lambda i: (i, 0))
    full = lambda a: pl.BlockSpec(a.shape, lambda i: (0,) * a.ndim)
    return pl.pallas_call(
        _outproj_kernel,
        grid=(t_all // tm,),
        in_specs=[tok(D_MODEL), tok(MLA_PAD_W), tok(ML_W), tok(SW_W), full(wa), full(wb), full(wc), full(g)],
        out_specs=tok(D_MODEL),
        out_shape=jax.ShapeDtypeStruct((t_all, D_MODEL), F32),
        compiler_params=_params("parallel"),
        name="outproj",
    )(x, ya, yb, yc, wa, wb, wc, g)


def _memkv_kernel(mem_ref, g_ref, w_ref, kv_ref):
    kv_ref[...] = _dot(_rms(mem_ref[...], g_ref[...]).astype(BF16), w_ref[...]).astype(BF16)


def _memkv(mem, g, w):
    n = mem.shape[0]
    full = lambda a: pl.BlockSpec(a.shape, lambda i: (0,) * a.ndim)
    return pl.pallas_call(
        _memkv_kernel,
        grid=(1,),
        in_specs=[full(mem), full(g), full(w)],
        out_specs=pl.BlockSpec((n, 2 * D_MODEL), lambda i: (0, 0)),
        out_shape=jax.ShapeDtypeStruct((n, 2 * D_MODEL), BF16),
        compiler_params=_params("arbitrary"),
        name="memkv",
    )(mem, g, w)


def _xattn_kernel(x_ref, gpre_ref, wq_ref, kv_ref, wo_ref, gpost_ref, out_ref, o_scr):
    x = x_ref[...]
    h = _rms(x, gpre_ref[...]).astype(BF16)
    q = (_dot(h, wq_ref[...]) * X_HD ** -0.5).astype(BF16)
    for hh in range(X_HEADS):
        hs = slice(hh * X_HD, (hh + 1) * X_HD)
        s = _dot_nt(q[:, hs], kv_ref[:, hs])
        e = jnp.exp(s - jnp.max(s, axis=1, keepdims=True))
        den = jnp.sum(e, axis=1, keepdims=True)
        o_scr[:, hs] = (_dot(e.astype(BF16), kv_ref[:, D_MODEL + hh * X_HD:D_MODEL + (hh + 1) * X_HD]) / den).astype(BF16)
    y = _dot(o_scr[...], wo_ref[...])
    out_ref[...] = x + _rms(y, gpost_ref[...])


def _xattn(x, gpre, wq, kv, wo, gpost, seq):
    t_all = x.shape[0]
    tm = TOK_TILE
    per_batch = seq // tm
    tok = pl.BlockSpec((tm, D_MODEL), lambda i: (i, 0))
    full = lambda a: pl.BlockSpec(a.shape, lambda i: (0,) * a.ndim)
    return pl.pallas_call(
        _xattn_kernel,
        grid=(t_all // tm,),
        in_specs=[tok, full(gpre), full(wq), pl.BlockSpec((N_MEM, 2 * D_MODEL), lambda i: (i // per_batch, 0)),
                  full(wo), full(gpost)],
        out_specs=tok,
        out_shape=jax.ShapeDtypeStruct((t_all, D_MODEL), F32),
        scratch_shapes=[pltpu.VMEM((tm, D_MODEL), BF16)],
        compiler_params=_params("parallel"),
        name="xattn",
    )(x, gpre, wq, kv, wo, gpost)


def _ffn_kernel(x_ref, gpre_ref, wg_ref, wu_ref, wd_ref, gpost_ref, out_ref, h_scr, acc_scr):
    f = pl.program_id(1)

    @pl.when(f == 0)
    def _():
        h_scr[...] = _rms(x_ref[...], gpre_ref[...]).astype(BF16)
        acc_scr[...] = jnp.zeros(acc_scr.shape, F32)

    h = h_scr[...]
    gate = _dot(h, wg_ref[...])
    up = _dot(h, wu_ref[...])
    acc_scr[...] += _dot((gate * jax.nn.sigmoid(gate) * up).astype(BF16), wd_ref[...])

    @pl.when(f == pl.num_programs(1) - 1)
    def _():
        out_ref[...] = x_ref[...] + _rms(acc_scr[...], gpost_ref[...])


def _ffn(x, gpre, w_gate_up, w_down, gpost):
    t_all = x.shape[0]
    tm = TOK_TILE
    nf = 2
    tf = D_FF // nf
    tok = pl.BlockSpec((tm, D_MODEL), lambda i, f: (i, 0))
    full = lambda a: pl.BlockSpec(a.shape, lambda i, f: (0,) * a.ndim)
    return pl.pallas_call(
        _ffn_kernel,
        grid=(t_all // tm, nf),
        in_specs=[tok, full(gpre),
                  pl.BlockSpec((D_MODEL, tf), lambda i, f: (0, f)),
                  pl.BlockSpec((D_MODEL, tf), lambda i, f: (0, f + nf)),
                  pl.BlockSpec((tf, D_MODEL), lambda i, f: (f, 0)),
                  full(gpost)],
        out_specs=tok,
        out_shape=jax.ShapeDtypeStruct((t_all, D_MODEL), F32),
        scratch_shapes=[pltpu.VMEM((tm, D_MODEL), BF16), pltpu.VMEM((tm, D_MODEL), F32)],
        compiler_params=_params("parallel", "arbitrary"),
        name="ffn",
    )(x, gpre, w_gate_up, w_gate_up, w_down, gpost)


def _rope_tables(seq):
    pos = jnp.arange(seq, dtype=F32)[:, None]

    def cos_sin(d_rot):
        inv = ROPE_THETA ** (-jnp.arange(0, d_rot, 2, dtype=F32) / d_rot)
        ang = pos * inv[None, :]
        return jnp.cos(ang), jnp.sin(ang)

    c, s = cos_sin(MLA_ROPE)
    ones = jnp.ones((seq, MLA_NOPE), F32)
    tail = jnp.ones((seq, LANES - MLA_NOPE - MLA_ROPE), F32)
    cm = jnp.concatenate([ones, c, c, tail], axis=1)
    sm = jnp.concatenate([0 * ones, -s, s, 0 * tail], axis=1)
    c, s = cos_sin(SW_ROT)
    rest = jnp.ones((seq, SW_HD - SW_ROT), F32)
    cs = jnp.tile(jnp.concatenate([c, c, rest], axis=1), (1, LANES // SW_HD))
    ss = jnp.tile(jnp.concatenate([-s, s, 0 * rest], axis=1), (1, LANES // SW_HD))
    return cm, sm, cs, ss


def _pad_heads(w, heads, width, offset=0):
    k = w.shape[0]
    w = w.reshape(k, heads, width)
    w = jnp.pad(w, ((0, 0), (0, 0), (offset, LANES - width - offset)))
    return w.reshape(k, heads * LANES)


def _prep_layer(w_in, w_uq, w_ukv, w_out):
    o = IN_OFFS
    col = lambda i: w_in[:, o[i]:o[i + 1]]
    z = lambda n: jnp.zeros((D_MODEL, n), F32)
    w_p = jnp.concatenate([
        col(0), col(1),
        z(MLA_NOPE), col(2), z(LANES - MLA_NOPE - MLA_ROPE),
        col(3), col(4), _pad_heads(col(5), ML_HEADS, ML_DV), col(6), col(9), col(10), col(11)], axis=1)
    assert w_p.shape[1] == P_TOTAL
    w_gates_t = jnp.concatenate([col(7), col(8)], axis=1).T
    uq = w_uq.reshape(MLA_Q_LORA, MLA_HEADS, MLA_NOPE + MLA_ROPE)
    uq = jnp.pad(uq, ((0, 0), (0, 0), (0, LANES - MLA_NOPE - MLA_ROPE))).reshape(MLA_Q_LORA, MLA_PAD_W)
    ukv = w_ukv.reshape(MLA_KV_LORA, MLA_HEADS, MLA_NOPE + MLA_V)
    pad_half = lambda a: jnp.pad(a, ((0, 0), (0, 0), (0, LANES - a.shape[2]))).reshape(MLA_KV_LORA, MLA_PAD_W)
    wk, wv = pad_half(ukv[:, :, :MLA_NOPE]), pad_half(ukv[:, :, MLA_NOPE:]).T
    wa = w_out[:MLA_W].reshape(MLA_HEADS, MLA_V, D_MODEL)
    wa = jnp.pad(wa, ((0, 0), (0, LANES - MLA_V), (0, 0))).reshape(MLA_PAD_W, D_MODEL)
    wb, wc = w_out[MLA_W:MLA_W + ML_W], w_out[MLA_W + ML_W:]
    bf = lambda a: a.astype(BF16)
    return bf(w_p), bf(w_gates_t), bf(uq), bf(wk), bf(wv), bf(wa), bf(wb), bf(wc)


def kernel(x, mem, w_in, mla_q_norm, mla_w_uq, mla_kv_norm, mla_w_ukv, ml_conv_w, ml_conv_b, ml_b_i, ml_b_f,
           w_out, x_w_q, x_w_kv, x_w_o, w_gate_up, w_down, norm_mix_pre, norm_mix_post, norm_mem,
           norm_x_pre, norm_x_post, norm_ffn_pre, norm_ffn_post):
    batch, seq, _ = x.shape
    depth = w_in.shape[0]
    t_all = batch * seq
    assert seq % SWA_TILE == 0 and seq % TOK_TILE == 0 and TOK_TILE == FLASH_TILE
    assert SWA_TILE == SW_DILATIONS[-1] * SW_BLOCK
    cm, sm, cs, ss = _rope_tables(seq)
    one = jnp.zeros((1, LANES), F32).at[0, MLA_V].set(1.0)
    row = lambda a: a.reshape(1, -1)
    colv = lambda a: a.reshape(-1, 1)
    xt = x.reshape(t_all, D_MODEL)
    mem2 = mem.reshape(batch * N_MEM, D_MODEL)
    for l in range(depth):
        w_p, w_gt, uq, wk, wv, wa, wb, wc = _prep_layer(w_in[l], mla_w_uq[l], mla_w_ukv[l], w_out[l])
        q, k, v, mqk, mv, mo, gates_t, sq, sk, sv = _inproj(
            xt, row(norm_mix_pre[l]), w_p, w_gt, row(mla_q_norm[l]), uq, row(mla_kv_norm[l]), wk, wv,
            cm, sm, cs, ss, one, seq)
        b3 = lambda a: a.reshape(batch, seq, a.shape[-1])
        ya = _flash(b3(q), b3(k), v).reshape(t_all, MLA_PAD_W)
        yb = _mlstm(b3(mqk), b3(mv), b3(mo), gates_t, ml_conv_w[l], row(ml_conv_b[l]), colv(ml_b_i[l]),
                    colv(ml_b_f[l])).reshape(t_all, ML_W)
        yc = _swa(sq, sk, sv, batch, seq)
        xt = _outproj(xt, ya, yb, yc, wa, wb, wc, row(norm_mix_post[l]))
        kv = _memkv(mem2, row(norm_mem[l]), x_w_kv[l].astype(BF16))
        xt = _xattn(xt, row(norm_x_pre[l]), x_w_q[l].astype(BF16), kv, x_w_o[l].astype(BF16),
                    row(norm_x_post[l]), seq)
        xt = _ffn(xt, row(norm_ffn_pre[l]), w_gate_up[l].astype(BF16), w_down[l].astype(BF16),
                  row(norm_ffn_post[l]))
    return xt.reshape(batch, seq, D_MODEL)
```

```python
import functools

import jax
import jax.numpy as jnp
import numpy as np
from jax import lax
from jax.experimental import pallas as pl
from jax.experimental.pallas import tpu as pltpu

F32 = jnp.float32
BF16 = jnp.bfloat16

D_MODEL = 1024
N_MEM = 256
ROPE_THETA = 500000.0
EPS = 1e-6
NEG = -1e30
LOG2_E = 1.4426950408889634
LANES = 128

MLA_HEADS, MLA_NOPE, MLA_ROPE, MLA_V = 8, 64, 32, 64
MLA_Q_LORA, MLA_KV_LORA = 256, 128
ML_HEADS, ML_DK, ML_DV, ML_CONV, ML_CHUNK = 4, 64, 64, 4, 128
SW_HEADS, SW_HD = 4, 64
SW_ROT = SW_HD // 4
SW_DILATIONS = (1, 4, 16)
SW_BLOCK = 128
X_HEADS = 4
X_HD = D_MODEL // X_HEADS
D_FF = 2816

MLA_W = MLA_HEADS * MLA_V
ML_W = ML_HEADS * ML_DV
SW_W = SW_HEADS * SW_HD
MLA_PAD_W = MLA_HEADS * LANES
ML_PAD_W = ML_HEADS * LANES

IN_SIZES = (MLA_Q_LORA, MLA_KV_LORA, MLA_ROPE, ML_HEADS * ML_DK, ML_HEADS * ML_DK, ML_W, ML_W,
            ML_HEADS, ML_HEADS, SW_W, SW_W, SW_W)
IN_OFFS = tuple(int(v) for v in np.cumsum((0,) + IN_SIZES))

P_CQ = 0
P_CKV = 256
P_KROPE = 384
P_MLQK = 512
P_MLV = 1024
P_MLO = 1536
P_SWQ = 1792
P_SWK = 2048
P_SWV = 2304
P_TOTAL = 2560

TOK_TILE = 512
FLASH_TILE = 1024
FLASH_KEY_TILE = 256
MLA_VT_ROWS = 80
SWA_TILE = 2048
VMEM_LIMIT = 56 * 1024 * 1024


def _rms(x, g):
    return x * lax.rsqrt(jnp.mean(x * x, axis=-1, keepdims=True) + EPS) * g


def _dot(a, b):
    return jnp.dot(a, b, preferred_element_type=F32)


def _dot_nt(a, b):
    return lax.dot_general(a, b, (((1,), (1,)), ((), ())), preferred_element_type=F32)


def _dot_tn(a, b):
    return lax.dot_general(a, b, (((0,), (0,)), ((), ())), preferred_element_type=F32)


def _params(*sem):
    return pltpu.CompilerParams(dimension_semantics=sem, vmem_limit_bytes=VMEM_LIMIT)


def _inproj_kernel(x_ref, g_ref, w_ref, wgt_ref, qg_ref, wuq_ref, kvg_ref, wk_ref, wvt_ref,
                   cm_ref, sm_ref, cs_ref, ss_ref, one_ref,
                   q_ref, k_ref, vt_ref, mqk_ref, mv_ref, mo_ref, gr_ref, sq_ref, sk_ref, sv_ref):
    tm = x_ref.shape[0]
    h = _rms(x_ref[...], g_ref[...]).astype(BF16)

    def proj(a, b):
        return _dot(h, w_ref[:, a:b])

    lane = lax.broadcasted_iota(jnp.int32, (tm, LANES), 1)
    cm, sm = cm_ref[...], sm_ref[...]
    mla_first = lane < MLA_NOPE + MLA_ROPE // 2

    def rope_mla(t):
        rot = jnp.where(mla_first, pltpu.roll(t, LANES - MLA_ROPE // 2, 1), pltpu.roll(t, MLA_ROPE // 2, 1))
        return t * cm + rot * sm

    cs, ss = cs_ref[...], ss_ref[...]
    sw_first = (lane % SW_HD) < SW_ROT // 2

    def rope_sw(t):
        rot = jnp.where(sw_first, pltpu.roll(t, LANES - SW_ROT // 2, 1), pltpu.roll(t, SW_ROT // 2, 1))
        return t * cs + rot * ss

    cqn = _rms(proj(P_CQ, P_CQ + MLA_Q_LORA), qg_ref[...]).astype(BF16)
    q_scale = (MLA_NOPE + MLA_ROPE) ** -0.5 * LOG2_E
    for hh in range(MLA_HEADS):
        sl = slice(hh * LANES, (hh + 1) * LANES)
        q_ref[:, sl] = (rope_mla(_dot(cqn, wuq_ref[:, sl])) * q_scale).astype(BF16)
    kvn = _rms(proj(P_CKV, P_CKV + MLA_KV_LORA), kvg_ref[...]).astype(BF16)
    kpe = rope_mla(proj(P_KROPE, P_KROPE + LANES))
    one = one_ref[...]
    one_col = (lax.broadcasted_iota(jnp.int32, (MLA_VT_ROWS, 1), 0) == MLA_V).astype(F32)
    for hh in range(MLA_HEADS):
        sl = slice(hh * LANES, (hh + 1) * LANES)
        k_ref[:, sl] = (_dot(kvn, wk_ref[:, sl]) + kpe).astype(BF16)
        rows = slice(hh * MLA_VT_ROWS, (hh + 1) * MLA_VT_ROWS)
        for u in range(tm // FLASH_KEY_TILE):
            toks = slice(u * FLASH_KEY_TILE, (u + 1) * FLASH_KEY_TILE)
            vt_ref[u, rows, :] = (_dot_nt(wvt_ref[rows, :], kvn[toks, :]) + one_col).astype(BF16)

    mqk_ref[...] = proj(P_MLQK, P_MLQK + 2 * ML_HEADS * ML_DK)
    for hh in range(ML_HEADS):
        sl = slice(hh * LANES, (hh + 1) * LANES)
        mv_ref[:, sl] = (proj(P_MLV + hh * LANES, P_MLV + (hh + 1) * LANES) + one).astype(BF16)
    mo_ref[...] = proj(P_MLO, P_MLO + ML_W)
    gr_ref[...] = _dot_nt(wgt_ref[...], h)

    for half in range(SW_W // LANES):
        sl = slice(half * LANES, (half + 1) * LANES)
        sq_ref[half] = rope_sw(proj(P_SWQ + half * LANES, P_SWQ + (half + 1) * LANES)) * (SW_HD ** -0.5 * LOG2_E)
        sk_ref[half] = rope_sw(proj(P_SWK + half * LANES, P_SWK + (half + 1) * LANES))
        sv_ref[half] = proj(P_SWV + half * LANES, P_SWV + (half + 1) * LANES)


def _inproj(x, g, w, wgt, qg, wuq, kvg, wk, wv, cm, sm, cs, ss, one, seq):
    t_all = x.shape[0]
    tm = TOK_TILE
    pos_tiles = seq // tm
    tok = lambda w_: pl.BlockSpec((tm, w_), lambda i: (i, 0))
    full = lambda a: pl.BlockSpec(a.shape, lambda i: (0,) * a.ndim)
    pos = pl.BlockSpec((tm, LANES), lambda i: (i % pos_tiles, 0))
    slabs = pl.BlockSpec((SW_W // LANES, tm, LANES), lambda i: (0, i, 0))
    out_shapes = (
        jax.ShapeDtypeStruct((t_all, MLA_PAD_W), BF16),
        jax.ShapeDtypeStruct((t_all, MLA_PAD_W), BF16),
        jax.ShapeDtypeStruct((t_all // FLASH_KEY_TILE, MLA_HEADS * MLA_VT_ROWS, FLASH_KEY_TILE), BF16),
        jax.ShapeDtypeStruct((t_all, 2 * ML_HEADS * ML_DK), F32),
        jax.ShapeDtypeStruct((t_all, ML_PAD_W), BF16),
        jax.ShapeDtypeStruct((t_all, ML_W), F32),
        jax.ShapeDtypeStruct((t_all // seq, 2 * ML_HEADS, seq), F32),
        jax.ShapeDtypeStruct((SW_W // LANES, t_all, LANES), F32),
        jax.ShapeDtypeStruct((SW_W // LANES, t_all, LANES), F32),
        jax.ShapeDtypeStruct((SW_W // LANES, t_all, LANES), F32),
    )
    out_specs = (tok(MLA_PAD_W), tok(MLA_PAD_W),
                 pl.BlockSpec((tm // FLASH_KEY_TILE, MLA_HEADS * MLA_VT_ROWS, FLASH_KEY_TILE), lambda i: (i, 0, 0)),
                 tok(2 * ML_HEADS * ML_DK), tok(ML_PAD_W),
                 tok(ML_W), pl.BlockSpec((None, 2 * ML_HEADS, tm), lambda i: (i // pos_tiles, 0, i % pos_tiles)),
                 slabs, slabs, slabs)
    return pl.pallas_call(
        _inproj_kernel,
        grid=(t_all // tm,),
        in_specs=[tok(D_MODEL), full(g), full(w), full(wgt), full(qg), full(wuq), full(kvg), full(wk), full(wv),
                  pos, pos, pos, pos, full(one)],
        out_specs=out_specs,
        out_shape=out_shapes,
        compiler_params=_params("parallel"),
        name="inproj",
    )(x, g, w, wgt, qg, wuq, kvg, wk, wv, cm, sm, cs, ss, one)


def _flash_kernel(q_ref, k_ref, vt_ref, o_ref, m_scr, acc_scr, s_scr, cmax_scr):
    tq = q_ref.shape[0]
    tk = FLASH_KEY_TILE
    per_q = tq // tk
    i = pl.program_id(2)
    n_tiles = (i + 1) * per_q
    qt = q_ref[...].astype(F32).T.astype(BF16)
    m_scr[...] = jnp.full(m_scr.shape, NEG, F32)
    acc_scr[...] = jnp.zeros(acc_scr.shape, F32)

    def scores(j, buf, masked):
        start = pl.multiple_of(j * tk, tk)
        st = _dot(k_ref[pl.ds(start, tk), :], qt)
        if masked:
            key = lax.broadcasted_iota(jnp.int32, (tk, tq), 0) + j * tk
            qry = lax.broadcasted_iota(jnp.int32, (tk, tq), 1) + i * tq
            st = jnp.where(key <= qry, st, NEG)
        s_scr[buf] = st
        cmax_scr[buf] = jnp.max(st, axis=0, keepdims=True)

    def accumulate(j, buf):
        m_old = m_scr[...]
        m_new = jnp.maximum(m_old, cmax_scr[buf])
        p = jnp.exp2(s_scr[buf] - m_new).astype(BF16)
        acc_scr[...] = acc_scr[...] * jnp.exp2(m_old - m_new) + _dot(vt_ref[j], p)
        m_scr[...] = m_new

    def pairs(pp, masks):
        for c, (mask_a, mask_b) in enumerate(masks):
            scores(2 * (pp + c) + 1, 1, mask_a)
            accumulate(2 * (pp + c), 0)
            scores(2 * (pp + c) + 2, 0, mask_b)
            accumulate(2 * (pp + c) + 1, 1)

    def finish():
        scores(n_tiles - 1, 1, True)
        accumulate(n_tiles - 2, 0)
        accumulate(n_tiles - 1, 1)
        acc = acc_scr[...]
        out_t = acc[0:MLA_V, :] / acc[MLA_V:MLA_V + 1, :]
        o_ref[...] = jnp.concatenate([out_t, jnp.zeros_like(out_t)], axis=0).T.astype(BF16)

    assert per_q == 4

    @pl.when(i == 0)
    def _():
        scores(0, 0, True)
        pairs(0, [(True, True)])
        finish()

    @pl.when(i > 0)
    def _():
        scores(0, 0, False)

        def body(g, carry):
            pairs(2 * g, [(False, False)] * 2)
            return carry

        lax.fori_loop(0, i - 1, body, 0)
        pairs(2 * (i - 1), [(False, False), (False, True), (True, True)])
        finish()


def _flash(q, k, vt):
    b, s, _ = q.shape
    tq, tk = FLASH_TILE, FLASH_KEY_TILE
    assert s % tq == 0 and tq % (2 * tk) == 0
    return pl.pallas_call(
        _flash_kernel,
        grid=(b, MLA_HEADS, s // tq),
        in_specs=[pl.BlockSpec((None, tq, LANES), lambda bb, hh, i: (bb, i, hh)),
                  pl.BlockSpec((None, s, LANES), lambda bb, hh, i: (bb, 0, hh)),
                  pl.BlockSpec((s // tk, MLA_VT_ROWS, tk), lambda bb, hh, i: (bb, hh, 0))],
        out_specs=pl.BlockSpec((None, tq, LANES), lambda bb, hh, i: (bb, i, hh)),
        out_shape=jax.ShapeDtypeStruct((b, s, MLA_PAD_W), BF16),
        scratch_shapes=[pltpu.VMEM((1, tq), F32), pltpu.VMEM((MLA_VT_ROWS, tq), F32),
                        pltpu.VMEM((2, tk, tq), F32), pltpu.VMEM((2, 1, tq), F32)],
        compiler_params=_params("parallel", "parallel", "arbitrary"),
        name="mla_flash",
    )(q, k, vt)


def _log_sigmoid(z):
    return jnp.minimum(z, 0.0) - jnp.log1p(jnp.exp(-jnp.abs(z)))


def _mlstm_kernel(qkc_ref, qkp_ref, v_ref, o_ref, g_ref, cw_ref, cb_ref, bi_ref, bf_ref,
                  y_ref, c_scr, m_scr):
    L = ML_CHUNK
    nb = qkc_ref.shape[0]
    c = pl.program_id(0)

    @pl.when(c == 0)
    def _():
        c_scr[...] = jnp.zeros(c_scr.shape, F32)
        m_scr[...] = jnp.full(m_scr.shape, NEG, F32)

    row = lax.broadcasted_iota(jnp.int32, (L, L), 0)
    col = lax.broadcasted_iota(jnp.int32, (L, L), 1)
    tri = col <= row
    eye = col == row

    def to_col(r):
        return jnp.sum(jnp.where(eye, r, 0.0), axis=1, keepdims=True)

    streams = [(b, hh) for b in range(nb) for hh in range(ML_HEADS)]
    qk, b_all, li_all = [], [], []
    for b in range(nb):
        cur = qkc_ref[b]
        prev = jnp.where(c > 0, qkp_ref[b], 0.0)
        row_w = lax.broadcasted_iota(jnp.int32, cur.shape, 0)
        conv = cb_ref[...] + cw_ref[ML_CONV - 1:ML_CONV, :] * cur
        for sh in range(1, ML_CONV):
            shifted = jnp.where(row_w >= sh, pltpu.roll(cur, sh, 0), pltpu.roll(prev, sh, 0))
            conv = conv + cw_ref[ML_CONV - 1 - sh:ML_CONV - sh, :] * shifted
        qk.append(conv * jax.nn.sigmoid(conv))
        gates = g_ref[b]
        li_all.append(gates[0:ML_HEADS, :] + bi_ref[...])
        lf = _log_sigmoid(gates[ML_HEADS:, :] + bf_ref[...])
        lane_g = lax.broadcasted_iota(jnp.int32, lf.shape, 1)
        step = 1
        while step < L:
            lf = lf + jnp.where(lane_g >= step, pltpu.roll(lf, step, 1), 0.0)
            step *= 2
        b_all.append(lf)

    st = []
    for n, (b, hh) in enumerate(streams):
        b_row = b_all[b][hh:hh + 1, :]
        li_row = li_all[b][hh:hh + 1, :]
        m_prev = m_scr[n:n + 1, 0:1]
        b_col = to_col(b_row)
        d = jnp.where(tri, b_col - b_row + li_row, NEG)
        inter = b_col + m_prev
        m_t = jnp.maximum(inter, jnp.max(d, axis=1, keepdims=True))
        b_last = b_row[:, L - 1:L]
        g = b_last - b_row + li_row
        m_new = jnp.maximum(b_last + m_prev, jnp.max(g, axis=1, keepdims=True))
        kf = qk[b][:, (ML_HEADS + hh) * ML_DK:(ML_HEADS + hh + 1) * ML_DK] * ML_DK ** -0.5
        st.append(dict(
            qh=qk[b][:, hh * ML_DK:(hh + 1) * ML_DK].astype(BF16),
            kh=kf.astype(BF16),
            kw=(kf * to_col(jnp.exp(g - m_new))).astype(BF16),
            vh=v_ref[b, :, hh * LANES:(hh + 1) * LANES],
            c_ext=c_scr[n],
            w_intra=jnp.exp(d - m_t), w_inter=jnp.exp(inter - m_t), floor=jnp.exp(-m_t),
            decay=jnp.exp(b_last + m_prev - m_new), m_new=m_new))

    for s_ in st:
        s_["qk"] = _dot_nt(s_["qh"], s_["kh"])
        s_["cross"] = _dot(s_["qh"], s_["c_ext"].astype(BF16))
    for s_ in st:
        s_["s"] = s_["qk"] * s_["w_intra"]
    for s_ in st:
        s_["intra"] = _dot(s_["s"].astype(BF16), s_["vh"])
        s_["kv"] = _dot_tn(s_["kw"], s_["vh"])
    for n, ((b, hh), s_) in enumerate(zip(streams, st)):
        num = s_["intra"][:, :ML_DV] + s_["w_inter"] * s_["cross"][:, :ML_DV]
        nq = jnp.sum(s_["s"], axis=1, keepdims=True) + s_["w_inter"] * s_["cross"][:, ML_DV:ML_DV + 1]
        hout = num / jnp.maximum(jnp.abs(nq), s_["floor"])
        og = jax.nn.sigmoid(o_ref[b, :, hh * ML_DV:(hh + 1) * ML_DV])
        y_ref[b, :, hh * ML_DV:(hh + 1) * ML_DV] = (og * hout).astype(BF16)
        c_scr[n] = s_["decay"] * s_["c_ext"] + s_["kv"]
        m_scr[n:n + 1, :] = jnp.broadcast_to(s_["m_new"], (1, LANES))


def _mlstm(qk_pre, v, o_pre, gates_t, conv_w, conv_b, b_i, b_f):
    batch, seq, wqk = qk_pre.shape
    L = ML_CHUNK
    full = lambda a: pl.BlockSpec(a.shape, lambda c: (0,) * a.ndim)
    chunk = lambda w_: pl.BlockSpec((batch, L, w_), lambda c: (0, c, 0))
    return pl.pallas_call(
        _mlstm_kernel,
        grid=(seq // L,),
        in_specs=[chunk(wqk),
                  pl.BlockSpec((batch, L, wqk), lambda c: (0, jnp.maximum(c - 1, 0), 0)),
                  chunk(ML_PAD_W), chunk(ML_W),
                  pl.BlockSpec((batch, 2 * ML_HEADS, L), lambda c: (0, 0, c)),
                  full(conv_w), full(conv_b), full(b_i), full(b_f)],
        out_specs=chunk(ML_W),
        out_shape=jax.ShapeDtypeStruct((batch, seq, ML_W), BF16),
        scratch_shapes=[pltpu.VMEM((batch * ML_HEADS, ML_DK, LANES), F32),
                        pltpu.VMEM((batch * ML_HEADS, LANES), F32)],
        compiler_params=_params("arbitrary"),
        name="mlstm",
    )(qk_pre, qk_pre, v, o_pre, gates_t, conv_w, conv_b, b_i, b_f)


def _swa_kernel(q_ref, k_ref, v_ref, y_ref, kk_scr, vv_scr, o_scr, l_scr):
    blk, tile = SW_BLOCK, SWA_TILE
    n_slab = SW_W // LANES
    n = pl.program_id(1)

    @pl.when(n == 0)
    def _():
        kk_scr[:, 0:tile, :] = jnp.zeros((n_slab, tile, LANES), F32)
        vv_scr[:, 0:tile, :] = jnp.zeros((n_slab, tile, LANES), F32)

    @pl.when(n > 0)
    def _():
        kk_scr[:, 0:tile, :] = kk_scr[:, tile:2 * tile, :]
        vv_scr[:, 0:tile, :] = vv_scr[:, tile:2 * tile, :]

    kk_scr[:, tile:2 * tile, :] = k_ref[...]
    vv_scr[:, tile:2 * tile, :] = v_ref[...]

    key = lax.broadcasted_iota(jnp.int32, (2 * blk, blk), 0)
    qry = lax.broadcasted_iota(jnp.int32, (2 * blk, blk), 1)
    dim =lax.broadcasted_iota(jnp.int32, (LANES, blk), 0)
    head_rows = [dim < SW_HD, dim >= SW_HD]

    for g, dil in enumerate(SW_DILATIONS):
        per_res = tile // (blk * dil)
        shift = per_res.bit_length() - 1

        def block_pair(it, carry, g=g, dil=dil, per_res=per_res, shift=shift):
            def rows(start):
                return pl.ds(start, blk, stride=dil) if dil > 1 else pl.ds(pl.multiple_of(start, blk), blk)

            work = []
            for u in range(2):
                bi = 2 * it + u
                r = lax.shift_right_logical(bi, shift)
                j = bi & (per_res - 1)
                q0 = r + j * (blk * dil)
                lowest = jnp.where((n == 0) & (j == 0), blk, 0)
                ok = (key >= jnp.maximum(qry, lowest)) & (key <= qry + blk)
                for sl in range(n_slab):
                    prev, cur = rows(tile + q0 - blk * dil), rows(tile + q0)
                    kb = jnp.concatenate([kk_scr[sl, prev, :], kk_scr[sl, cur, :]], axis=0)
                    vb = jnp.concatenate([vv_scr[sl, prev, :], vv_scr[sl, cur, :]], axis=0)
                    work.append(dict(q0=q0, sl=sl, ok=ok, qt=q_ref[sl, rows(q0), :].T,
                                     kb=kb.astype(BF16), vt=vb.T.astype(BF16)))
            for w in work:
                w["s"] = [_dot(w["kb"], jnp.where(head_rows[hh], w["qt"], 0.0).astype(BF16)) for hh in range(2)]
            for w in work:
                w["e"], w["den"], w["lse"] = [], [], []
                for hh in range(2):
                    s = jnp.where(w["ok"], w["s"][hh], NEG)
                    cmax = jnp.max(s, axis=0, keepdims=True)
                    e = jnp.exp2(s - cmax)
                    den = jnp.sum(e, axis=0, keepdims=True)
                    w["e"].append(e.astype(BF16))
                    w["den"].append(den)
                    w["lse"].append(cmax + jnp.log2(den))
            for w in work:
                ot = [_dot(w["vt"][hh * SW_HD:(hh + 1) * SW_HD, :], w["e"][hh]) / w["den"][hh] for hh in range(2)]
                lt = [jnp.broadcast_to(w["lse"][hh], (SW_HD, blk)) for hh in range(2)]
                o_scr[g, w["sl"], rows(w["q0"]), :] = jnp.concatenate(ot, axis=0).T
                l_scr[g, w["sl"], rows(w["q0"]), :] = jnp.concatenate(lt, axis=0).T
            return carry

        lax.fori_loop(0, tile // blk // 2, block_pair, 0)

    for sl in range(n_slab):
        ls = [l_scr[g, sl] for g in range(len(SW_DILATIONS))]
        mx = functools.reduce(jnp.maximum, ls)
        ws = [jnp.exp2(l - mx) for l in ls]
        num = functools.reduce(lambda a, b: a + b, [w * o_scr[g, sl] for g, w in enumerate(ws)])
        y_ref[:, sl * LANES:(sl + 1) * LANES] = (num / functools.reduce(lambda a, b: a + b, ws)).astype(BF16)


def _swa(q, k, v, batch, seq):
    tile = SWA_TILE
    n_slab = SW_W // LANES
    per_batch = seq // tile
    spec = pl.BlockSpec((n_slab, tile, LANES), lambda bb, n: (0, bb * per_batch + n, 0))
    n_br = len(SW_DILATIONS)
    return pl.pallas_call(
        _swa_kernel,
        grid=(batch, per_batch),
        in_specs=[spec, spec, spec],
        out_specs=pl.BlockSpec((tile, SW_W), lambda bb, n: (bb * per_batch + n, 0)),
        out_shape=jax.ShapeDtypeStruct((batch * seq, SW_W), BF16),
        scratch_shapes=[pltpu.VMEM((n_slab, 2 * tile, LANES), F32), pltpu.VMEM((n_slab, 2 * tile, LANES), F32),
                        pltpu.VMEM((n_br, n_slab, tile, LANES), F32), pltpu.VMEM((n_br, n_slab, tile, LANES), F32)],
        compiler_params=_params("parallel", "arbitrary"),
        name="swa",
    )(q, k, v)


def _outproj_kernel(x_ref, ya_ref, yb_ref, yc_ref, wa_ref, wb_ref, wc_ref, g_ref, out_ref):
    y = _dot(ya_ref[...], wa_ref[...]) + _dot(yb_ref[...], wb_ref[...]) + _dot(yc_ref[...], wc_ref[...])
    out_ref[...] = x_ref[...] + _rms(y, g_ref[...])


def _outproj(x, ya, yb, yc, wa, wb, wc, g):
    t_all = x.shape[0]
    tm = TOK_TILE
    tok = lambda w_: pl.BlockSpec((tm, w_), lambda i: (i, 0))
    full = lambda a: pl.BlockSpec(a.shape, lambda i: (0,) * a.ndim)
    return pl.pallas_call(
        _outproj_kernel,
        grid=(t_all // tm,),
        in_specs=[tok(D_MODEL), tok(MLA_PAD_W), tok(ML_W), tok(SW_W), full(wa), full(wb), full(wc), full(g)],
        out_specs=tok(D_MODEL),
        out_shape=jax.ShapeDtypeStruct((t_all, D_MODEL), F32),
        compiler_params=_params("parallel"),
        name="outproj",
    )(x, ya, yb, yc, wa, wb, wc, g)


def _memkv_kernel(mem_ref, g_ref, w_ref, kv_ref):
    kv_ref[...] = _dot(_rms(mem_ref[...], g_ref[...]).astype(BF16), w_ref[...]).astype(BF16)


def _memkv(mem, g, w):
    n = mem.shape[0]
    full = lambda a: pl.BlockSpec(a.shape, lambda i: (0,) * a.ndim)
    return pl.pallas_call(
        _memkv_kernel,
        grid=(1,),
        in_specs=[full(mem), full(g), full(w)],
        out_specs=pl.BlockSpec((n, 2 * D_MODEL), lambda i: (0, 0)),
        out_shape=jax.ShapeDtypeStruct((n, 2 * D_MODEL), BF16),
        compiler_params=_params("arbitrary"),
        name="memkv",
    )(mem, g, w)


def _xattn_kernel(x_ref, gpre_ref, wq_ref, kv_ref, wo_ref, gpost_ref, out_ref, o_scr):
    x = x_ref[...]
    h = _rms(x, gpre_ref[...]).astype(BF16)
    q = (_dot(h, wq_ref[...]) * X_HD ** -0.5).astype(BF16)
    for hh in range(X_HEADS):
        hs = slice(hh * X_HD, (hh + 1) * X_HD)
        s = _dot_nt(q[:, hs], kv_ref[:, hs])
        e = jnp.exp(s - jnp.max(s, axis=1, keepdims=True))
        den = jnp.sum(e, axis=1, keepdims=True)
        o_scr[:, hs] = (_dot(e.astype(BF16), kv_ref[:, D_MODEL + hh * X_HD:D_MODEL + (hh + 1) * X_HD]) / den).astype(BF16)
    y = _dot(o_scr[...], wo_ref[...])
    out_ref[...] = x + _rms(y, gpost_ref[...])


def _xattn(x, gpre, wq, kv, wo, gpost, seq):
    t_all = x.shape[0]
    tm = TOK_TILE
    per_batch = seq // tm
    tok = pl.BlockSpec((tm, D_MODEL), lambda i: (i, 0))
    full = lambda a: pl.BlockSpec(a.shape, lambda i: (0,) * a.ndim)
    return pl.pallas_call(
        _xattn_kernel,
        grid=(t_all // tm,),
        in_specs=[tok, full(gpre), full(wq), pl.BlockSpec((N_MEM, 2 * D_MODEL), lambda i: (i // per_batch, 0)),
                  full(wo), full(gpost)],
        out_specs=tok,
        out_shape=jax.ShapeDtypeStruct((t_all, D_MODEL), F32),
        scratch_shapes=[pltpu.VMEM((tm, D_MODEL), BF16)],
        compiler_params=_params("parallel"),
        name="xattn",
    )(x, gpre, wq, kv, wo, gpost)


def _ffn_kernel(x_ref, gpre_ref, wg_ref, wu_ref, wd_ref, gpost_ref, out_ref, h_scr, acc_scr):
    f = pl.program_id(1)

    @pl.when(f == 0)
    def _():
        h_scr[...] = _rms(x_ref[...], gpre_ref[...]).astype(BF16)
        acc_scr[...] = jnp.zeros(acc_scr.shape, F32)

    h = h_scr[...]
    gate = _dot(h, wg_ref[...])
    up = _dot(h, wu_ref[...])
    acc_scr[...] += _dot((gate * jax.nn.sigmoid(gate) * up).astype(BF16), wd_ref[...])

    @pl.when(f == pl.num_programs(1) - 1)
    def _():
        out_ref[...] = x_ref[...] + _rms(acc_scr[...], gpost_ref[...])


def _ffn(x, gpre, w_gate_up, w_down, gpost):
    t_all = x.shape[0]
    tm = TOK_TILE
    nf = 2
    tf = D_FF // nf
    tok = pl.BlockSpec((tm, D_MODEL), lambda i, f: (i, 0))
    full = lambda a: pl.BlockSpec(a.shape, lambda i, f: (0,) * a.ndim)
    return pl.pallas_call(
        _ffn_kernel,
        grid=(t_all // tm, nf),
        in_specs=[tok, full(gpre),
                  pl.BlockSpec((D_MODEL, tf), lambda i, f: (0, f)),
                  pl.BlockSpec((D_MODEL, tf), lambda i, f: (0, f + nf)),
                  pl.BlockSpec((tf, D_MODEL), lambda i, f: (f, 0)),
                  full(gpost)],
        out_specs=tok,
        out_shape=jax.ShapeDtypeStruct((t_all, D_MODEL), F32),
        scratch_shapes=[pltpu.VMEM((tm, D_MODEL), BF16), pltpu.VMEM((tm, D_MODEL), F32)],
        compiler_params=_params("parallel", "arbitrary"),
        name="ffn",
    )(x, gpre, w_gate_up, w_gate_up, w_down, gpost)


def _rope_tables(seq):
    pos = jnp.arange(seq, dtype=F32)[:, None]

    def cos_sin(d_rot):
        inv = ROPE_THETA ** (-jnp.arange(0, d_rot, 2, dtype=F32) / d_rot)
        ang = pos * inv[None, :]
        return jnp.cos(ang), jnp.sin(ang)

    c, s = cos_sin(MLA_ROPE)
    ones = jnp.ones((seq, MLA_NOPE), F32)
    tail = jnp.ones((seq, LANES - MLA_NOPE - MLA_ROPE), F32)
    cm = jnp.concatenate([ones, c, c, tail], axis=1)
    sm = jnp.concatenate([0 * ones, -s, s, 0 * tail], axis=1)
    c, s = cos_sin(SW_ROT)
    rest = jnp.ones((seq, SW_HD - SW_ROT), F32)
    cs = jnp.tile(jnp.concatenate([c, c, rest], axis=1), (1, LANES // SW_HD))
    ss = jnp.tile(jnp.concatenate([-s, s, 0 * rest], axis=1), (1, LANES // SW_HD))
    return cm, sm, cs, ss


def _pad_heads(w, heads, width, offset=0):
    k = w.shape[0]
    w = w.reshape(k, heads, width)
    w = jnp.pad(w, ((0, 0), (0, 0), (offset, LANES - width - offset)))
    return w.reshape(k, heads * LANES)


def _prep_layer(w_in, w_uq, w_ukv, w_out):
    o = IN_OFFS
    col = lambda i: w_in[:, o[i]:o[i + 1]]
    z = lambda n: jnp.zeros((D_MODEL, n), F32)
    w_p = jnp.concatenate([
        col(0), col(1),
        z(MLA_NOPE), col(2), z(LANES - MLA_NOPE - MLA_ROPE),
        col(3), col(4), _pad_heads(col(5), ML_HEADS, ML_DV), col(6), col(9), col(10), col(11)], axis=1)
    assert w_p.shape[1] == P_TOTAL
    w_gates_t = jnp.concatenate([col(7), col(8)], axis=1).T
    uq = w_uq.reshape(MLA_Q_LORA, MLA_HEADS, MLA_NOPE + MLA_ROPE)
    uq = jnp.pad(uq, ((0, 0), (0, 0), (0, LANES - MLA_NOPE - MLA_ROPE))).reshape(MLA_Q_LORA, MLA_PAD_W)
    ukv = w_ukv.reshape(MLA_KV_LORA, MLA_HEADS, MLA_NOPE + MLA_V)
    pad_half = lambda a: jnp.pad(a, ((0, 0), (0, 0), (0, LANES - a.shape[2]))).reshape(MLA_KV_LORA, MLA_PAD_W)
    wk = pad_half(ukv[:, :, :MLA_NOPE])
    wv = jnp.pad(ukv[:, :, MLA_NOPE:], ((0, 0), (0, 0), (0, MLA_VT_ROWS - MLA_V)))
    wv = wv.reshape(MLA_KV_LORA, MLA_HEADS * MLA_VT_ROWS).T
    wa = w_out[:MLA_W].reshape(MLA_HEADS, MLA_V, D_MODEL)
    wa = jnp.pad(wa, ((0, 0), (0, LANES - MLA_V), (0, 0))).reshape(MLA_PAD_W, D_MODEL)
    wb, wc = w_out[MLA_W:MLA_W + ML_W], w_out[MLA_W + ML_W:]
    bf = lambda a: a.astype(BF16)
    return bf(w_p), bf(w_gates_t), bf(uq), bf(wk), bf(wv), bf(wa), bf(wb), bf(wc)


def kernel(x, mem, w_in, mla_q_norm, mla_w_uq, mla_kv_norm, mla_w_ukv, ml_conv_w, ml_conv_b, ml_b_i, ml_b_f,
           w_out, x_w_q, x_w_kv, x_w_o, w_gate_up, w_down, norm_mix_pre, norm_mix_post, norm_mem,
           norm_x_pre, norm_x_post, norm_ffn_pre, norm_ffn_post):
    batch, seq, _ = x.shape
    depth = w_in.shape[0]
    t_all = batch * seq
    assert seq % SWA_TILE == 0 and seq % TOK_TILE == 0 and TOK_TILE % FLASH_KEY_TILE == 0
    assert SWA_TILE == SW_DILATIONS[-1] * SW_BLOCK
    cm, sm, cs, ss = _rope_tables(seq)
    one = jnp.zeros((1, LANES), F32).at[0, MLA_V].set(1.0)
    row = lambda a: a.reshape(1, -1)
    colv = lambda a: a.reshape(-1, 1)
    xt = x.reshape(t_all, D_MODEL)
    mem2 = mem.reshape(batch * N_MEM, D_MODEL)
    for l in range(depth):
        w_p, w_gt, uq, wk, wv, wa, wb, wc = _prep_layer(w_in[l], mla_w_uq[l], mla_w_ukv[l], w_out[l])
        q, k, v, mqk, mv, mo, gates_t, sq, sk, sv = _inproj(
            xt, row(norm_mix_pre[l]), w_p, w_gt, row(mla_q_norm[l]), uq, row(mla_kv_norm[l]), wk, wv,
            cm, sm, cs, ss, one, seq)
        b3 = lambda a: a.reshape(batch, seq, a.shape[-1])
        ya = _flash(b3(q), b3(k), v).reshape(t_all, MLA_PAD_W)
        yb = _mlstm(b3(mqk), b3(mv), b3(mo), gates_t, ml_conv_w[l], row(ml_conv_b[l]), colv(ml_b_i[l]),
                    colv(ml_b_f[l])).reshape(t_all, ML_W)
        yc = _swa(sq, sk, sv, batch, seq)
        xt = _outproj(xt, ya, yb, yc, wa, wb, wc, row(norm_mix_post[l]))
        kv = _memkv(mem2, row(norm_mem[l]), x_w_kv[l].astype(BF16))
        xt = _xattn(xt, row(norm_x_pre[l]), x_w_q[l].astype(BF16), kv, x_w_o[l].astype(BF16),
                    row(norm_x_post[l]), seq)
        xt = _ffn(xt, row(norm_ffn_pre[l]), w_gate_up[l].astype(BF16), w_down[l].astype(BF16),
                  row(norm_ffn_post[l]))
    return xt.reshape(batch, seq, D_MODEL)
```

```python
import functools

import jax
import jax.numpy as jnp
import numpy as np
from jax import lax
from jax.experimental import pallas as pl
from jax.experimental.pallas import tpu as pltpu

F32 = jnp.float32
BF16 = jnp.bfloat16

D_MODEL = 1024
N_MEM = 256
ROPE_THETA = 500000.0
EPS = 1e-6
NEG = -1e30
LOG2_E = 1.4426950408889634
LANES = 128

MLA_HEADS, MLA_NOPE, MLA_ROPE, MLA_V = 8, 64, 32, 64
MLA_Q_LORA, MLA_KV_LORA = 256, 128
ML_HEADS, ML_DK, ML_DV, ML_CONV, ML_CHUNK = 4, 64, 64, 4, 128
SW_HEADS, SW_HD = 4, 64
SW_ROT = SW_HD // 4
SW_DILATIONS = (1, 4, 16)
SW_BLOCK = 128
X_HEADS = 4
X_HD = D_MODEL // X_HEADS
D_FF = 2816

MLA_W = MLA_HEADS * MLA_V
ML_W = ML_HEADS * ML_DV
SW_W = SW_HEADS * SW_HD
MLA_PAD_W = MLA_HEADS * LANES

IN_SIZES = (MLA_Q_LORA, MLA_KV_LORA, MLA_ROPE, ML_HEADS * ML_DK, ML_HEADS * ML_DK, ML_W, ML_W,
            ML_HEADS, ML_HEADS, SW_W, SW_W, SW_W)
IN_OFFS = tuple(int(v) for v in np.cumsum((0,) + IN_SIZES))

P_CQ = 0
P_CKV = 256
P_KROPE = 384
P_MLQK = 512
P_MLO = 1024
P_SWQ = 1280
P_SWK = 1536
P_SWV = 1792
P_TOTAL = 2048
ML_VT_ROWS = 80
ML_T_ROWS = ML_HEADS * ML_VT_ROWS + 16

TOK_TILE = 512
FLASH_TILE = 1024
FLASH_KEY_TILE = 256
MLA_VT_ROWS = 80
SWA_TILE = 2048
VMEM_LIMIT = 56 * 1024 * 1024


def _rms(x, g):
    return x * lax.rsqrt(jnp.mean(x * x, axis=-1, keepdims=True) + EPS) * g


def _dot(a, b):
    return jnp.dot(a, b, preferred_element_type=F32)


def _dot_nt(a, b):
    return lax.dot_general(a, b, (((1,), (1,)), ((), ())), preferred_element_type=F32)


def _dot_tn(a, b):
    return lax.dot_general(a, b, (((0,), (0,)), ((), ())), preferred_element_type=F32)


def _params(*sem):
    return pltpu.CompilerParams(dimension_semantics=sem, vmem_limit_bytes=VMEM_LIMIT)


def _inproj_kernel(x_ref, g_ref, w_ref, wgt_ref, qg_ref, wuq_ref, kvg_ref, wk_ref, wvt_ref,
                   cm_ref, sm_ref, cs_ref, ss_ref,
                   q_ref, k_ref, vt_ref, mqk_ref, mv_ref, mo_ref, gr_ref, sq_ref, sk_ref, sv_ref):
    tm = x_ref.shape[0]
    h = _rms(x_ref[...], g_ref[...]).astype(BF16)

    def proj(a, b):
        return _dot(h, w_ref[:, a:b])

    lane = lax.broadcasted_iota(jnp.int32, (tm, LANES), 1)
    cm, sm = cm_ref[...], sm_ref[...]
    mla_first = lane < MLA_NOPE + MLA_ROPE // 2

    def rope_mla(t):
        rot = jnp.where(mla_first, pltpu.roll(t, LANES - MLA_ROPE // 2, 1), pltpu.roll(t, MLA_ROPE // 2, 1))
        return t * cm + rot * sm

    cs, ss = cs_ref[...], ss_ref[...]
    sw_first = (lane % SW_HD) < SW_ROT // 2

    def rope_sw(t):
        rot = jnp.where(sw_first, pltpu.roll(t, LANES - SW_ROT // 2, 1), pltpu.roll(t, SW_ROT // 2, 1))
        return t * cs + rot * ss

    cqn = _rms(proj(P_CQ, P_CQ + MLA_Q_LORA), qg_ref[...]).astype(BF16)
    q_scale = (MLA_NOPE + MLA_ROPE) ** -0.5 * LOG2_E
    for hh in range(MLA_HEADS):
        sl = slice(hh * LANES, (hh + 1) * LANES)
        q_ref[:, sl] = (rope_mla(_dot(cqn, wuq_ref[:, sl])) * q_scale).astype(BF16)
    kvn = _rms(proj(P_CKV, P_CKV + MLA_KV_LORA), kvg_ref[...]).astype(BF16)
    kpe = rope_mla(proj(P_KROPE, P_KROPE + LANES))
    one_col = (lax.broadcasted_iota(jnp.int32, (MLA_VT_ROWS, 1), 0) == MLA_V).astype(F32)
    for hh in range(MLA_HEADS):
        sl = slice(hh * LANES, (hh + 1) * LANES)
        k_ref[:, sl] = (_dot(kvn, wk_ref[:, sl]) + kpe).astype(BF16)
        rows = slice(hh * MLA_VT_ROWS, (hh + 1) * MLA_VT_ROWS)
        for u in range(tm // FLASH_KEY_TILE):
            toks = slice(u * FLASH_KEY_TILE, (u + 1) * FLASH_KEY_TILE)
            vt_ref[u, rows, :] = (_dot_nt(wvt_ref[rows, :], kvn[toks, :]) + one_col).astype(BF16)

    mqk_ref[...] = proj(P_MLQK, P_MLQK + 2 * ML_HEADS * ML_DK)
    mo_ref[...] = proj(P_MLO, P_MLO + ML_W)
    tr = _dot_nt(wgt_ref[...], h)
    n_v = ML_HEADS * ML_VT_ROWS
    v_row = lax.broadcasted_iota(jnp.int32, (n_v, 1), 0)
    mv_ref[...] = (tr[0:n_v, :] + ((v_row % ML_VT_ROWS) == ML_DV).astype(F32)).astype(BF16)
    gr_ref[...] = tr[n_v:n_v + 2 * ML_HEADS, :]

    for half in range(SW_W // LANES):
        sl = slice(half * LANES, (half + 1) * LANES)
        sq_ref[half] = rope_sw(proj(P_SWQ + half * LANES, P_SWQ + (half + 1) * LANES)) * (SW_HD ** -0.5 * LOG2_E)
        sk_ref[half] = rope_sw(proj(P_SWK + half * LANES, P_SWK + (half + 1) * LANES))
        sv_ref[half] = proj(P_SWV + half * LANES, P_SWV + (half + 1) * LANES)


def _inproj(x, g, w, wgt, qg, wuq, kvg, wk, wv, cm, sm, cs, ss, seq):
    t_all = x.shape[0]
    tm = TOK_TILE
    pos_tiles = seq // tm
    tok = lambda w_: pl.BlockSpec((tm, w_), lambda i: (i, 0))
    full = lambda a: pl.BlockSpec(a.shape, lambda i: (0,) * a.ndim)
    pos = pl.BlockSpec((tm, LANES), lambda i: (i % pos_tiles, 0))
    slabs = pl.BlockSpec((SW_W // LANES, tm, LANES), lambda i: (0, i, 0))
    out_shapes = (
        jax.ShapeDtypeStruct((t_all, MLA_PAD_W), BF16),
        jax.ShapeDtypeStruct((t_all, MLA_PAD_W), BF16),
        jax.ShapeDtypeStruct((t_all // FLASH_KEY_TILE, MLA_HEADS * MLA_VT_ROWS, FLASH_KEY_TILE), BF16),
        jax.ShapeDtypeStruct((t_all, 2 * ML_HEADS * ML_DK), F32),
        jax.ShapeDtypeStruct((t_all // seq, ML_HEADS * ML_VT_ROWS, seq), BF16),
        jax.ShapeDtypeStruct((t_all, ML_W), F32),
        jax.ShapeDtypeStruct((t_all // seq, 2 * ML_HEADS, seq), F32),
        jax.ShapeDtypeStruct((SW_W // LANES, t_all, LANES), F32),
        jax.ShapeDtypeStruct((SW_W // LANES, t_all, LANES), F32),
        jax.ShapeDtypeStruct((SW_W // LANES, t_all, LANES), F32),
    )
    out_specs = (tok(MLA_PAD_W), tok(MLA_PAD_W),
                 pl.BlockSpec((tm // FLASH_KEY_TILE, MLA_HEADS * MLA_VT_ROWS, FLASH_KEY_TILE), lambda i: (i, 0, 0)),
                 tok(2 * ML_HEADS * ML_DK),
                 pl.BlockSpec((None, ML_HEADS * ML_VT_ROWS, tm), lambda i: (i // pos_tiles, 0, i % pos_tiles)),
                 tok(ML_W), pl.BlockSpec((None, 2 * ML_HEADS, tm), lambda i: (i // pos_tiles, 0, i % pos_tiles)),
                 slabs, slabs, slabs)
    return pl.pallas_call(
        _inproj_kernel,
        grid=(t_all // tm,),
        in_specs=[tok(D_MODEL), full(g), full(w), full(wgt), full(qg), full(wuq), full(kvg), full(wk), full(wv),
                  pos, pos, pos, pos],
        out_specs=out_specs,
        out_shape=out_shapes,
        compiler_params=_params("parallel"),
        name="inproj",
    )(x, g, w, wgt, qg, wuq, kvg, wk, wv, cm, sm, cs, ss)


def _flash_kernel(q_ref, k_ref, vt_ref, o_ref, m_scr, acc_scr, s_scr, cmax_scr):
    tq = q_ref.shape[0]
    tk = FLASH_KEY_TILE
    per_q = tq // tk
    i = pl.program_id(2)
    n_tiles = (i + 1) * per_q
    qt = q_ref[...].astype(F32).T.astype(BF16)
    m_scr[...] = jnp.full(m_scr.shape, NEG, F32)
    acc_scr[...] = jnp.zeros(acc_scr.shape, F32)

    def scores(j, buf, masked):
        start = pl.multiple_of(j * tk, tk)
        st = _dot(k_ref[pl.ds(start, tk), :], qt)
        if masked:
            key = lax.broadcasted_iota(jnp.int32, (tk, tq), 0) + j * tk
            qry = lax.broadcasted_iota(jnp.int32, (tk, tq), 1) + i * tq
            st = jnp.where(key <= qry, st, NEG)
        s_scr[buf] = st
        cmax_scr[buf] = jnp.max(st, axis=0, keepdims=True)

    def accumulate(j, buf):
        m_old = m_scr[...]
        m_new = jnp.maximum(m_old, cmax_scr[buf])
        p = jnp.exp2(s_scr[buf] - m_new).astype(BF16)
        acc_scr[...] = acc_scr[...] * jnp.exp2(m_old - m_new) + _dot(vt_ref[j], p)
        m_scr[...] = m_new

    def pairs(pp, masks):
        for c, (mask_a, mask_b) in enumerate(masks):
            scores(2 * (pp + c) + 1, 1, mask_a)
            accumulate(2 * (pp + c), 0)
            scores(2 * (pp + c) + 2, 0, mask_b)
            accumulate(2 * (pp + c) + 1, 1)

    def finish():
        scores(n_tiles - 1, 1, True)
        accumulate(n_tiles - 2, 0)
        accumulate(n_tiles - 1, 1)
        acc = acc_scr[...]
        out_t = acc[0:MLA_V, :] / acc[MLA_V:MLA_V + 1, :]
        o_ref[...] = jnp.concatenate([out_t, jnp.zeros_like(out_t)], axis=0).T.astype(BF16)

    assert per_q == 4

    @pl.when(i == 0)
    def _():
        scores(0, 0, True)
        pairs(0, [(True, True)])
        finish()

    tail = [(False, False), (False, True), (True, True)]
    plain = [(False, False)] * 2
    n_eights = lax.shift_right_logical(jnp.maximum(i - 1, 0), 1)

    def body(g, carry):
        pairs(4 * g, plain + plain)
        return carry

    @pl.when((i > 0) & ((i & 1) == 1))
    def _():
        scores(0, 0, False)
        lax.fori_loop(0, n_eights, body, 0)
        pairs(4 * n_eights, tail)
        finish()

    @pl.when((i > 0) & ((i & 1) == 0))
    def _():
        scores(0, 0, False)
        lax.fori_loop(0, n_eights, body, 0)
        pairs(4 * n_eights, plain + tail)
        finish()


def _flash(q, k, vt):
    b, s, _ = q.shape
    tq, tk = FLASH_TILE, FLASH_KEY_TILE
    assert s % tq == 0 and tq % (2 * tk) == 0
    return pl.pallas_call(
        _flash_kernel,
        grid=(b, MLA_HEADS, s // tq),
        in_specs=[pl.BlockSpec((None, tq, LANES), lambda bb, hh, i: (bb, i, hh)),
                  pl.BlockSpec((None, s, LANES), lambda bb, hh, i: (bb, 0, hh)),
                  pl.BlockSpec((s // tk, MLA_VT_ROWS, tk), lambda bb, hh, i: (bb, hh, 0))],
        out_specs=pl.BlockSpec((None, tq, LANES), lambda bb, hh, i: (bb, i, hh)),
        out_shape=jax.ShapeDtypeStruct((b, s, MLA_PAD_W), BF16),
        scratch_shapes=[pltpu.VMEM((1, tq), F32), pltpu.VMEM((MLA_VT_ROWS, tq), F32),
                        pltpu.VMEM((2, tk, tq), F32), pltpu.VMEM((2, 1, tq), F32)],
        compiler_params=_params("parallel", "parallel", "arbitrary"),
        name="mla_flash",
    )(q, k, vt)


def _log_sigmoid(z):
    return jnp.minimum(z, 0.0) - jnp.log1p(jnp.exp(-jnp.abs(z)))


def _mlstm_kernel(qkc_ref, qkp_ref, vt_ref, o_ref, g_ref, cw_ref, cb_ref, bi_ref, bf_ref,
                  y_ref, c_scr, m_scr):
    L = ML_CHUNK
    nb = qkc_ref.shape[0]
    c = pl.program_id(0)

    @pl.when(c == 0)
    def _():
        c_scr[...] = jnp.zeros(c_scr.shape, F32)
        m_scr[...] = jnp.full(m_scr.shape, NEG, F32)

    src = lax.broadcasted_iota(jnp.int32, (L, L), 0)
    tgt = lax.broadcasted_iota(jnp.int32, (L, L), 1)
    causal = src <= tgt
    pick = (lax.broadcasted_iota(jnp.int32, (L, ML_HEADS * L), 0)
            == lax.broadcasted_iota(jnp.int32, (L, ML_HEADS * L), 1) // L).astype(BF16)

    streams = [(b, hh) for b in range(nb) for hh in range(ML_HEADS)]
    qk, b_all, li_all, x_col = [], [], [], []
    for b in range(nb):
        cur = qkc_ref[b]
        prev = jnp.where(c > 0, qkp_ref[b], 0.0)
        row_w = lax.broadcasted_iota(jnp.int32, cur.shape, 0)
        conv = cb_ref[...] + cw_ref[ML_CONV - 1:ML_CONV, :] * cur
        for sh in range(1, ML_CONV):
            shifted = jnp.where(row_w >= sh, pltpu.roll(cur, sh, 0), pltpu.roll(prev, sh, 0))
            conv = conv + cw_ref[ML_CONV - 1 - sh:ML_CONV - sh, :] * shifted
        qk.append(conv * jax.nn.sigmoid(conv))
        gates = g_ref[b]
        li = gates[0:ML_HEADS, :] + bi_ref[...]
        lf = _log_sigmoid(gates[ML_HEADS:, :] + bf_ref[...])
        lane_g = lax.broadcasted_iota(jnp.int32, lf.shape, 1)
        step = 1
        while step < L:
            lf = lf + jnp.where(lane_g >= step, pltpu.roll(lf, step, 1), 0.0)
            step *= 2
        li_all.append(li)
        b_all.append(lf)
        x = jnp.concatenate([li - lf, jnp.zeros((L - ML_HEADS, L), F32)], axis=0).T
        spread = jnp.zeros((L, ML_HEADS * L), F32)
        for _ in range(3):
            part = x.astype(BF16)
            spread = spread + _dot(part, pick)
            x = x - part.astype(F32)
        x_col.append(spread)

    st = []
    for n, (b, hh) in enumerate(streams):
        b_row = b_all[b][hh:hh + 1, :]
        li_row = li_all[b][hh:hh + 1, :]
        m_prev = m_scr[n:n + 1, 0:1]
        d_t = jnp.where(causal, x_col[b][:, hh * L:(hh + 1) * L] + b_row, NEG)
        inter = b_row + m_prev
        m_t = jnp.maximum(inter, jnp.max(d_t, axis=0, keepdims=True))
        b_last = b_row[:, L - 1:L]
        g = b_last - b_row + li_row
        m_new = jnp.maximum(b_last + m_prev, jnp.max(g, axis=1, keepdims=True))
        kf = qk[b][:, (ML_HEADS + hh) * ML_DK:(ML_HEADS + hh + 1) * ML_DK] * ML_DK ** -0.5
        vt = vt_ref[b, hh * ML_VT_ROWS:(hh + 1) * ML_VT_ROWS, :]
        st.append(dict(
            qh=qk[b][:, hh * ML_DK:(hh + 1) * ML_DK].astype(BF16),
            kh=kf.astype(BF16), vt=vt,
            vw=(vt.astype(F32) * jnp.exp(g - m_new)).astype(BF16),
            c_t=c_scr[n],
            w_intra=jnp.exp(d_t - m_t), w_inter=jnp.exp(inter - m_t), floor=jnp.exp(-m_t),
            decay=jnp.exp(b_last + m_prev - m_new), m_new=m_new))

    for s_ in st:
        s_["sraw"] = _dot_nt(s_["kh"], s_["qh"])
        s_["cross"] = _dot_nt(s_["c_t"].astype(BF16), s_["qh"])
    for s_ in st:
        s_["s"] = s_["sraw"] * s_["w_intra"]
    for s_ in st:
        s_["intra"] = _dot(s_["vt"], s_["s"].astype(BF16))
        s_["kv"] = _dot(s_["vw"], s_["kh"])
    h_t = []
    for n, s_ in enumerate(st):
        num = s_["intra"][:ML_DV, :] + s_["w_inter"] * s_["cross"][:ML_DV, :]
        nq = jnp.sum(s_["s"], axis=0, keepdims=True) + s_["w_inter"] * s_["cross"][ML_DV:ML_DV + 1, :]
        h_t.append(num / jnp.maximum(jnp.abs(nq), s_["floor"]))
        c_scr[n] = s_["decay"] * s_["c_t"] + s_["kv"]
        m_scr[n:n + 1, :] = jnp.broadcast_to(s_["m_new"], (1, LANES))
    for b in range(nb):
        hout = jnp.concatenate(h_t[b * ML_HEADS:(b + 1) * ML_HEADS], axis=0).T
        y_ref[b] = (jax.nn.sigmoid(o_ref[b]) * hout).astype(BF16)


def _mlstm(qk_pre, vt, o_pre, gates_t, conv_w, conv_b, b_i, b_f):
    batch, seq, wqk = qk_pre.shape
    L = ML_CHUNK
    full = lambda a: pl.BlockSpec(a.shape, lambda c: (0,) * a.ndim)
    chunk = lambda w_: pl.BlockSpec((batch, L, w_), lambda c: (0, c, 0))
    lanes = lambda r: pl.BlockSpec((batch, r, L), lambda c: (0, 0, c))
    return pl.pallas_call(
        _mlstm_kernel,
        grid=(seq // L,),
        in_specs=[chunk(wqk),
                  pl.BlockSpec((batch, L, wqk), lambda c: (0, jnp.maximum(c - 1, 0), 0)),
                  lanes(ML_HEADS * ML_VT_ROWS), chunk(ML_W), lanes(2 * ML_HEADS),
                  full(conv_w), full(conv_b), full(b_i), full(b_f)],
        out_specs=chunk(ML_W),
        out_shape=jax.ShapeDtypeStruct((batch, seq, ML_W), BF16),
        scratch_shapes=[pltpu.VMEM((batch * ML_HEADS, ML_VT_ROWS, ML_DK), F32),
                        pltpu.VMEM((batch * ML_HEADS, LANES), F32)],
        compiler_params=_params("arbitrary"),
        name="mlstm",
    )(qk_pre, qk_pre, vt, o_pre, gates_t, conv_w, conv_b, b_i, b_f)


def _swa_kernel(q_ref, k_ref, v_ref, y_ref, kk_scr, vv_scr, o_scr, l_scr):
    blk, tile = SW_BLOCK, SWA_TILE
    n_slab = SW_W // LANES
    n = pl.program_id(1)

    @pl.when(n == 0)
    def _():
        kk_scr[:, 0:tile, :] = jnp.zeros((n_slab, tile, LANES), F32)
        vv_scr[:, 0:tile, :] = jnp.zeros((n_slab, tile, LANES), F32)

    @pl.when(n > 0)
    def _():
        kk_scr[:, 0:tile, :] = kk_scr[:, tile:2 * tile, :]
        vv_scr[:, 0:tile, :] = vv_scr[:, tile:2 * tile, :]

    kk_scr[:, tile:2 * tile, :] = k_ref[...]
    vv_scr[:, tile:2 * tile, :] = v_ref[...]

    key = lax.broadcasted_iota(jnp.int32, (2 * blk, blk), 0)
    qry = lax.broadcasted_iota(jnp.int32, (2 * blk, blk), 1)
    dim =lax.broadcasted_iota(jnp.int32, (LANES, blk), 0)
    head_rows = [dim < SW_HD, dim >= SW_HD]

    for g, dil in enumerate(SW_DILATIONS):
        per_res = tile // (blk * dil)
        shift = per_res.bit_length() - 1

        def block_pair(it, carry, g=g, dil=dil, per_res=per_res, shift=shift):
            def rows(start):
                return pl.ds(start, blk, stride=dil) if dil > 1 else pl.ds(pl.multiple_of(start, blk), blk)

            work = []
            for u in range(2):
                bi = 2 * it + u
                r = lax.shift_right_logical(bi, shift)
                j = bi & (per_res - 1)
                q0 = r + j * (blk * dil)
                lowest = jnp.where((n == 0) & (j == 0), blk, 0)
                ok = (key >= jnp.maximum(qry, lowest)) & (key <= qry + blk)
                for sl in range(n_slab):
                    prev, cur = rows(tile + q0 - blk * dil), rows(tile + q0)
                    kb = jnp.concatenate([kk_scr[sl, prev, :], kk_scr[sl, cur, :]], axis=0)
                    vb = jnp.concatenate([vv_scr[sl, prev, :], vv_scr[sl, cur, :]], axis=0)
                    work.append(dict(q0=q0, sl=sl, ok=ok, qt=q_ref[sl, rows(q0), :].T,
                                     kb=kb.astype(BF16), vt=vb.T.astype(BF16)))
            for w in work:
                w["s"] = [_dot(w["kb"], jnp.where(head_rows[hh], w["qt"], 0.0).astype(BF16)) for hh in range(2)]
            for w in work:
                w["e"], w["den"], w["lse"] = [], [], []
                for hh in range(2):
                    s = jnp.where(w["ok"], w["s"][hh], NEG)
                    cmax = jnp.max(s, axis=0, keepdims=True)
                    e = jnp.exp2(s - cmax)
                    den = jnp.sum(e, axis=0, keepdims=True)
                    w["e"].append(e.astype(BF16))
                    w["den"].append(den)
                    w["lse"].append(cmax + jnp.log2(den))
            for w in work:
                ot = [_dot(w["vt"][hh * SW_HD:(hh + 1) * SW_HD, :], w["e"][hh]) / w["den"][hh] for hh in range(2)]
                lt = [jnp.broadcast_to(w["lse"][hh], (SW_HD, blk)) for hh in range(2)]
                o_scr[g, w["sl"], rows(w["q0"]), :] = jnp.concatenate(ot, axis=0).T
                l_scr[g, w["sl"], rows(w["q0"]), :] = jnp.concatenate(lt, axis=0).T
            return carry

        lax.fori_loop(0, tile // blk // 2, block_pair, 0)

    for sl in range(n_slab):
        ls = [l_scr[g, sl] for g in range(len(SW_DILATIONS))]
        mx = functools.reduce(jnp.maximum, ls)
        ws = [jnp.exp2(l - mx) for l in ls]
        num = functools.reduce(lambda a, b: a + b, [w * o_scr[g, sl] for g, w in enumerate(ws)])
        y_ref[:, sl * LANES:(sl + 1) * LANES] = (num / functools.reduce(lambda a, b: a + b, ws)).astype(BF16)


def _swa(q, k, v, batch, seq):
    tile = SWA_TILE
    n_slab = SW_W // LANES
    per_batch = seq // tile
    spec = pl.BlockSpec((n_slab, tile, LANES), lambda bb, n: (0, bb * per_batch + n, 0))
    n_br = len(SW_DILATIONS)
    return pl.pallas_call(
        _swa_kernel,
        grid=(batch, per_batch),
        in_specs=[spec, spec, spec],
        out_specs=pl.BlockSpec((tile, SW_W), lambda bb, n: (bb * per_batch + n, 0)),
        out_shape=jax.ShapeDtypeStruct((batch * seq, SW_W), BF16),
        scratch_shapes=[pltpu.VMEM((n_slab, 2 * tile, LANES), F32), pltpu.VMEM((n_slab, 2 * tile, LANES), F32),
                        pltpu.VMEM((n_br, n_slab, tile, LANES), F32), pltpu.VMEM((n_br, n_slab, tile, LANES), F32)],
        compiler_params=_params("parallel", "arbitrary"),
        name="swa",
    )(q, k, v)


def _outproj_kernel(x_ref, ya_ref, yb_ref, yc_ref, wa_ref, wb_ref, wc_ref, g_ref, out_ref):
    y = _dot(ya_ref[...], wa_ref[...]) + _dot(yb_ref[...], wb_ref[...]) + _dot(yc_ref[...], wc_ref[...])
    out_ref[...] = x_ref[...] + _rms(y, g_ref[...])


def _outproj(x, ya, yb, yc, wa, wb, wc, g):
    t_all = x.shape[0]
    tm = TOK_TILE
    tok = lambda w_: pl.BlockSpec((tm, w_), lambda i: (i, 0))
    full = lambda a: pl.BlockSpec(a.shape, lambda i: (0,) * a.ndim)
    return pl.pallas_call(
        _outproj_kernel,
        grid=(t_all // tm,),
        in_specs=[tok(D_MODEL), tok(MLA_PAD_W), tok(ML_W), tok(SW_W), full(wa), full(wb), full(wc), full(g)],
        out_specs=tok(D_MODEL),
        out_shape=jax.ShapeDtypeStruct((t_all, D_MODEL), F32),
        compiler_params=_params("parallel"),
        name="outproj",
    )(x, ya, yb, yc, wa, wb, wc, g)


def _memkv_kernel(mem_ref, g_ref, w_ref, kv_ref):
    kv_ref[...] = _dot(_rms(mem_ref[...], g_ref[...]).astype(BF16), w_ref[...]).astype(BF16)


def _memkv(mem, g, w):
    n = mem.shape[0]
    full = lambda a: pl.BlockSpec(a.shape, lambda i: (0,) * a.ndim)
    return pl.pallas_call(
        _memkv_kernel,
        grid=(1,),
        in_specs=[full(mem), full(g), full(w)],
        out_specs=pl.BlockSpec((n, 2 * D_MODEL), lambda i: (0, 0)),
        out_shape=jax.ShapeDtypeStruct((n, 2 * D_MODEL), BF16),
        compiler_params=_params("arbitrary"),
        name="memkv",
    )(mem, g, w)


def _xattn_kernel(x_ref, gpre_ref, wq_ref, kv_ref, wo_ref, gpost_ref, out_ref, o_scr):
    x = x_ref[...]
    h = _rms(x, gpre_ref[...]).astype(BF16)
    q = (_dot(h, wq_ref[...]) * X_HD ** -0.5).astype(BF16)
    for hh in range(X_HEADS):
        hs = slice(hh * X_HD, (hh + 1) * X_HD)
        s = _dot_nt(q[:, hs], kv_ref[:, hs])
        e = jnp.exp(s - jnp.max(s, axis=1, keepdims=True))
        den = jnp.sum(e, axis=1, keepdims=True)
        o_scr[:, hs] = (_dot(e.astype(BF16), kv_ref[:, D_MODEL + hh * X_HD:D_MODEL + (hh + 1) * X_HD]) / den).astype(BF16)
    y = _dot(o_scr[...], wo_ref[...])
    out_ref[...] = x + _rms(y, gpost_ref[...])


def _xattn(x, gpre, wq, kv, wo, gpost, seq):
    t_all = x.shape[0]
    tm = TOK_TILE
    per_batch = seq // tm
    tok = pl.BlockSpec((tm, D_MODEL), lambda i: (i, 0))
    full = lambda a: pl.BlockSpec(a.shape, lambda i: (0,) * a.ndim)
    return pl.pallas_call(
        _xattn_kernel,
        grid=(t_all // tm,),
        in_specs=[tok, full(gpre), full(wq), pl.BlockSpec((N_MEM, 2 * D_MODEL), lambda i: (i // per_batch, 0)),
                  full(wo), full(gpost)],
        out_specs=tok,
        out_shape=jax.ShapeDtypeStruct((t_all, D_MODEL), F32),
        scratch_shapes=[pltpu.VMEM((tm, D_MODEL), BF16)],
        compiler_params=_params("parallel"),
        name="xattn",
    )(x, gpre, wq, kv, wo, gpost)


def _ffn_kernel(x_ref, gpre_ref, wg_ref, wu_ref, wd_ref, gpost_ref, out_ref, h_scr, acc_scr):
    f = pl.program_id(1)

    @pl.when(f == 0)
    def _():
        h_scr[...] = _rms(x_ref[...], gpre_ref[...]).astype(BF16)
        acc_scr[...] = jnp.zeros(acc_scr.shape, F32)

    h = h_scr[...]
    gate = _dot(h, wg_ref[...])
    up = _dot(h, wu_ref[...])
    acc_scr[...] += _dot((gate * jax.nn.sigmoid(gate) * up).astype(BF16), wd_ref[...])

    @pl.when(f == pl.num_programs(1) - 1)
    def _():
        out_ref[...] = x_ref[...] + _rms(acc_scr[...], gpost_ref[...])


def _ffn(x, gpre, w_gate_up, w_down, gpost):
    t_all = x.shape[0]
    tm = TOK_TILE
    nf = 2
    tf = D_FF // nf
    tok = pl.BlockSpec((tm, D_MODEL), lambda i, f: (i, 0))
    full = lambda a: pl.BlockSpec(a.shape, lambda i, f: (0,) * a.ndim)
    return pl.pallas_call(
        _ffn_kernel,
        grid=(t_all // tm, nf),
        in_specs=[tok, full(gpre),
                  pl.BlockSpec((D_MODEL, tf), lambda i, f: (0, f)),
                  pl.BlockSpec((D_MODEL, tf), lambda i, f: (0, f + nf)),
                  pl.BlockSpec((tf, D_MODEL), lambda i, f: (f, 0)),
                  full(gpost)],
        out_specs=tok,
        out_shape=jax.ShapeDtypeStruct((t_all, D_MODEL), F32),
        scratch_shapes=[pltpu.VMEM((tm, D_MODEL), BF16), pltpu.VMEM((tm, D_MODEL), F32)],
        compiler_params=_params("parallel", "arbitrary"),
        name="ffn",
    )(x, gpre, w_gate_up, w_gate_up, w_down, gpost)


def _rope_tables(seq):
    pos = jnp.arange(seq, dtype=F32)[:, None]

    def cos_sin(d_rot):
        inv = ROPE_THETA ** (-jnp.arange(0, d_rot, 2, dtype=F32) / d_rot)
        ang = pos * inv[None, :]
        return jnp.cos(ang), jnp.sin(ang)

    c, s = cos_sin(MLA_ROPE)
    ones = jnp.ones((seq, MLA_NOPE), F32)
    tail = jnp.ones((seq, LANES - MLA_NOPE - MLA_ROPE), F32)
    cm = jnp.concatenate([ones, c, c, tail], axis=1)
    sm = jnp.concatenate([0 * ones, -s, s, 0 * tail], axis=1)
    c, s = cos_sin(SW_ROT)
    rest = jnp.ones((seq, SW_HD - SW_ROT), F32)
    cs = jnp.tile(jnp.concatenate([c, c, rest], axis=1), (1, LANES // SW_HD))
    ss = jnp.tile(jnp.concatenate([-s, s, 0 * rest], axis=1), (1, LANES // SW_HD))
    return cm, sm, cs, ss


def _prep_layer(w_in, w_uq, w_ukv, w_out):
    o = IN_OFFS
    col = lambda i: w_in[:, o[i]:o[i + 1]]
    z = lambda n: jnp.zeros((D_MODEL, n), F32)
    w_p = jnp.concatenate([
        col(0), col(1),
        z(MLA_NOPE), col(2), z(LANES - MLA_NOPE - MLA_ROPE),
        col(3), col(4), col(6), col(9), col(10), col(11)], axis=1)
    assert w_p.shape[1] == P_TOTAL
    mlv = jnp.pad(col(5).reshape(D_MODEL, ML_HEADS, ML_DV), ((0, 0), (0, 0), (0, ML_VT_ROWS - ML_DV)))
    w_gates_t = jnp.concatenate([mlv.reshape(D_MODEL, ML_HEADS * ML_VT_ROWS), col(7), col(8),
                                 z(ML_T_ROWS - ML_HEADS * ML_VT_ROWS - 2 * ML_HEADS)], axis=1).T
    uq = w_uq.reshape(MLA_Q_LORA, MLA_HEADS, MLA_NOPE + MLA_ROPE)
    uq = jnp.pad(uq, ((0, 0), (0, 0), (0, LANES - MLA_NOPE - MLA_ROPE))).reshape(MLA_Q_LORA, MLA_PAD_W)
    ukv = w_ukv.reshape(MLA_KV_LORA, MLA_HEADS, MLA_NOPE + MLA_V)
    pad_half = lambda a: jnp.pad(a, ((0, 0), (0, 0), (0, LANES - a.shape[2]))).reshape(MLA_KV_LORA, MLA_PAD_W)
    wk = pad_half(ukv[:, :, :MLA_NOPE])
    wv = jnp.pad(ukv[:, :, MLA_NOPE:], ((0, 0), (0, 0), (0, MLA_VT_ROWS - MLA_V)))
    wv = wv.reshape(MLA_KV_LORA, MLA_HEADS * MLA_VT_ROWS).T
    wa = w_out[:MLA_W].reshape(MLA_HEADS, MLA_V, D_MODEL)
    wa = jnp.pad(wa, ((0, 0), (0, LANES - MLA_V), (0, 0))).reshape(MLA_PAD_W, D_MODEL)
    wb, wc = w_out[MLA_W:MLA_W + ML_W], w_out[MLA_W + ML_W:]
    bf = lambda a: a.astype(BF16)
    return bf(w_p), bf(w_gates_t), bf(uq), bf(wk), bf(wv), bf(wa), bf(wb), bf(wc)


def kernel(x, mem, w_in, mla_q_norm, mla_w_uq, mla_kv_norm, mla_w_ukv, ml_conv_w, ml_conv_b, ml_b_i, ml_b_f,
           w_out, x_w_q, x_w_kv, x_w_o, w_gate_up, w_down, norm_mix_pre, norm_mix_post, norm_mem,
           norm_x_pre, norm_x_post, norm_ffn_pre, norm_ffn_post):
    batch, seq, _ = x.shape
    depth = w_in.shape[0]
    t_all = batch * seq
    assert seq % SWA_TILE == 0 and seq % TOK_TILE == 0 and TOK_TILE % FLASH_KEY_TILE == 0
    assert SWA_TILE == SW_DILATIONS[-1] * SW_BLOCK
    cm, sm, cs, ss = _rope_tables(seq)
    row = lambda a: a.reshape(1, -1)
    colv = lambda a: a.reshape(-1, 1)
    xt = x.reshape(t_all, D_MODEL)
    mem2 = mem.reshape(batch * N_MEM, D_MODEL)
    for l in range(depth):
        w_p, w_gt, uq, wk, wv, wa, wb, wc = _prep_layer(w_in[l], mla_w_uq[l], mla_w_ukv[l], w_out[l])
        q, k, v, mqk, mv, mo, gates_t, sq, sk, sv = _inproj(
            xt, row(norm_mix_pre[l]), w_p, w_gt, row(mla_q_norm[l]), uq, row(mla_kv_norm[l]), wk, wv,
            cm, sm, cs, ss, seq)
        b3 = lambda a: a.reshape(batch, seq, a.shape[-1])
        ya = _flash(b3(q), b3(k), v).reshape(t_all, MLA_PAD_W)
        yb = _mlstm(b3(mqk), mv, b3(mo), gates_t, ml_conv_w[l], row(ml_conv_b[l]), colv(ml_b_i[l]),
                    colv(ml_b_f[l])).reshape(t_all, ML_W)
        yc = _swa(sq, sk, sv, batch, seq)
        xt = _outproj(xt, ya, yb, yc, wa, wb, wc, row(norm_mix_post[l]))
        kv = _memkv(mem2, row(norm_mem[l]), x_w_kv[l].astype(BF16))
        xt = _xattn(xt, row(norm_x_pre[l]), x_w_q[l].astype(BF16), kv, x_w_o[l].astype(BF16),
                    row(norm_x_post[l]), seq)
        xt = _ffn(xt, row(norm_ffn_pre[l]), w_gate_up[l].astype(BF16), w_down[l].astype(BF16),
                  row(norm_ffn_post[l]))
    return xt.reshape(batch, seq, D_MODEL)
```

```python
import functools

import jax
import jax.numpy as jnp
import numpy as np
from jax import lax
from jax.experimental import pallas as pl
from jax.experimental.pallas import tpu as pltpu

F32 = jnp.float32
BF16 = jnp.bfloat16

D_MODEL = 1024
N_MEM = 256
ROPE_THETA = 500000.0
EPS = 1e-6
NEG = -1e30
LOG2_E = 1.4426950408889634
LANES = 128

MLA_HEADS, MLA_NOPE, MLA_ROPE, MLA_V = 8, 64, 32, 64
MLA_Q_LORA, MLA_KV_LORA = 256, 128
ML_HEADS, ML_DK, ML_DV, ML_CONV, ML_CHUNK = 4, 64, 64, 4, 128
SW_HEADS, SW_HD = 4, 64
SW_ROT = SW_HD // 4
SW_DILATIONS = (1, 4, 16)
SW_BLOCK = 128
X_HEADS = 4
X_HD = D_MODEL // X_HEADS
D_FF = 2816

MLA_W = MLA_HEADS * MLA_V
ML_W = ML_HEADS * ML_DV
SW_W = SW_HEADS * SW_HD
MLA_PAD_W = MLA_HEADS * LANES

IN_SIZES = (MLA_Q_LORA, MLA_KV_LORA, MLA_ROPE, ML_HEADS * ML_DK, ML_HEADS * ML_DK, ML_W, ML_W,
            ML_HEADS, ML_HEADS, SW_W, SW_W, SW_W)
IN_OFFS = tuple(int(v) for v in np.cumsum((0,) + IN_SIZES))

P_CQ = 0
P_CKV = 256
P_KROPE = 384
P_MLQK = 512
P_MLO = 1024
P_SWQ = 1280
P_SWK = 1536
P_SWV = 1792
P_TOTAL = 2048
ML_VT_ROWS = 80
ML_T_ROWS = ML_HEADS * ML_VT_ROWS + 16

TOK_TILE = 512
INPROJ_TILE = 1024
FLASH_TILE = 1024
FLASH_KEY_TILE = 256
MLA_VT_ROWS = 80
SWA_TILE = 2048
SWA_UNROLL = 4
FFN_CHUNKS = 2
VMEM_LIMIT = 56 * 1024 * 1024


def _rms(x, g):
    return x * lax.rsqrt(jnp.mean(x * x, axis=-1, keepdims=True) + EPS) * g


def _dot(a, b):
    return jnp.dot(a, b, preferred_element_type=F32)


def _dot_nt(a, b):
    return lax.dot_general(a, b, (((1,), (1,)), ((), ())), preferred_element_type=F32)


def _dot_tn(a, b):
    return lax.dot_general(a, b, (((0,), (0,)), ((), ())), preferred_element_type=F32)


def _params(*sem):
    return pltpu.CompilerParams(dimension_semantics=sem, vmem_limit_bytes=VMEM_LIMIT)


def _inproj_kernel(x_ref, g_ref, w_ref, wgt_ref, qg_ref, wuq_ref, kvg_ref, wk_ref, wvt_ref,
                   cm_ref, sm_ref, cs_ref, ss_ref,
                   q_ref, k_ref, vt_ref, mqk_ref, mv_ref, mo_ref, gr_ref, sq_ref, sk_ref, sv_ref):
    tm = x_ref.shape[0]
    h = _rms(x_ref[...], g_ref[...]).astype(BF16)

    def proj(a, b):
        return _dot(h, w_ref[:, a:b])

    lane = lax.broadcasted_iota(jnp.int32, (tm, LANES), 1)
    cm, sm = cm_ref[...], sm_ref[...]
    mla_first = lane < MLA_NOPE + MLA_ROPE // 2

    def rope_mla(t):
        rot = jnp.where(mla_first, pltpu.roll(t, LANES - MLA_ROPE // 2, 1), pltpu.roll(t, MLA_ROPE // 2, 1))
        return t * cm + rot * sm

    cs, ss = cs_ref[...], ss_ref[...]
    sw_first = (lane % SW_HD) < SW_ROT // 2

    def rope_sw(t):
        rot = jnp.where(sw_first, pltpu.roll(t, LANES - SW_ROT // 2, 1), pltpu.roll(t, SW_ROT // 2, 1))
        return t * cs + rot * ss

    cqn = _rms(proj(P_CQ, P_CQ + MLA_Q_LORA), qg_ref[...]).astype(BF16)
    q_scale = (MLA_NOPE + MLA_ROPE) ** -0.5 * LOG2_E
    for hh in range(MLA_HEADS):
        sl = slice(hh * LANES, (hh + 1) * LANES)
        q_ref[:, sl] = (rope_mla(_dot(cqn, wuq_ref[:, sl])) * q_scale).astype(BF16)
    kvn = _rms(proj(P_CKV, P_CKV + MLA_KV_LORA), kvg_ref[...]).astype(BF16)
    kpe = rope_mla(proj(P_KROPE, P_KROPE + LANES))
    one_col = (lax.broadcasted_iota(jnp.int32, (MLA_VT_ROWS, 1), 0) == MLA_V).astype(F32)
    for hh in range(MLA_HEADS):
        sl = slice(hh * LANES, (hh + 1) * LANES)
        k_ref[:, sl] = (_dot(kvn, wk_ref[:, sl]) + kpe).astype(BF16)
        rows = slice(hh * MLA_VT_ROWS, (hh + 1) * MLA_VT_ROWS)
        for u in range(tm // FLASH_KEY_TILE):
            toks = slice(u * FLASH_KEY_TILE, (u + 1) * FLASH_KEY_TILE)
            vt_ref[u, rows, :] = (_dot_nt(wvt_ref[rows, :], kvn[toks, :]) + one_col).astype(BF16)

    mqk_ref[...] = proj(P_MLQK, P_MLQK + 2 * ML_HEADS * ML_DK)
    mo_ref[...] = proj(P_MLO, P_MLO + ML_W)
    tr = _dot_nt(wgt_ref[...], h)
    n_v = ML_HEADS * ML_VT_ROWS
    v_row = lax.broadcasted_iota(jnp.int32, (n_v, 1), 0)
    mv_ref[...] = (tr[0:n_v, :] + ((v_row % ML_VT_ROWS) == ML_DV).astype(F32)).astype(BF16)
    gr_ref[...] = tr[n_v:n_v + 2 * ML_HEADS, :]

    for half in range(SW_W // LANES):
        sl = slice(half * LANES, (half + 1) * LANES)
        sq_ref[half] = rope_sw(proj(P_SWQ + half * LANES, P_SWQ + (half + 1) * LANES)) * (SW_HD ** -0.5 * LOG2_E)
        sk_ref[half] = rope_sw(proj(P_SWK + half * LANES, P_SWK + (half + 1) * LANES))
        sv_ref[half] = proj(P_SWV + half * LANES, P_SWV + (half + 1) * LANES)


def _inproj(x, g, w, wgt, qg, wuq, kvg, wk, wv, cm, sm, cs, ss, seq):
    t_all = x.shape[0]
    tm = INPROJ_TILE
    pos_tiles = seq // tm
    tok = lambda w_: pl.BlockSpec((tm, w_), lambda i: (i, 0))
    full = lambda a: pl.BlockSpec(a.shape, lambda i: (0,) * a.ndim)
    pos = pl.BlockSpec((tm, LANES), lambda i: (i % pos_tiles, 0))
    slabs = pl.BlockSpec((SW_W // LANES, tm, LANES), lambda i: (0, i, 0))
    out_shapes = (
        jax.ShapeDtypeStruct((t_all, MLA_PAD_W), BF16),
        jax.ShapeDtypeStruct((t_all, MLA_PAD_W), BF16),
        jax.ShapeDtypeStruct((t_all // FLASH_KEY_TILE, MLA_HEADS * MLA_VT_ROWS, FLASH_KEY_TILE), BF16),
        jax.ShapeDtypeStruct((t_all, 2 * ML_HEADS * ML_DK), F32),
        jax.ShapeDtypeStruct((t_all // seq, ML_HEADS * ML_VT_ROWS, seq), BF16),
        jax.ShapeDtypeStruct((t_all, ML_W), F32),
        jax.ShapeDtypeStruct((t_all // seq, 2 * ML_HEADS, seq), F32),
        jax.ShapeDtypeStruct((SW_W // LANES, t_all, LANES), F32),
        jax.ShapeDtypeStruct((SW_W // LANES, t_all, LANES), F32),
        jax.ShapeDtypeStruct((SW_W // LANES, t_all, LANES), F32),
    )
    out_specs = (tok(MLA_PAD_W), tok(MLA_PAD_W),
                 pl.BlockSpec((tm // FLASH_KEY_TILE, MLA_HEADS * MLA_VT_ROWS, FLASH_KEY_TILE), lambda i: (i, 0, 0)),
                 tok(2 * ML_HEADS * ML_DK),
                 pl.BlockSpec((None, ML_HEADS * ML_VT_ROWS, tm), lambda i: (i // pos_tiles, 0, i % pos_tiles)),
                 tok(ML_W), pl.BlockSpec((None, 2 * ML_HEADS, tm), lambda i: (i // pos_tiles, 0, i % pos_tiles)),
                 slabs, slabs, slabs)
    return pl.pallas_call(
        _inproj_kernel,
        grid=(t_all // tm,),
        in_specs=[tok(D_MODEL), full(g), full(w), full(wgt), full(qg), full(wuq), full(kvg), full(wk), full(wv),
                  pos, pos, pos, pos],
        out_specs=out_specs,
        out_shape=out_shapes,
        compiler_params=_params("parallel"),
        name="inproj",
    )(x, g, w, wgt, qg, wuq, kvg, wk, wv, cm, sm, cs, ss)


def _flash_kernel(q_ref, k_ref, vt_ref, o_ref, m_scr, acc_scr, s_scr, cmax_scr):
    tq = q_ref.shape[0]
    tk = FLASH_KEY_TILE
    per_q = tq // tk
    i = pl.program_id(2)
    n_tiles = (i + 1) * per_q
    qt = q_ref[...].astype(F32).T.astype(BF16)
    m_scr[...] = jnp.full(m_scr.shape, NEG, F32)
    acc_scr[...] = jnp.zeros(acc_scr.shape, F32)

    def scores(j, buf, masked):
        start = pl.multiple_of(j * tk, tk)
        st = _dot(k_ref[pl.ds(start, tk), :], qt)
        if masked:
            key = lax.broadcasted_iota(jnp.int32, (tk, tq), 0) + j * tk
            qry = lax.broadcasted_iota(jnp.int32, (tk, tq), 1) + i * tq
            st = jnp.where(key <= qry, st, NEG)
        s_scr[buf] = st
        cmax_scr[buf] = jnp.max(st, axis=0, keepdims=True)

    def accumulate(j, buf):
        m_old = m_scr[...]
        m_new = jnp.maximum(m_old, cmax_scr[buf])
        p = jnp.exp2(s_scr[buf] - m_new).astype(BF16)
        acc_scr[...] = acc_scr[...] * jnp.exp2(m_old - m_new) + _dot(vt_ref[j], p)
        m_scr[...] = m_new

    def pairs(pp, masks):
        for c, (mask_a, mask_b) in enumerate(masks):
            scores(2 * (pp + c) + 1, 1, mask_a)
            accumulate(2 * (pp + c), 0)
            scores(2 * (pp + c) + 2, 0, mask_b)
            accumulate(2 * (pp + c) + 1, 1)

    def finish():
        scores(n_tiles - 1, 1, True)
        accumulate(n_tiles - 2, 0)
        accumulate(n_tiles - 1, 1)
        acc = acc_scr[...]
        out_t = acc[0:MLA_V, :] / acc[MLA_V:MLA_V + 1, :]
        o_ref[...] = jnp.concatenate([out_t, jnp.zeros_like(out_t)], axis=0).T.astype(BF16)

    assert per_q == 4

    @pl.when(i == 0)
    def _():
        scores(0, 0, True)
        pairs(0, [(True, True)])
        finish()

    tail = [(False, False), (False, True), (True, True)]
    plain = [(False, False)] * 2
    n_eights = lax.shift_right_logical(jnp.maximum(i - 1, 0), 1)

    def body(g, carry):
        pairs(4 * g, plain + plain)
        return carry

    @pl.when((i > 0) & ((i & 1) == 1))
    def _():
        scores(0, 0, False)
        lax.fori_loop(0, n_eights, body, 0)
        pairs(4 * n_eights, tail)
        finish()

    @pl.when((i > 0) & ((i & 1) == 0))
    def _():
        scores(0, 0, False)
        lax.fori_loop(0, n_eights, body, 0)
        pairs(4 * n_eights, plain + tail)
        finish()


def _flash(q, k, vt):
    b, s, _ = q.shape
    tq, tk = FLASH_TILE, FLASH_KEY_TILE
    assert s % tq == 0 and tq % (2 * tk) == 0
    return pl.pallas_call(
        _flash_kernel,
        grid=(b, MLA_HEADS, s // tq),
        in_specs=[pl.BlockSpec((None, tq, LANES), lambda bb, hh, i: (bb, i, hh)),
                  pl.BlockSpec((None, s, LANES), lambda bb, hh, i: (bb, 0, hh)),
                  pl.BlockSpec((s // tk, MLA_VT_ROWS, tk), lambda bb, hh, i: (bb, hh, 0))],
        out_specs=pl.BlockSpec((None, tq, LANES), lambda bb, hh, i: (bb, i, hh)),
        out_shape=jax.ShapeDtypeStruct((b, s, MLA_PAD_W), BF16),
        scratch_shapes=[pltpu.VMEM((1, tq), F32), pltpu.VMEM((MLA_VT_ROWS, tq), F32),
                        pltpu.VMEM((2, tk, tq), F32), pltpu.VMEM((2, 1, tq), F32)],
        compiler_params=_params("parallel", "parallel", "arbitrary"),
        name="mla_flash",
    )(q, k, vt)


def _log_sigmoid(z):
    return jnp.minimum(z, 0.0) - jnp.log1p(jnp.exp(-jnp.abs(z)))


def _mlstm_kernel(qkc_ref, qkp_ref, vt_ref, o_ref, g_ref, cw_ref, cb_ref, bi_ref, bf_ref,
                  y_ref, c_scr, m_scr):
    L = ML_CHUNK
    nb = qkc_ref.shape[0]
    c = pl.program_id(0)

    @pl.when(c == 0)
    def _():
        c_scr[...] = jnp.zeros(c_scr.shape, F32)
        m_scr[...] = jnp.full(m_scr.shape, NEG, F32)

    src = lax.broadcasted_iota(jnp.int32, (L, L), 0)
    tgt = lax.broadcasted_iota(jnp.int32, (L, L), 1)
    causal = src <= tgt
    pick = (lax.broadcasted_iota(jnp.int32, (L, ML_HEADS * L), 0)
            == lax.broadcasted_iota(jnp.int32, (L, ML_HEADS * L), 1) // L).astype(BF16)

    streams = [(b, hh) for b in range(nb) for hh in range(ML_HEADS)]
    qk, b_all, li_all, x_col = [], [], [], []
    for b in range(nb):
        cur = qkc_ref[b]
        prev = jnp.where(c > 0, qkp_ref[b], 0.0)
        row_w = lax.broadcasted_iota(jnp.int32, cur.shape, 0)
        conv = cb_ref[...] + cw_ref[ML_CONV - 1:ML_CONV, :] * cur
        for sh in range(1, ML_CONV):
            shifted = jnp.where(row_w >= sh, pltpu.roll(cur, sh, 0), pltpu.roll(prev, sh, 0))
            conv = conv + cw_ref[ML_CONV - 1 - sh:ML_CONV - sh, :] * shifted
        qk.append(conv * jax.nn.sigmoid(conv))
        gates = g_ref[b]
        li = gates[0:ML_HEADS, :] + bi_ref[...]
        lf = _log_sigmoid(gates[ML_HEADS:, :] + bf_ref[...])
        lane_g = lax.broadcasted_iota(jnp.int32, lf.shape, 1)
        step = 1
        while step < L:
            lf = lf + jnp.where(lane_g >= step, pltpu.roll(lf, step, 1), 0.0)
            step *= 2
        li_all.append(li)
        b_all.append(lf)
        x = jnp.concatenate([li - lf, jnp.zeros((L - ML_HEADS, L), F32)], axis=0).T
        spread = jnp.zeros((L, ML_HEADS * L), F32)
        for _ in range(3):
            part = x.astype(BF16)
            spread = spread + _dot(part, pick)
            x = x - part.astype(F32)
        x_col.append(spread)

    st = []
    for n, (b, hh) in enumerate(streams):
        b_row = b_all[b][hh:hh + 1, :]
        li_row = li_all[b][hh:hh + 1, :]
        m_prev = m_scr[n:n + 1, 0:1]
        d_t = jnp.where(causal, x_col[b][:, hh * L:(hh + 1) * L] + b_row, NEG)
        inter = b_row + m_prev
        m_t = jnp.maximum(inter, jnp.max(d_t, axis=0, keepdims=True))
        b_last = b_row[:, L - 1:L]
        g = b_last - b_row + li_row
        m_new = jnp.maximum(b_last + m_prev, jnp.max(g, axis=1, keepdims=True))
        kf = qk[b][:, (ML_HEADS + hh) * ML_DK:(ML_HEADS + hh + 1) * ML_DK] * ML_DK ** -0.5
        vt = vt_ref[b, hh * ML_VT_ROWS:(hh + 1) * ML_VT_ROWS, :]
        st.append(dict(
            qh=qk[b][:, hh * ML_DK:(hh + 1) * ML_DK].astype(BF16),
            kh=kf.astype(BF16), vt=vt,
            vw=(vt.astype(F32) * jnp.exp(g - m_new)).astype(BF16),
            c_t=c_scr[n],
            w_intra=jnp.exp(d_t - m_t), w_inter=jnp.exp(inter - m_t), floor=jnp.exp(-m_t),
            decay=jnp.exp(b_last + m_prev - m_new), m_new=m_new))

    for s_ in st:
        s_["sraw"] = _dot_nt(s_["kh"], s_["qh"])
        s_["cross"] = _dot_nt(s_["c_t"].astype(BF16), s_["qh"])
    for s_ in st:
        s_["s"] = s_["sraw"] * s_["w_intra"]
    for s_ in st:
        s_["intra"] = _dot(s_["vt"], s_["s"].astype(BF16))
        s_["kv"] = _dot(s_["vw"], s_["kh"])
    h_t = []
    for n, s_ in enumerate(st):
        num = s_["intra"][:ML_DV, :] + s_["w_inter"] * s_["cross"][:ML_DV, :]
        nq = jnp.sum(s_["s"], axis=0, keepdims=True) + s_["w_inter"] * s_["cross"][ML_DV:ML_DV + 1, :]
        h_t.append(num / jnp.maximum(jnp.abs(nq), s_["floor"]))
        c_scr[n] = s_["decay"] * s_["c_t"] + s_["kv"]
        m_scr[n:n + 1, :] = jnp.broadcast_to(s_["m_new"], (1, LANES))
    for b in range(nb):
        hout = jnp.concatenate(h_t[b * ML_HEADS:(b + 1) * ML_HEADS], axis=0).T
        y_ref[b] = (jax.nn.sigmoid(o_ref[b]) * hout).astype(BF16)


def _mlstm(qk_pre, vt, o_pre, gates_t, conv_w, conv_b, b_i, b_f):
    batch, seq, wqk = qk_pre.shape
    L = ML_CHUNK
    full = lambda a: pl.BlockSpec(a.shape, lambda c: (0,) * a.ndim)
    chunk = lambda w_: pl.BlockSpec((batch, L, w_), lambda c: (0, c, 0))
    lanes = lambda r: pl.BlockSpec((batch, r, L), lambda c: (0, 0, c))
    return pl.pallas_call(
        _mlstm_kernel,
        grid=(seq // L,),
        in_specs=[chunk(wqk),
                  pl.BlockSpec((batch, L, wqk), lambda c: (0, jnp.maximum(c - 1, 0), 0)),
                  lanes(ML_HEADS * ML_VT_ROWS), chunk(ML_W), lanes(2 * ML_HEADS),
                  full(conv_w), full(conv_b), full(b_i), full(b_f)],
        out_specs=chunk(ML_W),
        out_shape=jax.ShapeDtypeStruct((batch, seq, ML_W), BF16),
        scratch_shapes=[pltpu.VMEM((batch * ML_HEADS, ML_VT_ROWS, ML_DK), F32),
                        pltpu.VMEM((batch * ML_HEADS, LANES), F32)],
        compiler_params=_params("arbitrary"),
        name="mlstm",
    )(qk_pre, qk_pre, vt, o_pre, gates_t, conv_w, conv_b, b_i, b_f)


def _swa_kernel(q_ref, k_ref, v_ref, y_ref, kk_scr, vv_scr, o_scr, l_scr):
    blk, tile = SW_BLOCK, SWA_TILE
    n_slab = SW_W // LANES
    n = pl.program_id(1)

    @pl.when(n == 0)
    def _():
        kk_scr[:, 0:tile, :] = jnp.zeros((n_slab, tile, LANES), F32)
        vv_scr[:, 0:tile, :] = jnp.zeros((n_slab, tile, LANES), F32)

    @pl.when(n > 0)
    def _():
        kk_scr[:, 0:tile, :] = kk_scr[:, tile:2 * tile, :]
        vv_scr[:, 0:tile, :] = vv_scr[:, tile:2 * tile, :]

    kk_scr[:, tile:2 * tile, :] = k_ref[...]
    vv_scr[:, tile:2 * tile, :] = v_ref[...]

    key = lax.broadcasted_iota(jnp.int32, (2 * blk, blk), 0)
    qry = lax.broadcasted_iota(jnp.int32, (2 * blk, blk), 1)
    dim =lax.broadcasted_iota(jnp.int32, (LANES, blk), 0)
    head_rows = [dim < SW_HD, dim >= SW_HD]

    for g, dil in enumerate(SW_DILATIONS):
        per_res = tile // (blk * dil)
        shift = per_res.bit_length() - 1

        def block_pair(it, carry, g=g, dil=dil, per_res=per_res, shift=shift):
            def rows(start):
                return pl.ds(start, blk, stride=dil) if dil > 1 else pl.ds(pl.multiple_of(start, blk), blk)

            work = []
            for u in range(SWA_UNROLL):
                bi = SWA_UNROLL * it + u
                r = lax.shift_right_logical(bi, shift)
                j = bi & (per_res - 1)
                q0 = r + j * (blk * dil)
                lowest = jnp.where((n == 0) & (j == 0), blk, 0)
                ok = (key >= jnp.maximum(qry, lowest)) & (key <= qry + blk)
                for sl in range(n_slab):
                    prev, cur = rows(tile + q0 - blk * dil), rows(tile + q0)
                    kb = jnp.concatenate([kk_scr[sl, prev, :], kk_scr[sl, cur, :]], axis=0)
                    vb = jnp.concatenate([vv_scr[sl, prev, :], vv_scr[sl, cur, :]], axis=0)
                    work.append(dict(q0=q0, sl=sl, ok=ok, qt=q_ref[sl, rows(q0), :].T,
                                     kb=kb.astype(BF16), vt=vb.T.astype(BF16)))
            for w in work:
                w["s"] = [_dot(w["kb"], jnp.where(head_rows[hh], w["qt"], 0.0).astype(BF16)) for hh in range(2)]
            for w in work:
                w["e"], w["den"], w["lse"] = [], [], []
                for hh in range(2):
                    s = jnp.where(w["ok"], w["s"][hh], NEG)
                    cmax = jnp.max(s, axis=0, keepdims=True)
                    e = jnp.exp2(s - cmax)
                    den = jnp.sum(e, axis=0, keepdims=True)
                    w["e"].append(e.astype(BF16))
                    w["den"].append(den)
                    w["lse"].append(cmax + jnp.log2(den))
            for w in work:
                ot = [_dot(w["vt"][hh * SW_HD:(hh + 1) * SW_HD, :], w["e"][hh]) / w["den"][hh] for hh in range(2)]
                lt = [jnp.broadcast_to(w["lse"][hh], (SW_HD, blk)) for hh in range(2)]
                o_scr[g, w["sl"], rows(w["q0"]), :] = jnp.concatenate(ot, axis=0).T
                l_scr[g, w["sl"], rows(w["q0"]), :] = jnp.concatenate(lt, axis=0).T
            return carry

        lax.fori_loop(0, tile // blk // SWA_UNROLL, block_pair, 0)

    for sl in range(n_slab):
        ls = [l_scr[g, sl] for g in range(len(SW_DILATIONS))]
        mx = functools.reduce(jnp.maximum, ls)
        ws = [jnp.exp2(l - mx) for l in ls]
        num = functools.reduce(lambda a, b: a + b, [w * o_scr[g, sl] for g, w in enumerate(ws)])
        y_ref[:, sl * LANES:(sl + 1) * LANES] = (num / functools.reduce(lambda a, b: a + b, ws)).astype(BF16)


def _swa(q, k, v, batch, seq):
    tile = SWA_TILE
    n_slab = SW_W // LANES
    per_batch = seq // tile
    spec = pl.BlockSpec((n_slab, tile, LANES), lambda bb, n: (0, bb * per_batch + n, 0))
    n_br = len(SW_DILATIONS)
    return pl.pallas_call(
        _swa_kernel,
        grid=(batch, per_batch),
        in_specs=[spec, spec, spec],
        out_specs=pl.BlockSpec((tile, SW_W), lambda bb, n: (bb * per_batch + n, 0)),
        out_shape=jax.ShapeDtypeStruct((batch * seq, SW_W), BF16),
        scratch_shapes=[pltpu.VMEM((n_slab, 2 * tile, LANES), F32), pltpu.VMEM((n_slab, 2 * tile, LANES), F32),
                        pltpu.VMEM((n_br, n_slab, tile, LANES), F32), pltpu.VMEM((n_br, n_slab, tile, LANES), F32)],
        compiler_params=_params("parallel", "arbitrary"),
        name="swa",
    )(q, k, v)


def _outproj_kernel(x_ref, ya_ref, yb_ref, yc_ref, wa_ref, wb_ref, wc_ref, g_ref, out_ref):
    y = _dot(ya_ref[...], wa_ref[...]) + _dot(yb_ref[...], wb_ref[...]) + _dot(yc_ref[...], wc_ref[...])
    out_ref[...] = x_ref[...] + _rms(y, g_ref[...])


def _outproj(x, ya, yb, yc, wa, wb, wc, g):
    t_all = x.shape[0]
    tm = TOK_TILE
    tok = lambda w_: pl.BlockSpec((tm, w_), lambda i: (i, 0))
    full = lambda a: pl.BlockSpec(a.shape, lambda i: (0,) * a.ndim)
    return pl.pallas_call(
        _outproj_kernel,
        grid=(t_all // tm,),
        in_specs=[tok(D_MODEL), tok(MLA_PAD_W), tok(ML_W), tok(SW_W), full(wa), full(wb), full(wc), full(g)],
        out_specs=tok(D_MODEL),
        out_shape=jax.ShapeDtypeStruct((t_all, D_MODEL), F32),
        compiler_params=_params("parallel"),
        name="outproj",
    )(x, ya, yb, yc, wa, wb, wc, g)


def _memkv_kernel(mem_ref, g_ref, w_ref, kv_ref):
    kv_ref[...] = _dot(_rms(mem_ref[...], g_ref[...]).astype(BF16), w_ref[...]).astype(BF16)


def _memkv(mem, g, w):
    n = mem.shape[0]
    full = lambda a: pl.BlockSpec(a.shape, lambda i: (0,) * a.ndim)
    return pl.pallas_call(
        _memkv_kernel,
        grid=(1,),
        in_specs=[full(mem), full(g), full(w)],
        out_specs=pl.BlockSpec((n, 2 * D_MODEL), lambda i: (0, 0)),
        out_shape=jax.ShapeDtypeStruct((n, 2 * D_MODEL), BF16),
        compiler_params=_params("arbitrary"),
        name="memkv",
    )(mem, g, w)


def _xattn_kernel(x_ref, gpre_ref, wq_ref, kv_ref, wo_ref, gpost_ref, out_ref, o_scr):
    x = x_ref[...]
    h = _rms(x, gpre_ref[...]).astype(BF16)
    q = (_dot(h, wq_ref[...]) * X_HD ** -0.5).astype(BF16)
    for hh in range(X_HEADS):
        hs = slice(hh * X_HD, (hh + 1) * X_HD)
        s = _dot_nt(q[:, hs], kv_ref[:, hs])
        e = jnp.exp(s - jnp.max(s, axis=1, keepdims=True))
        den = jnp.sum(e, axis=1, keepdims=True)
        o_scr[:, hs] = (_dot(e.astype(BF16), kv_ref[:, D_MODEL + hh * X_HD:D_MODEL + (hh + 1) * X_HD]) / den).astype(BF16)
    y = _dot(o_scr[...], wo_ref[...])
    out_ref[...] = x + _rms(y, gpost_ref[...])


def _xattn(x, gpre, wq, kv, wo, gpost, seq):
    t_all = x.shape[0]
    tm = TOK_TILE
    per_batch = seq // tm
    tok = pl.BlockSpec((tm, D_MODEL), lambda i: (i, 0))
    full = lambda a: pl.BlockSpec(a.shape, lambda i: (0,) * a.ndim)
    return pl.pallas_call(
        _xattn_kernel,
        grid=(t_all // tm,),
        in_specs=[tok, full(gpre), full(wq), pl.BlockSpec((N_MEM, 2 * D_MODEL), lambda i: (i // per_batch, 0)),
                  full(wo), full(gpost)],
        out_specs=tok,
        out_shape=jax.ShapeDtypeStruct((t_all, D_MODEL), F32),
        scratch_shapes=[pltpu.VMEM((tm, D_MODEL), BF16)],
        compiler_params=_params("parallel"),
        name="xattn",
    )(x, gpre, wq, kv, wo, gpost)


def _ffn_kernel(x_ref, gpre_ref, wgu_ref, wd_ref, gpost_ref, out_ref):
    x = x_ref[...]
    h = _rms(x, gpre_ref[...]).astype(BF16)
    tf = D_FF // FFN_CHUNKS
    y = None
    for f in range(FFN_CHUNKS):
        gate = _dot(h, wgu_ref[:, f * tf:(f + 1) * tf])
        up = _dot(h, wgu_ref[:, D_FF + f * tf:D_FF + (f + 1) * tf])
        part = _dot((gate * jax.nn.sigmoid(gate) * up).astype(BF16), wd_ref[f * tf:(f + 1) * tf, :])
        y = part if y is None else y + part
    out_ref[...] = x + _rms(y, gpost_ref[...])


def _ffn(x, gpre, w_gate_up, w_down, gpost):
    t_all = x.shape[0]
    tm = TOK_TILE
    tok = pl.BlockSpec((tm, D_MODEL), lambda i: (i, 0))
    full = lambda a: pl.BlockSpec(a.shape, lambda i: (0,) * a.ndim)
    return pl.pallas_call(
        _ffn_kernel,
        grid=(t_all // tm,),
        in_specs=[tok, full(gpre), full(w_gate_up), full(w_down), full(gpost)],
        out_specs=tok,
        out_shape=jax.ShapeDtypeStruct((t_all, D_MODEL), F32),
        compiler_params=_params("parallel"),
        name="ffn",
    )(x, gpre, w_gate_up, w_down, gpost)


def _rope_tables(seq):
    pos = jnp.arange(seq, dtype=F32)[:, None]

    def cos_sin(d_rot):
        inv = ROPE_THETA ** (-jnp.arange(0, d_rot, 2, dtype=F32) / d_rot)
        ang = pos * inv[None, :]
        return jnp.cos(ang), jnp.sin(ang)

    c, s = cos_sin(MLA_ROPE)
    ones = jnp.ones((seq, MLA_NOPE), F32)
    tail = jnp.ones((seq, LANES - MLA_NOPE - MLA_ROPE), F32)
    cm = jnp.concatenate([ones, c, c, tail], axis=1)
    sm = jnp.concatenate([0 * ones, -s, s, 0 * tail], axis=1)
    c, s = cos_sin(SW_ROT)
    rest = jnp.ones((seq, SW_HD - SW_ROT), F32)
    cs = jnp.tile(jnp.concatenate([c, c, rest], axis=1), (1, LANES // SW_HD))
    ss = jnp.tile(jnp.concatenate([-s, s, 0 * rest], axis=1), (1, LANES // SW_HD))
    return cm, sm, cs, ss


def _prep_layer(w_in, w_uq, w_ukv, w_out):
    o = IN_OFFS
    col = lambda i: w_in[:, o[i]:o[i + 1]]
    z = lambda n: jnp.zeros((D_MODEL, n), F32)
    w_p = jnp.concatenate([
        col(0), col(1),
        z(MLA_NOPE), col(2), z(LANES - MLA_NOPE - MLA_ROPE),
        col(3), col(4), col(6), col(9), col(10), col(11)], axis=1)
    assert w_p.shape[1] == P_TOTAL
    mlv = jnp.pad(col(5).reshape(D_MODEL, ML_HEADS, ML_DV), ((0, 0), (0, 0), (0, ML_VT_ROWS - ML_DV)))
    w_gates_t = jnp.concatenate([mlv.reshape(D_MODEL, ML_HEADS * ML_VT_ROWS), col(7), col(8),
                                 z(ML_T_ROWS - ML_HEADS * ML_VT_ROWS - 2 * ML_HEADS)], axis=1).T
    uq = w_uq.reshape(MLA_Q_LORA, MLA_HEADS, MLA_NOPE + MLA_ROPE)
    uq = jnp.pad(uq, ((0, 0), (0, 0), (0, LANES - MLA_NOPE - MLA_ROPE))).reshape(MLA_Q_LORA, MLA_PAD_W)
    ukv = w_ukv.reshape(MLA_KV_LORA, MLA_HEADS, MLA_NOPE + MLA_V)
    pad_half = lambda a: jnp.pad(a, ((0, 0), (0, 0), (0, LANES - a.shape[2]))).reshape(MLA_KV_LORA, MLA_PAD_W)
    wk = pad_half(ukv[:, :, :MLA_NOPE])
    wv = jnp.pad(ukv[:, :, MLA_NOPE:], ((0, 0), (0, 0), (0, MLA_VT_ROWS - MLA_V)))
    wv = wv.reshape(MLA_KV_LORA, MLA_HEADS * MLA_VT_ROWS).T
    wa = w_out[:MLA_W].reshape(MLA_HEADS, MLA_V, D_MODEL)
    wa = jnp.pad(wa, ((0, 0), (0, LANES - MLA_V), (0, 0))).reshape(MLA_PAD_W, D_MODEL)
    wb, wc = w_out[MLA_W:MLA_W + ML_W], w_out[MLA_W + ML_W:]
    bf = lambda a: a.astype(BF16)
    return bf(w_p), bf(w_gates_t), bf(uq), bf(wk), bf(wv), bf(wa), bf(wb), bf(wc)


def kernel(x, mem, w_in, mla_q_norm, mla_w_uq, mla_kv_norm, mla_w_ukv, ml_conv_w, ml_conv_b, ml_b_i, ml_b_f,
           w_out, x_w_q, x_w_kv, x_w_o, w_gate_up, w_down, norm_mix_pre, norm_mix_post, norm_mem,
           norm_x_pre, norm_x_post, norm_ffn_pre, norm_ffn_post):
    batch, seq, _ = x.shape
    depth = w_in.shape[0]
    t_all = batch * seq
    assert seq % SWA_TILE == 0 and seq % TOK_TILE == 0 and seq % INPROJ_TILE == 0
    assert INPROJ_TILE % FLASH_KEY_TILE == 0
    assert SWA_TILE == SW_DILATIONS[-1] * SW_BLOCK
    cm, sm, cs, ss = _rope_tables(seq)
    row = lambda a: a.reshape(1, -1)
    colv = lambda a: a.reshape(-1, 1)
    xt = x.reshape(t_all, D_MODEL)
    mem2 = mem.reshape(batch * N_MEM, D_MODEL)
    for l in range(depth):
        w_p, w_gt, uq, wk, wv, wa, wb, wc = _prep_layer(w_in[l], mla_w_uq[l], mla_w_ukv[l], w_out[l])
        q, k, v, mqk, mv, mo, gates_t, sq, sk, sv = _inproj(
            xt, row(norm_mix_pre[l]), w_p, w_gt, row(mla_q_norm[l]), uq, row(mla_kv_norm[l]), wk, wv,
            cm, sm, cs, ss, seq)
        b3 = lambda a: a.reshape(batch, seq, a.shape[-1])
        ya = _flash(b3(q), b3(k), v).reshape(t_all, MLA_PAD_W)
        yb = _mlstm(b3(mqk), mv, b3(mo), gates_t, ml_conv_w[l], row(ml_conv_b[l]), colv(ml_b_i[l]),
                    colv(ml_b_f[l])).reshape(t_all, ML_W)
        yc = _swa(sq, sk, sv, batch, seq)
        xt = _outproj(xt, ya, yb, yc, wa, wb, wc, row(norm_mix_post[l]))
        kv = _memkv(mem2, row(norm_mem[l]), x_w_kv[l].astype(BF16))
        xt = _xattn(xt, row(norm_x_pre[l]), x_w_q[l].astype(BF16), kv, x_w_o[l].astype(BF16),
                    row(norm_x_post[l]), seq)
        xt = _ffn(xt, row(norm_ffn_pre[l]), w_gate_up[l].astype(BF16), w_down[l].astype(BF16),
                  row(norm_ffn_post[l]))
    return xt.reshape(batch, seq, D_MODEL)
```

```python
import functools

import jax
import jax.numpy as jnp
import numpy as np
from jax import lax
from jax.experimental import pallas as pl
from jax.experimental.pallas import tpu as pltpu

F32 = jnp.float32
BF16 = jnp.bfloat16

D_MODEL = 1024
N_MEM = 256
ROPE_THETA = 500000.0
EPS = 1e-6
NEG = -1e30
LOG2_E = 1.4426950408889634
LANES = 128

MLA_HEADS, MLA_NOPE, MLA_ROPE, MLA_V = 8, 64, 32, 64
MLA_Q_LORA, MLA_KV_LORA = 256, 128
ML_HEADS, ML_DK, ML_DV, ML_CONV, ML_CHUNK = 4, 64, 64, 4, 128
SW_HEADS, SW_HD = 4, 64
SW_ROT = SW_HD // 4
SW_DILATIONS = (1, 4, 16)
SW_BLOCK = 128
X_HEADS = 4
X_HD = D_MODEL // X_HEADS
D_FF = 2816

MLA_W = MLA_HEADS * MLA_V
ML_W = ML_HEADS * ML_DV
SW_W = SW_HEADS * SW_HD
MLA_PAD_W = MLA_HEADS * LANES

IN_SIZES = (MLA_Q_LORA, MLA_KV_LORA, MLA_ROPE, ML_HEADS * ML_DK, ML_HEADS * ML_DK, ML_W, ML_W,
            ML_HEADS, ML_HEADS, SW_W, SW_W, SW_W)
IN_OFFS = tuple(int(v) for v in np.cumsum((0,) + IN_SIZES))

P_CQ = 0
P_CKV = 256
P_KROPE = 384
P_MLQK = 512
P_MLO = 1024
P_SWQ = 1280
P_SWK = 1536
P_SWV = 1792
P_TOTAL = 2048
ML_VT_ROWS = 80
ML_T_ROWS = ML_HEADS * ML_VT_ROWS + 16

TOK_TILE = 512
WIDE_TILE = 1024
FLASH_TILE = 1024
FLASH_KEY_TILE = 256
MLA_VT_ROWS = 80
SWA_TILE = 2048
SWA_UNROLL = 4
FFN_CHUNKS = 2
VMEM_LIMIT = 56 * 1024 * 1024


def _rms(x, g):
    return x * lax.rsqrt(jnp.mean(x * x, axis=-1, keepdims=True) + EPS) * g


def _dot(a, b):
    return jnp.dot(a, b, preferred_element_type=F32)


def _dot_nt(a, b):
    return lax.dot_general(a, b, (((1,), (1,)), ((), ())), preferred_element_type=F32)


def _dot_tn(a, b):
    return lax.dot_general(a, b, (((0,), (0,)), ((), ())), preferred_element_type=F32)


def _params(*sem):
    return pltpu.CompilerParams(dimension_semantics=sem, vmem_limit_bytes=VMEM_LIMIT)


def _inproj_kernel(x_ref, g_ref, w_ref, wgt_ref, qg_ref, wuq_ref, kvg_ref, wk_ref, wvt_ref,
                   cm_ref, sm_ref, cs_ref, ss_ref,
                   q_ref, k_ref, vt_ref, mqk_ref, mv_ref, mo_ref, gr_ref, sq_ref, sk_ref, sv_ref):
    tm = x_ref.shape[0]
    h = _rms(x_ref[...], g_ref[...]).astype(BF16)

    def proj(a, b):
        return _dot(h, w_ref[:, a:b])

    lane = lax.broadcasted_iota(jnp.int32, (tm, LANES), 1)
    cm, sm = cm_ref[...], sm_ref[...]
    mla_first = lane < MLA_NOPE + MLA_ROPE // 2

    def rope_mla(t):
        rot = jnp.where(mla_first, pltpu.roll(t, LANES - MLA_ROPE // 2, 1), pltpu.roll(t, MLA_ROPE // 2, 1))
        return t * cm + rot * sm

    cs, ss = cs_ref[...], ss_ref[...]
    sw_first = (lane % SW_HD) < SW_ROT // 2

    def rope_sw(t):
        rot = jnp.where(sw_first, pltpu.roll(t, LANES - SW_ROT // 2, 1), pltpu.roll(t, SW_ROT // 2, 1))
        return t * cs + rot * ss

    cqn = _rms(proj(P_CQ, P_CQ + MLA_Q_LORA), qg_ref[...]).astype(BF16)
    q_scale = (MLA_NOPE + MLA_ROPE) ** -0.5 * LOG2_E
    for hh in range(MLA_HEADS):
        sl = slice(hh * LANES, (hh + 1) * LANES)
        q_ref[hh] = (rope_mla(_dot(cqn, wuq_ref[:, sl])) * q_scale).astype(BF16)
    kvn = _rms(proj(P_CKV, P_CKV + MLA_KV_LORA), kvg_ref[...]).astype(BF16)
    kpe = rope_mla(proj(P_KROPE, P_KROPE + LANES))
    one_col = (lax.broadcasted_iota(jnp.int32, (MLA_VT_ROWS, 1), 0) == MLA_V).astype(F32)
    for hh in range(MLA_HEADS):
        sl = slice(hh * LANES, (hh + 1) * LANES)
        k_ref[hh] = (_dot(kvn, wk_ref[:, sl]) + kpe).astype(BF16)
        rows = slice(hh * MLA_VT_ROWS, (hh + 1) * MLA_VT_ROWS)
        for u in range(tm // FLASH_KEY_TILE):
            toks = slice(u * FLASH_KEY_TILE, (u + 1) * FLASH_KEY_TILE)
            vt_ref[hh, u] = (_dot_nt(wvt_ref[rows, :], kvn[toks, :]) + one_col).astype(BF16)

    mqk_ref[...] = proj(P_MLQK, P_MLQK + 2 * ML_HEADS * ML_DK)
    mo_ref[...] = proj(P_MLO, P_MLO + ML_W)
    tr = _dot_nt(wgt_ref[...], h)
    n_v = ML_HEADS * ML_VT_ROWS
    v_row = lax.broadcasted_iota(jnp.int32, (n_v, 1), 0)
    mv_ref[...] = (tr[0:n_v, :] + ((v_row % ML_VT_ROWS) == ML_DV).astype(F32)).astype(BF16)
    gr_ref[...] = tr[n_v:n_v + 2 * ML_HEADS, :]

    for half in range(SW_W // LANES):
        sl = slice(half * LANES, (half + 1) * LANES)
        sq_ref[half] = rope_sw(proj(P_SWQ + half * LANES, P_SWQ + (half + 1) * LANES)) * (SW_HD ** -0.5 * LOG2_E)
        sk_ref[half] = rope_sw(proj(P_SWK + half * LANES, P_SWK + (half + 1) * LANES))
        sv_ref[half] = proj(P_SWV + half * LANES, P_SWV + (half + 1) * LANES)


def _inproj(x, g, w, wgt, qg, wuq, kvg, wk, wv, cm, sm, cs, ss, seq):
    t_all = x.shape[0]
    tm = WIDE_TILE
    pos_tiles = seq // tm
    tok = lambda w_: pl.BlockSpec((tm, w_), lambda i: (i, 0))
    full = lambda a: pl.BlockSpec(a.shape, lambda i: (0,) * a.ndim)
    pos = pl.BlockSpec((tm, LANES), lambda i: (i % pos_tiles, 0))
    slabs = pl.BlockSpec((SW_W // LANES, tm, LANES), lambda i: (0, i, 0))
    out_shapes = (
        jax.ShapeDtypeStruct((MLA_HEADS, t_all, LANES), BF16),
        jax.ShapeDtypeStruct((MLA_HEADS, t_all, LANES), BF16),
        jax.ShapeDtypeStruct((MLA_HEADS, t_all // FLASH_KEY_TILE, MLA_VT_ROWS, FLASH_KEY_TILE), BF16),
        jax.ShapeDtypeStruct((t_all, 2 * ML_HEADS * ML_DK), F32),
        jax.ShapeDtypeStruct((t_all // seq, ML_HEADS * ML_VT_ROWS, seq), BF16),
        jax.ShapeDtypeStruct((t_all, ML_W), F32),
        jax.ShapeDtypeStruct((t_all // seq, 2 * ML_HEADS, seq), F32),
        jax.ShapeDtypeStruct((SW_W // LANES, t_all, LANES), F32),
        jax.ShapeDtypeStruct((SW_W // LANES, t_all, LANES), F32),
        jax.ShapeDtypeStruct((SW_W // LANES, t_all, LANES), F32),
    )
    heads = pl.BlockSpec((MLA_HEADS, tm, LANES), lambda i: (0, i, 0))
    out_specs = (heads, heads,
                 pl.BlockSpec((MLA_HEADS, tm // FLASH_KEY_TILE, MLA_VT_ROWS, FLASH_KEY_TILE), lambda i: (0, i, 0, 0)),
                 tok(2 * ML_HEADS * ML_DK),
                 pl.BlockSpec((None, ML_HEADS * ML_VT_ROWS, tm), lambda i: (i // pos_tiles, 0, i % pos_tiles)),
                 tok(ML_W), pl.BlockSpec((None, 2 * ML_HEADS, tm), lambda i: (i // pos_tiles, 0, i % pos_tiles)),
                 slabs, slabs, slabs)
    return pl.pallas_call(
        _inproj_kernel,
        grid=(t_all // tm,),
        in_specs=[tok(D_MODEL), full(g), full(w), full(wgt), full(qg), full(wuq), full(kvg), full(wk), full(wv),
                  pos, pos, pos, pos],
        out_specs=out_specs,
        out_shape=out_shapes,
        compiler_params=_params("parallel"),
        name="inproj",
    )(x, g, w, wgt, qg, wuq, kvg, wk, wv, cm, sm, cs, ss)


def _flash_kernel(q_ref, k_ref, vt_ref, o_ref, m_scr, acc_scr, s_scr, cmax_scr):
    tq = q_ref.shape[0]
    tk = FLASH_KEY_TILE
    per_q = tq // tk
    i = pl.program_id(2)
    n_tiles = (i + 1) * per_q
    qt = q_ref[...].astype(F32).T.astype(BF16)
    m_scr[...] = jnp.full(m_scr.shape, NEG, F32)
    acc_scr[...] = jnp.zeros(acc_scr.shape, F32)

    def scores(j, buf, masked):
        start = pl.multiple_of(j * tk, tk)
        st = _dot(k_ref[pl.ds(start, tk), :], qt)
        if masked:
            key = lax.broadcasted_iota(jnp.int32, (tk, tq), 0) + j * tk
            qry = lax.broadcasted_iota(jnp.int32, (tk, tq), 1) + i * tq
            st = jnp.where(key <= qry, st, NEG)
        s_scr[buf] = st
        cmax_scr[buf] = jnp.max(st, axis=0, keepdims=True)

    def accumulate(j, buf):
        m_old = m_scr[...]
        m_new = jnp.maximum(m_old, cmax_scr[buf])
        p = jnp.exp2(s_scr[buf] - m_new).astype(BF16)
        acc_scr[...] = acc_scr[...] * jnp.exp2(m_old - m_new) + _dot(vt_ref[j], p)
        m_scr[...] = m_new

    def pairs(pp, masks):
        for c, (mask_a, mask_b) in enumerate(masks):
            scores(2 * (pp + c) + 1, 1, mask_a)
            accumulate(2 * (pp + c), 0)
            scores(2 * (pp + c) + 2, 0, mask_b)
            accumulate(2 * (pp + c) + 1, 1)

    def finish():
        scores(n_tiles - 1, 1, True)
        accumulate(n_tiles - 2, 0)
        accumulate(n_tiles - 1, 1)
        acc = acc_scr[...]
        out_t = acc[0:MLA_V, :] / acc[MLA_V:MLA_V + 1, :]
        o_ref[...] = jnp.concatenate([out_t, jnp.zeros_like(out_t)], axis=0).T.astype(BF16)

    assert per_q == 4

    @pl.when(i == 0)
    def _():
        scores(0, 0, True)
        pairs(0, [(True, True)])
        finish()

    tail = [(False, False), (False, True), (True, True)]
    plain = [(False, False)] * 2
    n_eights = lax.shift_right_logical(jnp.maximum(i - 1, 0), 1)

    def body(g, carry):
        pairs(4 * g, plain + plain)
        return carry

    @pl.when((i > 0) & ((i & 1) == 1))
    def _():
        scores(0, 0, False)
        lax.fori_loop(0, n_eights, body, 0)
        pairs(4 * n_eights, tail)
        finish()

    @pl.when((i > 0) & ((i & 1) == 0))
    def _():
        scores(0, 0, False)
        lax.fori_loop(0, n_eights, body, 0)
        pairs(4 * n_eights, plain + tail)
        finish()


def _flash(q, k, vt, b, s):
    tq, tk = FLASH_TILE, FLASH_KEY_TILE
    assert s % tq == 0 and tq % (2 * tk) == 0
    nq = s // tq
    rows = pl.BlockSpec((None, tq, LANES), lambda bb, hh, i: (hh, bb * nq + i, 0))
    return pl.pallas_call(
        _flash_kernel,
        grid=(b, MLA_HEADS, nq),
        in_specs=[rows,
                  pl.BlockSpec((None, s, LANES), lambda bb, hh, i: (hh, bb, 0)),
                  pl.BlockSpec((None, s // tk, MLA_VT_ROWS, tk), lambda bb, hh, i: (hh, bb, 0, 0))],
        out_specs=rows,
        out_shape=jax.ShapeDtypeStruct((MLA_HEADS, b * s, LANES), BF16),
        scratch_shapes=[pltpu.VMEM((1, tq), F32), pltpu.VMEM((MLA_VT_ROWS, tq), F32),
                        pltpu.VMEM((2, tk, tq), F32), pltpu.VMEM((2, 1, tq), F32)],
        compiler_params=_params("parallel", "parallel", "arbitrary"),
        name="mla_flash",
    )(q, k, vt)


def _log_sigmoid(z):
    return jnp.minimum(z, 0.0) - jnp.log1p(jnp.exp(-jnp.abs(z)))


def _mlstm_kernel(qkc_ref, qkp_ref, vt_ref, o_ref, g_ref, cw_ref, cb_ref, bi_ref, bf_ref,
                  y_ref, c_scr, m_scr):
    L = ML_CHUNK
    nb = qkc_ref.shape[0]
    c = pl.program_id(0)

    @pl.when(c == 0)
    def _():
        c_scr[...] = jnp.zeros(c_scr.shape, F32)
        m_scr[...] = jnp.full(m_scr.shape, NEG, F32)

    src = lax.broadcasted_iota(jnp.int32, (L, L), 0)
    tgt = lax.broadcasted_iota(jnp.int32, (L, L), 1)
    causal = src <= tgt
    pick = (lax.broadcasted_iota(jnp.int32, (L, ML_HEADS * L), 0)
            == lax.broadcasted_iota(jnp.int32, (L, ML_HEADS * L), 1) // L).astype(BF16)

    streams = [(b, hh) for b in range(nb) for hh in range(ML_HEADS)]
    qk, b_all, li_all, x_col = [], [], [], []
    for b in range(nb):
        cur = qkc_ref[b]
        prev = jnp.where(c > 0, qkp_ref[b], 0.0)
        row_w = lax.broadcasted_iota(jnp.int32, cur.shape, 0)
        conv = cb_ref[...] + cw_ref[ML_CONV - 1:ML_CONV, :] * cur
        for sh in range(1, ML_CONV):
            shifted = jnp.where(row_w >= sh, pltpu.roll(cur, sh, 0), pltpu.roll(prev, sh, 0))
            conv = conv + cw_ref[ML_CONV - 1 - sh:ML_CONV - sh, :] * shifted
        qk.append(conv * jax.nn.sigmoid(conv))
        gates = g_ref[b]
        li = gates[0:ML_HEADS, :] + bi_ref[...]
        lf = _log_sigmoid(gates[ML_HEADS:, :] + bf_ref[...])
        lane_g = lax.broadcasted_iota(jnp.int32, lf.shape, 1)
        step = 1
        while step < L:
            lf = lf + jnp.where(lane_g >= step, pltpu.roll(lf, step, 1), 0.0)
            step *= 2
        li_all.append(li)
        b_all.append(lf)
        x = jnp.concatenate([li - lf, jnp.zeros((L - ML_HEADS, L), F32)], axis=0).T
        spread = jnp.zeros((L, ML_HEADS * L), F32)
        for _ in range(3):
            part = x.astype(BF16)
            spread = spread + _dot(part, pick)
            x = x - part.astype(F32)
        x_col.append(spread)

    st = []
    for n, (b, hh) in enumerate(streams):
        b_row = b_all[b][hh:hh + 1, :]
        li_row = li_all[b][hh:hh + 1, :]
        m_prev = m_scr[n:n + 1, 0:1]
        d_t = jnp.where(causal, x_col[b][:, hh * L:(hh + 1) * L] + b_row, NEG)
        inter = b_row + m_prev
        m_t = jnp.maximum(inter, jnp.max(d_t, axis=0, keepdims=True))
        b_last = b_row[:, L - 1:L]
        g = b_last - b_row + li_row
        m_new = jnp.maximum(b_last + m_prev, jnp.max(g, axis=1, keepdims=True))
        kf = qk[b][:, (ML_HEADS + hh) * ML_DK:(ML_HEADS + hh + 1) * ML_DK] * ML_DK ** -0.5
        vt = vt_ref[b, hh * ML_VT_ROWS:(hh + 1) * ML_VT_ROWS, :]
        st.append(dict(
            qh=qk[b][:, hh * ML_DK:(hh + 1) * ML_DK].astype(BF16),
            kh=kf.astype(BF16), vt=vt,
            vw=(vt.astype(F32) * jnp.exp(g - m_new)).astype(BF16),
            c_t=c_scr[n],
            w_intra=jnp.exp(d_t - m_t), w_inter=jnp.exp(inter - m_t), floor=jnp.exp(-m_t),
            decay=jnp.exp(b_last + m_prev - m_new), m_new=m_new))

    for s_ in st:
        s_["sraw"] = _dot_nt(s_["kh"], s_["qh"])
        s_["cross"] = _dot_nt(s_["c_t"].astype(BF16), s_["qh"])
    for s_ in st:
        s_["s"] = s_["sraw"] * s_["w_intra"]
    for s_ in st:
        s_["intra"] = _dot(s_["vt"], s_["s"].astype(BF16))
        s_["kv"] = _dot(s_["vw"], s_["kh"])
    h_t = []
    for n, s_ in enumerate(st):
        num = s_["intra"][:ML_DV, :] + s_["w_inter"] * s_["cross"][:ML_DV, :]
        nq = jnp.sum(s_["s"], axis=0, keepdims=True) + s_["w_inter"] * s_["cross"][ML_DV:ML_DV + 1, :]
        h_t.append(num / jnp.maximum(jnp.abs(nq), s_["floor"]))
        c_scr[n] = s_["decay"] * s_["c_t"] + s_["kv"]
        m_scr[n:n + 1, :] = jnp.broadcast_to(s_["m_new"], (1, LANES))
    for b in range(nb):
        hout = jnp.concatenate(h_t[b * ML_HEADS:(b + 1) * ML_HEADS], axis=0).T
        y_ref[b] = (jax.nn.sigmoid(o_ref[b]) * hout).astype(BF16)


def _mlstm(qk_pre, vt, o_pre, gates_t, conv_w, conv_b, b_i, b_f):
    batch, seq, wqk = qk_pre.shape
    L = ML_CHUNK
    full = lambda a: pl.BlockSpec(a.shape, lambda c: (0,) * a.ndim)
    chunk = lambda w_: pl.BlockSpec((batch, L, w_), lambda c: (0, c, 0))
    lanes = lambda r: pl.BlockSpec((batch, r, L), lambda c: (0, 0, c))
    return pl.pallas_call(
        _mlstm_kernel,
        grid=(seq // L,),
        in_specs=[chunk(wqk),
                  pl.BlockSpec((batch, L, wqk), lambda c: (0, jnp.maximum(c - 1, 0), 0)),
                  lanes(ML_HEADS * ML_VT_ROWS), chunk(ML_W), lanes(2 * ML_HEADS),
                  full(conv_w), full(conv_b), full(b_i), full(b_f)],
        out_specs=chunk(ML_W),
        out_shape=jax.ShapeDtypeStruct((batch, seq, ML_W), BF16),
        scratch_shapes=[pltpu.VMEM((batch * ML_HEADS, ML_VT_ROWS, ML_DK), F32),
                        pltpu.VMEM((batch * ML_HEADS, LANES), F32)],
        compiler_params=_params("arbitrary"),
        name="mlstm",
    )(qk_pre, qk_pre, vt, o_pre, gates_t, conv_w, conv_b, b_i, b_f)


def _swa_kernel(q_ref, k_ref, v_ref, y_ref, kk_scr, vv_scr, o_scr, l_scr):
    blk, tile = SW_BLOCK, SWA_TILE
    n_slab = SW_W // LANES
    n = pl.program_id(1)

    @pl.when(n == 0)
    def _():
        kk_scr[:, 0:tile, :] = jnp.zeros((n_slab, tile, LANES), F32)
        vv_scr[:, 0:tile, :] = jnp.zeros((n_slab, tile, LANES), F32)

    @pl.when(n > 0)
    def _():
        kk_scr[:, 0:tile, :] = kk_scr[:, tile:2 * tile, :]
        vv_scr[:, 0:tile, :] = vv_scr[:, tile:2 * tile, :]

    kk_scr[:, tile:2 * tile, :] = k_ref[...]
    vv_scr[:, tile:2 * tile, :] = v_ref[...]

    key = lax.broadcasted_iota(jnp.int32, (2 * blk, blk), 0)
    qry = lax.broadcasted_iota(jnp.int32, (2 * blk, blk), 1)
    dim =lax.broadcasted_iota(jnp.int32, (LANES, blk), 0)
    head_rows = [dim < SW_HD, dim >= SW_HD]

    for g, dil in enumerate(SW_DILATIONS):
        per_res = tile // (blk * dil)
        shift = per_res.bit_length() - 1

        def block_pair(it, carry, g=g, dil=dil, per_res=per_res, shift=shift):
            def rows(start):
                return pl.ds(start, blk, stride=dil) if dil > 1 else pl.ds(pl.multiple_of(start, blk), blk)

            work = []
            for u in range(SWA_UNROLL):
                bi = SWA_UNROLL * it + u
                r = lax.shift_right_logical(bi, shift)
                j = bi & (per_res - 1)
                q0 = r + j * (blk * dil)
                lowest = jnp.where((n == 0) & (j == 0), blk, 0)
                ok = (key >= jnp.maximum(qry, lowest)) & (key <= qry + blk)
                for sl in range(n_slab):
                    prev, cur = rows(tile + q0 - blk * dil), rows(tile + q0)
                    kb = jnp.concatenate([kk_scr[sl, prev, :], kk_scr[sl, cur, :]], axis=0)
                    vb = jnp.concatenate([vv_scr[sl, prev, :], vv_scr[sl, cur, :]], axis=0)
                    work.append(dict(q0=q0, sl=sl, ok=ok, qt=q_ref[sl, rows(q0), :].T,
                                     kb=kb.astype(BF16), vt=vb.T.astype(BF16)))
            for w in work:
                w["s"] = [_dot(w["kb"], jnp.where(head_rows[hh], w["qt"], 0.0).astype(BF16)) for hh in range(2)]
            for w in work:
                w["e"], w["den"], w["lse"] = [], [], []
                for hh in range(2):
                    s = jnp.where(w["ok"], w["s"][hh], NEG)
                    cmax = jnp.max(s, axis=0, keepdims=True)
                    e = jnp.exp2(s - cmax)
                    den = jnp.sum(e, axis=0, keepdims=True)
                    w["e"].append(e.astype(BF16))
                    w["den"].append(den)
                    w["lse"].append(cmax + jnp.log2(den))
            for w in work:
                ot = [_dot(w["vt"][hh * SW_HD:(hh + 1) * SW_HD, :], w["e"][hh]) / w["den"][hh] for hh in range(2)]
                lt = [jnp.broadcast_to(w["lse"][hh], (SW_HD, blk)) for hh in range(2)]
                o_scr[g, w["sl"], rows(w["q0"]), :] = jnp.concatenate(ot, axis=0).T
                l_scr[g, w["sl"], rows(w["q0"]), :] = jnp.concatenate(lt, axis=0).T
            return carry

        lax.fori_loop(0, tile // blk // SWA_UNROLL, block_pair, 0)

    for sl in range(n_slab):
        ls = [l_scr[g, sl] for g in range(len(SW_DILATIONS))]
        mx = functools.reduce(jnp.maximum, ls)
        ws = [jnp.exp2(l - mx) for l in ls]
        num = functools.reduce(lambda a, b: a + b, [w * o_scr[g, sl] for g, w in enumerate(ws)])
        y_ref[:, sl * LANES:(sl + 1) * LANES] = (num / functools.reduce(lambda a, b: a + b, ws)).astype(BF16)


def _swa(q, k, v, batch, seq):
    tile = SWA_TILE
    n_slab = SW_W // LANES
    per_batch = seq // tile
    spec = pl.BlockSpec((n_slab, tile, LANES), lambda bb, n: (0, bb * per_batch + n, 0))
    n_br = len(SW_DILATIONS)
    return pl.pallas_call(
        _swa_kernel,
        grid=(batch, per_batch),
        in_specs=[spec, spec, spec],
        out_specs=pl.BlockSpec((tile, SW_W), lambda bb, n: (bb * per_batch + n, 0)),
        out_shape=jax.ShapeDtypeStruct((batch * seq, SW_W), BF16),
        scratch_shapes=[pltpu.VMEM((n_slab, 2 * tile, LANES), F32), pltpu.VMEM((n_slab, 2 * tile, LANES), F32),
                        pltpu.VMEM((n_br, n_slab, tile, LANES), F32), pltpu.VMEM((n_br, n_slab, tile, LANES), F32)],
        compiler_params=_params("parallel", "arbitrary"),
        name="swa",
    )(q, k, v)


def _outproj_kernel(x_ref, ya_ref, yb_ref, yc_ref, wa_ref, wb_ref, wc_ref, g_ref, out_ref):
    ya = jnp.concatenate([ya_ref[hh] for hh in range(MLA_HEADS)], axis=1)
    y = _dot(ya, wa_ref[...]) + _dot(yb_ref[...], wb_ref[...]) + _dot(yc_ref[...], wc_ref[...])
    out_ref[...] = x_ref[...] + _rms(y, g_ref[...])


def _outproj(x, ya, yb, yc, wa, wb, wc, g):
    t_all = x.shape[0]
    tm = WIDE_TILE
    tok = lambda w_: pl.BlockSpec((tm, w_), lambda i: (i, 0))
    full = lambda a: pl.BlockSpec(a.shape, lambda i: (0,) * a.ndim)
    return pl.pallas_call(
        _outproj_kernel,
        grid=(t_all // tm,),
        in_specs=[tok(D_MODEL), pl.BlockSpec((MLA_HEADS, tm, LANES), lambda i: (0, i, 0)), tok(ML_W), tok(SW_W),
                  full(wa), full(wb), full(wc), full(g)],
        out_specs=tok(D_MODEL),
        out_shape=jax.ShapeDtypeStruct((t_all, D_MODEL), F32),
        compiler_params=_params("parallel"),
        name="outproj",
    )(x, ya, yb, yc, wa, wb, wc, g)


def _memkv_kernel(mem_ref, g_ref, w_ref, kv_ref):
    kv_ref[...] = _dot(_rms(mem_ref[...], g_ref[...]).astype(BF16), w_ref[...]).astype(BF16)


def _memkv(mem, g, w):
    n = mem.shape[0]
    full = lambda a: pl.BlockSpec(a.shape, lambda i: (0,) * a.ndim)
    return pl.pallas_call(
        _memkv_kernel,
        grid=(1,),
        in_specs=[full(mem), full(g), full(w)],
        out_specs=pl.BlockSpec((n, 2 * D_MODEL), lambda i: (0, 0)),
        out_shape=jax.ShapeDtypeStruct((n, 2 * D_MODEL), BF16),
        compiler_params=_params("arbitrary"),
        name="memkv",
    )(mem, g, w)


def _xattn_kernel(x_ref, gpre_ref, wq_ref, kv_ref, wo_ref, gpost_ref, out_ref, o_scr):
    x = x_ref[...]
    h = _rms(x, gpre_ref[...]).astype(BF16)
    q = (_dot(h, wq_ref[...]) * X_HD ** -0.5).astype(BF16)
    for hh in range(X_HEADS):
        hs = slice(hh * X_HD, (hh + 1) * X_HD)
        s = _dot_nt(q[:, hs], kv_ref[:, hs])
        e = jnp.exp(s - jnp.max(s, axis=1, keepdims=True))
        den = jnp.sum(e, axis=1, keepdims=True)
        o_scr[:, hs] = (_dot(e.astype(BF16), kv_ref[:, D_MODEL + hh * X_HD:D_MODEL + (hh + 1) * X_HD]) / den).astype(BF16)
    y = _dot(o_scr[...], wo_ref[...])
    out_ref[...] = x + _rms(y, gpost_ref[...])


def _xattn(x, gpre, wq, kv, wo, gpost, seq):
    t_all = x.shape[0]
    tm = WIDE_TILE
    per_batch = seq // tm
    tok = pl.BlockSpec((tm, D_MODEL), lambda i: (i, 0))
    full = lambda a: pl.BlockSpec(a.shape, lambda i: (0,) * a.ndim)
    return pl.pallas_call(
        _xattn_kernel,
        grid=(t_all // tm,),
        in_specs=[tok, full(gpre), full(wq), pl.BlockSpec((N_MEM, 2 * D_MODEL), lambda i: (i // per_batch, 0)),
                  full(wo), full(gpost)],
        out_specs=tok,
        out_shape=jax.ShapeDtypeStruct((t_all, D_MODEL), F32),
        scratch_shapes=[pltpu.VMEM((tm, D_MODEL), BF16)],
        compiler_params=_params("parallel"),
        name="xattn",
    )(x, gpre, wq, kv, wo, gpost)


def _ffn_kernel(x_ref, gpre_ref, wgu_ref, wd_ref, gpost_ref, out_ref):
    x = x_ref[...]
    h = _rms(x, gpre_ref[...]).astype(BF16)
    tf = D_FF // FFN_CHUNKS
    y = None
    for f in range(FFN_CHUNKS):
        gate = _dot(h, wgu_ref[:, f * tf:(f + 1) * tf])
        up = _dot(h, wgu_ref[:, D_FF + f * tf:D_FF + (f + 1) * tf])
        part = _dot((gate * jax.nn.sigmoid(gate) * up).astype(BF16), wd_ref[f * tf:(f + 1) * tf, :])
        y = part if y is None else y + part
    out_ref[...] = x + _rms(y, gpost_ref[...])


def _ffn(x, gpre, w_gate_up, w_down, gpost):
    t_all = x.shape[0]
    tm = TOK_TILE
    tok = pl.BlockSpec((tm, D_MODEL), lambda i: (i, 0))
    full = lambda a: pl.BlockSpec(a.shape, lambda i: (0,) * a.ndim)
    return pl.pallas_call(
        _ffn_kernel,
        grid=(t_all // tm,),
        in_specs=[tok, full(gpre), full(w_gate_up), full(w_down), full(gpost)],
        out_specs=tok,
        out_shape=jax.ShapeDtypeStruct((t_all, D_MODEL), F32),
        compiler_params=_params("parallel"),
        name="ffn",
    )(x, gpre, w_gate_up, w_down, gpost)


def _rope_tables(seq):
    pos = jnp.arange(seq, dtype=F32)[:, None]

    def cos_sin(d_rot):
        inv = ROPE_THETA ** (-jnp.arange(0, d_rot, 2, dtype=F32) / d_rot)
        ang = pos * inv[None, :]
        return jnp.cos(ang), jnp.sin(ang)

    c, s = cos_sin(MLA_ROPE)
    ones = jnp.ones((seq, MLA_NOPE), F32)
    tail = jnp.ones((seq, LANES - MLA_NOPE - MLA_ROPE), F32)
    cm = jnp.concatenate([ones, c, c, tail], axis=1)
    sm = jnp.concatenate([0 * ones, -s, s, 0 * tail], axis=1)
    c, s = cos_sin(SW_ROT)
    rest = jnp.ones((seq, SW_HD - SW_ROT), F32)
    cs = jnp.tile(jnp.concatenate([c, c, rest], axis=1), (1, LANES // SW_HD))
    ss = jnp.tile(jnp.concatenate([-s, s, 0 * rest], axis=1), (1, LANES // SW_HD))
    return cm, sm, cs, ss


def _prep_layer(w_in, w_uq, w_ukv, w_out):
    o = IN_OFFS
    col = lambda i: w_in[:, o[i]:o[i + 1]]
    z = lambda n: jnp.zeros((D_MODEL, n), F32)
    w_p = jnp.concatenate([
        col(0), col(1),
        z(MLA_NOPE), col(2), z(LANES - MLA_NOPE - MLA_ROPE),
        col(3), col(4), col(6), col(9), col(10), col(11)], axis=1)
    assert w_p.shape[1] == P_TOTAL
    mlv = jnp.pad(col(5).reshape(D_MODEL, ML_HEADS, ML_DV), ((0, 0), (0, 0), (0, ML_VT_ROWS - ML_DV)))
    w_gates_t = jnp.concatenate([mlv.reshape(D_MODEL, ML_HEADS * ML_VT_ROWS), col(7), col(8),
                                 z(ML_T_ROWS - ML_HEADS * ML_VT_ROWS - 2 * ML_HEADS)], axis=1).T
    uq = w_uq.reshape(MLA_Q_LORA, MLA_HEADS, MLA_NOPE + MLA_ROPE)
    uq = jnp.pad(uq, ((0, 0), (0, 0), (0, LANES - MLA_NOPE - MLA_ROPE))).reshape(MLA_Q_LORA, MLA_PAD_W)
    ukv = w_ukv.reshape(MLA_KV_LORA, MLA_HEADS, MLA_NOPE + MLA_V)
    pad_half = lambda a: jnp.pad(a, ((0, 0), (0, 0), (0, LANES - a.shape[2]))).reshape(MLA_KV_LORA, MLA_PAD_W)
    wk = pad_half(ukv[:, :, :MLA_NOPE])
    wv = jnp.pad(ukv[:, :, MLA_NOPE:], ((0, 0), (0, 0), (0, MLA_VT_ROWS - MLA_V)))
    wv = wv.reshape(MLA_KV_LORA, MLA_HEADS * MLA_VT_ROWS).T
    wa = w_out[:MLA_W].reshape(MLA_HEADS, MLA_V, D_MODEL)
    wa = jnp.pad(wa, ((0, 0), (0, LANES - MLA_V), (0, 0))).reshape(MLA_PAD_W, D_MODEL)
    wb, wc = w_out[MLA_W:MLA_W + ML_W], w_out[MLA_W + ML_W:]
    bf = lambda a: a.astype(BF16)
    return bf(w_p), bf(w_gates_t), bf(uq), bf(wk), bf(wv), bf(wa), bf(wb), bf(wc)


def kernel(x, mem, w_in, mla_q_norm, mla_w_uq, mla_kv_norm, mla_w_ukv, ml_conv_w, ml_conv_b, ml_b_i, ml_b_f,
           w_out, x_w_q, x_w_kv, x_w_o, w_gate_up, w_down, norm_mix_pre, norm_mix_post, norm_mem,
           norm_x_pre, norm_x_post, norm_ffn_pre, norm_ffn_post):
    batch, seq, _ = x.shape
    depth = w_in.shape[0]
    t_all = batch * seq
    assert seq % SWA_TILE == 0 and seq % TOK_TILE == 0 and seq % WIDE_TILE == 0
    assert WIDE_TILE % FLASH_KEY_TILE == 0
    assert SWA_TILE == SW_DILATIONS[-1] * SW_BLOCK
    cm, sm, cs, ss = _rope_tables(seq)
    row = lambda a: a.reshape(1, -1)
    colv = lambda a: a.reshape(-1, 1)
    xt = x.reshape(t_all, D_MODEL)
    mem2 = mem.reshape(batch * N_MEM, D_MODEL)
    for l in range(depth):
        w_p, w_gt, uq, wk, wv, wa, wb, wc = _prep_layer(w_in[l], mla_w_uq[l], mla_w_ukv[l], w_out[l])
        q, k, v, mqk, mv, mo, gates_t, sq, sk, sv = _inproj(
            xt, row(norm_mix_pre[l]), w_p, w_gt, row(mla_q_norm[l]), uq, row(mla_kv_norm[l]), wk, wv,
            cm, sm, cs, ss, seq)
        b3 = lambda a: a.reshape(batch, seq, a.shape[-1])
        ya = _flash(q, k, v, batch, seq)
        yb = _mlstm(b3(mqk), mv, b3(mo), gates_t, ml_conv_w[l], row(ml_conv_b[l]), colv(ml_b_i[l]),
                    colv(ml_b_f[l])).reshape(t_all, ML_W)
        yc = _swa(sq, sk, sv, batch, seq)
        xt = _outproj(xt, ya, yb, yc, wa, wb, wc, row(norm_mix_post[l]))
        kv = _memkv(mem2, row(norm_mem[l]), x_w_kv[l].astype(BF16))
        xt = _xattn(xt, row(norm_x_pre[l]), x_w_q[l].astype(BF16), kv, x_w_o[l].astype(BF16),
                    row(norm_x_post[l]), seq)
        xt = _ffn(xt, row(norm_ffn_pre[l]), w_gate_up[l].astype(BF16), w_down[l].astype(BF16),
                  row(norm_ffn_post[l]))
    return xt.reshape(batch, seq, D_MODEL)
```

```python
import functools

import jax
import jax.numpy as jnp
import numpy as np
from jax import lax
from jax.experimental import pallas as pl
from jax.experimental.pallas import tpu as pltpu

F32 = jnp.float32
BF16 = jnp.bfloat16

D_MODEL = 1024
N_MEM = 256
ROPE_THETA = 500000.0
EPS = 1e-6
NEG = -1e30
LOG2_E = 1.4426950408889634
LANES = 128

MLA_HEADS, MLA_NOPE, MLA_ROPE, MLA_V = 8, 64, 32, 64
MLA_Q_LORA, MLA_KV_LORA = 256, 128
ML_HEADS, ML_DK, ML_DV, ML_CONV, ML_CHUNK = 4, 64, 64, 4, 128
SW_HEADS, SW_HD = 4, 64
SW_ROT = SW_HD // 4
SW_DILATIONS = (1, 4, 16)
SW_BLOCK = 128
X_HEADS = 4
X_HD = D_MODEL // X_HEADS
D_FF = 2816

MLA_W = MLA_HEADS * MLA_V
ML_W = ML_HEADS * ML_DV
SW_W = SW_HEADS * SW_HD
MLA_PAD_W = MLA_HEADS * LANES

IN_SIZES = (MLA_Q_LORA, MLA_KV_LORA, MLA_ROPE, ML_HEADS * ML_DK, ML_HEADS * ML_DK, ML_W, ML_W,
            ML_HEADS, ML_HEADS, SW_W, SW_W, SW_W)
IN_OFFS = tuple(int(v) for v in np.cumsum((0,) + IN_SIZES))

P_CQ = 0
P_CKV = 256
P_KROPE = 384
P_MLQK = 512
P_MLO = 1024
P_SWQ = 1280
P_SWK = 1536
P_SWV = 1792
P_TOTAL = 2048
ML_VT_ROWS = 80
ML_T_ROWS = ML_HEADS * ML_VT_ROWS + 16

TOK_TILE = 512
WIDE_TILE = 1024
FLASH_TILE = 1024
FLASH_KEY_TILE = 256
MLA_VT_ROWS = 80
SWA_TILE = 2048
SWA_UNROLL = 8
FFN_CHUNKS = 2
ML_CHUNKS_PER_STEP = 4
VMEM_LIMIT = 56 * 1024 * 1024


def _rms(x, g):
    return x * lax.rsqrt(jnp.mean(x * x, axis=-1, keepdims=True) + EPS) * g


def _dot(a, b):
    return jnp.dot(a, b, preferred_element_type=F32)


def _dot_nt(a, b):
    return lax.dot_general(a, b, (((1,), (1,)), ((), ())), preferred_element_type=F32)


def _dot_tn(a, b):
    return lax.dot_general(a, b, (((0,), (0,)), ((), ())), preferred_element_type=F32)


def _params(*sem):
    return pltpu.CompilerParams(dimension_semantics=sem, vmem_limit_bytes=VMEM_LIMIT)


def _inproj_kernel(x_ref, g_ref, w_ref, wgt_ref, qg_ref, wuq_ref, kvg_ref, wk_ref, wvt_ref,
                   cm_ref, sm_ref, cs_ref, ss_ref,
                   q_ref, k_ref, vt_ref, mqk_ref, mv_ref, mo_ref, gr_ref, sq_ref, sk_ref, sv_ref):
    tm = x_ref.shape[0]
    h = _rms(x_ref[...], g_ref[...]).astype(BF16)

    def proj(a, b):
        return _dot(h, w_ref[:, a:b])

    lane = lax.broadcasted_iota(jnp.int32, (tm, LANES), 1)
    cm, sm = cm_ref[...], sm_ref[...]
    mla_first = lane < MLA_NOPE + MLA_ROPE // 2

    def rope_mla(t):
        rot = jnp.where(mla_first, pltpu.roll(t, LANES - MLA_ROPE // 2, 1), pltpu.roll(t, MLA_ROPE // 2, 1))
        return t * cm + rot * sm

    cs, ss = cs_ref[...], ss_ref[...]
    sw_first = (lane % SW_HD) < SW_ROT // 2

    def rope_sw(t):
        rot = jnp.where(sw_first, pltpu.roll(t, LANES - SW_ROT // 2, 1), pltpu.roll(t, SW_ROT // 2, 1))
        return t * cs + rot * ss

    cqn = _rms(proj(P_CQ, P_CQ + MLA_Q_LORA), qg_ref[...]).astype(BF16)
    q_scale = (MLA_NOPE + MLA_ROPE) ** -0.5 * LOG2_E
    for hh in range(MLA_HEADS):
        sl = slice(hh * LANES, (hh + 1) * LANES)
        q_ref[hh] = (rope_mla(_dot(cqn, wuq_ref[:, sl])) * q_scale).astype(BF16)
    kvn = _rms(proj(P_CKV, P_CKV + MLA_KV_LORA), kvg_ref[...]).astype(BF16)
    kpe = rope_mla(proj(P_KROPE, P_KROPE + LANES))
    one_col = (lax.broadcasted_iota(jnp.int32, (MLA_VT_ROWS, 1), 0) == MLA_V).astype(F32)
    for hh in range(MLA_HEADS):
        sl = slice(hh * LANES, (hh + 1) * LANES)
        k_ref[hh] = (_dot(kvn, wk_ref[:, sl]) + kpe).astype(BF16)
        rows = slice(hh * MLA_VT_ROWS, (hh + 1) * MLA_VT_ROWS)
        for u in range(tm // FLASH_KEY_TILE):
            toks = slice(u * FLASH_KEY_TILE, (u + 1) * FLASH_KEY_TILE)
            vt_ref[hh, u] = (_dot_nt(wvt_ref[rows, :], kvn[toks, :]) + one_col).astype(BF16)

    mqk_ref[...] = proj(P_MLQK, P_MLQK + 2 * ML_HEADS * ML_DK)
    mo_ref[...] = proj(P_MLO, P_MLO + ML_W)
    tr = _dot_nt(wgt_ref[...], h)
    n_v = ML_HEADS * ML_VT_ROWS
    v_row = lax.broadcasted_iota(jnp.int32, (n_v, 1), 0)
    mv_ref[...] = (tr[0:n_v, :] + ((v_row % ML_VT_ROWS) == ML_DV).astype(F32)).astype(BF16)
    gr_ref[...] = tr[n_v:n_v + 2 * ML_HEADS, :]

    for half in range(SW_W // LANES):
        sl = slice(half * LANES, (half + 1) * LANES)
        sq_ref[half] = rope_sw(proj(P_SWQ + half * LANES, P_SWQ + (half + 1) * LANES)) * (SW_HD ** -0.5 * LOG2_E)
        sk_ref[half] = rope_sw(proj(P_SWK + half * LANES, P_SWK + (half + 1) * LANES))
        sv_ref[half] = proj(P_SWV + half * LANES, P_SWV + (half + 1) * LANES)


def _inproj(x, g, w, wgt, qg, wuq, kvg, wk, wv, cm, sm, cs, ss, seq):
    t_all = x.shape[0]
    tm = WIDE_TILE
    pos_tiles = seq // tm
    tok = lambda w_: pl.BlockSpec((tm, w_), lambda i: (i, 0))
    full = lambda a: pl.BlockSpec(a.shape, lambda i: (0,) * a.ndim)
    pos = pl.BlockSpec((tm, LANES), lambda i: (i % pos_tiles, 0))
    slabs = pl.BlockSpec((SW_W // LANES, tm, LANES), lambda i: (0, i, 0))
    out_shapes = (
        jax.ShapeDtypeStruct((MLA_HEADS, t_all, LANES), BF16),
        jax.ShapeDtypeStruct((MLA_HEADS, t_all, LANES), BF16),
        jax.ShapeDtypeStruct((MLA_HEADS, t_all // FLASH_KEY_TILE, MLA_VT_ROWS, FLASH_KEY_TILE), BF16),
        jax.ShapeDtypeStruct((t_all, 2 * ML_HEADS * ML_DK), F32),
        jax.ShapeDtypeStruct((t_all // seq, ML_HEADS * ML_VT_ROWS, seq), BF16),
        jax.ShapeDtypeStruct((t_all, ML_W), F32),
        jax.ShapeDtypeStruct((t_all // seq, 2 * ML_HEADS, seq), F32),
        jax.ShapeDtypeStruct((SW_W // LANES, t_all, LANES), F32),
        jax.ShapeDtypeStruct((SW_W // LANES, t_all, LANES), F32),
        jax.ShapeDtypeStruct((SW_W // LANES, t_all, LANES), F32),
    )
    heads = pl.BlockSpec((MLA_HEADS, tm, LANES), lambda i: (0, i, 0))
    out_specs = (heads, heads,
                 pl.BlockSpec((MLA_HEADS, tm // FLASH_KEY_TILE, MLA_VT_ROWS, FLASH_KEY_TILE), lambda i: (0, i, 0, 0)),
                 tok(2 * ML_HEADS * ML_DK),
                 pl.BlockSpec((None, ML_HEADS * ML_VT_ROWS, tm), lambda i: (i // pos_tiles, 0, i % pos_tiles)),
                 tok(ML_W), pl.BlockSpec((None, 2 * ML_HEADS, tm), lambda i: (i // pos_tiles, 0, i % pos_tiles)),
                 slabs, slabs, slabs)
    return pl.pallas_call(
        _inproj_kernel,
        grid=(t_all // tm,),
        in_specs=[tok(D_MODEL), full(g), full(w), full(wgt), full(qg), full(wuq), full(kvg), full(wk), full(wv),
                  pos, pos, pos, pos],
        out_specs=out_specs,
        out_shape=out_shapes,
        compiler_params=_params("parallel"),
        name="inproj",
    )(x, g, w, wgt, qg, wuq, kvg, wk, wv, cm, sm, cs, ss)


def _flash_kernel(q_ref, k_ref, vt_ref, o_ref, m_scr, acc_scr, s_scr, cmax_scr):
    tq = q_ref.shape[0]
    tk = FLASH_KEY_TILE
    per_q = tq // tk
    i = pl.program_id(2)
    n_tiles = (i + 1) * per_q
    qt = q_ref[...].astype(F32).T.astype(BF16)
    m_scr[...] = jnp.full(m_scr.shape, NEG, F32)
    acc_scr[...] = jnp.zeros(acc_scr.shape, F32)

    def scores(j, buf, masked):
        start = pl.multiple_of(j * tk, tk)
        st = _dot(k_ref[pl.ds(start, tk), :], qt)
        if masked:
            key = lax.broadcasted_iota(jnp.int32, (tk, tq), 0) + j * tk
            qry = lax.broadcasted_iota(jnp.int32, (tk, tq), 1) + i * tq
            st = jnp.where(key <= qry, st, NEG)
        s_scr[buf] = st
        cmax_scr[buf] = jnp.max(st, axis=0, keepdims=True)

    def accumulate(j, buf):
        m_old = m_scr[...]
        m_new = jnp.maximum(m_old, cmax_scr[buf])
        p = jnp.exp2(s_scr[buf] - m_new).astype(BF16)
        acc_scr[...] = acc_scr[...] * jnp.exp2(m_old - m_new) + _dot(vt_ref[j], p)
        m_scr[...] = m_new

    def pairs(pp, masks):
        for c, (mask_a, mask_b) in enumerate(masks):
            scores(2 * (pp + c) + 1, 1, mask_a)
            accumulate(2 * (pp + c), 0)
            scores(2 * (pp + c) + 2, 0, mask_b)
            accumulate(2 * (pp + c) + 1, 1)

    def finish():
        scores(n_tiles - 1, 1, True)
        accumulate(n_tiles - 2, 0)
        accumulate(n_tiles - 1, 1)
        acc = acc_scr[...]
        out_t = acc[0:MLA_V, :] / acc[MLA_V:MLA_V + 1, :]
        o_ref[...] = jnp.concatenate([out_t, jnp.zeros_like(out_t)], axis=0).T.astype(BF16)

    assert per_q == 4

    @pl.when(i == 0)
    def _():
        scores(0, 0, True)
        pairs(0, [(True, True)])
        finish()

    tail = [(False, False), (False, True), (True, True)]
    plain = [(False, False)] * 2
    n_eights = lax.shift_right_logical(jnp.maximum(i - 1, 0), 1)

    def body(g, carry):
        pairs(4 * g, plain + plain)
        return carry

    @pl.when((i > 0) & ((i & 1) == 1))
    def _():
        scores(0, 0, False)
        lax.fori_loop(0, n_eights, body, 0)
        pairs(4 * n_eights, tail)
        finish()

    @pl.when((i > 0) & ((i & 1) == 0))
    def _():
        scores(0, 0, False)
        lax.fori_loop(0, n_eights, body, 0)
        pairs(4 * n_eights, plain + tail)
        finish()


def _flash(q, k, vt, b, s):
    tq, tk = FLASH_TILE, FLASH_KEY_TILE
    assert s % tq == 0 and tq % (2 * tk) == 0
    nq = s // tq
    rows = pl.BlockSpec((None, tq, LANES), lambda bb, hh, i: (hh, bb * nq + i, 0))
    return pl.pallas_call(
        _flash_kernel,
        grid=(b, MLA_HEADS, nq),
        in_specs=[rows,
                  pl.BlockSpec((None, s, LANES), lambda bb, hh, i: (hh, bb, 0)),
                  pl.BlockSpec((None, s // tk, MLA_VT_ROWS, tk), lambda bb, hh, i: (hh, bb, 0, 0))],
        out_specs=rows,
        out_shape=jax.ShapeDtypeStruct((MLA_HEADS, b * s, LANES), BF16),
        scratch_shapes=[pltpu.VMEM((1, tq), F32), pltpu.VMEM((MLA_VT_ROWS, tq), F32),
                        pltpu.VMEM((2, tk, tq), F32), pltpu.VMEM((2, 1, tq), F32)],
        compiler_params=_params("parallel", "parallel", "arbitrary"),
        name="mla_flash",
    )(q, k, vt)


def _log_sigmoid(z):
    return jnp.minimum(z, 0.0) - jnp.log1p(jnp.exp(-jnp.abs(z)))


def _mlstm_kernel(qkc_ref, qkp_ref, vt_ref, o_ref, g_ref, cw_ref, cb_ref, bi_ref, bf_ref,
                  y_ref, c_scr, m_scr):
    L = ML_CHUNK
    nb = qkc_ref.shape[0]
    c = pl.program_id(0)

    @pl.when(c == 0)
    def _():
        c_scr[...] = jnp.zeros(c_scr.shape, F32)
        m_scr[...] = jnp.full(m_scr.shape, NEG, F32)

    src = lax.broadcasted_iota(jnp.int32, (L, L), 0)
    tgt = lax.broadcasted_iota(jnp.int32, (L, L), 1)
    causal = src <= tgt
    pick = (lax.broadcasted_iota(jnp.int32, (L, ML_HEADS * L), 0)
            == lax.broadcasted_iota(jnp.int32, (L, ML_HEADS * L), 1) // L).astype(BF16)

    def chunk(sub):
        rows_t = slice(sub * L, (sub + 1) * L)
        streams = [(b, hh) for b in range(nb) for hh in range(ML_HEADS)]
        qk, b_all, li_all, x_col = [], [], [], []
        for b in range(nb):
            cur = qkc_ref[b, rows_t, :]
            prev = jnp.where(c > 0, qkp_ref[b], 0.0) if sub == 0 else qkc_ref[b, (sub - 1) * L:sub * L, :]
            row_w = lax.broadcasted_iota(jnp.int32, cur.shape, 0)
            conv = cb_ref[...] + cw_ref[ML_CONV - 1:ML_CONV, :] * cur
            for sh in range(1, ML_CONV):
                shifted = jnp.where(row_w >= sh, pltpu.roll(cur, sh, 0), pltpu.roll(prev, sh, 0))
                conv = conv + cw_ref[ML_CONV - 1 - sh:ML_CONV - sh, :] * shifted
            qk.append(conv * jax.nn.sigmoid(conv))
            gates = g_ref[b, :, rows_t]
            li = gates[0:ML_HEADS, :] + bi_ref[...]
            lf = _log_sigmoid(gates[ML_HEADS:, :] + bf_ref[...])
            lane_g = lax.broadcasted_iota(jnp.int32, lf.shape, 1)
            step = 1
            while step < L:
                lf = lf + jnp.where(lane_g >= step, pltpu.roll(lf, step, 1), 0.0)
                step *= 2
            li_all.append(li)
            b_all.append(lf)
            x = jnp.concatenate([li - lf, jnp.zeros((L - ML_HEADS, L), F32)], axis=0).T
            spread = jnp.zeros((L, ML_HEADS * L), F32)
            for _ in range(3):
                part = x.astype(BF16)
                spread = spread + _dot(part, pick)
                x = x - part.astype(F32)
            x_col.append(spread)

        st = []
        for n, (b, hh) in enumerate(streams):
            b_row = b_all[b][hh:hh + 1, :]
            li_row = li_all[b][hh:hh + 1, :]
            m_prev = m_scr[n:n + 1, 0:1]
            d_t = jnp.where(causal, x_col[b][:, hh * L:(hh + 1) * L] + b_row, NEG)
            inter = b_row + m_prev
            m_t = jnp.maximum(inter, jnp.max(d_t, axis=0, keepdims=True))
            b_last = b_row[:, L - 1:L]
            g = b_last - b_row + li_row
            m_new = jnp.maximum(b_last + m_prev, jnp.max(g, axis=1, keepdims=True))
            kf = qk[b][:, (ML_HEADS + hh) * ML_DK:(ML_HEADS + hh + 1) * ML_DK] * ML_DK ** -0.5
            vt = vt_ref[b, hh * ML_VT_ROWS:(hh + 1) * ML_VT_ROWS, rows_t]
            st.append(dict(
                qh=qk[b][:, hh * ML_DK:(hh + 1) * ML_DK].astype(BF16),
                kh=kf.astype(BF16), vt=vt,
                vw=(vt.astype(F32) * jnp.exp(g - m_new)).astype(BF16),
                c_t=c_scr[n],
                w_intra=jnp.exp(d_t - m_t), w_inter=jnp.exp(inter - m_t), floor=jnp.exp(-m_t),
                decay=jnp.exp(b_last + m_prev - m_new), m_new=m_new))

        for s_ in st:
            s_["sraw"] = _dot_nt(s_["kh"], s_["qh"])
            s_["cross"] = _dot_nt(s_["c_t"].astype(BF16), s_["qh"])
        for s_ in st:
            s_["s"] = s_["sraw"] * s_["w_intra"]
        for s_ in st:
            s_["intra"] = _dot(s_["vt"], s_["s"].astype(BF16))
            s_["kv"] = _dot(s_["vw"], s_["kh"])
        h_t = []
        for n, s_ in enumerate(st):
            num = s_["intra"][:ML_DV, :] + s_["w_inter"] * s_["cross"][:ML_DV, :]
            nq = jnp.sum(s_["s"], axis=0, keepdims=True) + s_["w_inter"] * s_["cross"][ML_DV:ML_DV + 1, :]
            h_t.append(num / jnp.maximum(jnp.abs(nq), s_["floor"]))
            c_scr[n] = s_["decay"] * s_["c_t"] + s_["kv"]
            m_scr[n:n + 1, :] = jnp.broadcast_to(s_["m_new"], (1, LANES))
        for b in range(nb):
            hout = jnp.concatenate(h_t[b * ML_HEADS:(b + 1) * ML_HEADS], axis=0).T
            y_ref[b, rows_t, :] = (jax.nn.sigmoid(o_ref[b, rows_t, :]) * hout).astype(BF16)

    for sub in range(qkc_ref.shape[1] // L):
        chunk(sub)


def _mlstm(qk_pre, vt, o_pre, gates_t, conv_w, conv_b, b_i, b_f):
    batch, seq, wqk = qk_pre.shape
    L = ML_CHUNK
    step = ML_CHUNKS_PER_STEP * L
    full = lambda a: pl.BlockSpec(a.shape, lambda c: (0,) * a.ndim)
    chunk = lambda w_: pl.BlockSpec((batch, step, w_), lambda c: (0, c, 0))
    lanes = lambda r: pl.BlockSpec((batch, r, step), lambda c: (0, 0, c))
    return pl.pallas_call(
        _mlstm_kernel,
        grid=(seq // step,),
        in_specs=[chunk(wqk),
                  pl.BlockSpec((batch, L, wqk), lambda c: (0, jnp.maximum(ML_CHUNKS_PER_STEP * c - 1, 0), 0)),
                  lanes(ML_HEADS * ML_VT_ROWS), chunk(ML_W), lanes(2 * ML_HEADS),
                  full(conv_w), full(conv_b), full(b_i), full(b_f)],
        out_specs=chunk(ML_W),
        out_shape=jax.ShapeDtypeStruct((batch, seq, ML_W), BF16),
        scratch_shapes=[pltpu.VMEM((batch * ML_HEADS, ML_VT_ROWS, ML_DK), F32),
                        pltpu.VMEM((batch * ML_HEADS, LANES), F32)],
        compiler_params=_params("arbitrary"),
        name="mlstm",
    )(qk_pre, qk_pre, vt, o_pre, gates_t, conv_w, conv_b, b_i, b_f)


def _swa_kernel(q_ref, k_ref, v_ref, y_ref, kk_scr, vv_scr, o_scr, l_scr):
    blk, tile = SW_BLOCK, SWA_TILE
    n_slab = SW_W // LANES
    n = pl.program_id(1)

    @pl.when(n == 0)
    def _():
        kk_scr[:, 0:tile, :] = jnp.zeros((n_slab, tile, LANES), F32)
        vv_scr[:, 0:tile, :] = jnp.zeros((n_slab, tile, LANES), F32)

    @pl.when(n > 0)
    def _():
        kk_scr[:, 0:tile, :] = kk_scr[:, tile:2 * tile, :]
        vv_scr[:, 0:tile, :] = vv_scr[:, tile:2 * tile, :]

    kk_scr[:, tile:2 * tile, :] = k_ref[...]
    vv_scr[:, tile:2 * tile, :] = v_ref[...]

    key = lax.broadcasted_iota(jnp.int32, (2 * blk, blk), 0)
    qry = lax.broadcasted_iota(jnp.int32, (2 * blk, blk), 1)
    dim =lax.broadcasted_iota(jnp.int32, (LANES, blk), 0)
    head_rows = [dim < SW_HD, dim >= SW_HD]

    for g, dil in enumerate(SW_DILATIONS):
        per_res = tile // (blk * dil)
        shift = per_res.bit_length() - 1

        def block_pair(it, carry, g=g, dil=dil, per_res=per_res, shift=shift):
            def rows(start):
                return pl.ds(start, blk, stride=dil) if dil > 1 else pl.ds(pl.multiple_of(start, blk), blk)

            work = []
            for u in range(SWA_UNROLL):
                bi = SWA_UNROLL * it + u
                r = lax.shift_right_logical(bi, shift)
                j = bi & (per_res - 1)
                q0 = r + j * (blk * dil)
                lowest = jnp.where((n == 0) & (j == 0), blk, 0)
                ok = (key >= jnp.maximum(qry, lowest)) & (key <= qry + blk)
                for sl in range(n_slab):
                    prev, cur = rows(tile + q0 - blk * dil), rows(tile + q0)
                    kb = jnp.concatenate([kk_scr[sl, prev, :], kk_scr[sl, cur, :]], axis=0)
                    vb = jnp.concatenate([vv_scr[sl, prev, :], vv_scr[sl, cur, :]], axis=0)
                    work.append(dict(q0=q0, sl=sl, ok=ok, qt=q_ref[sl, rows(q0), :].T,
                                     kb=kb.astype(BF16), vt=vb.T.astype(BF16)))
            for w in work:
                w["s"] = [_dot(w["kb"], jnp.where(head_rows[hh], w["qt"], 0.0).astype(BF16)) for hh in range(2)]
            for w in work:
                w["e"], w["den"], w["lse"] = [], [], []
                for hh in range(2):
                    s = jnp.where(w["ok"], w["s"][hh], NEG)
                    cmax = jnp.max(s, axis=0, keepdims=True)
                    e = jnp.exp2(s - cmax)
                    den = jnp.sum(e, axis=0, keepdims=True)
                    w["e"].append(e.astype(BF16))
                    w["den"].append(den)
                    w["lse"].append(cmax + jnp.log2(den))
            for w in work:
                ot = [_dot(w["vt"][hh * SW_HD:(hh + 1) * SW_HD, :], w["e"][hh]) / w["den"][hh] for hh in range(2)]
                lt = [jnp.broadcast_to(w["lse"][hh], (SW_HD, blk)) for hh in range(2)]
                o_scr[g, w["sl"], rows(w["q0"]), :] = jnp.concatenate(ot, axis=0).T
                l_scr[g, w["sl"], rows(w["q0"]), :] = jnp.concatenate(lt, axis=0).T
            return carry

        lax.fori_loop(0, tile // blk // SWA_UNROLL, block_pair, 0)

    for sl in range(n_slab):
        ls = [l_scr[g, sl] for g in range(len(SW_DILATIONS))]
        mx = functools.reduce(jnp.maximum, ls)
        ws = [jnp.exp2(l - mx) for l in ls]
        num = functools.reduce(lambda a, b: a + b, [w * o_scr[g, sl] for g, w in enumerate(ws)])
        y_ref[:, sl * LANES:(sl + 1) * LANES] = (num / functools.reduce(lambda a, b: a + b, ws)).astype(BF16)


def _swa(q, k, v, batch, seq):
    tile = SWA_TILE
    n_slab = SW_W // LANES
    per_batch = seq // tile
    spec = pl.BlockSpec((n_slab, tile, LANES), lambda bb, n: (0, bb * per_batch + n, 0))
    n_br = len(SW_DILATIONS)
    return pl.pallas_call(
        _swa_kernel,
        grid=(batch, per_batch),
        in_specs=[spec, spec, spec],
        out_specs=pl.BlockSpec((tile, SW_W), lambda bb, n: (bb * per_batch + n, 0)),
        out_shape=jax.ShapeDtypeStruct((batch * seq, SW_W), BF16),
        scratch_shapes=[pltpu.VMEM((n_slab, 2 * tile, LANES), F32), pltpu.VMEM((n_slab, 2 * tile, LANES), F32),
                        pltpu.VMEM((n_br, n_slab, tile, LANES), F32), pltpu.VMEM((n_br, n_slab, tile, LANES), F32)],
        compiler_params=_params("parallel", "arbitrary"),
        name="swa",
    )(q, k, v)


def _outproj_kernel(x_ref, ya_ref, yb_ref, yc_ref, wa_ref, wb_ref, wc_ref, g_ref, out_ref):
    ya = jnp.concatenate([ya_ref[hh] for hh in range(MLA_HEADS)], axis=1)
    y = _dot(ya, wa_ref[...]) + _dot(yb_ref[...], wb_ref[...]) + _dot(yc_ref[...], wc_ref[...])
    out_ref[...] = x_ref[...] + _rms(y, g_ref[...])


def _outproj(x, ya, yb, yc, wa, wb, wc, g):
    t_all = x.shape[0]
    tm = WIDE_TILE
    tok = lambda w_: pl.BlockSpec((tm, w_), lambda i: (i, 0))
    full = lambda a: pl.BlockSpec(a.shape, lambda i: (0,) * a.ndim)
    return pl.pallas_call(
        _outproj_kernel,
        grid=(t_all // tm,),
        in_specs=[tok(D_MODEL), pl.BlockSpec((MLA_HEADS, tm, LANES), lambda i: (0, i, 0)), tok(ML_W), tok(SW_W),
                  full(wa), full(wb), full(wc), full(g)],
        out_specs=tok(D_MODEL),
        out_shape=jax.ShapeDtypeStruct((t_all, D_MODEL), F32),
        compiler_params=_params("parallel"),
        name="outproj",
    )(x, ya, yb, yc, wa, wb, wc, g)


def _memkv_kernel(mem_ref, g_ref, w_ref, kv_ref):
    kv_ref[...] = _dot(_rms(mem_ref[...], g_ref[...]).astype(BF16), w_ref[...]).astype(BF16)


def _memkv(mem, g, w):
    n = mem.shape[0]
    depth = w.shape[0]
    return pl.pallas_call(
        _memkv_kernel,
        grid=(depth,),
        in_specs=[pl.BlockSpec(mem.shape, lambda l: (0, 0)),
                  pl.BlockSpec((None, 1, D_MODEL), lambda l: (l, 0, 0)),
                  pl.BlockSpec((None, D_MODEL, 2 * D_MODEL), lambda l: (l, 0, 0))],
        out_specs=pl.BlockSpec((None, n, 2 * D_MODEL), lambda l: (l, 0, 0)),
        out_shape=jax.ShapeDtypeStruct((depth, n, 2 * D_MODEL), BF16),
        compiler_params=_params("parallel"),
        name="memkv",
    )(mem, g, w)


def _xattn_kernel(x_ref, gpre_ref, wq_ref, kv_ref, wo_ref, gpost_ref, out_ref, o_scr):
    x = x_ref[...]
    h = _rms(x, gpre_ref[...]).astype(BF16)
    q = (_dot(h, wq_ref[...]) * X_HD ** -0.5).astype(BF16)
    for hh in range(X_HEADS):
        hs = slice(hh * X_HD, (hh + 1) * X_HD)
        s = _dot_nt(q[:, hs], kv_ref[:, hs])
        e = jnp.exp(s - jnp.max(s, axis=1, keepdims=True))
        den = jnp.sum(e, axis=1, keepdims=True)
        o_scr[:, hs] = (_dot(e.astype(BF16), kv_ref[:, D_MODEL + hh * X_HD:D_MODEL + (hh + 1) * X_HD]) / den).astype(BF16)
    y = _dot(o_scr[...], wo_ref[...])
    out_ref[...] = x + _rms(y, gpost_ref[...])


def _xattn(x, gpre, wq, kv, wo, gpost, seq):
    t_all = x.shape[0]
    tm = WIDE_TILE
    per_batch = seq // tm
    tok = pl.BlockSpec((tm, D_MODEL), lambda i: (i, 0))
    full = lambda a: pl.BlockSpec(a.shape, lambda i: (0,) * a.ndim)
    return pl.pallas_call(
        _xattn_kernel,
        grid=(t_all // tm,),
        in_specs=[tok, full(gpre), full(wq), pl.BlockSpec((N_MEM, 2 * D_MODEL), lambda i: (i // per_batch, 0)),
                  full(wo), full(gpost)],
        out_specs=tok,
        out_shape=jax.ShapeDtypeStruct((t_all, D_MODEL), F32),
        scratch_shapes=[pltpu.VMEM((tm, D_MODEL), BF16)],
        compiler_params=_params("parallel"),
        name="xattn",
    )(x, gpre, wq, kv, wo, gpost)


def _ffn_kernel(x_ref, gpre_ref, wgu_ref, wd_ref, gpost_ref, out_ref):
    x = x_ref[...]
    h = _rms(x, gpre_ref[...]).astype(BF16)
    tf = D_FF // FFN_CHUNKS
    y = None
    for f in range(FFN_CHUNKS):
        gate = _dot(h, wgu_ref[:, f * tf:(f + 1) * tf])
        up = _dot(h, wgu_ref[:, D_FF + f * tf:D_FF + (f + 1) * tf])
        part = _dot((gate * jax.nn.sigmoid(gate) * up).astype(BF16), wd_ref[f * tf:(f + 1) * tf, :])
        y = part if y is None else y + part
    out_ref[...] = x + _rms(y, gpost_ref[...])


def _ffn(x, gpre, w_gate_up, w_down, gpost):
    t_all = x.shape[0]
    tm = TOK_TILE
    tok = pl.BlockSpec((tm, D_MODEL), lambda i: (i, 0))
    full = lambda a: pl.BlockSpec(a.shape, lambda i: (0,) * a.ndim)
    return pl.pallas_call(
        _ffn_kernel,
        grid=(t_all // tm,),
        in_specs=[tok, full(gpre), full(w_gate_up), full(w_down), full(gpost)],
        out_specs=tok,
        out_shape=jax.ShapeDtypeStruct((t_all, D_MODEL), F32),
        compiler_params=_params("parallel"),
        name="ffn",
    )(x, gpre, w_gate_up, w_down, gpost)


def _rope_tables(seq):
    pos = jnp.arange(seq, dtype=F32)[:, None]

    def cos_sin(d_rot):
        inv = ROPE_THETA ** (-jnp.arange(0, d_rot, 2, dtype=F32) / d_rot)
        ang = pos * inv[None, :]
        return jnp.cos(ang), jnp.sin(ang)

    c, s = cos_sin(MLA_ROPE)
    ones = jnp.ones((seq, MLA_NOPE), F32)
    tail = jnp.ones((seq, LANES - MLA_NOPE - MLA_ROPE), F32)
    cm = jnp.concatenate([ones, c, c, tail], axis=1)
    sm = jnp.concatenate([0 * ones, -s, s, 0 * tail], axis=1)
    c, s = cos_sin(SW_ROT)
    rest = jnp.ones((seq, SW_HD - SW_ROT), F32)
    cs = jnp.tile(jnp.concatenate([c, c, rest], axis=1), (1, LANES // SW_HD))
    ss = jnp.tile(jnp.concatenate([-s, s, 0 * rest], axis=1), (1, LANES // SW_HD))
    return cm, sm, cs, ss


def _prep_layer(w_in, w_uq, w_ukv, w_out):
    o = IN_OFFS
    col = lambda i: w_in[:, o[i]:o[i + 1]]
    z = lambda n: jnp.zeros((D_MODEL, n), F32)
    w_p = jnp.concatenate([
        col(0), col(1),
        z(MLA_NOPE), col(2), z(LANES - MLA_NOPE - MLA_ROPE),
        col(3), col(4), col(6), col(9), col(10), col(11)], axis=1)
    assert w_p.shape[1] == P_TOTAL
    mlv = jnp.pad(col(5).reshape(D_MODEL, ML_HEADS, ML_DV), ((0, 0), (0, 0), (0, ML_VT_ROWS - ML_DV)))
    w_gates_t = jnp.concatenate([mlv.reshape(D_MODEL, ML_HEADS * ML_VT_ROWS), col(7), col(8),
                                 z(ML_T_ROWS - ML_HEADS * ML_VT_ROWS - 2 * ML_HEADS)], axis=1).T
    uq = w_uq.reshape(MLA_Q_LORA, MLA_HEADS, MLA_NOPE + MLA_ROPE)
    uq = jnp.pad(uq, ((0, 0), (0, 0), (0, LANES - MLA_NOPE - MLA_ROPE))).reshape(MLA_Q_LORA, MLA_PAD_W)
    ukv = w_ukv.reshape(MLA_KV_LORA, MLA_HEADS, MLA_NOPE + MLA_V)
    pad_half = lambda a: jnp.pad(a, ((0, 0), (0, 0), (0, LANES - a.shape[2]))).reshape(MLA_KV_LORA, MLA_PAD_W)
    wk = pad_half(ukv[:, :, :MLA_NOPE])
    wv = jnp.pad(ukv[:, :, MLA_NOPE:], ((0, 0), (0, 0), (0, MLA_VT_ROWS - MLA_V)))
    wv = wv.reshape(MLA_KV_LORA, MLA_HEADS * MLA_VT_ROWS).T
    wa = w_out[:MLA_W].reshape(MLA_HEADS, MLA_V, D_MODEL)
    wa = jnp.pad(wa, ((0, 0), (0, LANES - MLA_V), (0, 0))).reshape(MLA_PAD_W, D_MODEL)
    wb, wc = w_out[MLA_W:MLA_W + ML_W], w_out[MLA_W + ML_W:]
    bf = lambda a: a.astype(BF16)
    return bf(w_p), bf(w_gates_t), bf(uq), bf(wk), bf(wv), bf(wa), bf(wb), bf(wc)


def kernel(x, mem, w_in, mla_q_norm, mla_w_uq, mla_kv_norm, mla_w_ukv, ml_conv_w, ml_conv_b, ml_b_i, ml_b_f,
           w_out, x_w_q, x_w_kv, x_w_o, w_gate_up, w_down, norm_mix_pre, norm_mix_post, norm_mem,
           norm_x_pre, norm_x_post, norm_ffn_pre, norm_ffn_post):
    batch, seq, _ = x.shape
    depth = w_in.shape[0]
    t_all = batch * seq
    assert seq % SWA_TILE == 0 and seq % TOK_TILE == 0 and seq % WIDE_TILE == 0
    assert WIDE_TILE % FLASH_KEY_TILE == 0
    assert SWA_TILE == SW_DILATIONS[-1] * SW_BLOCK
    cm, sm, cs, ss = _rope_tables(seq)
    row = lambda a: a.reshape(1, -1)
    colv = lambda a: a.reshape(-1, 1)
    xt = x.reshape(t_all, D_MODEL)
    mem2 = mem.reshape(batch * N_MEM, D_MODEL)
    w_p_all, w_gt_all, uq_all, wk_all, wv_all, wa_all, wb_all, wc_all = jax.vmap(_prep_layer)(
        w_in, mla_w_uq, mla_w_ukv, w_out)
    x_w_q, x_w_o, w_gate_up, w_down = (a.astype(BF16) for a in (x_w_q, x_w_o, w_gate_up, w_down))
    kv_all = _memkv(mem2, norm_mem.reshape(depth, 1, D_MODEL), x_w_kv.astype(BF16))
    for l in range(depth):
        w_p, w_gt, uq, wk, wv, wa, wb, wc = (a[l] for a in (w_p_all, w_gt_all, uq_all, wk_all, wv_all,
                                                            wa_all, wb_all, wc_all))
        q, k, v, mqk, mv, mo, gates_t, sq, sk, sv = _inproj(
            xt, row(norm_mix_pre[l]), w_p, w_gt, row(mla_q_norm[l]), uq, row(mla_kv_norm[l]), wk, wv,
            cm, sm, cs, ss, seq)
        b3 = lambda a: a.reshape(batch, seq, a.shape[-1])
        ya = _flash(q, k, v, batch, seq)
        yb = _mlstm(b3(mqk), mv, b3(mo), gates_t, ml_conv_w[l], row(ml_conv_b[l]), colv(ml_b_i[l]),
                    colv(ml_b_f[l])).reshape(t_all, ML_W)
        yc = _swa(sq, sk, sv, batch, seq)
        xt = _outproj(xt, ya, yb, yc, wa, wb, wc, row(norm_mix_post[l]))
        xt = _xattn(xt, row(norm_x_pre[l]), x_w_q[l], kv_all[l], x_w_o[l], row(norm_x_post[l]), seq)
        xt = _ffn(xt, row(norm_ffn_pre[l]), w_gate_up[l], w_down[l], row(norm_ffn_post[l]))
    return xt.reshape(batch, seq, D_MODEL)
```

```python
import functools

import jax
import jax.numpy as jnp
import numpy as np
from jax import lax
from jax.experimental import pallas as pl
from jax.experimental.pallas import tpu as pltpu

F32 = jnp.float32
BF16 = jnp.bfloat16

D_MODEL = 1024
N_MEM = 256
ROPE_THETA = 500000.0
EPS = 1e-6
NEG = -1e30
LOG2_E = 1.4426950408889634
LANES = 128

MLA_HEADS, MLA_NOPE, MLA_ROPE, MLA_V = 8, 64, 32, 64
MLA_Q_LORA, MLA_KV_LORA = 256, 128
ML_HEADS, ML_DK, ML_DV, ML_CONV, ML_CHUNK = 4, 64, 64, 4, 128
SW_HEADS, SW_HD = 4, 64
SW_ROT = SW_HD // 4
SW_DILATIONS = (1, 4, 16)
SW_BLOCK = 128
X_HEADS = 4
X_HD = D_MODEL // X_HEADS
D_FF = 2816

MLA_W = MLA_HEADS * MLA_V
ML_W = ML_HEADS * ML_DV
SW_W = SW_HEADS * SW_HD
MLA_PAD_W = MLA_HEADS * LANES

IN_SIZES = (MLA_Q_LORA, MLA_KV_LORA, MLA_ROPE, ML_HEADS * ML_DK, ML_HEADS * ML_DK, ML_W, ML_W,
            ML_HEADS, ML_HEADS, SW_W, SW_W, SW_W)
IN_OFFS = tuple(int(v) for v in np.cumsum((0,) + IN_SIZES))

P_CQ = 0
P_CKV = 256
P_KROPE = 384
P_MLQK = 512
P_MLO = 1024
P_SWQ = 1280
P_SWK = 1536
P_SWV = 1792
P_TOTAL = 2048
ML_VT_ROWS = 80
ML_T_ROWS = ML_HEADS * ML_VT_ROWS + 16

TOK_TILE = 512
WIDE_TILE = 1024
FLASH_TILE = 1024
FLASH_KEY_TILE = 256
MLA_VT_ROWS = 80
SWA_TILE = 2048
SWA_UNROLL = 8
FFN_CHUNKS = 2
ML_CHUNKS_PER_STEP = 4
VMEM_LIMIT = 56 * 1024 * 1024


def _rms(x, g):
    return x * lax.rsqrt(jnp.mean(x * x, axis=-1, keepdims=True) + EPS) * g


def _dot(a, b):
    return jnp.dot(a, b, preferred_element_type=F32)


def _dot_nt(a, b):
    return lax.dot_general(a, b, (((1,), (1,)), ((), ())), preferred_element_type=F32)


def _dot_tn(a, b):
    return lax.dot_general(a, b, (((0,), (0,)), ((), ())), preferred_element_type=F32)


def _params(*sem):
    return pltpu.CompilerParams(dimension_semantics=sem, vmem_limit_bytes=VMEM_LIMIT)


def _inproj_kernel(x_ref, g_ref, w_ref, wgt_ref, qg_ref, wuq_ref, kvg_ref, wk_ref, wvt_ref,
                   cm_ref, sm_ref, cs_ref, ss_ref,
                   q_ref, k_ref, vt_ref, mqk_ref, mv_ref, mo_ref, gr_ref, sq_ref, sk_ref, sv_ref):
    tm = x_ref.shape[0]
    h = _rms(x_ref[...], g_ref[...]).astype(BF16)

    def proj(a, b):
        return _dot(h, w_ref[:, a:b])

    lane = lax.broadcasted_iota(jnp.int32, (tm, LANES), 1)
    cm, sm = cm_ref[...], sm_ref[...]
    mla_first = lane < MLA_NOPE + MLA_ROPE // 2

    def rope_mla(t):
        rot = jnp.where(mla_first, pltpu.roll(t, LANES - MLA_ROPE // 2, 1), pltpu.roll(t, MLA_ROPE // 2, 1))
        return t * cm + rot * sm

    cs, ss = cs_ref[...], ss_ref[...]
    sw_first = (lane % SW_HD) < SW_ROT // 2

    def rope_sw(t):
        rot = jnp.where(sw_first, pltpu.roll(t, LANES - SW_ROT // 2, 1), pltpu.roll(t, SW_ROT // 2, 1))
        return t * cs + rot * ss

    cqn = _rms(proj(P_CQ, P_CQ + MLA_Q_LORA), qg_ref[...]).astype(BF16)
    q_scale = (MLA_NOPE + MLA_ROPE) ** -0.5 * LOG2_E
    for hh in range(MLA_HEADS):
        sl = slice(hh * LANES, (hh + 1) * LANES)
        q_ref[hh] = (rope_mla(_dot(cqn, wuq_ref[:, sl])) * q_scale).astype(BF16)
    kvn = _rms(proj(P_CKV, P_CKV + MLA_KV_LORA), kvg_ref[...]).astype(BF16)
    kpe = rope_mla(proj(P_KROPE, P_KROPE + LANES))
    one_col = (lax.broadcasted_iota(jnp.int32, (MLA_VT_ROWS, 1), 0) == MLA_V).astype(F32)
    for hh in range(MLA_HEADS):
        sl = slice(hh * LANES, (hh + 1) * LANES)
        k_ref[hh] = (_dot(kvn, wk_ref[:, sl]) + kpe).astype(BF16)
        rows = slice(hh * MLA_VT_ROWS, (hh + 1) * MLA_VT_ROWS)
        for u in range(tm // FLASH_KEY_TILE):
            toks = slice(u * FLASH_KEY_TILE, (u + 1) * FLASH_KEY_TILE)
            vt_ref[hh, u] = (_dot_nt(wvt_ref[rows, :], kvn[toks, :]) + one_col).astype(BF16)

    mqk_ref[...] = proj(P_MLQK, P_MLQK + 2 * ML_HEADS * ML_DK)
    mo_ref[...] = proj(P_MLO, P_MLO + ML_W)
    tr = _dot_nt(wgt_ref[...], h)
    n_v = ML_HEADS * ML_VT_ROWS
    v_row = lax.broadcasted_iota(jnp.int32, (n_v, 1), 0)
    mv_ref[...] = (tr[0:n_v, :] + ((v_row % ML_VT_ROWS) == ML_DV).astype(F32)).astype(BF16)
    gr_ref[...] = tr[n_v:n_v + 2 * ML_HEADS, :]

    for half in range(SW_W // LANES):
        sl = slice(half * LANES, (half + 1) * LANES)
        sq_ref[half] = rope_sw(proj(P_SWQ + half * LANES, P_SWQ + (half + 1) * LANES)) * (SW_HD ** -0.5 * LOG2_E)
        sk_ref[half] = rope_sw(proj(P_SWK + half * LANES, P_SWK + (half + 1) * LANES))
        sv_ref[half] = proj(P_SWV + half * LANES, P_SWV + (half + 1) * LANES)


def _inproj(x, g, w, wgt, qg, wuq, kvg, wk, wv, cm, sm, cs, ss, seq):
    t_all = x.shape[0]
    tm = WIDE_TILE
    pos_tiles = seq // tm
    tok = lambda w_: pl.BlockSpec((tm, w_), lambda i: (i, 0))
    full = lambda a: pl.BlockSpec(a.shape, lambda i: (0,) * a.ndim)
    pos = pl.BlockSpec((tm, LANES), lambda i: (i % pos_tiles, 0))
    slabs = pl.BlockSpec((SW_W // LANES, tm, LANES), lambda i: (0, i, 0))
    out_shapes = (
        jax.ShapeDtypeStruct((MLA_HEADS, t_all, LANES), BF16),
        jax.ShapeDtypeStruct((MLA_HEADS, t_all, LANES), BF16),
        jax.ShapeDtypeStruct((MLA_HEADS, t_all // FLASH_KEY_TILE, MLA_VT_ROWS, FLASH_KEY_TILE), BF16),
        jax.ShapeDtypeStruct((t_all, 2 * ML_HEADS * ML_DK), F32),
        jax.ShapeDtypeStruct((t_all // seq, ML_HEADS * ML_VT_ROWS, seq), BF16),
        jax.ShapeDtypeStruct((t_all, ML_W), F32),
        jax.ShapeDtypeStruct((t_all // seq, 2 * ML_HEADS, seq), F32),
        jax.ShapeDtypeStruct((SW_W // LANES, t_all, LANES), F32),
        jax.ShapeDtypeStruct((SW_W // LANES, t_all, LANES), F32),
        jax.ShapeDtypeStruct((SW_W // LANES, t_all, LANES), F32),
    )
    heads = pl.BlockSpec((MLA_HEADS, tm, LANES), lambda i: (0, i, 0))
    out_specs = (heads, heads,
                 pl.BlockSpec((MLA_HEADS, tm // FLASH_KEY_TILE, MLA_VT_ROWS, FLASH_KEY_TILE), lambda i: (0, i, 0, 0)),
                 tok(2 * ML_HEADS * ML_DK),
                 pl.BlockSpec((None, ML_HEADS * ML_VT_ROWS, tm), lambda i: (i // pos_tiles, 0, i % pos_tiles)),
                 tok(ML_W), pl.BlockSpec((None, 2 * ML_HEADS, tm), lambda i: (i // pos_tiles, 0, i % pos_tiles)),
                 slabs, slabs, slabs)
    return pl.pallas_call(
        _inproj_kernel,
        grid=(t_all // tm,),
        in_specs=[tok(D_MODEL), full(g), full(w), full(wgt), full(qg), full(wuq), full(kvg), full(wk), full(wv),
                  pos, pos, pos, pos],
        out_specs=out_specs,
        out_shape=out_shapes,
        compiler_params=_params("parallel"),
        name="inproj",
    )(x, g, w, wgt, qg, wuq, kvg, wk, wv, cm, sm, cs, ss)


def _flash_kernel(q_ref, k_ref, vt_ref, o_ref, m_scr, acc_scr, s_scr, cmax_scr):
    tq = q_ref.shape[0]
    tk = FLASH_KEY_TILE
    per_q = tq // tk
    i = pl.program_id(2)
    n_tiles = (i + 1) * per_q
    qt = q_ref[...].astype(F32).T.astype(BF16)
    m_scr[...] = jnp.full(m_scr.shape, NEG, F32)
    acc_scr[...] = jnp.zeros(acc_scr.shape, F32)

    key_l = lax.broadcasted_iota(jnp.int32, (tk, tk), 0)
    qry_l = lax.broadcasted_iota(jnp.int32, (tk, tk), 1)

    def scores(j, buf, d):
        start = pl.multiple_of(j * tk, tk)
        c0 = 0 if d is None else d * tk
        st = _dot(k_ref[pl.ds(start, tk), :], qt[:, c0:])
        if d is not None:
            tri = jnp.where(key_l <= qry_l, st[:, :tk], NEG)
            st = tri if st.shape[1] == tk else jnp.concatenate([tri, st[:, tk:]], axis=1)
        s_scr[buf, :, c0:] = st
        cmax_scr[buf, :, c0:] = jnp.max(st, axis=0, keepdims=True)

    def accumulate(j, buf, d):
        cols = slice(0 if d is None else d * tk, None)
        m_old = m_scr[:, cols]
        m_new = jnp.maximum(m_old, cmax_scr[buf, :, cols])
        p = jnp.exp2(s_scr[buf, :, cols] - m_new).astype(BF16)
        acc_scr[:, cols] = acc_scr[:, cols] * jnp.exp2(m_old - m_new) + _dot(vt_ref[j], p)
        m_scr[:, cols] = m_new

    def pairs(pp, offsets, held0):
        held = [held0, None]
        for c, (d_a, d_b) in enumerate(offsets):
            scores(2 * (pp + c) + 1, 1, d_a)
            held[1] = d_a
            accumulate(2 * (pp + c), 0, held[0])
            scores(2 * (pp + c) + 2, 0, d_b)
            held[0] = d_b
            accumulate(2 * (pp + c) + 1, 1, held[1])
        return held[0]

    def finish(held0):
        scores(n_tiles - 1, 1, per_q - 1)
        accumulate(n_tiles - 2, 0, held0)
        accumulate(n_tiles - 1, 1, per_q - 1)
        acc = acc_scr[...]
        out_t = acc[0:MLA_V, :] / acc[MLA_V:MLA_V + 1, :]
        o_ref[...] = jnp.concatenate([out_t, jnp.zeros_like(out_t)], axis=0).T.astype(BF16)

    assert per_q == 4

    @pl.when(i == 0)
    def _():
        scores(0, 0, 0)
        finish(pairs(0, [(1, 2)], 0))

    tail = [(None, None), (None, 0), (1, 2)]
    plain = [(None, None)] * 2
    n_eights = lax.shift_right_logical(jnp.maximum(i - 1, 0), 1)

    def body(g, carry):
        pairs(4 * g, plain + plain, None)
        return carry

    @pl.when((i > 0) & ((i & 1) == 1))
    def _():
        scores(0, 0, None)
        lax.fori_loop(0, n_eights, body, 0)
        finish(pairs(4 * n_eights, tail, None))

    @pl.when((i > 0) & ((i & 1) == 0))
    def _():
        scores(0, 0, None)
        lax.fori_loop(0, n_eights, body, 0)
        finish(pairs(4 * n_eights, plain + tail, None))


def _flash(q, k, vt, b, s):
    tq, tk = FLASH_TILE, FLASH_KEY_TILE
    assert s % tq == 0 and tq % (2 * tk) == 0
    nq = s // tq
    rows = pl.BlockSpec((None, tq, LANES), lambda bb, hh, i: (hh, bb * nq + i, 0))
    return pl.pallas_call(
        _flash_kernel,
        grid=(b, MLA_HEADS, nq),
        in_specs=[rows,
                  pl.BlockSpec((None, s, LANES), lambda bb, hh, i: (hh, bb, 0)),
                  pl.BlockSpec((None, s // tk, MLA_VT_ROWS, tk), lambda bb, hh, i: (hh, bb, 0, 0))],
        out_specs=rows,
        out_shape=jax.ShapeDtypeStruct((MLA_HEADS, b * s, LANES), BF16),
        scratch_shapes=[pltpu.VMEM((1, tq), F32), pltpu.VMEM((MLA_VT_ROWS, tq), F32),
                        pltpu.VMEM((2, tk, tq), F32), pltpu.VMEM((2, 1, tq), F32)],
        compiler_params=_params("parallel", "parallel", "arbitrary"),
        name="mla_flash",
    )(q, k, vt)


def _log_sigmoid(z):
    return jnp.minimum(z, 0.0) - jnp.log1p(jnp.exp(-jnp.abs(z)))


def _mlstm_kernel(qkc_ref, qkp_ref, vt_ref, o_ref, g_ref, cw_ref, cb_ref, bi_ref, bf_ref,
                  y_ref, c_scr, m_scr):
    L = ML_CHUNK
    nb = qkc_ref.shape[0]
    c = pl.program_id(0)

    @pl.when(c == 0)
    def _():
        c_scr[...] = jnp.zeros(c_scr.shape, F32)
        m_scr[...] = jnp.full(m_scr.shape, NEG, F32)

    src = lax.broadcasted_iota(jnp.int32, (L, L), 0)
    tgt = lax.broadcasted_iota(jnp.int32, (L, L), 1)
    causal = src <= tgt
    pick = (lax.broadcasted_iota(jnp.int32, (L, ML_HEADS * L), 0)
            == lax.broadcasted_iota(jnp.int32, (L, ML_HEADS * L), 1) // L).astype(BF16)

    def chunk(sub):
        rows_t = slice(sub * L, (sub + 1) * L)
        streams = [(b, hh) for b in range(nb) for hh in range(ML_HEADS)]
        qk, b_all, li_all, x_col = [], [], [], []
        for b in range(nb):
            cur = qkc_ref[b, rows_t, :]
            prev = jnp.where(c > 0, qkp_ref[b], 0.0) if sub == 0 else qkc_ref[b, (sub - 1) * L:sub * L, :]
            row_w = lax.broadcasted_iota(jnp.int32, cur.shape, 0)
            conv = cb_ref[...] + cw_ref[ML_CONV - 1:ML_CONV, :] * cur
            for sh in range(1, ML_CONV):
                shifted = jnp.where(row_w >= sh, pltpu.roll(cur, sh, 0), pltpu.roll(prev, sh, 0))
                conv = conv + cw_ref[ML_CONV - 1 - sh:ML_CONV - sh, :] * shifted
            qk.append(conv * jax.nn.sigmoid(conv))
            gates = g_ref[b, :, rows_t]
            li = gates[0:ML_HEADS, :] + bi_ref[...]
            lf = _log_sigmoid(gates[ML_HEADS:, :] + bf_ref[...])
            lane_g = lax.broadcasted_iota(jnp.int32, lf.shape, 1)
            step = 1
            while step < L:
                lf = lf + jnp.where(lane_g >= step, pltpu.roll(lf, step, 1), 0.0)
                step *= 2
            li_all.append(li)
            b_all.append(lf)
            x = jnp.concatenate([li - lf, jnp.zeros((L - ML_HEADS, L), F32)], axis=0).T
            spread = jnp.zeros((L, ML_HEADS * L), F32)
            for _ in range(3):
                part = x.astype(BF16)
                spread = spread + _dot(part, pick)
                x = x - part.astype(F32)
            x_col.append(spread)

        st = []
        for n, (b, hh) in enumerate(streams):
            b_row = b_all[b][hh:hh + 1, :]
            li_row = li_all[b][hh:hh + 1, :]
            m_prev = m_scr[n:n + 1, 0:1]
            d_t = jnp.where(causal, x_col[b][:, hh * L:(hh + 1) * L] + b_row, NEG)
            inter = b_row + m_prev
            m_t = jnp.maximum(inter, jnp.max(d_t, axis=0, keepdims=True))
            b_last = b_row[:, L - 1:L]
            g = b_last - b_row + li_row
            m_new = jnp.maximum(b_last + m_prev, jnp.max(g, axis=1, keepdims=True))
            kf = qk[b][:, (ML_HEADS + hh) * ML_DK:(ML_HEADS + hh + 1) * ML_DK] * ML_DK ** -0.5
            vt = vt_ref[b, hh * ML_VT_ROWS:(hh + 1) * ML_VT_ROWS, rows_t]
            st.append(dict(
                qh=qk[b][:, hh * ML_DK:(hh + 1) * ML_DK].astype(BF16),
                kh=kf.astype(BF16), vt=vt,
                vw=(vt.astype(F32) * jnp.exp(g - m_new)).astype(BF16),
                c_t=c_scr[n],
                w_intra=jnp.exp(d_t - m_t), w_inter=jnp.exp(inter - m_t), floor=jnp.exp(-m_t),
                decay=jnp.exp(b_last + m_prev - m_new), m_new=m_new))

        for s_ in st:
            s_["sraw"] = _dot_nt(s_["kh"], s_["qh"])
            s_["cross"] = _dot_nt(s_["c_t"].astype(BF16), s_["qh"])
        for s_ in st:
            s_["s"] = s_["sraw"] * s_["w_intra"]
        for s_ in st:
            s_["intra"] = _dot(s_["vt"], s_["s"].astype(BF16))
            s_["kv"] = _dot(s_["vw"], s_["kh"])
        h_t = []
        for n, s_ in enumerate(st):
            num = s_["intra"][:ML_DV, :] + s_["w_inter"] * s_["cross"][:ML_DV, :]
            nq = jnp.sum(s_["s"], axis=0, keepdims=True) + s_["w_inter"] * s_["cross"][ML_DV:ML_DV + 1, :]
            h_t.append(num / jnp.maximum(jnp.abs(nq), s_["floor"]))
            c_scr[n] = s_["decay"] * s_["c_t"] + s_["kv"]
            m_scr[n:n + 1, :] = jnp.broadcast_to(s_["m_new"], (1, LANES))
        for b in range(nb):
            hout = jnp.concatenate(h_t[b * ML_HEADS:(b + 1) * ML_HEADS], axis=0).T
            y_ref[b, rows_t, :] = (jax.nn.sigmoid(o_ref[b, rows_t, :]) * hout).astype(BF16)

    for sub in range(qkc_ref.shape[1] // L):
        chunk(sub)


def _mlstm(qk_pre, vt, o_pre, gates_t, conv_w, conv_b, b_i, b_f):
    batch, seq, wqk = qk_pre.shape
    L = ML_CHUNK
    step = ML_CHUNKS_PER_STEP * L
    full = lambda a: pl.BlockSpec(a.shape, lambda c: (0,) * a.ndim)
    chunk = lambda w_: pl.BlockSpec((batch, step, w_), lambda c: (0, c, 0))
    lanes = lambda r: pl.BlockSpec((batch, r, step), lambda c: (0, 0, c))
    return pl.pallas_call(
        _mlstm_kernel,
        grid=(seq // step,),
        in_specs=[chunk(wqk),
                  pl.BlockSpec((batch, L, wqk), lambda c: (0, jnp.maximum(ML_CHUNKS_PER_STEP * c - 1, 0), 0)),
                  lanes(ML_HEADS * ML_VT_ROWS), chunk(ML_W), lanes(2 * ML_HEADS),
                  full(conv_w), full(conv_b), full(b_i), full(b_f)],
        out_specs=chunk(ML_W),
        out_shape=jax.ShapeDtypeStruct((batch, seq, ML_W), BF16),
        scratch_shapes=[pltpu.VMEM((batch * ML_HEADS, ML_VT_ROWS, ML_DK), F32),
                        pltpu.VMEM((batch * ML_HEADS, LANES), F32)],
        compiler_params=_params("arbitrary"),
        name="mlstm",
    )(qk_pre, qk_pre, vt, o_pre, gates_t, conv_w, conv_b, b_i, b_f)


def _swa_kernel(q_ref, k_ref, v_ref, y_ref, kk_scr, vv_scr, o_scr, l_scr):
    blk, tile = SW_BLOCK, SWA_TILE
    n_slab = SW_W // LANES
    n = pl.program_id(1)

    @pl.when(n == 0)
    def _():
        kk_scr[:, 0:tile, :] = jnp.zeros((n_slab, tile, LANES), F32)
        vv_scr[:, 0:tile, :] = jnp.zeros((n_slab, tile, LANES), F32)

    @pl.when(n > 0)
    def _():
        kk_scr[:, 0:tile, :] = kk_scr[:, tile:2 * tile, :]
        vv_scr[:, 0:tile, :] = vv_scr[:, tile:2 * tile, :]

    kk_scr[:, tile:2 * tile, :] = k_ref[...]
    vv_scr[:, tile:2 * tile, :] = v_ref[...]

    key = lax.broadcasted_iota(jnp.int32, (2 * blk, blk), 0)
    qry = lax.broadcasted_iota(jnp.int32, (2 * blk, blk), 1)
    dim =lax.broadcasted_iota(jnp.int32, (LANES, blk), 0)
    head_rows = [dim < SW_HD, dim >= SW_HD]

    for g, dil in enumerate(SW_DILATIONS):
        per_res = tile // (blk * dil)
        shift = per_res.bit_length() - 1

        def block_pair(it, carry, g=g, dil=dil, per_res=per_res, shift=shift):
            def rows(start):
                return pl.ds(start, blk, stride=dil) if dil > 1 else pl.ds(pl.multiple_of(start, blk), blk)

            work = []
            for u in range(SWA_UNROLL):
                bi = SWA_UNROLL * it + u
                r = lax.shift_right_logical(bi, shift)
                j = bi & (per_res - 1)
                q0 = r + j * (blk * dil)
                lowest = jnp.where((n == 0) & (j == 0), blk, 0)
                ok = (key >= jnp.maximum(qry, lowest)) & (key <= qry + blk)
                for sl in range(n_slab):
                    prev, cur = rows(tile + q0 - blk * dil), rows(tile + q0)
                    kb = jnp.concatenate([kk_scr[sl, prev, :], kk_scr[sl, cur, :]], axis=0)
                    vb = jnp.concatenate([vv_scr[sl, prev, :], vv_scr[sl, cur, :]], axis=0)
                    work.append(dict(q0=q0, sl=sl, ok=ok, qt=q_ref[sl, rows(q0), :].T,
                                     kb=kb.astype(BF16), vt=vb.T.astype(BF16)))
            for w in work:
                w["s"] = [_dot(w["kb"], jnp.where(head_rows[hh], w["qt"], 0.0).astype(BF16)) for hh in range(2)]
            for w in work:
                w["e"], w["den"], w["lse"] = [], [], []
                for hh in range(2):
                    s = jnp.where(w["ok"], w["s"][hh], NEG)
                    cmax = jnp.max(s, axis=0, keepdims=True)
                    e = jnp.exp2(s - cmax)
                    den = jnp.sum(e, axis=0, keepdims=True)
                    w["e"].append(e.astype(BF16))
                    w["den"].append(den)
                    w["lse"].append(cmax + jnp.log2(den))
            for w in work:
                ot = [_dot(w["vt"][hh * SW_HD:(hh + 1) * SW_HD, :], w["e"][hh]) / w["den"][hh] for hh in range(2)]
                lt = [jnp.broadcast_to(w["lse"][hh], (SW_HD, blk)) for hh in range(2)]
                o_scr[g, w["sl"], rows(w["q0"]), :] = jnp.concatenate(ot, axis=0).T
                l_scr[g, w["sl"], rows(w["q0"]), :] = jnp.concatenate(lt, axis=0).T
            return carry

        lax.fori_loop(0, tile // blk // SWA_UNROLL, block_pair, 0)

    for sl in range(n_slab):
        ls = [l_scr[g, sl] for g in range(len(SW_DILATIONS))]
        mx = functools.reduce(jnp.maximum, ls)
        ws = [jnp.exp2(l - mx) for l in ls]
        num = functools.reduce(lambda a, b: a + b, [w * o_scr[g, sl] for g, w in enumerate(ws)])
        y_ref[:, sl * LANES:(sl + 1) * LANES] = (num / functools.reduce(lambda a, b: a + b, ws)).astype(BF16)


def _swa(q, k, v, batch, seq):
    tile = SWA_TILE
    n_slab = SW_W // LANES
    per_batch = seq // tile
    spec = pl.BlockSpec((n_slab, tile, LANES), lambda bb, n: (0, bb * per_batch + n, 0))
    n_br = len(SW_DILATIONS)
    return pl.pallas_call(
        _swa_kernel,
        grid=(batch, per_batch),
        in_specs=[spec, spec, spec],
        out_specs=pl.BlockSpec((tile, SW_W), lambda bb, n: (bb * per_batch + n, 0)),
        out_shape=jax.ShapeDtypeStruct((batch * seq, SW_W), BF16),
        scratch_shapes=[pltpu.VMEM((n_slab, 2 * tile, LANES), F32), pltpu.VMEM((n_slab, 2 * tile, LANES), F32),
                        pltpu.VMEM((n_br, n_slab, tile, LANES), F32), pltpu.VMEM((n_br, n_slab, tile, LANES), F32)],
        compiler_params=_params("parallel", "arbitrary"),
        name="swa",
    )(q, k, v)


def _outproj_kernel(x_ref, ya_ref, yb_ref, yc_ref, wa_ref, wb_ref, wc_ref, g_ref, out_ref):
    ya = jnp.concatenate([ya_ref[hh] for hh in range(MLA_HEADS)], axis=1)
    y = _dot(ya, wa_ref[...]) + _dot(yb_ref[...], wb_ref[...]) + _dot(yc_ref[...], wc_ref[...])
    out_ref[...] = x_ref[...] + _rms(y, g_ref[...])


def _outproj(x, ya, yb, yc, wa, wb, wc, g):
    t_all = x.shape[0]
    tm = WIDE_TILE
    tok = lambda w_: pl.BlockSpec((tm, w_), lambda i: (i, 0))
    full = lambda a: pl.BlockSpec(a.shape, lambda i: (0,) * a.ndim)
    return pl.pallas_call(
        _outproj_kernel,
        grid=(t_all // tm,),
        in_specs=[tok(D_MODEL), pl.BlockSpec((MLA_HEADS, tm, LANES), lambda i: (0, i, 0)), tok(ML_W), tok(SW_W),
                  full(wa), full(wb), full(wc), full(g)],
        out_specs=tok(D_MODEL),
        out_shape=jax.ShapeDtypeStruct((t_all, D_MODEL), F32),
        compiler_params=_params("parallel"),
        name="outproj",
    )(x, ya, yb, yc, wa, wb, wc, g)


def _memkv_kernel(mem_ref, g_ref, w_ref, kv_ref):
    kv_ref[...] = _dot(_rms(mem_ref[...], g_ref[...]).astype(BF16), w_ref[...]).astype(BF16)


def _memkv(mem, g, w):
    n = mem.shape[0]
    depth = w.shape[0]
    return pl.pallas_call(
        _memkv_kernel,
        grid=(depth,),
        in_specs=[pl.BlockSpec(mem.shape, lambda l: (0, 0)),
                  pl.BlockSpec((None, 1, D_MODEL), lambda l: (l, 0, 0)),
                  pl.BlockSpec((None, D_MODEL, 2 * D_MODEL), lambda l: (l, 0, 0))],
        out_specs=pl.BlockSpec((None, n, 2 * D_MODEL), lambda l: (l, 0, 0)),
        out_shape=jax.ShapeDtypeStruct((depth, n, 2 * D_MODEL), BF16),
        compiler_params=_params("parallel"),
        name="memkv",
    )(mem, g, w)


def _xattn_kernel(x_ref, gpre_ref, wq_ref, kv_ref, wo_ref, gpost_ref, out_ref, o_scr):
    x = x_ref[...]
    h = _rms(x, gpre_ref[...]).astype(BF16)
    q = (_dot(h, wq_ref[...]) * X_HD ** -0.5).astype(BF16)
    for hh in range(X_HEADS):
        hs = slice(hh * X_HD, (hh + 1) * X_HD)
        s = _dot_nt(q[:, hs], kv_ref[:, hs])
        e = jnp.exp(s - jnp.max(s, axis=1, keepdims=True))
        den = jnp.sum(e, axis=1, keepdims=True)
        o_scr[:, hs] = (_dot(e.astype(BF16), kv_ref[:, D_MODEL + hh * X_HD:D_MODEL + (hh + 1) * X_HD]) / den).astype(BF16)
    y = _dot(o_scr[...], wo_ref[...])
    out_ref[...] = x + _rms(y, gpost_ref[...])


def _xattn(x, gpre, wq, kv, wo, gpost, seq):
    t_all = x.shape[0]
    tm = WIDE_TILE
    per_batch = seq // tm
    tok = pl.BlockSpec((tm, D_MODEL), lambda i: (i, 0))
    full = lambda a: pl.BlockSpec(a.shape, lambda i: (0,) * a.ndim)
    return pl.pallas_call(
        _xattn_kernel,
        grid=(t_all // tm,),
        in_specs=[tok, full(gpre), full(wq), pl.BlockSpec((N_MEM, 2 * D_MODEL), lambda i: (i // per_batch, 0)),
                  full(wo), full(gpost)],
        out_specs=tok,
        out_shape=jax.ShapeDtypeStruct((t_all, D_MODEL), F32),
        scratch_shapes=[pltpu.VMEM((tm, D_MODEL), BF16)],
        compiler_params=_params("parallel"),
        name="xattn",
    )(x, gpre, wq, kv, wo, gpost)


def _ffn_kernel(x_ref, gpre_ref, wgu_ref, wd_ref, gpost_ref, out_ref):
    x = x_ref[...]
    h = _rms(x, gpre_ref[...]).astype(BF16)
    tf = D_FF // FFN_CHUNKS
    y = None
    for f in range(FFN_CHUNKS):
        gate = _dot(h, wgu_ref[:, f * tf:(f + 1) * tf])
        up = _dot(h, wgu_ref[:, D_FF + f * tf:D_FF + (f + 1) * tf])
        part = _dot((gate * jax.nn.sigmoid(gate) * up).astype(BF16), wd_ref[f * tf:(f + 1) * tf, :])
        y = part if y is None else y + part
    out_ref[...] = x + _rms(y, gpost_ref[...])


def _ffn(x, gpre, w_gate_up, w_down, gpost):
    t_all = x.shape[0]
    tm = TOK_TILE
    tok = pl.BlockSpec((tm, D_MODEL), lambda i: (i, 0))
    full = lambda a: pl.BlockSpec(a.shape, lambda i: (0,) * a.ndim)
    return pl.pallas_call(
        _ffn_kernel,
        grid=(t_all // tm,),
        in_specs=[tok, full(gpre), full(w_gate_up), full(w_down), full(gpost)],
        out_specs=tok,
        out_shape=jax.ShapeDtypeStruct((t_all, D_MODEL), F32),
        compiler_params=_params("parallel"),
        name="ffn",
    )(x, gpre, w_gate_up, w_down, gpost)


def _rope_tables(seq):
    pos = jnp.arange(seq, dtype=F32)[:, None]

    def cos_sin(d_rot):
        inv = ROPE_THETA ** (-jnp.arange(0, d_rot, 2, dtype=F32) / d_rot)
        ang = pos * inv[None, :]
        return jnp.cos(ang), jnp.sin(ang)

    c, s = cos_sin(MLA_ROPE)
    ones = jnp.ones((seq, MLA_NOPE), F32)
    tail = jnp.ones((seq, LANES - MLA_NOPE - MLA_ROPE), F32)
    cm = jnp.concatenate([ones, c, c, tail], axis=1)
    sm = jnp.concatenate([0 * ones, -s, s, 0 * tail], axis=1)
    c, s = cos_sin(SW_ROT)
    rest = jnp.ones((seq, SW_HD - SW_ROT), F32)
    cs = jnp.tile(jnp.concatenate([c, c, rest], axis=1), (1, LANES // SW_HD))
    ss = jnp.tile(jnp.concatenate([-s, s, 0 * rest], axis=1), (1, LANES // SW_HD))
    return cm, sm, cs, ss


def _prep_layer(w_in, w_uq, w_ukv, w_out):
    o = IN_OFFS
    col = lambda i: w_in[:, o[i]:o[i + 1]]
    z = lambda n: jnp.zeros((D_MODEL, n), F32)
    w_p = jnp.concatenate([
        col(0), col(1),
        z(MLA_NOPE), col(2), z(LANES - MLA_NOPE - MLA_ROPE),
        col(3), col(4), col(6), col(9), col(10), col(11)], axis=1)
    assert w_p.shape[1] == P_TOTAL
    mlv = jnp.pad(col(5).reshape(D_MODEL, ML_HEADS, ML_DV), ((0, 0), (0, 0), (0, ML_VT_ROWS - ML_DV)))
    w_gates_t = jnp.concatenate([mlv.reshape(D_MODEL, ML_HEADS * ML_VT_ROWS), col(7), col(8),
                                 z(ML_T_ROWS - ML_HEADS * ML_VT_ROWS - 2 * ML_HEADS)], axis=1).T
    uq = w_uq.reshape(MLA_Q_LORA, MLA_HEADS, MLA_NOPE + MLA_ROPE)
    uq = jnp.pad(uq, ((0, 0), (0, 0), (0, LANES - MLA_NOPE - MLA_ROPE))).reshape(MLA_Q_LORA, MLA_PAD_W)
    ukv = w_ukv.reshape(MLA_KV_LORA, MLA_HEADS, MLA_NOPE + MLA_V)
    pad_half = lambda a: jnp.pad(a, ((0, 0), (0, 0), (0, LANES - a.shape[2]))).reshape(MLA_KV_LORA, MLA_PAD_W)
    wk = pad_half(ukv[:, :, :MLA_NOPE])
    wv = jnp.pad(ukv[:, :, MLA_NOPE:], ((0, 0), (0, 0), (0, MLA_VT_ROWS - MLA_V)))
    wv = wv.reshape(MLA_KV_LORA, MLA_HEADS * MLA_VT_ROWS).T
    wa = w_out[:MLA_W].reshape(MLA_HEADS, MLA_V, D_MODEL)
    wa = jnp.pad(wa, ((0, 0), (0, LANES - MLA_V), (0, 0))).reshape(MLA_PAD_W, D_MODEL)
    wb, wc = w_out[MLA_W:MLA_W + ML_W], w_out[MLA_W + ML_W:]
    bf = lambda a: a.astype(BF16)
    return bf(w_p), bf(w_gates_t), bf(uq), bf(wk), bf(wv), bf(wa), bf(wb), bf(wc)


def kernel(x, mem, w_in, mla_q_norm, mla_w_uq, mla_kv_norm, mla_w_ukv, ml_conv_w, ml_conv_b, ml_b_i, ml_b_f,
           w_out, x_w_q, x_w_kv, x_w_o, w_gate_up, w_down, norm_mix_pre, norm_mix_post, norm_mem,
           norm_x_pre, norm_x_post, norm_ffn_pre, norm_ffn_post):
    batch, seq, _ = x.shape
    depth = w_in.shape[0]
    t_all = batch * seq
    assert seq % SWA_TILE == 0 and seq % TOK_TILE == 0 and seq % WIDE_TILE == 0
    assert WIDE_TILE % FLASH_KEY_TILE == 0
    assert SWA_TILE == SW_DILATIONS[-1] * SW_BLOCK
    cm, sm, cs, ss = _rope_tables(seq)
    row = lambda a: a.reshape(1, -1)
    colv = lambda a: a.reshape(-1, 1)
    xt = x.reshape(t_all, D_MODEL)
    mem2 = mem.reshape(batch * N_MEM, D_MODEL)
    w_p_all, w_gt_all, uq_all, wk_all, wv_all, wa_all, wb_all, wc_all = jax.vmap(_prep_layer)(
        w_in, mla_w_uq, mla_w_ukv, w_out)
    x_w_q, x_w_o, w_gate_up, w_down = (a.astype(BF16) for a in (x_w_q, x_w_o, w_gate_up, w_down))
    kv_all = _memkv(mem2, norm_mem.reshape(depth, 1, D_MODEL), x_w_kv.astype(BF16))
    for l in range(depth):
        w_p, w_gt, uq, wk, wv, wa, wb, wc = (a[l] for a in (w_p_all, w_gt_all, uq_all, wk_all, wv_all,
                                                            wa_all, wb_all, wc_all))
        q, k, v, mqk, mv, mo, gates_t, sq, sk, sv = _inproj(
            xt, row(norm_mix_pre[l]), w_p, w_gt, row(mla_q_norm[l]), uq, row(mla_kv_norm[l]), wk, wv,
            cm, sm, cs, ss, seq)
        b3 = lambda a: a.reshape(batch, seq, a.shape[-1])
        ya = _flash(q, k, v, batch, seq)
        yb = _mlstm(b3(mqk), mv, b3(mo), gates_t, ml_conv_w[l], row(ml_conv_b[l]), colv(ml_b_i[l]),
                    colv(ml_b_f[l])).reshape(t_all, ML_W)
        yc = _swa(sq, sk, sv, batch, seq)
        xt = _outproj(xt, ya, yb, yc, wa, wb, wc, row(norm_mix_post[l]))
        xt = _xattn(xt, row(norm_x_pre[l]), x_w_q[l], kv_all[l], x_w_o[l], row(norm_x_post[l]), seq)
        xt = _ffn(xt, row(norm_ffn_pre[l]), w_gate_up[l], w_down[l], row(norm_ffn_post[l]))
    return xt.reshape(batch, seq, D_MODEL)
```

```python
import functools

import jax
import jax.numpy as jnp
import numpy as np
from jax import lax
from jax.experimental import pallas as pl
from jax.experimental.pallas import tpu as pltpu

F32 = jnp.float32
BF16 = jnp.bfloat16

D_MODEL = 1024
N_MEM = 256
ROPE_THETA = 500000.0
EPS = 1e-6
NEG = -1e30
LOG2_E = 1.4426950408889634
LANES = 128

MLA_HEADS, MLA_NOPE, MLA_ROPE, MLA_V = 8, 64, 32, 64
MLA_Q_LORA, MLA_KV_LORA = 256, 128
ML_HEADS, ML_DK, ML_DV, ML_CONV, ML_CHUNK = 4, 64, 64, 4, 128
SW_HEADS, SW_HD = 4, 64
SW_ROT = SW_HD // 4
SW_DILATIONS = (1, 4, 16)
SW_BLOCK = 128
X_HEADS = 4
X_HD = D_MODEL // X_HEADS
D_FF = 2816

MLA_W = MLA_HEADS * MLA_V
ML_W = ML_HEADS * ML_DV
SW_W = SW_HEADS * SW_HD
MLA_PAD_W = MLA_HEADS * LANES

IN_SIZES = (MLA_Q_LORA, MLA_KV_LORA, MLA_ROPE, ML_HEADS * ML_DK, ML_HEADS * ML_DK, ML_W, ML_W,
            ML_HEADS, ML_HEADS, SW_W, SW_W, SW_W)
IN_OFFS = tuple(int(v) for v in np.cumsum((0,) + IN_SIZES))

P_CQ = 0
P_CKV = 256
P_KROPE = 384
P_MLQK = 512
P_MLO = 1024
P_SWQ = 1280
P_SWK = 1536
P_SWV = 1792
P_TOTAL = 2048
ML_VT_ROWS = 80
ML_T_ROWS = ML_HEADS * ML_VT_ROWS + 16

TOK_TILE = 512
WIDE_TILE = 1024
FLASH_TILE = 1024
FLASH_KEY_TILE = 256
MLA_VT_ROWS = 80
SWA_TILE = 2048
SWA_UNROLL = 8
FFN_SPLITS = (0, 1536, D_FF)
ML_CHUNKS_PER_STEP = 4
VMEM_LIMIT = 56 * 1024 * 1024


def _rms(x, g):
    return x * lax.rsqrt(jnp.mean(x * x, axis=-1, keepdims=True) + EPS) * g


def _dot(a, b):
    return jnp.dot(a, b, preferred_element_type=F32)


def _dot_nt(a, b):
    return lax.dot_general(a, b, (((1,), (1,)), ((), ())), preferred_element_type=F32)


def _dot_tn(a, b):
    return lax.dot_general(a, b, (((0,), (0,)), ((), ())), preferred_element_type=F32)


def _params(*sem):
    return pltpu.CompilerParams(dimension_semantics=sem, vmem_limit_bytes=VMEM_LIMIT)


def _inproj_kernel(x_ref, g_ref, w_ref, wgt_ref, qg_ref, wuq_ref, kvg_ref, wk_ref, wvt_ref,
                   cm_ref, sm_ref, cs_ref, ss_ref,
                   q_ref, k_ref, vt_ref, mqk_ref, mv_ref, mo_ref, gr_ref, sq_ref, sk_ref, sv_ref):
    tm = x_ref.shape[0]
    h = _rms(x_ref[...], g_ref[...]).astype(BF16)

    def proj(a, b):
        return _dot(h, w_ref[:, a:b])

    lane = lax.broadcasted_iota(jnp.int32, (tm, LANES), 1)
    cm, sm = cm_ref[...], sm_ref[...]
    mla_first = lane < MLA_NOPE + MLA_ROPE // 2

    def rope_mla(t):
        rot = jnp.where(mla_first, pltpu.roll(t, LANES - MLA_ROPE // 2, 1), pltpu.roll(t, MLA_ROPE // 2, 1))
        return t * cm + rot * sm

    cs, ss = cs_ref[...], ss_ref[...]
    sw_first = (lane % SW_HD) < SW_ROT // 2

    def rope_sw(t):
        rot = jnp.where(sw_first, pltpu.roll(t, LANES - SW_ROT // 2, 1), pltpu.roll(t, SW_ROT // 2, 1))
        return t * cs + rot * ss

    cqn = _rms(proj(P_CQ, P_CQ + MLA_Q_LORA), qg_ref[...]).astype(BF16)
    q_scale = (MLA_NOPE + MLA_ROPE) ** -0.5 * LOG2_E
    for pair in range(MLA_HEADS // 2):
        two = _dot(cqn, wuq_ref[:, 2 * pair * LANES:2 * (pair + 1) * LANES])
        for sub in range(2):
            q_ref[2 * pair + sub] = (rope_mla(two[:, sub * LANES:(sub + 1) * LANES]) * q_scale).astype(BF16)
    ckv_kr = proj(P_CKV, P_KROPE + LANES)
    kvn = _rms(ckv_kr[:, :MLA_KV_LORA], kvg_ref[...]).astype(BF16)
    kpe = rope_mla(ckv_kr[:, MLA_KV_LORA:])
    one_col = (lax.broadcasted_iota(jnp.int32, (MLA_VT_ROWS, 1), 0) == MLA_V).astype(F32)
    for pair in range(MLA_HEADS // 2):
        two = _dot(kvn, wk_ref[:, 2 * pair * LANES:2 * (pair + 1) * LANES])
        for sub in range(2):
            k_ref[2 * pair + sub] = (two[:, sub * LANES:(sub + 1) * LANES] + kpe).astype(BF16)
    for hh in range(MLA_HEADS):
        rows = slice(hh * MLA_VT_ROWS, (hh + 1) * MLA_VT_ROWS)
        for u in range(tm // FLASH_KEY_TILE):
            toks = slice(u * FLASH_KEY_TILE, (u + 1) * FLASH_KEY_TILE)
            vt_ref[hh, u] = (_dot_nt(wvt_ref[rows, :], kvn[toks, :]) + one_col).astype(BF16)

    mqk_ref[...] = proj(P_MLQK, P_MLQK + 2 * ML_HEADS * ML_DK)
    mo_ref[...] = proj(P_MLO, P_MLO + ML_W)
    tr = _dot_nt(wgt_ref[...], h)
    n_v = ML_HEADS * ML_VT_ROWS
    v_row = lax.broadcasted_iota(jnp.int32, (n_v, 1), 0)
    mv_ref[...] = (tr[0:n_v, :] + ((v_row % ML_VT_ROWS) == ML_DV).astype(F32)).astype(BF16)
    gr_ref[...] = tr[n_v:n_v + 2 * ML_HEADS, :]

    sw_q, sw_k, sw_v = (proj(p0, p0 + SW_W) for p0 in (P_SWQ, P_SWK, P_SWV))
    for half in range(SW_W // LANES):
        sl = slice(half * LANES, (half + 1) * LANES)
        sq_ref[half] = rope_sw(sw_q[:, sl]) * (SW_HD ** -0.5 * LOG2_E)
        sk_ref[half] = rope_sw(sw_k[:, sl])
        sv_ref[half] = sw_v[:, sl]


def _inproj(x, g, w, wgt, qg, wuq, kvg, wk, wv, cm, sm, cs, ss, seq):
    t_all = x.shape[0]
    tm = WIDE_TILE
    pos_tiles = seq // tm
    tok = lambda w_: pl.BlockSpec((tm, w_), lambda i: (i, 0))
    full = lambda a: pl.BlockSpec(a.shape, lambda i: (0,) * a.ndim)
    pos = pl.BlockSpec((tm, LANES), lambda i: (i % pos_tiles, 0))
    slabs = pl.BlockSpec((SW_W // LANES, tm, LANES), lambda i: (0, i, 0))
    out_shapes = (
        jax.ShapeDtypeStruct((MLA_HEADS, t_all, LANES), BF16),
        jax.ShapeDtypeStruct((MLA_HEADS, t_all, LANES), BF16),
        jax.ShapeDtypeStruct((MLA_HEADS, t_all // FLASH_KEY_TILE, MLA_VT_ROWS, FLASH_KEY_TILE), BF16),
        jax.ShapeDtypeStruct((t_all, 2 * ML_HEADS * ML_DK), F32),
        jax.ShapeDtypeStruct((t_all // seq, ML_HEADS * ML_VT_ROWS, seq), BF16),
        jax.ShapeDtypeStruct((t_all, ML_W), F32),
        jax.ShapeDtypeStruct((t_all // seq, 2 * ML_HEADS, seq), F32),
        jax.ShapeDtypeStruct((SW_W // LANES, t_all, LANES), F32),
        jax.ShapeDtypeStruct((SW_W // LANES, t_all, LANES), F32),
        jax.ShapeDtypeStruct((SW_W // LANES, t_all, LANES), F32),
    )
    heads = pl.BlockSpec((MLA_HEADS, tm, LANES), lambda i: (0, i, 0))
    out_specs = (heads, heads,
                 pl.BlockSpec((MLA_HEADS, tm // FLASH_KEY_TILE, MLA_VT_ROWS, FLASH_KEY_TILE), lambda i: (0, i, 0, 0)),
                 tok(2 * ML_HEADS * ML_DK),
                 pl.BlockSpec((None, ML_HEADS * ML_VT_ROWS, tm), lambda i: (i // pos_tiles, 0, i % pos_tiles)),
                 tok(ML_W), pl.BlockSpec((None, 2 * ML_HEADS, tm), lambda i: (i // pos_tiles, 0, i % pos_tiles)),
                 slabs, slabs, slabs)
    return pl.pallas_call(
        _inproj_kernel,
        grid=(t_all // tm,),
        in_specs=[tok(D_MODEL), full(g), full(w), full(wgt), full(qg), full(wuq), full(kvg), full(wk), full(wv),
                  pos, pos, pos, pos],
        out_specs=out_specs,
        out_shape=out_shapes,
        compiler_params=_params("parallel"),
        name="inproj",
    )(x, g, w, wgt, qg, wuq, kvg, wk, wv, cm, sm, cs, ss)


def _flash_kernel(q_ref, k_ref, vt_ref, o_ref, m_scr, acc_scr, s_scr, cmax_scr):
    tq = q_ref.shape[0]
    tk = FLASH_KEY_TILE
    per_q = tq // tk
    i = pl.program_id(2)
    n_tiles = (i + 1) * per_q
    qt = q_ref[...].astype(F32).T.astype(BF16)
    m_scr[...] = jnp.full(m_scr.shape, NEG, F32)
    acc_scr[...] = jnp.zeros(acc_scr.shape, F32)

    key_l = lax.broadcasted_iota(jnp.int32, (tk, tk), 0)
    qry_l = lax.broadcasted_iota(jnp.int32, (tk, tk), 1)

    def scores(j, buf, d):
        start = pl.multiple_of(j * tk, tk)
        c0 = 0 if d is None else d * tk
        st = _dot(k_ref[pl.ds(start, tk), :], qt[:, c0:])
        if d is not None:
            tri = jnp.where(key_l <= qry_l, st[:, :tk], NEG)
            st = tri if st.shape[1] == tk else jnp.concatenate([tri, st[:, tk:]], axis=1)
        s_scr[buf, :, c0:] = st
        cmax_scr[buf, :, c0:] = jnp.max(st, axis=0, keepdims=True)

    def accumulate(j, buf, d):
        cols = slice(0 if d is None else d * tk, None)
        m_old = m_scr[:, cols]
        m_new = jnp.maximum(m_old, cmax_scr[buf, :, cols])
        p = jnp.exp2(s_scr[buf, :, cols] - m_new).astype(BF16)
        acc_scr[:, cols] = acc_scr[:, cols] * jnp.exp2(m_old - m_new) + _dot(vt_ref[j], p)
        m_scr[:, cols] = m_new

    def pairs(pp, offsets, held0):
        held = [held0, None]
        for c, (d_a, d_b) in enumerate(offsets):
            scores(2 * (pp + c) + 1, 1, d_a)
            held[1] = d_a
            accumulate(2 * (pp + c), 0, held[0])
            scores(2 * (pp + c) + 2, 0, d_b)
            held[0] = d_b
            accumulate(2 * (pp + c) + 1, 1, held[1])
        return held[0]

    def finish(held0):
        scores(n_tiles - 1, 1, per_q - 1)
        accumulate(n_tiles - 2, 0, held0)
        accumulate(n_tiles - 1, 1, per_q - 1)
        acc = acc_scr[...]
        out_t = acc[0:MLA_V, :] / acc[MLA_V:MLA_V + 1, :]
        o_ref[...] = jnp.concatenate([out_t, jnp.zeros_like(out_t)], axis=0).T.astype(BF16)

    assert per_q == 4

    @pl.when(i == 0)
    def _():
        scores(0, 0, 0)
        finish(pairs(0, [(1, 2)], 0))

    tail = [(None, None), (None, 0), (1, 2)]
    plain = [(None, None)] * 2
    n_eights = lax.shift_right_logical(jnp.maximum(i - 1, 0), 1)

    def body(g, carry):
        pairs(4 * g, plain + plain, None)
        return carry

    @pl.when((i > 0) & ((i & 1) == 1))
    def _():
        scores(0, 0, None)
        lax.fori_loop(0, n_eights, body, 0)
        finish(pairs(4 * n_eights, tail, None))

    @pl.when((i > 0) & ((i & 1) == 0))
    def _():
        scores(0, 0, None)
        lax.fori_loop(0, n_eights, body, 0)
        finish(pairs(4 * n_eights, plain + tail, None))


def _flash(q, k, vt, b, s):
    tq, tk = FLASH_TILE, FLASH_KEY_TILE
    assert s % tq == 0 and tq % (2 * tk) == 0
    nq = s // tq
    rows = pl.BlockSpec((None, tq, LANES), lambda bb, hh, i: (hh, bb * nq + i, 0))
    return pl.pallas_call(
        _flash_kernel,
        grid=(b, MLA_HEADS, nq),
        in_specs=[rows,
                  pl.BlockSpec((None, s, LANES), lambda bb, hh, i: (hh, bb, 0)),
                  pl.BlockSpec((None, s // tk, MLA_VT_ROWS, tk), lambda bb, hh, i: (hh, bb, 0, 0))],
        out_specs=rows,
        out_shape=jax.ShapeDtypeStruct((MLA_HEADS, b * s, LANES), BF16),
        scratch_shapes=[pltpu.VMEM((1, tq), F32), pltpu.VMEM((MLA_VT_ROWS, tq), F32),
                        pltpu.VMEM((2, tk, tq), F32), pltpu.VMEM((2, 1, tq), F32)],
        compiler_params=_params("parallel", "parallel", "arbitrary"),
        name="mla_flash",
    )(q, k, vt)


def _log_sigmoid(z):
    return jnp.minimum(z, 0.0) - jnp.log1p(jnp.exp(-jnp.abs(z)))


def _mlstm_kernel(qkc_ref, qkp_ref, vt_ref, o_ref, g_ref, cw_ref, cb_ref, bi_ref, bf_ref,
                  y_ref, c_scr, m_scr):
    L = ML_CHUNK
    nb = qkc_ref.shape[0]
    c = pl.program_id(0)

    @pl.when(c == 0)
    def _():
        c_scr[...] = jnp.zeros(c_scr.shape, F32)
        m_scr[...] = jnp.full(m_scr.shape, NEG, F32)

    src = lax.broadcasted_iota(jnp.int32, (L, L), 0)
    tgt = lax.broadcasted_iota(jnp.int32, (L, L), 1)
    causal = src <= tgt
    pick = (lax.broadcasted_iota(jnp.int32, (L, ML_HEADS * L), 0)
            == lax.broadcasted_iota(jnp.int32, (L, ML_HEADS * L), 1) // L).astype(BF16)

    def chunk(sub):
        rows_t = slice(sub * L, (sub + 1) * L)
        streams = [(b, hh) for b in range(nb) for hh in range(ML_HEADS)]
        qk, b_all, li_all, x_col = [], [], [], []
        for b in range(nb):
            cur = qkc_ref[b, rows_t, :]
            prev = jnp.where(c > 0, qkp_ref[b], 0.0) if sub == 0 else qkc_ref[b, (sub - 1) * L:sub * L, :]
            row_w = lax.broadcasted_iota(jnp.int32, cur.shape, 0)
            conv = cb_ref[...] + cw_ref[ML_CONV - 1:ML_CONV, :] * cur
            for sh in range(1, ML_CONV):
                shifted = jnp.where(row_w >= sh, pltpu.roll(cur, sh, 0), pltpu.roll(prev, sh, 0))
                conv = conv + cw_ref[ML_CONV - 1 - sh:ML_CONV - sh, :] * shifted
            qk.append(conv * jax.nn.sigmoid(conv))
            gates = g_ref[b, :, rows_t]
            li = gates[0:ML_HEADS, :] + bi_ref[...]
            lf = _log_sigmoid(gates[ML_HEADS:, :] + bf_ref[...])
            lane_g = lax.broadcasted_iota(jnp.int32, lf.shape, 1)
            step = 1
            while step < L:
                lf = lf + jnp.where(lane_g >= step, pltpu.roll(lf, step, 1), 0.0)
                step *= 2
            li_all.append(li)
            b_all.append(lf)
            x = jnp.concatenate([li - lf, jnp.zeros((L - ML_HEADS, L), F32)], axis=0).T
            spread = jnp.zeros((L, ML_HEADS * L), F32)
            for _ in range(3):
                part = x.astype(BF16)
                spread = spread + _dot(part, pick)
                x = x - part.astype(F32)
            x_col.append(spread)

        st = []
        for n, (b, hh) in enumerate(streams):
            b_row = b_all[b][hh:hh + 1, :]
            li_row = li_all[b][hh:hh + 1, :]
            m_prev = m_scr[n:n + 1, 0:1]
            d_t = jnp.where(causal, x_col[b][:, hh * L:(hh + 1) * L] + b_row, NEG)
            inter = b_row + m_prev
            m_t = jnp.maximum(inter, jnp.max(d_t, axis=0, keepdims=True))
            b_last = b_row[:, L - 1:L]
            g = b_last - b_row + li_row
            m_new = jnp.maximum(b_last + m_prev, jnp.max(g, axis=1, keepdims=True))
            kf = qk[b][:, (ML_HEADS + hh) * ML_DK:(ML_HEADS + hh + 1) * ML_DK] * ML_DK ** -0.5
            vt = vt_ref[b, hh * ML_VT_ROWS:(hh + 1) * ML_VT_ROWS, rows_t]
            st.append(dict(
                qh=qk[b][:, hh * ML_DK:(hh + 1) * ML_DK].astype(BF16),
                kh=kf.astype(BF16), vt=vt,
                vw=(vt.astype(F32) * jnp.exp(g - m_new)).astype(BF16),
                c_t=c_scr[n],
                w_intra=jnp.exp(d_t - m_t), w_inter=jnp.exp(inter - m_t), floor=jnp.exp(-m_t),
                decay=jnp.exp(b_last + m_prev - m_new), m_new=m_new))

        for s_ in st:
            s_["sraw"] = _dot_nt(s_["kh"], s_["qh"])
            s_["cross"] = _dot_nt(s_["c_t"].astype(BF16), s_["qh"])
        for s_ in st:
            s_["s"] = s_["sraw"] * s_["w_intra"]
        for s_ in st:
            s_["intra"] = _dot(s_["vt"], s_["s"].astype(BF16))
            s_["kv"] = _dot(s_["vw"], s_["kh"])
        h_t = []
        for n, s_ in enumerate(st):
            num = s_["intra"][:ML_DV, :] + s_["w_inter"] * s_["cross"][:ML_DV, :]
            nq = jnp.sum(s_["s"], axis=0, keepdims=True) + s_["w_inter"] * s_["cross"][ML_DV:ML_DV + 1, :]
            h_t.append(num / jnp.maximum(jnp.abs(nq), s_["floor"]))
            c_scr[n] = s_["decay"] * s_["c_t"] + s_["kv"]
            m_scr[n:n + 1, :] = jnp.broadcast_to(s_["m_new"], (1, LANES))
        for b in range(nb):
            hout = jnp.concatenate(h_t[b * ML_HEADS:(b + 1) * ML_HEADS], axis=0).T
            y_ref[b, rows_t, :] = (jax.nn.sigmoid(o_ref[b, rows_t, :]) * hout).astype(BF16)

    for sub in range(qkc_ref.shape[1] // L):
        chunk(sub)


def _mlstm(qk_pre, vt, o_pre, gates_t, conv_w, conv_b, b_i, b_f):
    batch, seq, wqk = qk_pre.shape
    L = ML_CHUNK
    step = ML_CHUNKS_PER_STEP * L
    full = lambda a: pl.BlockSpec(a.shape, lambda c: (0,) * a.ndim)
    chunk = lambda w_: pl.BlockSpec((batch, step, w_), lambda c: (0, c, 0))
    lanes = lambda r: pl.BlockSpec((batch, r, step), lambda c: (0, 0, c))
    return pl.pallas_call(
        _mlstm_kernel,
        grid=(seq // step,),
        in_specs=[chunk(wqk),
                  pl.BlockSpec((batch, L, wqk), lambda c: (0, jnp.maximum(ML_CHUNKS_PER_STEP * c - 1, 0), 0)),
                  lanes(ML_HEADS * ML_VT_ROWS), chunk(ML_W), lanes(2 * ML_HEADS),
                  full(conv_w), full(conv_b), full(b_i), full(b_f)],
        out_specs=chunk(ML_W),
        out_shape=jax.ShapeDtypeStruct((batch, seq, ML_W), BF16),
        scratch_shapes=[pltpu.VMEM((batch * ML_HEADS, ML_VT_ROWS, ML_DK), F32),
                        pltpu.VMEM((batch * ML_HEADS, LANES), F32)],
        compiler_params=_params("arbitrary"),
        name="mlstm",
    )(qk_pre, qk_pre, vt, o_pre, gates_t, conv_w, conv_b, b_i, b_f)


def _swa_kernel(q_ref, k_ref, v_ref, y_ref, kk_scr, vv_scr, o_scr, l_scr):
    blk, tile = SW_BLOCK, SWA_TILE
    n_slab = SW_W // LANES
    n = pl.program_id(1)

    @pl.when(n == 0)
    def _():
        kk_scr[:, 0:tile, :] = jnp.zeros((n_slab, tile, LANES), F32)
        vv_scr[:, 0:tile, :] = jnp.zeros((n_slab, tile, LANES), F32)

    @pl.when(n > 0)
    def _():
        kk_scr[:, 0:tile, :] = kk_scr[:, tile:2 * tile, :]
        vv_scr[:, 0:tile, :] = vv_scr[:, tile:2 * tile, :]

    kk_scr[:, tile:2 * tile, :] = k_ref[...]
    vv_scr[:, tile:2 * tile, :] = v_ref[...]

    key = lax.broadcasted_iota(jnp.int32, (2 * blk, blk), 0)
    qry = lax.broadcasted_iota(jnp.int32, (2 * blk, blk), 1)
    dim =lax.broadcasted_iota(jnp.int32, (LANES, blk), 0)
    head_rows = [dim < SW_HD, dim >= SW_HD]

    for g, dil in enumerate(SW_DILATIONS):
        per_res = tile // (blk * dil)
        shift = per_res.bit_length() - 1

        def block_pair(it, carry, g=g, dil=dil, per_res=per_res, shift=shift):
            def rows(start):
                return pl.ds(start, blk, stride=dil) if dil > 1 else pl.ds(pl.multiple_of(start, blk), blk)

            work = []
            for u in range(SWA_UNROLL):
                bi = SWA_UNROLL * it + u
                r = lax.shift_right_logical(bi, shift)
                j = bi & (per_res - 1)
                q0 = r + j * (blk * dil)
                lowest = jnp.where((n == 0) & (j == 0), blk, 0)
                ok = (key >= jnp.maximum(qry, lowest)) & (key <= qry + blk)
                for sl in range(n_slab):
                    prev, cur = rows(tile + q0 - blk * dil), rows(tile + q0)
                    kb = jnp.concatenate([kk_scr[sl, prev, :], kk_scr[sl, cur, :]], axis=0)
                    vb = jnp.concatenate([vv_scr[sl, prev, :], vv_scr[sl, cur, :]], axis=0)
                    work.append(dict(q0=q0, sl=sl, ok=ok, qt=q_ref[sl, rows(q0), :].T,
                                     kb=kb.astype(BF16), vt=vb.T.astype(BF16)))
            for w in work:
                w["s"] = [_dot(w["kb"], jnp.where(head_rows[hh], w["qt"], 0.0).astype(BF16)) for hh in range(2)]
            for w in work:
                w["e"], w["den"], w["lse"] = [], [], []
                for hh in range(2):
                    s = jnp.where(w["ok"], w["s"][hh], NEG)
                    cmax = jnp.max(s, axis=0, keepdims=True)
                    e = jnp.exp2(s - cmax)
                    den = jnp.sum(e, axis=0, keepdims=True)
                    w["e"].append(e.astype(BF16))
                    w["den"].append(den)
                    w["lse"].append(cmax + jnp.log2(den))
            for w in work:
                ot = [_dot(w["vt"][hh * SW_HD:(hh + 1) * SW_HD, :], w["e"][hh]) / w["den"][hh] for hh in range(2)]
                lt = [jnp.broadcast_to(w["lse"][hh], (SW_HD, blk)) for hh in range(2)]
                o_scr[g, w["sl"], rows(w["q0"]), :] = jnp.concatenate(ot, axis=0).T
                l_scr[g, w["sl"], rows(w["q0"]), :] = jnp.concatenate(lt, axis=0).T
            return carry

        lax.fori_loop(0, tile // blk // SWA_UNROLL, block_pair, 0)

    for sl in range(n_slab):
        ls = [l_scr[g, sl] for g in range(len(SW_DILATIONS))]
        mx = functools.reduce(jnp.maximum, ls)
        ws = [jnp.exp2(l - mx) for l in ls]
        num = functools.reduce(lambda a, b: a + b, [w * o_scr[g, sl] for g, w in enumerate(ws)])
        y_ref[:, sl * LANES:(sl + 1) * LANES] = (num / functools.reduce(lambda a, b: a + b, ws)).astype(BF16)


def _swa(q, k, v, batch, seq):
    tile = SWA_TILE
    n_slab = SW_W // LANES
    per_batch = seq // tile
    spec = pl.BlockSpec((n_slab, tile, LANES), lambda bb, n: (0, bb * per_batch + n, 0))
    n_br = len(SW_DILATIONS)
    return pl.pallas_call(
        _swa_kernel,
        grid=(batch, per_batch),
        in_specs=[spec, spec, spec],
        out_specs=pl.BlockSpec((tile, SW_W), lambda bb, n: (bb * per_batch + n, 0)),
        out_shape=jax.ShapeDtypeStruct((batch * seq, SW_W), BF16),
        scratch_shapes=[pltpu.VMEM((n_slab, 2 * tile, LANES), F32), pltpu.VMEM((n_slab, 2 * tile, LANES), F32),
                        pltpu.VMEM((n_br, n_slab, tile, LANES), F32), pltpu.VMEM((n_br, n_slab, tile, LANES), F32)],
        compiler_params=_params("parallel", "arbitrary"),
        name="swa",
    )(q, k, v)


def _outproj_kernel(x_ref, ya_ref, yb_ref, yc_ref, wa_ref, wb_ref, wc_ref, g_ref, out_ref):
    ya = jnp.concatenate([ya_ref[hh] for hh in range(MLA_HEADS)], axis=1)
    y = _dot(ya, wa_ref[...]) + _dot(yb_ref[...], wb_ref[...]) + _dot(yc_ref[...], wc_ref[...])
    out_ref[...] = x_ref[...] + _rms(y, g_ref[...])


def _outproj(x, ya, yb, yc, wa, wb, wc, g):
    t_all = x.shape[0]
    tm = WIDE_TILE
    tok = lambda w_: pl.BlockSpec((tm, w_), lambda i: (i, 0))
    full = lambda a: pl.BlockSpec(a.shape, lambda i: (0,) * a.ndim)
    return pl.pallas_call(
        _outproj_kernel,
        grid=(t_all // tm,),
        in_specs=[tok(D_MODEL), pl.BlockSpec((MLA_HEADS, tm, LANES), lambda i: (0, i, 0)), tok(ML_W), tok(SW_W),
                  full(wa), full(wb), full(wc), full(g)],
        out_specs=tok(D_MODEL),
        out_shape=jax.ShapeDtypeStruct((t_all, D_MODEL), F32),
        compiler_params=_params("parallel"),
        name="outproj",
    )(x, ya, yb, yc, wa, wb, wc, g)


def _memkv_kernel(mem_ref, g_ref, w_ref, kv_ref):
    kv_ref[...] = _dot(_rms(mem_ref[...], g_ref[...]).astype(BF16), w_ref[...]).astype(BF16)


def _memkv(mem, g, w):
    n = mem.shape[0]
    depth = w.shape[0]
    return pl.pallas_call(
        _memkv_kernel,
        grid=(depth,),
        in_specs=[pl.BlockSpec(mem.shape, lambda l: (0, 0)),
                  pl.BlockSpec((None, 1, D_MODEL), lambda l: (l, 0, 0)),
                  pl.BlockSpec((None, D_MODEL, 2 * D_MODEL), lambda l: (l, 0, 0))],
        out_specs=pl.BlockSpec((None, n, 2 * D_MODEL), lambda l: (l, 0, 0)),
        out_shape=jax.ShapeDtypeStruct((depth, n, 2 * D_MODEL), BF16),
        compiler_params=_params("parallel"),
        name="memkv",
    )(mem, g, w)


def _xattn_kernel(x_ref, gpre_ref, wq_ref, kv_ref, wo_ref, gpost_ref, out_ref, o_scr):
    x = x_ref[...]
    h = _rms(x, gpre_ref[...]).astype(BF16)
    q = (_dot(h, wq_ref[...]) * X_HD ** -0.5).astype(BF16)
    for hh in range(X_HEADS):
        hs = slice(hh * X_HD, (hh + 1) * X_HD)
        s = _dot_nt(q[:, hs], kv_ref[:, hs])
        e = jnp.exp(s - jnp.max(s, axis=1, keepdims=True))
        den = jnp.sum(e, axis=1, keepdims=True)
        o_scr[:, hs] = (_dot(e.astype(BF16), kv_ref[:, D_MODEL + hh * X_HD:D_MODEL + (hh + 1) * X_HD]) / den).astype(BF16)
    y = _dot(o_scr[...], wo_ref[...])
    out_ref[...] = x + _rms(y, gpost_ref[...])


def _xattn(x, gpre, wq, kv, wo, gpost, seq):
    t_all = x.shape[0]
    tm = WIDE_TILE
    per_batch = seq // tm
    tok = pl.BlockSpec((tm, D_MODEL), lambda i: (i, 0))
    full = lambda a: pl.BlockSpec(a.shape, lambda i: (0,) * a.ndim)
    return pl.pallas_call(
        _xattn_kernel,
        grid=(t_all // tm,),
        in_specs=[tok, full(gpre), full(wq), pl.BlockSpec((N_MEM, 2 * D_MODEL), lambda i: (i // per_batch, 0)),
                  full(wo), full(gpost)],
        out_specs=tok,
        out_shape=jax.ShapeDtypeStruct((t_all, D_MODEL), F32),
        scratch_shapes=[pltpu.VMEM((tm, D_MODEL), BF16)],
        compiler_params=_params("parallel"),
        name="xattn",
    )(x, gpre, wq, kv, wo, gpost)


def _ffn_kernel(x_ref, gpre_ref, wgu_ref, wd_ref, gpost_ref, out_ref):
    x = x_ref[...]
    h = _rms(x, gpre_ref[...]).astype(BF16)
    y = None
    for lo, hi in zip(FFN_SPLITS[:-1], FFN_SPLITS[1:]):
        gate = _dot(h, wgu_ref[:, lo:hi])
        up = _dot(h, wgu_ref[:, D_FF + lo:D_FF + hi])
        part = _dot((gate * jax.nn.sigmoid(gate) * up).astype(BF16), wd_ref[lo:hi, :])
        y = part if y is None else y + part
    out_ref[...] = x + _rms(y, gpost_ref[...])


def _ffn(x, gpre, w_gate_up, w_down, gpost):
    t_all = x.shape[0]
    tm = TOK_TILE
    tok = pl.BlockSpec((tm, D_MODEL), lambda i: (i, 0))
    full = lambda a: pl.BlockSpec(a.shape, lambda i: (0,) * a.ndim)
    return pl.pallas_call(
        _ffn_kernel,
        grid=(t_all // tm,),
        in_specs=[tok, full(gpre), full(w_gate_up), full(w_down), full(gpost)],
        out_specs=tok,
        out_shape=jax.ShapeDtypeStruct((t_all, D_MODEL), F32),
        compiler_params=_params("parallel"),
        name="ffn",
    )(x, gpre, w_gate_up, w_down, gpost)


def _rope_tables(seq):
    pos = jnp.arange(seq, dtype=F32)[:, None]

    def cos_sin(d_rot):
        inv = ROPE_THETA ** (-jnp.arange(0, d_rot, 2, dtype=F32) / d_rot)
        ang = pos * inv[None, :]
        return jnp.cos(ang), jnp.sin(ang)

    c, s = cos_sin(MLA_ROPE)
    ones = jnp.ones((seq, MLA_NOPE), F32)
    tail = jnp.ones((seq, LANES - MLA_NOPE - MLA_ROPE), F32)
    cm = jnp.concatenate([ones, c, c, tail], axis=1)
    sm = jnp.concatenate([0 * ones, -s, s, 0 * tail], axis=1)
    c, s = cos_sin(SW_ROT)
    rest = jnp.ones((seq, SW_HD - SW_ROT), F32)
    cs = jnp.tile(jnp.concatenate([c, c, rest], axis=1), (1, LANES // SW_HD))
    ss = jnp.tile(jnp.concatenate([-s, s, 0 * rest], axis=1), (1, LANES // SW_HD))
    return cm, sm, cs, ss


def _prep_layer(w_in, w_uq, w_ukv, w_out):
    o = IN_OFFS
    col = lambda i: w_in[:, o[i]:o[i + 1]]
    z = lambda n: jnp.zeros((D_MODEL, n), F32)
    w_p = jnp.concatenate([
        col(0), col(1),
        z(MLA_NOPE), col(2), z(LANES - MLA_NOPE - MLA_ROPE),
        col(3), col(4), col(6), col(9), col(10), col(11)], axis=1)
    assert w_p.shape[1] == P_TOTAL
    mlv = jnp.pad(col(5).reshape(D_MODEL, ML_HEADS, ML_DV), ((0, 0), (0, 0), (0, ML_VT_ROWS - ML_DV)))
    w_gates_t = jnp.concatenate([mlv.reshape(D_MODEL, ML_HEADS * ML_VT_ROWS), col(7), col(8),
                                 z(ML_T_ROWS - ML_HEADS * ML_VT_ROWS - 2 * ML_HEADS)], axis=1).T
    uq = w_uq.reshape(MLA_Q_LORA, MLA_HEADS, MLA_NOPE + MLA_ROPE)
    uq = jnp.pad(uq, ((0, 0), (0, 0), (0, LANES - MLA_NOPE - MLA_ROPE))).reshape(MLA_Q_LORA, MLA_PAD_W)
    ukv = w_ukv.reshape(MLA_KV_LORA, MLA_HEADS, MLA_NOPE + MLA_V)
    pad_half = lambda a: jnp.pad(a, ((0, 0), (0, 0), (0, LANES - a.shape[2]))).reshape(MLA_KV_LORA, MLA_PAD_W)
    wk = pad_half(ukv[:, :, :MLA_NOPE])
    wv = jnp.pad(ukv[:, :, MLA_NOPE:], ((0, 0), (0, 0), (0, MLA_VT_ROWS - MLA_V)))
    wv = wv.reshape(MLA_KV_LORA, MLA_HEADS * MLA_VT_ROWS).T
    wa = w_out[:MLA_W].reshape(MLA_HEADS, MLA_V, D_MODEL)
    wa = jnp.pad(wa, ((0, 0), (0, LANES - MLA_V), (0, 0))).reshape(MLA_PAD_W, D_MODEL)
    wb, wc = w_out[MLA_W:MLA_W + ML_W], w_out[MLA_W + ML_W:]
    bf = lambda a: a.astype(BF16)
    return bf(w_p), bf(w_gates_t), bf(uq), bf(wk), bf(wv), bf(wa), bf(wb), bf(wc)


def kernel(x, mem, w_in, mla_q_norm, mla_w_uq, mla_kv_norm, mla_w_ukv, ml_conv_w, ml_conv_b, ml_b_i, ml_b_f,
           w_out, x_w_q, x_w_kv, x_w_o, w_gate_up, w_down, norm_mix_pre, norm_mix_post, norm_mem,
           norm_x_pre, norm_x_post, norm_ffn_pre, norm_ffn_post):
    batch, seq, _ = x.shape
    depth = w_in.shape[0]
    t_all = batch * seq
    assert seq % SWA_TILE == 0 and seq % TOK_TILE == 0 and seq % WIDE_TILE == 0
    assert WIDE_TILE % FLASH_KEY_TILE == 0
    assert SWA_TILE == SW_DILATIONS[-1] * SW_BLOCK
    cm, sm, cs, ss = _rope_tables(seq)
    row = lambda a: a.reshape(1, -1)
    colv = lambda a: a.reshape(-1, 1)
    xt = x.reshape(t_all, D_MODEL)
    mem2 = mem.reshape(batch * N_MEM, D_MODEL)
    w_p_all, w_gt_all, uq_all, wk_all, wv_all, wa_all, wb_all, wc_all = jax.vmap(_prep_layer)(
        w_in, mla_w_uq, mla_w_ukv, w_out)
    x_w_q, x_w_o, w_gate_up, w_down = (a.astype(BF16) for a in (x_w_q, x_w_o, w_gate_up, w_down))
    kv_all = _memkv(mem2, norm_mem.reshape(depth, 1, D_MODEL), x_w_kv.astype(BF16))
    for l in range(depth):
        w_p, w_gt, uq, wk, wv, wa, wb, wc = (a[l] for a in (w_p_all, w_gt_all, uq_all, wk_all, wv_all,
                                                            wa_all, wb_all, wc_all))
        q, k, v, mqk, mv, mo, gates_t, sq, sk, sv = _inproj(
            xt, row(norm_mix_pre[l]), w_p, w_gt, row(mla_q_norm[l]), uq, row(mla_kv_norm[l]), wk, wv,
            cm, sm, cs, ss, seq)
        b3 = lambda a: a.reshape(batch, seq, a.shape[-1])
        ya = _flash(q, k, v, batch, seq)
        yb = _mlstm(b3(mqk), mv, b3(mo), gates_t, ml_conv_w[l], row(ml_conv_b[l]), colv(ml_b_i[l]),
                    colv(ml_b_f[l])).reshape(t_all, ML_W)
        yc = _swa(sq, sk, sv, batch, seq)
        xt = _outproj(xt, ya, yb, yc, wa, wb, wc, row(norm_mix_post[l]))
        xt = _xattn(xt, row(norm_x_pre[l]), x_w_q[l], kv_all[l], x_w_o[l], row(norm_x_post[l]), seq)
        xt = _ffn(xt, row(norm_ffn_pre[l]), w_gate_up[l], w_down[l], row(norm_ffn_post[l]))
    return xt.reshape(batch, seq, D_MODEL)
```

```python
import functools

import jax
import jax.numpy as jnp
import numpy as np
from jax import lax
from jax.experimental import pallas as pl
from jax.experimental.pallas import tpu as pltpu

F32 = jnp.float32
BF16 = jnp.bfloat16

D_MODEL = 1024
N_MEM = 256
ROPE_THETA = 500000.0
EPS = 1e-6
NEG = -1e30
LOG2_E = 1.4426950408889634
LANES = 128

MLA_HEADS, MLA_NOPE, MLA_ROPE, MLA_V = 8, 64, 32, 64
MLA_Q_LORA, MLA_KV_LORA = 256, 128
ML_HEADS, ML_DK, ML_DV, ML_CONV, ML_CHUNK = 4, 64, 64, 4, 128
SW_HEADS, SW_HD = 4, 64
SW_ROT = SW_HD // 4
SW_DILATIONS = (1, 4, 16)
SW_BLOCK = 128
X_HEADS = 4
X_HD = D_MODEL // X_HEADS
D_FF = 2816

MLA_W = MLA_HEADS * MLA_V
ML_W = ML_HEADS * ML_DV
SW_W = SW_HEADS * SW_HD
MLA_PAD_W = MLA_HEADS * LANES

IN_SIZES = (MLA_Q_LORA, MLA_KV_LORA, MLA_ROPE, ML_HEADS * ML_DK, ML_HEADS * ML_DK, ML_W, ML_W,
            ML_HEADS, ML_HEADS, SW_W, SW_W, SW_W)
IN_OFFS = tuple(int(v) for v in np.cumsum((0,) + IN_SIZES))

P_CQ = 0
P_CKV = 256
P_KROPE = 384
P_MLQK = 512
P_MLO = 1024
P_SWQ = 1280
P_SWK = 1536
P_SWV = 1792
P_TOTAL = 2048
ML_VT_ROWS = 80
ML_T_ROWS = ML_HEADS * ML_VT_ROWS + 16

TOK_TILE = 512
WIDE_TILE = 1024
FLASH_TILE = 1024
FLASH_KEY_TILE = 256
MLA_VT_ROWS = 80
SWA_TILE = 2048
SWA_UNROLL = 8
FFN_SPLITS = (0, 1536, D_FF)
ML_CHUNKS_PER_STEP = 8
VMEM_LIMIT = 56 * 1024 * 1024


def _rms(x, g):
    return x * lax.rsqrt(jnp.mean(x * x, axis=-1, keepdims=True) + EPS) * g


def _dot(a, b):
    return jnp.dot(a, b, preferred_element_type=F32)


def _dot_nt(a, b):
    return lax.dot_general(a, b, (((1,), (1,)), ((), ())), preferred_element_type=F32)


def _params(*sem):
    return pltpu.CompilerParams(dimension_semantics=sem, vmem_limit_bytes=VMEM_LIMIT)


def _inproj_kernel(x_ref, g_ref, w_ref, wgt_ref, qg_ref, wuq_ref, kvg_ref, wk_ref, wvt_ref,
                   cm_ref, sm_ref, cs_ref, ss_ref,
                   q_ref, k_ref, vt_ref, mqk_ref, mv_ref, mo_ref, gr_ref, sq_ref, sk_ref, sv_ref):
    tm = x_ref.shape[0]
    h = _rms(x_ref[...], g_ref[...]).astype(BF16)

    def proj(a, b):
        return _dot(h, w_ref[:, a:b])

    lane = lax.broadcasted_iota(jnp.int32, (tm, LANES), 1)
    cm, sm = cm_ref[...], sm_ref[...]
    mla_first = lane < MLA_NOPE + MLA_ROPE // 2

    def rope_mla(t):
        rot = jnp.where(mla_first, pltpu.roll(t, LANES - MLA_ROPE // 2, 1), pltpu.roll(t, MLA_ROPE // 2, 1))
        return t * cm + rot * sm

    cs, ss = cs_ref[...], ss_ref[...]
    sw_first = (lane % SW_HD) < SW_ROT // 2

    def rope_sw(t):
        rot = jnp.where(sw_first, pltpu.roll(t, LANES - SW_ROT // 2, 1), pltpu.roll(t, SW_ROT // 2, 1))
        return t * cs + rot * ss

    cqn = _rms(proj(P_CQ, P_CQ + MLA_Q_LORA), qg_ref[...]).astype(BF16)
    q_scale = (MLA_NOPE + MLA_ROPE) ** -0.5 * LOG2_E
    for pair in range(MLA_HEADS // 2):
        two = _dot(cqn, wuq_ref[:, 2 * pair * LANES:2 * (pair + 1) * LANES])
        for sub in range(2):
            q_ref[2 * pair + sub] = (rope_mla(two[:, sub * LANES:(sub + 1) * LANES]) * q_scale).astype(BF16)
    ckv_kr = proj(P_CKV, P_KROPE + LANES)
    kvn = _rms(ckv_kr[:, :MLA_KV_LORA], kvg_ref[...]).astype(BF16)
    kpe = rope_mla(ckv_kr[:, MLA_KV_LORA:])
    one_col = (lax.broadcasted_iota(jnp.int32, (MLA_VT_ROWS, 1), 0) == MLA_V).astype(F32)
    for pair in range(MLA_HEADS // 2):
        two = _dot(kvn, wk_ref[:, 2 * pair * LANES:2 * (pair + 1) * LANES])
        for sub in range(2):
            k_ref[2 * pair + sub] = (two[:, sub * LANES:(sub + 1) * LANES] + kpe).astype(BF16)
    for hh in range(MLA_HEADS):
        rows = slice(hh * MLA_VT_ROWS, (hh + 1) * MLA_VT_ROWS)
        for u in range(tm // FLASH_KEY_TILE):
            toks = slice(u * FLASH_KEY_TILE, (u + 1) * FLASH_KEY_TILE)
            vt_ref[hh, u] = (_dot_nt(wvt_ref[rows, :], kvn[toks, :]) + one_col).astype(BF16)

    mqk_ref[...] = proj(P_MLQK, P_MLQK + 2 * ML_HEADS * ML_DK)
    mo_ref[...] = proj(P_MLO, P_MLO + ML_W)
    tr = _dot_nt(wgt_ref[...], h)
    n_v = ML_HEADS * ML_VT_ROWS
    v_row = lax.broadcasted_iota(jnp.int32, (n_v, 1), 0)
    mv_ref[...] = (tr[0:n_v, :] + ((v_row % ML_VT_ROWS) == ML_DV).astype(F32)).astype(BF16)
    gr_ref[...] = tr[n_v:n_v + 2 * ML_HEADS, :]

    sw_q, sw_k, sw_v = (proj(p0, p0 + SW_W) for p0 in (P_SWQ, P_SWK, P_SWV))
    for half in range(SW_W // LANES):
        sl = slice(half * LANES, (half + 1) * LANES)
        sq_ref[half] = rope_sw(sw_q[:, sl]) * (SW_HD ** -0.5 * LOG2_E)
        sk_ref[half] = rope_sw(sw_k[:, sl])
        sv_ref[half] = sw_v[:, sl]


def _inproj(x, g, w, wgt, qg, wuq, kvg, wk, wv, cm, sm, cs, ss, seq):
    t_all = x.shape[0]
    tm = WIDE_TILE
    pos_tiles = seq // tm
    tok = lambda w_: pl.BlockSpec((tm, w_), lambda i: (i, 0))
    full = lambda a: pl.BlockSpec(a.shape, lambda i: (0,) * a.ndim)
    pos = pl.BlockSpec((tm, LANES), lambda i: (i % pos_tiles, 0))
    slabs = pl.BlockSpec((SW_W // LANES, tm, LANES), lambda i: (0, i, 0))
    out_shapes = (
        jax.ShapeDtypeStruct((MLA_HEADS, t_all, LANES), BF16),
        jax.ShapeDtypeStruct((MLA_HEADS, t_all, LANES), BF16),
        jax.ShapeDtypeStruct((MLA_HEADS, t_all // FLASH_KEY_TILE, MLA_VT_ROWS, FLASH_KEY_TILE), BF16),
        jax.ShapeDtypeStruct((t_all, 2 * ML_HEADS * ML_DK), F32),
        jax.ShapeDtypeStruct((t_all // seq, ML_HEADS * ML_VT_ROWS, seq), BF16),
        jax.ShapeDtypeStruct((t_all, ML_W), F32),
        jax.ShapeDtypeStruct((t_all // seq, 2 * ML_HEADS, seq), F32),
        jax.ShapeDtypeStruct((SW_W // LANES, t_all, LANES), F32),
        jax.ShapeDtypeStruct((SW_W // LANES, t_all, LANES), F32),
        jax.ShapeDtypeStruct((SW_W // LANES, t_all, LANES), F32),
    )
    heads = pl.BlockSpec((MLA_HEADS, tm, LANES), lambda i: (0, i, 0))
    out_specs = (heads, heads,
                 pl.BlockSpec((MLA_HEADS, tm // FLASH_KEY_TILE, MLA_VT_ROWS, FLASH_KEY_TILE), lambda i: (0, i, 0, 0)),
                 tok(2 * ML_HEADS * ML_DK),
                 pl.BlockSpec((None, ML_HEADS * ML_VT_ROWS, tm), lambda i: (i // pos_tiles, 0, i % pos_tiles)),
                 tok(ML_W), pl.BlockSpec((None, 2 * ML_HEADS, tm), lambda i: (i // pos_tiles, 0, i % pos_tiles)),
                 slabs, slabs, slabs)
    return pl.pallas_call(
        _inproj_kernel,
        grid=(t_all // tm,),
        in_specs=[tok(D_MODEL), full(g), full(w), full(wgt), full(qg), full(wuq), full(kvg), full(wk), full(wv),
                  pos, pos, pos, pos],
        out_specs=out_specs,
        out_shape=out_shapes,
        compiler_params=_params("parallel"),
        name="inproj",
    )(x, g, w, wgt, qg, wuq, kvg, wk, wv, cm, sm, cs, ss)


def _flash_kernel(q_ref, k_ref, vt_ref, o_ref, m_scr, acc_scr, s_scr, cmax_scr):
    tq = q_ref.shape[0]
    tk = FLASH_KEY_TILE
    per_q = tq // tk
    i = pl.program_id(2)
    n_tiles = (i + 1) * per_q
    qt = q_ref[...].astype(F32).T.astype(BF16)
    m_scr[...] = jnp.full(m_scr.shape, NEG, F32)
    acc_scr[...] = jnp.zeros(acc_scr.shape, F32)

    key_l = lax.broadcasted_iota(jnp.int32, (tk, tk), 0)
    qry_l = lax.broadcasted_iota(jnp.int32, (tk, tk), 1)

    def scores(j, buf, d):
        start = pl.multiple_of(j * tk, tk)
        c0 = 0 if d is None else d * tk
        st = _dot(k_ref[pl.ds(start, tk), :], qt[:, c0:])
        if d is not None:
            tri = jnp.where(key_l <= qry_l, st[:, :tk], NEG)
            st = tri if st.shape[1] == tk else jnp.concatenate([tri, st[:, tk:]], axis=1)
        s_scr[buf, :, c0:] = st
        cmax_scr[buf, :, c0:] = jnp.max(st, axis=0, keepdims=True)

    def accumulate(j, buf, d):
        cols = slice(0 if d is None else d * tk, None)
        m_old = m_scr[:, cols]
        m_new = jnp.maximum(m_old, cmax_scr[buf, :, cols])
        p = jnp.exp2(s_scr[buf, :, cols] - m_new).astype(BF16)
        acc_scr[:, cols] = acc_scr[:, cols] * jnp.exp2(m_old - m_new) + _dot(vt_ref[j], p)
        m_scr[:, cols] = m_new

    def pairs(pp, offsets, held0):
        held = [held0, None]
        for c, (d_a, d_b) in enumerate(offsets):
            scores(2 * (pp + c) + 1, 1, d_a)
            held[1] = d_a
            accumulate(2 * (pp + c), 0, held[0])
            scores(2 * (pp + c) + 2, 0, d_b)
            held[0] = d_b
            accumulate(2 * (pp + c) + 1, 1, held[1])
        return held[0]

    def finish(held0):
        scores(n_tiles - 1, 1, per_q - 1)
        accumulate(n_tiles - 2, 0, held0)
        accumulate(n_tiles - 1, 1, per_q - 1)
        acc = acc_scr[...]
        out_t = acc[0:MLA_V, :] / acc[MLA_V:MLA_V + 1, :]
        o_ref[...] = jnp.concatenate([out_t, jnp.zeros_like(out_t)], axis=0).T.astype(BF16)

    assert per_q == 4

    @pl.when(i == 0)
    def _():
        scores(0, 0, 0)
        finish(pairs(0, [(1, 2)], 0))

    tail = [(None, None), (None, 0), (1, 2)]
    plain = [(None, None)] * 2
    n_eights = lax.shift_right_logical(jnp.maximum(i - 1, 0), 1)

    def body(g, carry):
        pairs(4 * g, plain + plain, None)
        return carry

    @pl.when((i > 0) & ((i & 1) == 1))
    def _():
        scores(0, 0, None)
        lax.fori_loop(0, n_eights, body, 0)
        finish(pairs(4 * n_eights, tail, None))

    @pl.when((i > 0) & ((i & 1) == 0))
    def _():
        scores(0, 0, None)
        lax.fori_loop(0, n_eights, body, 0)
        finish(pairs(4 * n_eights, plain + tail, None))


def _flash(q, k, vt, b, s):
    tq, tk = FLASH_TILE, FLASH_KEY_TILE
    assert s % tq == 0 and tq % (2 * tk) == 0
    nq = s // tq
    rows = pl.BlockSpec((None, tq, LANES), lambda bb, hh, i: (hh, bb * nq + i, 0))
    return pl.pallas_call(
        _flash_kernel,
        grid=(b, MLA_HEADS, nq),
        in_specs=[rows,
                  pl.BlockSpec((None, s, LANES), lambda bb, hh, i: (hh, bb, 0)),
                  pl.BlockSpec((None, s // tk, MLA_VT_ROWS, tk), lambda bb, hh, i: (hh, bb, 0, 0))],
        out_specs=rows,
        out_shape=jax.ShapeDtypeStruct((MLA_HEADS, b * s, LANES), BF16),
        scratch_shapes=[pltpu.VMEM((1, tq), F32), pltpu.VMEM((MLA_VT_ROWS, tq), F32),
                        pltpu.VMEM((2, tk, tq), F32), pltpu.VMEM((2, 1, tq), F32)],
        compiler_params=_params("parallel", "parallel", "arbitrary"),
        name="mla_flash",
    )(q, k, vt)


def _log_sigmoid(z):
    return jnp.minimum(z, 0.0) - jnp.log1p(jnp.exp(-jnp.abs(z)))


def _mlstm_kernel(qkc_ref, qkp_ref, vt_ref, o_ref, g_ref, cw_ref, cb_ref, bi_ref, bf_ref,
                  y_ref, c_scr, m_scr):
    L = ML_CHUNK
    nb = qkc_ref.shape[0]
    c = pl.program_id(0)

    @pl.when(c == 0)
    def _():
        c_scr[...] = jnp.zeros(c_scr.shape, F32)
        m_scr[...] = jnp.full(m_scr.shape, NEG, F32)

    src = lax.broadcasted_iota(jnp.int32, (L, L), 0)
    tgt = lax.broadcasted_iota(jnp.int32, (L, L), 1)
    causal = src <= tgt
    pick = (lax.broadcasted_iota(jnp.int32, (L, ML_HEADS * L), 0)
            == lax.broadcasted_iota(jnp.int32, (L, ML_HEADS * L), 1) // L).astype(BF16)

    def chunk(sub):
        rows_t = slice(sub * L, (sub + 1) * L)
        streams = [(b, hh) for b in range(nb) for hh in range(ML_HEADS)]
        qk, b_all, li_all, x_col = [], [], [], []
        for b in range(nb):
            cur = qkc_ref[b, rows_t, :]
            prev = jnp.where(c > 0, qkp_ref[b], 0.0) if sub == 0 else qkc_ref[b, (sub - 1) * L:sub * L, :]
            row_w = lax.broadcasted_iota(jnp.int32, cur.shape, 0)
            conv = cb_ref[...] + cw_ref[ML_CONV - 1:ML_CONV, :] * cur
            for sh in range(1, ML_CONV):
                shifted = jnp.where(row_w >= sh, pltpu.roll(cur, sh, 0), pltpu.roll(prev, sh, 0))
                conv = conv + cw_ref[ML_CONV - 1 - sh:ML_CONV - sh, :] * shifted
            qk.append(conv * jax.nn.sigmoid(conv))
            gates = g_ref[b, :, rows_t]
            li = gates[0:ML_HEADS, :] + bi_ref[...]
            lf = _log_sigmoid(gates[ML_HEADS:, :] + bf_ref[...])
            lane_g = lax.broadcasted_iota(jnp.int32, lf.shape, 1)
            step = 1
            while step < L:
                lf = lf + jnp.where(lane_g >= step, pltpu.roll(lf, step, 1), 0.0)
                step *= 2
            li_all.append(li)
            b_all.append(lf)
            x = jnp.concatenate([li - lf, jnp.zeros((L - ML_HEADS, L), F32)], axis=0).T
            spread = jnp.zeros((L, ML_HEADS * L), F32)
            for _ in range(3):
                part = x.astype(BF16)
                spread = spread + _dot(part, pick)
                x = x - part.astype(F32)
            x_col.append(spread)

        st = []
        for n, (b, hh) in enumerate(streams):
            b_row = b_all[b][hh:hh + 1, :]
            li_row = li_all[b][hh:hh + 1, :]
            m_prev = m_scr[n:n + 1, 0:1]
            d_t = jnp.where(causal, x_col[b][:, hh * L:(hh + 1) * L] + b_row, NEG)
            inter = b_row + m_prev
            m_t = jnp.maximum(inter, jnp.max(d_t, axis=0, keepdims=True))
            b_last = b_row[:, L - 1:L]
            g = b_last - b_row + li_row
            m_new = jnp.maximum(b_last + m_prev, jnp.max(g, axis=1, keepdims=True))
            kf = qk[b][:, (ML_HEADS + hh) * ML_DK:(ML_HEADS + hh + 1) * ML_DK] * ML_DK ** -0.5
            vt = vt_ref[b, hh * ML_VT_ROWS:(hh + 1) * ML_VT_ROWS, rows_t]
            st.append(dict(
                qh=qk[b][:, hh * ML_DK:(hh + 1) * ML_DK].astype(BF16),
                kh=kf.astype(BF16), vt=vt,
                vw=(vt.astype(F32) * jnp.exp(g - m_new)).astype(BF16),
                c_t=c_scr[n],
                w_intra=jnp.exp(d_t - m_t), w_inter=jnp.exp(inter - m_t), floor=jnp.exp(-m_t),
                decay=jnp.exp(b_last + m_prev - m_new), m_new=m_new))

        for s_ in st:
            s_["sraw"] = _dot_nt(s_["kh"], s_["qh"])
            s_["cross"] = _dot_nt(s_["c_t"].astype(BF16), s_["qh"])
        for s_ in st:
            s_["s"] = s_["sraw"] * s_["w_intra"]
        for s_ in st:
            s_["intra"] = _dot(s_["vt"], s_["s"].astype(BF16))
            s_["kv"] = _dot(s_["vw"], s_["kh"])
        h_t = []
        for n, s_ in enumerate(st):
            num = s_["intra"][:ML_DV, :] + s_["w_inter"] * s_["cross"][:ML_DV, :]
            nq = jnp.sum(s_["s"], axis=0, keepdims=True) + s_["w_inter"] * s_["cross"][ML_DV:ML_DV + 1, :]
            h_t.append(num / jnp.maximum(jnp.abs(nq), s_["floor"]))
            c_scr[n] = s_["decay"] * s_["c_t"] + s_["kv"]
            m_scr[n:n + 1, :] = jnp.broadcast_to(s_["m_new"], (1, LANES))
        for b in range(nb):
            hout = jnp.concatenate(h_t[b * ML_HEADS:(b + 1) * ML_HEADS], axis=0).T
            y_ref[b, rows_t, :] = (jax.nn.sigmoid(o_ref[b, rows_t, :]) * hout).astype(BF16)

    for sub in range(qkc_ref.shape[1] // L):
        chunk(sub)


def _mlstm(qk_pre, vt, o_pre, gates_t, conv_w, conv_b, b_i, b_f):
    batch, seq, wqk = qk_pre.shape
    L = ML_CHUNK
    step = ML_CHUNKS_PER_STEP * L
    full = lambda a: pl.BlockSpec(a.shape, lambda c: (0,) * a.ndim)
    chunk = lambda w_: pl.BlockSpec((batch, step, w_), lambda c: (0, c, 0))
    lanes = lambda r: pl.BlockSpec((batch, r, step), lambda c: (0, 0, c))
    return pl.pallas_call(
        _mlstm_kernel,
        grid=(seq // step,),
        in_specs=[chunk(wqk),
                  pl.BlockSpec((batch, L, wqk), lambda c: (0, jnp.maximum(ML_CHUNKS_PER_STEP * c - 1, 0), 0)),
                  lanes(ML_HEADS * ML_VT_ROWS), chunk(ML_W), lanes(2 * ML_HEADS),
                  full(conv_w), full(conv_b), full(b_i), full(b_f)],
        out_specs=chunk(ML_W),
        out_shape=jax.ShapeDtypeStruct((batch, seq, ML_W), BF16),
        scratch_shapes=[pltpu.VMEM((batch * ML_HEADS, ML_VT_ROWS, ML_DK), F32),
                        pltpu.VMEM((batch * ML_HEADS, LANES), F32)],
        compiler_params=_params("arbitrary"),
        name="mlstm",
    )(qk_pre, qk_pre, vt, o_pre, gates_t, conv_w, conv_b, b_i, b_f)


def _swa_kernel(q_ref, k_ref, v_ref, y_ref, kk_scr, vv_scr, o_scr, l_scr):
    blk, tile = SW_BLOCK, SWA_TILE
    n_slab = SW_W // LANES
    n = pl.program_id(1)

    @pl.when(n == 0)
    def _():
        kk_scr[:, 0:tile, :] = jnp.zeros((n_slab, tile, LANES), F32)
        vv_scr[:, 0:tile, :] = jnp.zeros((n_slab, tile, LANES), F32)

    @pl.when(n > 0)
    def _():
        kk_scr[:, 0:tile, :] = kk_scr[:, tile:2 * tile, :]
        vv_scr[:, 0:tile, :] = vv_scr[:, tile:2 * tile, :]

    kk_scr[:, tile:2 * tile, :] = k_ref[...]
    vv_scr[:, tile:2 * tile, :] = v_ref[...]

    key = lax.broadcasted_iota(jnp.int32, (2 * blk, blk), 0)
    qry = lax.broadcasted_iota(jnp.int32, (2 * blk, blk), 1)
    dim =lax.broadcasted_iota(jnp.int32, (LANES, blk), 0)
    head_rows = [dim < SW_HD, dim >= SW_HD]
    ones_rows = jnp.ones((16, 2 * blk), BF16)

    for g, dil in enumerate(SW_DILATIONS):
        per_res = tile // (blk * dil)
        shift = per_res.bit_length() - 1

        def block_pair(it, carry, g=g, dil=dil, per_res=per_res, shift=shift):
            def rows(start):
                return pl.ds(start, blk, stride=dil) if dil > 1 else pl.ds(pl.multiple_of(start, blk), blk)

            work = []
            for u in range(SWA_UNROLL):
                bi = SWA_UNROLL * it + u
                r = lax.shift_right_logical(bi, shift)
                j = bi & (per_res - 1)
                q0 = r + j * (blk * dil)
                lowest = jnp.where((n == 0) & (j == 0), blk, 0)
                ok = (key >= jnp.maximum(qry, lowest)) & (key <= qry + blk)
                for sl in range(n_slab):
                    prev, cur = rows(tile + q0 - blk * dil), rows(tile + q0)
                    kb = jnp.concatenate([kk_scr[sl, prev, :], kk_scr[sl, cur, :]], axis=0)
                    vb = jnp.concatenate([vv_scr[sl, prev, :], vv_scr[sl, cur, :]], axis=0)
                    work.append(dict(q0=q0, sl=sl, ok=ok, qt=q_ref[sl, rows(q0), :].T,
                                     kb=kb.astype(BF16), vt=vb.T.astype(BF16)))
            for w in work:
                w["s"] = [_dot(w["kb"], jnp.where(head_rows[hh], w["qt"], 0.0).astype(BF16)) for hh in range(2)]
            for w in work:
                w["e"], w["cmax"] = [], []
                for hh in range(2):
                    s = jnp.where(w["ok"], w["s"][hh], NEG)
                    cmax = jnp.max(s, axis=0, keepdims=True)
                    w["e"].append(jnp.exp2(s - cmax).astype(BF16))
                    w["cmax"].append(cmax)
            for w in work:
                ot, lt = [], []
                for hh in range(2):
                    vt_ext = jnp.concatenate([w["vt"][hh * SW_HD:(hh + 1) * SW_HD, :], ones_rows], axis=0)
                    prod = _dot(vt_ext, w["e"][hh])
                    den = prod[SW_HD:SW_HD + 1, :]
                    ot.append(prod[:SW_HD, :] / den)
                    lt.append(jnp.broadcast_to(w["cmax"][hh] + jnp.log2(den), (SW_HD, blk)))
                o_scr[g, w["sl"], rows(w["q0"]), :] = jnp.concatenate(ot, axis=0).T
                l_scr[g, w["sl"], rows(w["q0"]), :] = jnp.concatenate(lt, axis=0).T
            return carry

        lax.fori_loop(0, tile // blk // SWA_UNROLL, block_pair, 0)

    for sl in range(n_slab):
        ls = [l_scr[g, sl] for g in range(len(SW_DILATIONS))]
        mx = functools.reduce(jnp.maximum, ls)
        ws = [jnp.exp2(l - mx) for l in ls]
        num = functools.reduce(lambda a, b: a + b, [w * o_scr[g, sl] for g, w in enumerate(ws)])
        y_ref[:, sl * LANES:(sl + 1) * LANES] = (num / functools.reduce(lambda a, b: a + b, ws)).astype(BF16)


def _swa(q, k, v, batch, seq):
    tile = SWA_TILE
    n_slab = SW_W // LANES
    per_batch = seq // tile
    spec = pl.BlockSpec((n_slab, tile, LANES), lambda bb, n: (0, bb * per_batch + n, 0))
    n_br = len(SW_DILATIONS)
    return pl.pallas_call(
        _swa_kernel,
        grid=(batch, per_batch),
        in_specs=[spec, spec, spec],
        out_specs=pl.BlockSpec((tile, SW_W), lambda bb, n: (bb * per_batch + n, 0)),
        out_shape=jax.ShapeDtypeStruct((batch * seq, SW_W), BF16),
        scratch_shapes=[pltpu.VMEM((n_slab, 2 * tile, LANES), F32), pltpu.VMEM((n_slab, 2 * tile, LANES), F32),
                        pltpu.VMEM((n_br, n_slab, tile, LANES), F32), pltpu.VMEM((n_br, n_slab, tile, LANES), F32)],
        compiler_params=_params("parallel", "arbitrary"),
        name="swa",
    )(q, k, v)


def _outproj_kernel(x_ref, ya_ref, yb_ref, yc_ref, wa_ref, wb_ref, wc_ref, g_ref, out_ref):
    ya = jnp.concatenate([ya_ref[hh] for hh in range(MLA_HEADS)], axis=1)
    y = _dot(ya, wa_ref[...]) + _dot(yb_ref[...], wb_ref[...]) + _dot(yc_ref[...], wc_ref[...])
    out_ref[...] = x_ref[...] + _rms(y, g_ref[...])


def _outproj(x, ya, yb, yc, wa, wb, wc, g):
    t_all = x.shape[0]
    tm = WIDE_TILE
    tok = lambda w_: pl.BlockSpec((tm, w_), lambda i: (i, 0))
    full = lambda a: pl.BlockSpec(a.shape, lambda i: (0,) * a.ndim)
    return pl.pallas_call(
        _outproj_kernel,
        grid=(t_all // tm,),
        in_specs=[tok(D_MODEL), pl.BlockSpec((MLA_HEADS, tm, LANES), lambda i: (0, i, 0)), tok(ML_W), tok(SW_W),
                  full(wa), full(wb), full(wc), full(g)],
        out_specs=tok(D_MODEL),
        out_shape=jax.ShapeDtypeStruct((t_all, D_MODEL), F32),
        compiler_params=_params("parallel"),
        name="outproj",
    )(x, ya, yb, yc, wa, wb, wc, g)


def _memkv_kernel(mem_ref, g_ref, w_ref, kv_ref):
    kv_ref[...] = _dot(_rms(mem_ref[...], g_ref[...]).astype(BF16), w_ref[...]).astype(BF16)


def _memkv(mem, g, w):
    n = mem.shape[0]
    depth = w.shape[0]
    return pl.pallas_call(
        _memkv_kernel,
        grid=(depth,),
        in_specs=[pl.BlockSpec(mem.shape, lambda l: (0, 0)),
                  pl.BlockSpec((None, 1, D_MODEL), lambda l: (l, 0, 0)),
                  pl.BlockSpec((None, D_MODEL, 2 * D_MODEL), lambda l: (l, 0, 0))],
        out_specs=pl.BlockSpec((None, n, 2 * D_MODEL), lambda l: (l, 0, 0)),
        out_shape=jax.ShapeDtypeStruct((depth, n, 2 * D_MODEL), BF16),
        compiler_params=_params("parallel"),
        name="memkv",
    )(mem, g, w)


def _xattn_kernel(x_ref, gpre_ref, wq_ref, kv_ref, wo_ref, gpost_ref, out_ref, o_scr):
    x = x_ref[...]
    h = _rms(x, gpre_ref[...]).astype(BF16)
    q = (_dot(h, wq_ref[...]) * X_HD ** -0.5).astype(BF16)
    for hh in range(X_HEADS):
        hs = slice(hh * X_HD, (hh + 1) * X_HD)
        s = _dot_nt(q[:, hs], kv_ref[:, hs])
        e = jnp.exp(s - jnp.max(s, axis=1, keepdims=True))
        den = jnp.sum(e, axis=1, keepdims=True)
        o_scr[:, hs] = (_dot(e.astype(BF16), kv_ref[:, D_MODEL + hh * X_HD:D_MODEL + (hh + 1) * X_HD]) / den).astype(BF16)
    y = _dot(o_scr[...], wo_ref[...])
    out_ref[...] = x + _rms(y, gpost_ref[...])


def _xattn(x, gpre, wq, kv, wo, gpost, seq):
    t_all = x.shape[0]
    tm = WIDE_TILE
    per_batch = seq // tm
    tok = pl.BlockSpec((tm, D_MODEL), lambda i: (i, 0))
    full = lambda a: pl.BlockSpec(a.shape, lambda i: (0,) * a.ndim)
    return pl.pallas_call(
        _xattn_kernel,
        grid=(t_all // tm,),
        in_specs=[tok, full(gpre), full(wq), pl.BlockSpec((N_MEM, 2 * D_MODEL), lambda i: (i // per_batch, 0)),
                  full(wo), full(gpost)],
        out_specs=tok,
        out_shape=jax.ShapeDtypeStruct((t_all, D_MODEL), F32),
        scratch_shapes=[pltpu.VMEM((tm, D_MODEL), BF16)],
        compiler_params=_params("parallel"),
        name="xattn",
    )(x, gpre, wq, kv, wo, gpost)


def _ffn_kernel(x_ref, gpre_ref, wgu_ref, wd_ref, gpost_ref, out_ref):
    x = x_ref[...]
    h = _rms(x, gpre_ref[...]).astype(BF16)
    y = None
    for lo, hi in zip(FFN_SPLITS[:-1], FFN_SPLITS[1:]):
        gate = _dot(h, wgu_ref[:, lo:hi])
        up = _dot(h, wgu_ref[:, D_FF + lo:D_FF + hi])
        part = _dot((gate * jax.nn.sigmoid(gate) * up).astype(BF16), wd_ref[lo:hi, :])
        y = part if y is None else y + part
    out_ref[...] = x + _rms(y, gpost_ref[...])


def _ffn(x, gpre, w_gate_up, w_down, gpost):
    t_all = x.shape[0]
    tm = TOK_TILE
    tok = pl.BlockSpec((tm, D_MODEL), lambda i: (i, 0))
    full = lambda a: pl.BlockSpec(a.shape, lambda i: (0,) * a.ndim)
    return pl.pallas_call(
        _ffn_kernel,
        grid=(t_all // tm,),
        in_specs=[tok, full(gpre), full(w_gate_up), full(w_down), full(gpost)],
        out_specs=tok,
        out_shape=jax.ShapeDtypeStruct((t_all, D_MODEL), F32),
        compiler_params=_params("parallel"),
        name="ffn",
    )(x, gpre, w_gate_up, w_down, gpost)


def _rope_tables(seq):
    pos = jnp.arange(seq, dtype=F32)[:, None]

    def cos_sin(d_rot):
        inv = ROPE_THETA ** (-jnp.arange(0, d_rot, 2, dtype=F32) / d_rot)
        ang = pos * inv[None, :]
        return jnp.cos(ang), jnp.sin(ang)

    c, s = cos_sin(MLA_ROPE)
    ones = jnp.ones((seq, MLA_NOPE), F32)
    tail = jnp.ones((seq, LANES - MLA_NOPE - MLA_ROPE), F32)
    cm = jnp.concatenate([ones, c, c, tail], axis=1)
    sm = jnp.concatenate([0 * ones, -s, s, 0 * tail], axis=1)
    c, s = cos_sin(SW_ROT)
    rest = jnp.ones((seq, SW_HD - SW_ROT), F32)
    cs = jnp.tile(jnp.concatenate([c, c, rest], axis=1), (1, LANES // SW_HD))
    ss = jnp.tile(jnp.concatenate([-s, s, 0 * rest], axis=1), (1, LANES // SW_HD))
    return cm, sm, cs, ss


def _prep_layer(w_in, w_uq, w_ukv, w_out):
    o = IN_OFFS
    col = lambda i: w_in[:, o[i]:o[i + 1]]
    z = lambda n: jnp.zeros((D_MODEL, n), F32)
    w_p = jnp.concatenate([
        col(0), col(1),
        z(MLA_NOPE), col(2), z(LANES - MLA_NOPE - MLA_ROPE),
        col(3), col(4), col(6), col(9), col(10), col(11)], axis=1)
    assert w_p.shape[1] == P_TOTAL
    mlv = jnp.pad(col(5).reshape(D_MODEL, ML_HEADS, ML_DV), ((0, 0), (0, 0), (0, ML_VT_ROWS - ML_DV)))
    w_gates_t = jnp.concatenate([mlv.reshape(D_MODEL, ML_HEADS * ML_VT_ROWS), col(7), col(8),
                                 z(ML_T_ROWS - ML_HEADS * ML_VT_ROWS - 2 * ML_HEADS)], axis=1).T
    uq = w_uq.reshape(MLA_Q_LORA, MLA_HEADS, MLA_NOPE + MLA_ROPE)
    uq = jnp.pad(uq, ((0, 0), (0, 0), (0, LANES - MLA_NOPE - MLA_ROPE))).reshape(MLA_Q_LORA, MLA_PAD_W)
    ukv = w_ukv.reshape(MLA_KV_LORA, MLA_HEADS, MLA_NOPE + MLA_V)
    pad_half = lambda a: jnp.pad(a, ((0, 0), (0, 0), (0, LANES - a.shape[2]))).reshape(MLA_KV_LORA, MLA_PAD_W)
    wk = pad_half(ukv[:, :, :MLA_NOPE])
    wv = jnp.pad(ukv[:, :, MLA_NOPE:], ((0, 0), (0, 0), (0, MLA_VT_ROWS - MLA_V)))
    wv = wv.reshape(MLA_KV_LORA, MLA_HEADS * MLA_VT_ROWS).T
    wa = w_out[:MLA_W].reshape(MLA_HEADS, MLA_V, D_MODEL)
    wa = jnp.pad(wa, ((0, 0), (0, LANES - MLA_V), (0, 0))).reshape(MLA_PAD_W, D_MODEL)
    wb, wc = w_out[MLA_W:MLA_W + ML_W], w_out[MLA_W + ML_W:]
    bf = lambda a: a.astype(BF16)
    return bf(w_p), bf(w_gates_t), bf(uq), bf(wk), bf(wv), bf(wa), bf(wb), bf(wc)


def kernel(x, mem, w_in, mla_q_norm, mla_w_uq, mla_kv_norm, mla_w_ukv, ml_conv_w, ml_conv_b, ml_b_i, ml_b_f,
           w_out, x_w_q, x_w_kv, x_w_o, w_gate_up, w_down, norm_mix_pre, norm_mix_post, norm_mem,
           norm_x_pre, norm_x_post, norm_ffn_pre, norm_ffn_post):
    batch, seq, _ = x.shape
    depth = w_in.shape[0]
    t_all = batch * seq
    assert seq % SWA_TILE == 0 and seq % TOK_TILE == 0 and seq % WIDE_TILE == 0
    assert WIDE_TILE % FLASH_KEY_TILE == 0
    assert SWA_TILE == SW_DILATIONS[-1] * SW_BLOCK
    cm, sm, cs, ss = _rope_tables(seq)
    row = lambda a: a.reshape(1, -1)
    colv = lambda a: a.reshape(-1, 1)
    xt = x.reshape(t_all, D_MODEL)
    mem2 = mem.reshape(batch * N_MEM, D_MODEL)
    w_p_all, w_gt_all, uq_all, wk_all, wv_all, wa_all, wb_all, wc_all = jax.vmap(_prep_layer)(
        w_in, mla_w_uq, mla_w_ukv, w_out)
    x_w_q, x_w_o, w_gate_up, w_down = (a.astype(BF16) for a in (x_w_q, x_w_o, w_gate_up, w_down))
    kv_all = _memkv(mem2, norm_mem.reshape(depth, 1, D_MODEL), x_w_kv.astype(BF16))
    for l in range(depth):
        w_p, w_gt, uq, wk, wv, wa, wb, wc = (a[l] for a in (w_p_all, w_gt_all, uq_all, wk_all, wv_all,
                                                            wa_all, wb_all, wc_all))
        q, k, v, mqk, mv, mo, gates_t, sq, sk, sv = _inproj(
            xt, row(norm_mix_pre[l]), w_p, w_gt, row(mla_q_norm[l]), uq, row(mla_kv_norm[l]), wk, wv,
            cm, sm, cs, ss, seq)
        b3 = lambda a: a.reshape(batch, seq, a.shape[-1])
        ya = _flash(q, k, v, batch, seq)
        yb = _mlstm(b3(mqk), mv, b3(mo), gates_t, ml_conv_w[l], row(ml_conv_b[l]), colv(ml_b_i[l]),
                    colv(ml_b_f[l])).reshape(t_all, ML_W)
        yc = _swa(sq, sk, sv, batch, seq)
        xt = _outproj(xt, ya, yb, yc, wa, wb, wc, row(norm_mix_post[l]))
        xt = _xattn(xt, row(norm_x_pre[l]), x_w_q[l], kv_all[l], x_w_o[l], row(norm_x_post[l]), seq)
        xt = _ffn(xt, row(norm_ffn_pre[l]), w_gate_up[l], w_down[l], row(norm_ffn_post[l]))
    return xt.reshape(batch, seq, D_MODEL)
```

```python
import functools

import jax
import jax.numpy as jnp
import numpy as np
from jax import lax
from jax.experimental import pallas as pl
from jax.experimental.pallas import tpu as pltpu

F32 = jnp.float32
BF16 = jnp.bfloat16

D_MODEL = 1024
N_MEM = 256
ROPE_THETA = 500000.0
EPS = 1e-6
NEG = -1e30
LOG2_E = 1.4426950408889634
LANES = 128

MLA_HEADS, MLA_NOPE, MLA_ROPE, MLA_V = 8, 64, 32, 64
MLA_Q_LORA, MLA_KV_LORA = 256, 128
ML_HEADS, ML_DK, ML_DV, ML_CONV, ML_CHUNK = 4, 64, 64, 4, 128
SW_HEADS, SW_HD = 4, 64
SW_ROT = SW_HD // 4
SW_DILATIONS = (1, 4, 16)
SW_BLOCK = 128
X_HEADS = 4
X_HD = D_MODEL // X_HEADS
D_FF = 2816

MLA_W = MLA_HEADS * MLA_V
ML_W = ML_HEADS * ML_DV
SW_W = SW_HEADS * SW_HD
MLA_PAD_W = MLA_HEADS * LANES

IN_SIZES = (MLA_Q_LORA, MLA_KV_LORA, MLA_ROPE, ML_HEADS * ML_DK, ML_HEADS * ML_DK, ML_W, ML_W,
            ML_HEADS, ML_HEADS, SW_W, SW_W, SW_W)
IN_OFFS = tuple(int(v) for v in np.cumsum((0,) + IN_SIZES))

P_CQ = 0
P_CKV = 256
P_KROPE = 384
P_MLQK = 512
P_MLO = 1024
P_SWQ = 1280
P_SWK = 1536
P_SWV = 1792
P_TOTAL = 2048
ML_VT_ROWS = 80
ML_T_ROWS = ML_HEADS * ML_VT_ROWS + 16

TOK_TILE = 512
WIDE_TILE = 1024
FLASH_TILE = 1024
FLASH_KEY_TILE = 256
MLA_VT_ROWS = 80
SWA_TILE = 2048
SWA_UNROLL = 8
FFN_SPLITS = (0, 1536, D_FF)
ML_CHUNKS_PER_STEP = 8
VMEM_LIMIT = 56 * 1024 * 1024


def _rms(x, g):
    return x * lax.rsqrt(jnp.mean(x * x, axis=-1, keepdims=True) + EPS) * g


def _dot(a, b):
    return jnp.dot(a, b, preferred_element_type=F32)


def _dot_nt(a, b):
    return lax.dot_general(a, b, (((1,), (1,)), ((), ())), preferred_element_type=F32)


def _params(*sem):
    return pltpu.CompilerParams(dimension_semantics=sem, vmem_limit_bytes=VMEM_LIMIT)


def _inproj_kernel(x_ref, g_ref, w_ref, wgt_ref, qg_ref, wuq_ref, kvg_ref, wk_ref, wvt_ref,
                   cm_ref, sm_ref, cs_ref, ss_ref,
                   q_ref, k_ref, vt_ref, mqk_ref, mv_ref, mo_ref, gr_ref, sq_ref, sk_ref, sv_ref):
    tm = x_ref.shape[0]
    h = _rms(x_ref[...], g_ref[...]).astype(BF16)

    def proj(a, b):
        return _dot(h, w_ref[:, a:b])

    lane = lax.broadcasted_iota(jnp.int32, (tm, LANES), 1)
    cm, sm = cm_ref[...], sm_ref[...]
    mla_first = lane < MLA_NOPE + MLA_ROPE // 2

    def rope_mla(t):
        rot = jnp.where(mla_first, pltpu.roll(t, LANES - MLA_ROPE // 2, 1), pltpu.roll(t, MLA_ROPE // 2, 1))
        return t * cm + rot * sm

    cs, ss = cs_ref[...], ss_ref[...]
    sw_first = (lane % SW_HD) < SW_ROT // 2

    def rope_sw(t):
        rot = jnp.where(sw_first, pltpu.roll(t, LANES - SW_ROT // 2, 1), pltpu.roll(t, SW_ROT // 2, 1))
        return t * cs + rot * ss

    cqn = _rms(proj(P_CQ, P_CQ + MLA_Q_LORA), qg_ref[...]).astype(BF16)
    q_scale = (MLA_NOPE + MLA_ROPE) ** -0.5 * LOG2_E
    for pair in range(MLA_HEADS // 2):
        two = _dot(cqn, wuq_ref[:, 2 * pair * LANES:2 * (pair + 1) * LANES])
        for sub in range(2):
            q_ref[2 * pair + sub] = (rope_mla(two[:, sub * LANES:(sub + 1) * LANES]) * q_scale).astype(BF16)
    ckv_kr = proj(P_CKV, P_KROPE + LANES)
    kvn = _rms(ckv_kr[:, :MLA_KV_LORA], kvg_ref[...]).astype(BF16)
    kpe = rope_mla(ckv_kr[:, MLA_KV_LORA:])
    one_col = (lax.broadcasted_iota(jnp.int32, (MLA_VT_ROWS, 1), 0) == MLA_V).astype(F32)
    for pair in range(MLA_HEADS // 2):
        two = _dot(kvn, wk_ref[:, 2 * pair * LANES:2 * (pair + 1) * LANES])
        for sub in range(2):
            k_ref[2 * pair + sub] = (two[:, sub * LANES:(sub + 1) * LANES] + kpe).astype(BF16)
    for hh in range(MLA_HEADS):
        rows = slice(hh * MLA_VT_ROWS, (hh + 1) * MLA_VT_ROWS)
        for u in range(tm // FLASH_KEY_TILE):
            toks = slice(u * FLASH_KEY_TILE, (u + 1) * FLASH_KEY_TILE)
            vt_ref[hh, u] = (_dot_nt(wvt_ref[rows, :], kvn[toks, :]) + one_col).astype(BF16)

    mqk_ref[...] = proj(P_MLQK, P_MLQK + 2 * ML_HEADS * ML_DK)
    mo_ref[...] = proj(P_MLO, P_MLO + ML_W)
    tr = _dot_nt(wgt_ref[...], h)
    n_v = ML_HEADS * ML_VT_ROWS
    v_row = lax.broadcasted_iota(jnp.int32, (n_v, 1), 0)
    mv_ref[...] = (tr[0:n_v, :] + ((v_row % ML_VT_ROWS) == ML_DV).astype(F32)).astype(BF16)
    gr_ref[...] = tr[n_v:n_v + 2 * ML_HEADS, :]

    sw_q, sw_k, sw_v = (proj(p0, p0 + SW_W) for p0 in (P_SWQ, P_SWK, P_SWV))
    for half in range(SW_W // LANES):
        sl = slice(half * LANES, (half + 1) * LANES)
        sq_ref[half] = rope_sw(sw_q[:, sl]) * (SW_HD ** -0.5 * LOG2_E)
        sk_ref[half] = rope_sw(sw_k[:, sl])
        sv_ref[half] = sw_v[:, sl]


def _inproj(x, g, w, wgt, qg, wuq, kvg, wk, wv, cm, sm, cs, ss, seq):
    t_all = x.shape[0]
    tm = WIDE_TILE
    pos_tiles = seq // tm
    tok = lambda w_: pl.BlockSpec((tm, w_), lambda i: (i, 0))
    full = lambda a: pl.BlockSpec(a.shape, lambda i: (0,) * a.ndim)
    pos = pl.BlockSpec((tm, LANES), lambda i: (i % pos_tiles, 0))
    slabs = pl.BlockSpec((SW_W // LANES, tm, LANES), lambda i: (0, i, 0))
    out_shapes = (
        jax.ShapeDtypeStruct((MLA_HEADS, t_all, LANES), BF16),
        jax.ShapeDtypeStruct((MLA_HEADS, t_all, LANES), BF16),
        jax.ShapeDtypeStruct((MLA_HEADS, t_all // FLASH_KEY_TILE, MLA_VT_ROWS, FLASH_KEY_TILE), BF16),
        jax.ShapeDtypeStruct((t_all, 2 * ML_HEADS * ML_DK), F32),
        jax.ShapeDtypeStruct((t_all // seq, ML_HEADS * ML_VT_ROWS, seq), BF16),
        jax.ShapeDtypeStruct((t_all, ML_W), F32),
        jax.ShapeDtypeStruct((t_all // seq, 2 * ML_HEADS, seq), F32),
        jax.ShapeDtypeStruct((SW_W // LANES, t_all, LANES), F32),
        jax.ShapeDtypeStruct((SW_W // LANES, t_all, LANES), F32),
        jax.ShapeDtypeStruct((SW_W // LANES, t_all, LANES), F32),
    )
    heads = pl.BlockSpec((MLA_HEADS, tm, LANES), lambda i: (0, i, 0))
    out_specs = (heads, heads,
                 pl.BlockSpec((MLA_HEADS, tm // FLASH_KEY_TILE, MLA_VT_ROWS, FLASH_KEY_TILE), lambda i: (0, i, 0, 0)),
                 tok(2 * ML_HEADS * ML_DK),
                 pl.BlockSpec((None, ML_HEADS * ML_VT_ROWS, tm), lambda i: (i // pos_tiles, 0, i % pos_tiles)),
                 tok(ML_W), pl.BlockSpec((None, 2 * ML_HEADS, tm), lambda i: (i // pos_tiles, 0, i % pos_tiles)),
                 slabs, slabs, slabs)
    return pl.pallas_call(
        _inproj_kernel,
        grid=(t_all // tm,),
        in_specs=[tok(D_MODEL), full(g), full(w), full(wgt), full(qg), full(wuq), full(kvg), full(wk), full(wv),
                  pos, pos, pos, pos],
        out_specs=out_specs,
        out_shape=out_shapes,
        compiler_params=_params("parallel"),
        name="inproj",
    )(x, g, w, wgt, qg, wuq, kvg, wk, wv, cm, sm, cs, ss)


def _flash_kernel(q_ref, k_ref, vt_ref, o_ref, m_scr, acc_scr, s_scr, cmax_scr):
    tq = q_ref.shape[0]
    tk = FLASH_KEY_TILE
    per_q = tq // tk
    i = pl.program_id(2)
    n_tiles = (i + 1) * per_q
    qt = q_ref[...].astype(F32).T.astype(BF16)
    m_scr[...] = jnp.full(m_scr.shape, NEG, F32)
    acc_scr[...] = jnp.zeros(acc_scr.shape, F32)

    key_l = lax.broadcasted_iota(jnp.int32, (tk, tk), 0)
    qry_l = lax.broadcasted_iota(jnp.int32, (tk, tk), 1)

    def scores(j, buf, d):
        start = pl.multiple_of(j * tk, tk)
        c0 = 0 if d is None else d * tk
        st = _dot(k_ref[pl.ds(start, tk), :], qt[:, c0:])
        if d is not None:
            tri = jnp.where(key_l <= qry_l, st[:, :tk], NEG)
            st = tri if st.shape[1] == tk else jnp.concatenate([tri, st[:, tk:]], axis=1)
        s_scr[buf, :, c0:] = st
        cmax_scr[buf, :, c0:] = jnp.max(st, axis=0, keepdims=True)

    def accumulate(j, buf, d):
        cols = slice(0 if d is None else d * tk, None)
        m_old = m_scr[:, cols]
        m_new = jnp.maximum(m_old, cmax_scr[buf, :, cols])
        p = jnp.exp2(s_scr[buf, :, cols] - m_new).astype(BF16)
        acc_scr[:, cols] = acc_scr[:, cols] * jnp.exp2(m_old - m_new) + _dot(vt_ref[j], p)
        m_scr[:, cols] = m_new

    def pairs(pp, offsets, held0):
        held = [held0, None]
        for c, (d_a, d_b) in enumerate(offsets):
            scores(2 * (pp + c) + 1, 1, d_a)
            held[1] = d_a
            accumulate(2 * (pp + c), 0, held[0])
            scores(2 * (pp + c) + 2, 0, d_b)
            held[0] = d_b
            accumulate(2 * (pp + c) + 1, 1, held[1])
        return held[0]

    def finish(held0):
        scores(n_tiles - 1, 1, per_q - 1)
        accumulate(n_tiles - 2, 0, held0)
        accumulate(n_tiles - 1, 1, per_q - 1)
        acc = acc_scr[...]
        out_t = acc[0:MLA_V, :] / acc[MLA_V:MLA_V + 1, :]
        o_ref[...] = jnp.concatenate([out_t, jnp.zeros_like(out_t)], axis=0).T.astype(BF16)

    assert per_q == 4

    @pl.when(i == 0)
    def _():
        scores(0, 0, 0)
        finish(pairs(0, [(1, 2)], 0))

    tail = [(None, None), (None, 0), (1, 2)]
    plain = [(None, None)] * 2
    n_eights = lax.shift_right_logical(jnp.maximum(i - 1, 0), 1)

    def body(g, carry):
        pairs(4 * g, plain + plain, None)
        return carry

    @pl.when((i > 0) & ((i & 1) == 1))
    def _():
        scores(0, 0, None)
        lax.fori_loop(0, n_eights, body, 0)
        finish(pairs(4 * n_eights, tail, None))

    @pl.when((i > 0) & ((i & 1) == 0))
    def _():
        scores(0, 0, None)
        lax.fori_loop(0, n_eights, body, 0)
        finish(pairs(4 * n_eights, plain + tail, None))


def _flash(q, k, vt, b, s):
    tq, tk = FLASH_TILE, FLASH_KEY_TILE
    assert s % tq == 0 and tq % (2 * tk) == 0
    nq = s // tq
    rows = pl.BlockSpec((None, tq, LANES), lambda bb, hh, i: (hh, bb * nq + i, 0))
    return pl.pallas_call(
        _flash_kernel,
        grid=(b, MLA_HEADS, nq),
        in_specs=[rows,
                  pl.BlockSpec((None, s, LANES), lambda bb, hh, i: (hh, bb, 0)),
                  pl.BlockSpec((None, s // tk, MLA_VT_ROWS, tk), lambda bb, hh, i: (hh, bb, 0, 0))],
        out_specs=rows,
        out_shape=jax.ShapeDtypeStruct((MLA_HEADS, b * s, LANES), BF16),
        scratch_shapes=[pltpu.VMEM((1, tq), F32), pltpu.VMEM((MLA_VT_ROWS, tq), F32),
                        pltpu.VMEM((2, tk, tq), F32), pltpu.VMEM((2, 1, tq), F32)],
        compiler_params=_params("parallel", "parallel", "arbitrary"),
        name="mla_flash",
    )(q, k, vt)


def _log_sigmoid(z):
    return jnp.minimum(z, 0.0) - jnp.log1p(jnp.exp(-jnp.abs(z)))


def _mlstm_kernel(qkc_ref, qkp_ref, vt_ref, o_ref, g_ref, cw_ref, cb_ref, bi_ref, bf_ref,
                  y_ref, c_scr, m_scr):
    L = ML_CHUNK
    nb = qkc_ref.shape[0]
    c = pl.program_id(0)

    @pl.when(c == 0)
    def _():
        c_scr[...] = jnp.zeros(c_scr.shape, F32)
        m_scr[...] = jnp.full(m_scr.shape, NEG, F32)

    src = lax.broadcasted_iota(jnp.int32, (L, L), 0)
    tgt = lax.broadcasted_iota(jnp.int32, (L, L), 1)
    causal = src <= tgt
    pick = (lax.broadcasted_iota(jnp.int32, (L, ML_HEADS * L), 0)
            == lax.broadcasted_iota(jnp.int32, (L, ML_HEADS * L), 1) // L).astype(BF16)

    def chunk(sub):
        rows_t = slice(sub * L, (sub + 1) * L)
        streams = [(b, hh) for b in range(nb) for hh in range(ML_HEADS)]
        qk, b_all, li_all, x_col = [], [], [], []
        for b in range(nb):
            cur = qkc_ref[b, rows_t, :]
            prev = jnp.where(c > 0, qkp_ref[b], 0.0) if sub == 0 else qkc_ref[b, (sub - 1) * L:sub * L, :]
            row_w = lax.broadcasted_iota(jnp.int32, cur.shape, 0)
            conv = cb_ref[...] + cw_ref[ML_CONV - 1:ML_CONV, :] * cur
            for sh in range(1, ML_CONV):
                shifted = jnp.where(row_w >= sh, pltpu.roll(cur, sh, 0), pltpu.roll(prev, sh, 0))
                conv = conv + cw_ref[ML_CONV - 1 - sh:ML_CONV - sh, :] * shifted
            qk.append(conv * jax.nn.sigmoid(conv))
            gates = g_ref[b, :, rows_t]
            li = gates[0:ML_HEADS, :] + bi_ref[...]
            lf = _log_sigmoid(gates[ML_HEADS:, :] + bf_ref[...])
            lane_g = lax.broadcasted_iota(jnp.int32, lf.shape, 1)
            step = 1
            while step < L:
                lf = lf + jnp.where(lane_g >= step, pltpu.roll(lf, step, 1), 0.0)
                step *= 2
            li_all.append(li)
            b_all.append(lf)
            x = jnp.concatenate([li - lf, jnp.zeros((L - ML_HEADS, L), F32)], axis=0).T
            spread = jnp.zeros((L, ML_HEADS * L), F32)
            for _ in range(3):
                part = x.astype(BF16)
                spread = spread + _dot(part, pick)
                x = x - part.astype(F32)
            x_col.append(spread)

        st = []
        for n, (b, hh) in enumerate(streams):
            b_row = b_all[b][hh:hh + 1, :]
            li_row = li_all[b][hh:hh + 1, :]
            m_prev = m_scr[n:n + 1, 0:1]
            d_t = jnp.where(causal, x_col[b][:, hh * L:(hh + 1) * L] + b_row, NEG)
            inter = b_row + m_prev
            m_t = jnp.maximum(inter, jnp.max(d_t, axis=0, keepdims=True))
            b_last = b_row[:, L - 1:L]
            g = b_last - b_row + li_row
            m_new = jnp.maximum(b_last + m_prev, jnp.max(g, axis=1, keepdims=True))
            kf = qk[b][:, (ML_HEADS + hh) * ML_DK:(ML_HEADS + hh + 1) * ML_DK] * ML_DK ** -0.5
            vt = vt_ref[b, hh * ML_VT_ROWS:(hh + 1) * ML_VT_ROWS, rows_t]
            st.append(dict(
                qh=qk[b][:, hh * ML_DK:(hh + 1) * ML_DK].astype(BF16),
                kh=kf.astype(BF16), vt=vt,
                vw=(vt.astype(F32) * jnp.exp(g - m_new)).astype(BF16),
                c_t=c_scr[n],
                w_intra=jnp.exp(d_t - m_t), w_inter=jnp.exp(inter - m_t), floor=jnp.exp(-m_t),
                decay=jnp.exp(b_last + m_prev - m_new), m_new=m_new))

        for s_ in st:
            s_["sraw"] = _dot_nt(s_["kh"], s_["qh"])
            s_["cross"] = _dot_nt(s_["c_t"].astype(BF16), s_["qh"])
        for s_ in st:
            s_["s"] = s_["sraw"] * s_["w_intra"]
        for s_ in st:
            s_["intra"] = _dot(s_["vt"], s_["s"].astype(BF16))
            s_["kv"] = _dot(s_["vw"], s_["kh"])
        h_t = []
        for n, s_ in enumerate(st):
            num = s_["intra"][:ML_DV, :] + s_["w_inter"] * s_["cross"][:ML_DV, :]
            nq = jnp.sum(s_["s"], axis=0, keepdims=True) + s_["w_inter"] * s_["cross"][ML_DV:ML_DV + 1, :]
            h_t.append(num / jnp.maximum(jnp.abs(nq), s_["floor"]))
            c_scr[n] = s_["decay"] * s_["c_t"] + s_["kv"]
            m_scr[n:n + 1, :] = jnp.broadcast_to(s_["m_new"], (1, LANES))
        for b in range(nb):
            hout = jnp.concatenate(h_t[b * ML_HEADS:(b + 1) * ML_HEADS], axis=0).T
            y_ref[b, rows_t, :] = (jax.nn.sigmoid(o_ref[b, rows_t, :]) * hout).astype(BF16)

    for sub in range(qkc_ref.shape[1] // L):
        chunk(sub)


def _mlstm(qk_pre, vt, o_pre, gates_t, conv_w, conv_b, b_i, b_f):
    batch, seq, wqk = qk_pre.shape
    L = ML_CHUNK
    step = ML_CHUNKS_PER_STEP * L
    full = lambda a: pl.BlockSpec(a.shape, lambda c: (0,) * a.ndim)
    chunk = lambda w_: pl.BlockSpec((batch, step, w_), lambda c: (0, c, 0))
    lanes = lambda r: pl.BlockSpec((batch, r, step), lambda c: (0, 0, c))
    return pl.pallas_call(
        _mlstm_kernel,
        grid=(seq // step,),
        in_specs=[chunk(wqk),
                  pl.BlockSpec((batch, L, wqk), lambda c: (0, jnp.maximum(ML_CHUNKS_PER_STEP * c - 1, 0), 0)),
                  lanes(ML_HEADS * ML_VT_ROWS), chunk(ML_W), lanes(2 * ML_HEADS),
                  full(conv_w), full(conv_b), full(b_i), full(b_f)],
        out_specs=chunk(ML_W),
        out_shape=jax.ShapeDtypeStruct((batch, seq, ML_W), BF16),
        scratch_shapes=[pltpu.VMEM((batch * ML_HEADS, ML_VT_ROWS, ML_DK), F32),
                        pltpu.VMEM((batch * ML_HEADS, LANES), F32)],
        compiler_params=_params("arbitrary"),
        name="mlstm",
    )(qk_pre, qk_pre, vt, o_pre, gates_t, conv_w, conv_b, b_i, b_f)


def _swa_kernel(q_ref, k_ref, v_ref, y_ref, kk_scr, vv_scr, o_scr, l_scr):
    blk, tile = SW_BLOCK, SWA_TILE
    n_slab = SW_W // LANES
    n = pl.program_id(1)

    @pl.when(n == 0)
    def _():
        kk_scr[:, 0:tile, :] = jnp.zeros((n_slab, tile, LANES), F32)
        vv_scr[:, 0:tile, :] = jnp.zeros((n_slab, tile, LANES), F32)

    @pl.when(n > 0)
    def _():
        kk_scr[:, 0:tile, :] = kk_scr[:, tile:2 * tile, :]
        vv_scr[:, 0:tile, :] = vv_scr[:, tile:2 * tile, :]

    kk_scr[:, tile:2 * tile, :] = k_ref[...]
    vv_scr[:, tile:2 * tile, :] = v_ref[...]

    key = lax.broadcasted_iota(jnp.int32, (2 * blk, blk), 0)
    qry = lax.broadcasted_iota(jnp.int32, (2 * blk, blk), 1)
    dim =lax.broadcasted_iota(jnp.int32, (LANES, blk), 0)
    head_rows = [dim < SW_HD, dim >= SW_HD]
    ones_rows = jnp.ones((16, 2 * blk), BF16)

    for g, dil in enumerate(SW_DILATIONS):
        per_res = tile // (blk * dil)
        shift = per_res.bit_length() - 1

        def block_pair(it, carry, g=g, dil=dil, per_res=per_res, shift=shift):
            def rows(start):
                return pl.ds(start, blk, stride=dil) if dil > 1 else pl.ds(pl.multiple_of(start, blk), blk)

            work = []
            for u in range(SWA_UNROLL):
                bi = SWA_UNROLL * it + u
                r = lax.shift_right_logical(bi, shift)
                j = bi & (per_res - 1)
                q0 = r + j * (blk * dil)
                lowest = jnp.where((n == 0) & (j == 0), blk, 0)
                ok = (key >= jnp.maximum(qry, lowest)) & (key <= qry + blk)
                for sl in range(n_slab):
                    prev, cur = rows(tile + q0 - blk * dil), rows(tile + q0)
                    kb = jnp.concatenate([kk_scr[sl, prev, :], kk_scr[sl, cur, :]], axis=0)
                    vb = jnp.concatenate([vv_scr[sl, prev, :], vv_scr[sl, cur, :]], axis=0)
                    work.append(dict(q0=q0, sl=sl, ok=ok, qt=q_ref[sl, rows(q0), :].T,
                                     kb=kb.astype(BF16), vt=vb.T.astype(BF16)))
            for w in work:
                w["s"] = [_dot(w["kb"], jnp.where(head_rows[hh], w["qt"], 0.0).astype(BF16)) for hh in range(2)]
            for w in work:
                w["e"], w["cmax"] = [], []
                for hh in range(2):
                    s = jnp.where(w["ok"], w["s"][hh], NEG)
                    cmax = jnp.max(s, axis=0, keepdims=True)
                    w["e"].append(jnp.exp2(s - cmax).astype(BF16))
                    w["cmax"].append(cmax)
            for w in work:
                ot, lt = [], []
                for hh in range(2):
                    vt_ext = jnp.concatenate([w["vt"][hh * SW_HD:(hh + 1) * SW_HD, :], ones_rows], axis=0)
                    prod = _dot(vt_ext, w["e"][hh])
                    den = prod[SW_HD:SW_HD + 1, :]
                    ot.append(prod[:SW_HD, :] / den)
                    lt.append(jnp.broadcast_to(w["cmax"][hh] + jnp.log2(den), (SW_HD, blk)))
                o_scr[g, w["sl"], rows(w["q0"]), :] = jnp.concatenate(ot, axis=0).T
                l_scr[g, w["sl"], rows(w["q0"]), :] = jnp.concatenate(lt, axis=0).T
            return carry

        lax.fori_loop(0, tile // blk // SWA_UNROLL, block_pair, 0)

    for sl in range(n_slab):
        ls = [l_scr[g, sl] for g in range(len(SW_DILATIONS))]
        mx = functools.reduce(jnp.maximum, ls)
        ws = [jnp.exp2(l - mx) for l in ls]
        num = functools.reduce(lambda a, b: a + b, [w * o_scr[g, sl] for g, w in enumerate(ws)])
        y_ref[:, sl * LANES:(sl + 1) * LANES] = (num / functools.reduce(lambda a, b: a + b, ws)).astype(BF16)


def _swa(q, k, v, batch, seq):
    tile = SWA_TILE
    n_slab = SW_W // LANES
    per_batch = seq // tile
    spec = pl.BlockSpec((n_slab, tile, LANES), lambda bb, n: (0, bb * per_batch + n, 0))
    n_br = len(SW_DILATIONS)
    return pl.pallas_call(
        _swa_kernel,
        grid=(batch, per_batch),
        in_specs=[spec, spec, spec],
        out_specs=pl.BlockSpec((tile, SW_W), lambda bb, n: (bb * per_batch + n, 0)),
        out_shape=jax.ShapeDtypeStruct((batch * seq, SW_W), BF16),
        scratch_shapes=[pltpu.VMEM((n_slab, 2 * tile, LANES), F32), pltpu.VMEM((n_slab, 2 * tile, LANES), F32),
                        pltpu.VMEM((n_br, n_slab, tile, LANES), F32), pltpu.VMEM((n_br, n_slab, tile, LANES), F32)],
        compiler_params=_params("parallel", "arbitrary"),
        name="swa",
    )(q, k, v)


def _memkv_kernel(mem_ref, g_ref, w_ref, kv_ref):
    kv_ref[...] = _dot(_rms(mem_ref[...], g_ref[...]).astype(BF16), w_ref[...]).astype(BF16)


def _memkv(mem, g, w):
    n = mem.shape[0]
    depth = w.shape[0]
    return pl.pallas_call(
        _memkv_kernel,
        grid=(depth,),
        in_specs=[pl.BlockSpec(mem.shape, lambda l: (0, 0)),
                  pl.BlockSpec((None, 1, D_MODEL), lambda l: (l, 0, 0)),
                  pl.BlockSpec((None, D_MODEL, 2 * D_MODEL), lambda l: (l, 0, 0))],
        out_specs=pl.BlockSpec((None, n, 2 * D_MODEL), lambda l: (l, 0, 0)),
        out_shape=jax.ShapeDtypeStruct((depth, n, 2 * D_MODEL), BF16),
        compiler_params=_params("parallel"),
        name="memkv",
    )(mem, g, w)


def _mixout_xattn_kernel(x_ref, ya_ref, yb_ref, yc_ref, wa_ref, wb_ref, wc_ref, gmix_ref,
                         gpre_ref, wq_ref, kv_ref, wo_ref, gpost_ref, out_ref, o_scr):
    ya = jnp.concatenate([ya_ref[hh] for hh in range(MLA_HEADS)], axis=1)
    y = _dot(ya, wa_ref[...]) + _dot(yb_ref[...], wb_ref[...]) + _dot(yc_ref[...], wc_ref[...])
    x = x_ref[...] + _rms(y, gmix_ref[...])
    h = _rms(x, gpre_ref[...]).astype(BF16)
    q = (_dot(h, wq_ref[...]) * X_HD ** -0.5).astype(BF16)
    for hh in range(X_HEADS):
        hs = slice(hh * X_HD, (hh + 1) * X_HD)
        s = _dot_nt(q[:, hs], kv_ref[:, hs])
        e = jnp.exp(s - jnp.max(s, axis=1, keepdims=True))
        den = jnp.sum(e, axis=1, keepdims=True)
        o_scr[:, hs] = (_dot(e.astype(BF16), kv_ref[:, D_MODEL + hh * X_HD:D_MODEL + (hh + 1) * X_HD]) / den).astype(BF16)
    out_ref[...] = x + _rms(_dot(o_scr[...], wo_ref[...]), gpost_ref[...])


def _mixout_xattn(x, ya, yb, yc, wa, wb, wc, gmix, gpre, wq, kv, wo, gpost, seq):
    t_all = x.shape[0]
    tm = WIDE_TILE
    per_batch = seq // tm
    tok = lambda w_: pl.BlockSpec((tm, w_), lambda i: (i, 0))
    full = lambda a: pl.BlockSpec(a.shape, lambda i: (0,) * a.ndim)
    return pl.pallas_call(
        _mixout_xattn_kernel,
        grid=(t_all // tm,),
        in_specs=[tok(D_MODEL), pl.BlockSpec((MLA_HEADS, tm, LANES), lambda i: (0, i, 0)), tok(ML_W), tok(SW_W),
                  full(wa), full(wb), full(wc), full(gmix), full(gpre), full(wq),
                  pl.BlockSpec((N_MEM, 2 * D_MODEL), lambda i: (i // per_batch, 0)), full(wo), full(gpost)],
        out_specs=tok(D_MODEL),
        out_shape=jax.ShapeDtypeStruct((t_all, D_MODEL), F32),
        scratch_shapes=[pltpu.VMEM((tm, D_MODEL), BF16)],
        compiler_params=_params("parallel"),
        name="mixout_xattn",
    )(x, ya, yb, yc, wa, wb, wc, gmix, gpre, wq, kv, wo, gpost)


def _ffn_kernel(x_ref, gpre_ref, wgu_ref, wd_ref, gpost_ref, out_ref):
    x = x_ref[...]
    h = _rms(x, gpre_ref[...]).astype(BF16)
    y = None
    for lo, hi in zip(FFN_SPLITS[:-1], FFN_SPLITS[1:]):
        gate = _dot(h, wgu_ref[:, lo:hi])
        up = _dot(h, wgu_ref[:, D_FF + lo:D_FF + hi])
        part = _dot((gate * jax.nn.sigmoid(gate) * up).astype(BF16), wd_ref[lo:hi, :])
        y = part if y is None else y + part
    out_ref[...] = x + _rms(y, gpost_ref[...])


def _ffn(x, gpre, w_gate_up, w_down, gpost):
    t_all = x.shape[0]
    tm = TOK_TILE
    tok = pl.BlockSpec((tm, D_MODEL), lambda i: (i, 0))
    full = lambda a: pl.BlockSpec(a.shape, lambda i: (0,) * a.ndim)
    return pl.pallas_call(
        _ffn_kernel,
        grid=(t_all // tm,),
        in_specs=[tok, full(gpre), full(w_gate_up), full(w_down), full(gpost)],
        out_specs=tok,
        out_shape=jax.ShapeDtypeStruct((t_all, D_MODEL), F32),
        compiler_params=_params("parallel"),
        name="ffn",
    )(x, gpre, w_gate_up, w_down, gpost)


def _rope_tables(seq):
    pos = jnp.arange(seq, dtype=F32)[:, None]

    def cos_sin(d_rot):
        inv = ROPE_THETA ** (-jnp.arange(0, d_rot, 2, dtype=F32) / d_rot)
        ang = pos * inv[None, :]
        return jnp.cos(ang), jnp.sin(ang)

    c, s = cos_sin(MLA_ROPE)
    ones = jnp.ones((seq, MLA_NOPE), F32)
    tail = jnp.ones((seq, LANES - MLA_NOPE - MLA_ROPE), F32)
    cm = jnp.concatenate([ones, c, c, tail], axis=1)
    sm = jnp.concatenate([0 * ones, -s, s, 0 * tail], axis=1)
    c, s = cos_sin(SW_ROT)
    rest = jnp.ones((seq, SW_HD - SW_ROT), F32)
    cs = jnp.tile(jnp.concatenate([c, c, rest], axis=1), (1, LANES // SW_HD))
    ss = jnp.tile(jnp.concatenate([-s, s, 0 * rest], axis=1), (1, LANES // SW_HD))
    return cm, sm, cs, ss


def _prep_layer(w_in, w_uq, w_ukv, w_out):
    o = IN_OFFS
    col = lambda i: w_in[:, o[i]:o[i + 1]]
    z = lambda n: jnp.zeros((D_MODEL, n), F32)
    w_p = jnp.concatenate([
        col(0), col(1),
        z(MLA_NOPE), col(2), z(LANES - MLA_NOPE - MLA_ROPE),
        col(3), col(4), col(6), col(9), col(10), col(11)], axis=1)
    assert w_p.shape[1] == P_TOTAL
    mlv = jnp.pad(col(5).reshape(D_MODEL, ML_HEADS, ML_DV), ((0, 0), (0, 0), (0, ML_VT_ROWS - ML_DV)))
    w_gates_t = jnp.concatenate([mlv.reshape(D_MODEL, ML_HEADS * ML_VT_ROWS), col(7), col(8),
                                 z(ML_T_ROWS - ML_HEADS * ML_VT_ROWS - 2 * ML_HEADS)], axis=1).T
    uq = w_uq.reshape(MLA_Q_LORA, MLA_HEADS, MLA_NOPE + MLA_ROPE)
    uq = jnp.pad(uq, ((0, 0), (0, 0), (0, LANES - MLA_NOPE - MLA_ROPE))).reshape(MLA_Q_LORA, MLA_PAD_W)
    ukv = w_ukv.reshape(MLA_KV_LORA, MLA_HEADS, MLA_NOPE + MLA_V)
    pad_half = lambda a: jnp.pad(a, ((0, 0), (0, 0), (0, LANES - a.shape[2]))).reshape(MLA_KV_LORA, MLA_PAD_W)
    wk = pad_half(ukv[:, :, :MLA_NOPE])
    wv = jnp.pad(ukv[:, :, MLA_NOPE:], ((0, 0), (0, 0), (0, MLA_VT_ROWS - MLA_V)))
    wv = wv.reshape(MLA_KV_LORA, MLA_HEADS * MLA_VT_ROWS).T
    wa = w_out[:MLA_W].reshape(MLA_HEADS, MLA_V, D_MODEL)
    wa = jnp.pad(wa, ((0, 0), (0, LANES - MLA_V), (0, 0))).reshape(MLA_PAD_W, D_MODEL)
    wb, wc = w_out[MLA_W:MLA_W + ML_W], w_out[MLA_W + ML_W:]
    bf = lambda a: a.astype(BF16)
    return bf(w_p), bf(w_gates_t), bf(uq), bf(wk), bf(wv), bf(wa), bf(wb), bf(wc)


def kernel(x, mem, w_in, mla_q_norm, mla_w_uq, mla_kv_norm, mla_w_ukv, ml_conv_w, ml_conv_b, ml_b_i, ml_b_f,
           w_out, x_w_q, x_w_kv, x_w_o, w_gate_up, w_down, norm_mix_pre, norm_mix_post, norm_mem,
           norm_x_pre, norm_x_post, norm_ffn_pre, norm_ffn_post):
    batch, seq, _ = x.shape
    depth = w_in.shape[0]
    t_all = batch * seq
    assert seq % SWA_TILE == 0 and seq % TOK_TILE == 0 and seq % WIDE_TILE == 0
    assert WIDE_TILE % FLASH_KEY_TILE == 0
    assert SWA_TILE == SW_DILATIONS[-1] * SW_BLOCK
    cm, sm, cs, ss = _rope_tables(seq)
    row = lambda a: a.reshape(1, -1)
    colv = lambda a: a.reshape(-1, 1)
    xt = x.reshape(t_all, D_MODEL)
    mem2 = mem.reshape(batch * N_MEM, D_MODEL)
    w_p_all, w_gt_all, uq_all, wk_all, wv_all, wa_all, wb_all, wc_all = jax.vmap(_prep_layer)(
        w_in, mla_w_uq, mla_w_ukv, w_out)
    x_w_q, x_w_o, w_gate_up, w_down = (a.astype(BF16) for a in (x_w_q, x_w_o, w_gate_up, w_down))
    kv_all = _memkv(mem2, norm_mem.reshape(depth, 1, D_MODEL), x_w_kv.astype(BF16))
    for l in range(depth):
        w_p, w_gt, uq, wk, wv, wa, wb, wc = (a[l] for a in (w_p_all, w_gt_all, uq_all, wk_all, wv_all,
                                                            wa_all, wb_all, wc_all))
        q, k, v, mqk, mv, mo, gates_t, sq, sk, sv = _inproj(
            xt, row(norm_mix_pre[l]), w_p, w_gt, row(mla_q_norm[l]), uq, row(mla_kv_norm[l]), wk, wv,
            cm, sm, cs, ss, seq)
        b3 = lambda a: a.reshape(batch, seq, a.shape[-1])
        ya = _flash(q, k, v, batch, seq)
        yb = _mlstm(b3(mqk), mv, b3(mo), gates_t, ml_conv_w[l], row(ml_conv_b[l]), colv(ml_b_i[l]),
                    colv(ml_b_f[l])).reshape(t_all, ML_W)
        yc = _swa(sq, sk, sv, batch, seq)
        xt = _mixout_xattn(xt, ya, yb, yc, wa, wb, wc, row(norm_mix_post[l]),
                           row(norm_x_pre[l]), x_w_q[l], kv_all[l], x_w_o[l], row(norm_x_post[l]), seq)
        xt = _ffn(xt, row(norm_ffn_pre[l]), w_gate_up[l], w_down[l], row(norm_ffn_post[l]))
    return xt.reshape(batch, seq, D_MODEL)
```

```python
import functools

import jax
import jax.numpy as jnp
import numpy as np
from jax import lax
from jax.experimental import pallas as pl
from jax.experimental.pallas import tpu as pltpu

F32 = jnp.float32
BF16 = jnp.bfloat16

D_MODEL = 1024
N_MEM = 256
ROPE_THETA = 500000.0
EPS = 1e-6
NEG = -1e30
LOG2_E = 1.4426950408889634
LANES = 128

MLA_HEADS, MLA_NOPE, MLA_ROPE, MLA_V = 8, 64, 32, 64
MLA_Q_LORA, MLA_KV_LORA = 256, 128
ML_HEADS, ML_DK, ML_DV, ML_CONV, ML_CHUNK = 4, 64, 64, 4, 128
SW_HEADS, SW_HD = 4, 64
SW_ROT = SW_HD // 4
SW_DILATIONS = (1, 4, 16)
SW_BLOCK = 128
X_HEADS = 4
X_HD = D_MODEL // X_HEADS
D_FF = 2816

MLA_W = MLA_HEADS * MLA_V
ML_W = ML_HEADS * ML_DV
SW_W = SW_HEADS * SW_HD
MLA_PAD_W = MLA_HEADS * LANES

IN_SIZES = (MLA_Q_LORA, MLA_KV_LORA, MLA_ROPE, ML_HEADS * ML_DK, ML_HEADS * ML_DK, ML_W, ML_W,
            ML_HEADS, ML_HEADS, SW_W, SW_W, SW_W)
IN_OFFS = tuple(int(v) for v in np.cumsum((0,) + IN_SIZES))

P_CQ = 0
P_CKV = 256
P_KROPE = 384
P_MLQK = 512
P_MLO = 1024
P_SWQ = 1280
P_SWK = 1536
P_SWV = 1792
P_TOTAL = 2048
ML_VT_ROWS = 80
ML_T_ROWS = ML_HEADS * ML_VT_ROWS + 16

TOK_TILE = 512
WIDE_TILE = 1024
FLASH_TILE = 1024
FLASH_KEY_TILE = 256
MLA_VT_ROWS = 80
SWA_TILE = 2048
SWA_UNROLL = 8
FFN_SPLITS = (0, 1536, D_FF)
ML_CHUNKS_PER_STEP = 8
VMEM_LIMIT = 56 * 1024 * 1024


def _rms(x, g):
    return x * lax.rsqrt(jnp.mean(x * x, axis=-1, keepdims=True) + EPS) * g


def _dot(a, b):
    return jnp.dot(a, b, preferred_element_type=F32)


def _dot_nt(a, b):
    return lax.dot_general(a, b, (((1,), (1,)), ((), ())), preferred_element_type=F32)


def _params(*sem):
    return pltpu.CompilerParams(dimension_semantics=sem, vmem_limit_bytes=VMEM_LIMIT)


def _inproj_kernel(x_ref, g_ref, w_ref, wgt_ref, qg_ref, wuq_ref, kvg_ref, wk_ref, wvt_ref,
                   cm_ref, sm_ref, cs_ref, ss_ref,
                   q_ref, k_ref, vt_ref, mqk_ref, mv_ref, mo_ref, gr_ref, sq_ref, sk_ref, sv_ref):
    tm = x_ref.shape[0]
    h = _rms(x_ref[...], g_ref[...]).astype(BF16)

    def proj(a, b):
        return _dot(h, w_ref[:, a:b])

    lane = lax.broadcasted_iota(jnp.int32, (tm, LANES), 1)
    cm, sm = cm_ref[...], sm_ref[...]
    mla_first = lane < MLA_NOPE + MLA_ROPE // 2

    def rope_mla(t):
        rot = jnp.where(mla_first, pltpu.roll(t, LANES - MLA_ROPE // 2, 1), pltpu.roll(t, MLA_ROPE // 2, 1))
        return t * cm + rot * sm

    cs, ss = cs_ref[...], ss_ref[...]
    sw_first = (lane % SW_HD) < SW_ROT // 2

    def rope_sw(t):
        rot = jnp.where(sw_first, pltpu.roll(t, LANES - SW_ROT // 2, 1), pltpu.roll(t, SW_ROT // 2, 1))
        return t * cs + rot * ss

    cqn = _rms(proj(P_CQ, P_CQ + MLA_Q_LORA), qg_ref[...]).astype(BF16)
    q_scale = (MLA_NOPE + MLA_ROPE) ** -0.5 * LOG2_E
    for pair in range(MLA_HEADS // 2):
        two = _dot(cqn, wuq_ref[:, 2 * pair * LANES:2 * (pair + 1) * LANES])
        for sub in range(2):
            q_ref[2 * pair + sub] = (rope_mla(two[:, sub * LANES:(sub + 1) * LANES]) * q_scale).astype(BF16)
    ckv_kr = proj(P_CKV, P_KROPE + LANES)
    kvn = _rms(ckv_kr[:, :MLA_KV_LORA], kvg_ref[...]).astype(BF16)
    kpe = rope_mla(ckv_kr[:, MLA_KV_LORA:])
    one_col = (lax.broadcasted_iota(jnp.int32, (MLA_VT_ROWS, 1), 0) == MLA_V).astype(F32)
    for pair in range(MLA_HEADS // 2):
        two = _dot(kvn, wk_ref[:, 2 * pair * LANES:2 * (pair + 1) * LANES])
        for sub in range(2):
            k_ref[2 * pair + sub] = (two[:, sub * LANES:(sub + 1) * LANES] + kpe).astype(BF16)
    for hh in range(MLA_HEADS):
        rows = slice(hh * MLA_VT_ROWS, (hh + 1) * MLA_VT_ROWS)
        for u in range(tm // FLASH_KEY_TILE):
            toks = slice(u * FLASH_KEY_TILE, (u + 1) * FLASH_KEY_TILE)
            vt_ref[hh, u] = (_dot_nt(wvt_ref[rows, :], kvn[toks, :]) + one_col).astype(BF16)

    mqk_ref[...] = proj(P_MLQK, P_MLQK + 2 * ML_HEADS * ML_DK)
    mo_ref[...] = proj(P_MLO, P_MLO + ML_W)
    tr = _dot_nt(wgt_ref[...], h)
    n_v = ML_HEADS * ML_VT_ROWS
    v_row = lax.broadcasted_iota(jnp.int32, (n_v, 1), 0)
    mv_ref[...] = (tr[0:n_v, :] + ((v_row % ML_VT_ROWS) == ML_DV).astype(F32)).astype(BF16)
    gr_ref[...] = tr[n_v:n_v + 2 * ML_HEADS, :]

    sw_q, sw_k, sw_v = (proj(p0, p0 + SW_W) for p0 in (P_SWQ, P_SWK, P_SWV))
    for half in range(SW_W // LANES):
        sl = slice(half * LANES, (half + 1) * LANES)
        sq_ref[half] = rope_sw(sw_q[:, sl]) * (SW_HD ** -0.5 * LOG2_E)
        sk_ref[half] = rope_sw(sw_k[:, sl])
        sv_ref[half] = sw_v[:, sl]


def _inproj(x, g, w, wgt, qg, wuq, kvg, wk, wv, cm, sm, cs, ss, seq):
    t_all = x.shape[0]
    tm = WIDE_TILE
    pos_tiles = seq // tm
    tok = lambda w_: pl.BlockSpec((tm, w_), lambda i: (i, 0))
    full = lambda a: pl.BlockSpec(a.shape, lambda i: (0,) * a.ndim)
    pos = pl.BlockSpec((tm, LANES), lambda i: (i % pos_tiles, 0))
    slabs = pl.BlockSpec((SW_W // LANES, tm, LANES), lambda i: (0, i, 0))
    out_shapes = (
        jax.ShapeDtypeStruct((MLA_HEADS, t_all, LANES), BF16),
        jax.ShapeDtypeStruct((MLA_HEADS, t_all, LANES), BF16),
        jax.ShapeDtypeStruct((MLA_HEADS, t_all // FLASH_KEY_TILE, MLA_VT_ROWS, FLASH_KEY_TILE), BF16),
        jax.ShapeDtypeStruct((t_all, 2 * ML_HEADS * ML_DK), F32),
        jax.ShapeDtypeStruct((t_all // seq, ML_HEADS * ML_VT_ROWS, seq), BF16),
        jax.ShapeDtypeStruct((t_all, ML_W), F32),
        jax.ShapeDtypeStruct((t_all // seq, 2 * ML_HEADS, seq), F32),
        jax.ShapeDtypeStruct((SW_W // LANES, t_all, LANES), F32),
        jax.ShapeDtypeStruct((SW_W // LANES, t_all, LANES), F32),
        jax.ShapeDtypeStruct((SW_W // LANES, t_all, LANES), F32),
    )
    heads = pl.BlockSpec((MLA_HEADS, tm, LANES), lambda i: (0, i, 0))
    out_specs = (heads, heads,
                 pl.BlockSpec((MLA_HEADS, tm // FLASH_KEY_TILE, MLA_VT_ROWS, FLASH_KEY_TILE), lambda i: (0, i, 0, 0)),
                 tok(2 * ML_HEADS * ML_DK),
                 pl.BlockSpec((None, ML_HEADS * ML_VT_ROWS, tm), lambda i: (i // pos_tiles, 0, i % pos_tiles)),
                 tok(ML_W), pl.BlockSpec((None, 2 * ML_HEADS, tm), lambda i: (i // pos_tiles, 0, i % pos_tiles)),
                 slabs, slabs, slabs)
    return pl.pallas_call(
        _inproj_kernel,
        grid=(t_all // tm,),
        in_specs=[tok(D_MODEL), full(g), full(w), full(wgt), full(qg), full(wuq), full(kvg), full(wk), full(wv),
                  pos, pos, pos, pos],
        out_specs=out_specs,
        out_shape=out_shapes,
        compiler_params=_params("parallel"),
        name="inproj",
    )(x, g, w, wgt, qg, wuq, kvg, wk, wv, cm, sm, cs, ss)


def _flash_kernel(q_ref, k_ref, vt_ref, o_ref, m_scr, acc_scr, s_scr, cmax_scr, qt_scr):
    tq, tk = FLASH_TILE, FLASH_KEY_TILE
    per_q = tq // tk
    assert per_q == 4
    g = pl.program_id(2)
    key_l = lax.broadcasted_iota(jnp.int32, (tk, tk), 0)
    qry_l = lax.broadcasted_iota(jnp.int32, (tk, tk), 1)
    tail = [(None, None), (None, 0), (1, 2)]
    plain = [(None, None)] * 2

    def query_tile(i, half):
        n_tiles = (i + 1) * per_q
        rows = slice(half * tq, (half + 1) * tq)

        def start():
            qt_scr[...] = q_ref[rows, :].astype(F32).T.astype(BF16)
            m_scr[...] = jnp.full(m_scr.shape, NEG, F32)
            acc_scr[...] = jnp.zeros(acc_scr.shape, F32)

        def scores(j, buf, d):
            start_k = pl.multiple_of(j * tk, tk)
            c0 = 0 if d is None else d * tk
            st = _dot(k_ref[pl.ds(start_k, tk), :], qt_scr[:, c0:])
            if d is not None:
                tri = jnp.where(key_l <= qry_l, st[:, :tk], NEG)
                st = tri if st.shape[1] == tk else jnp.concatenate([tri, st[:, tk:]], axis=1)
            s_scr[buf, :, c0:] = st
            cmax_scr[buf, :, c0:] = jnp.max(st, axis=0, keepdims=True)

        def accumulate(j, buf, d):
            cols = slice(0 if d is None else d * tk, None)
            m_old = m_scr[:, cols]
            m_new = jnp.maximum(m_old, cmax_scr[buf, :, cols])
            p = jnp.exp2(s_scr[buf, :, cols] - m_new).astype(BF16)
            acc_scr[:, cols] = acc_scr[:, cols] * jnp.exp2(m_old - m_new) + _dot(vt_ref[j], p)
            m_scr[:, cols] = m_new

        def pairs(pp, offsets, held0):
            held = [held0, None]
            for c, (d_a, d_b) in enumerate(offsets):
                scores(2 * (pp + c) + 1, 1, d_a)
                held[1] = d_a
                accumulate(2 * (pp + c), 0, held[0])
                scores(2 * (pp + c) + 2, 0, d_b)
                held[0] = d_b
                accumulate(2 * (pp + c) + 1, 1, held[1])
            return held[0]

        def plain_loop(trips):
            def body(t, carry):
                pairs(4 * t, plain + plain, None)
                return carry
            lax.fori_loop(0, trips, body, 0)

        def finish(held0):
            scores(n_tiles - 1, 1, per_q - 1)
            accumulate(n_tiles - 2, 0, held0)
            accumulate(n_tiles - 1, 1, per_q - 1)
            acc = acc_scr[...]
            out_t = acc[0:MLA_V, :] / acc[MLA_V:MLA_V + 1, :]
            o_ref[rows, :] = jnp.concatenate([out_t, jnp.zeros_like(out_t)], axis=0).T.astype(BF16)

        return start, scores, pairs, plain_loop, finish

    a_start, a_scores, a_pairs, a_loop, a_finish = query_tile(2 * g, 0)
    b_start, b_scores, b_pairs, b_loop, b_finish = query_tile(2 * g + 1, 1)

    @pl.when(g == 0)
    def _():
        a_start()
        a_scores(0, 0, 0)
        a_finish(a_pairs(0, [(1, 2)], 0))
        b_start()
        b_scores(0, 0, None)

    @pl.when(g > 0)
    def _():
        a_start()
        a_scores(0, 0, None)
        a_loop(g - 1)
        a_finish(a_pairs(4 * (g - 1), plain + tail, None))
        b_start()
        b_scores(0, 0, None)

    b_loop(g)
    b_finish(b_pairs(4 * g, tail, None))


def _flash(q, k, vt, b, s):
    tq, tk = FLASH_TILE, FLASH_KEY_TILE
    assert s % (2 * tq) == 0 and tq % (2 * tk) == 0
    steps = s // (2 * tq)
    rows = pl.BlockSpec((None, 2 * tq, LANES), lambda bb, hh, g: (hh, bb * steps + g, 0))
    return pl.pallas_call(
        _flash_kernel,
        grid=(b, MLA_HEADS, steps),
        in_specs=[rows,
                  pl.BlockSpec((None, s, LANES), lambda bb, hh, g: (hh, bb, 0)),
                  pl.BlockSpec((None, s // tk, MLA_VT_ROWS, tk), lambda bb, hh, g: (hh, bb, 0, 0))],
        out_specs=rows,
        out_shape=jax.ShapeDtypeStruct((MLA_HEADS, b * s, LANES), BF16),
        scratch_shapes=[pltpu.VMEM((1, tq), F32), pltpu.VMEM((MLA_VT_ROWS, tq), F32),
                        pltpu.VMEM((2, tk, tq), F32), pltpu.VMEM((2, 1, tq), F32), pltpu.VMEM((LANES, tq), BF16)],
        compiler_params=_params("parallel", "parallel", "arbitrary"),
        name="mla_flash",
    )(q, k, vt)


def _log_sigmoid(z):
    return jnp.minimum(z, 0.0) - jnp.log1p(jnp.exp(-jnp.abs(z)))


def _mlstm_kernel(qkc_ref, qkp_ref, vt_ref, o_ref, g_ref, cw_ref, cb_ref, bi_ref, bf_ref,
                  y_ref, c_scr, m_scr):
    L = ML_CHUNK
    nb = qkc_ref.shape[0]
    c = pl.program_id(0)

    @pl.when(c == 0)
    def _():
        c_scr[...] = jnp.zeros(c_scr.shape, F32)
        m_scr[...] = jnp.full(m_scr.shape, NEG, F32)

    src = lax.broadcasted_iota(jnp.int32, (L, L), 0)
    tgt = lax.broadcasted_iota(jnp.int32, (L, L), 1)
    causal = src <= tgt
    pick = (lax.broadcasted_iota(jnp.int32, (L, ML_HEADS * L), 0)
            == lax.broadcasted_iota(jnp.int32, (L, ML_HEADS * L), 1) // L).astype(BF16)

    def chunk(sub):
        rows_t = slice(sub * L, (sub + 1) * L)
        streams = [(b, hh) for b in range(nb) for hh in range(ML_HEADS)]
        qk, b_all, li_all, x_col = [], [], [], []
        for b in range(nb):
            cur = qkc_ref[b, rows_t, :]
            prev = jnp.where(c > 0, qkp_ref[b], 0.0) if sub == 0 else qkc_ref[b, (sub - 1) * L:sub * L, :]
            row_w = lax.broadcasted_iota(jnp.int32, cur.shape, 0)
            conv = cb_ref[...] + cw_ref[ML_CONV - 1:ML_CONV, :] * cur
            for sh in range(1, ML_CONV):
                shifted = jnp.where(row_w >= sh, pltpu.roll(cur, sh, 0), pltpu.roll(prev, sh, 0))
                conv = conv + cw_ref[ML_CONV - 1 - sh:ML_CONV - sh, :] * shifted
            qk.append(conv * jax.nn.sigmoid(conv))
            gates = g_ref[b, :, rows_t]
            li = gates[0:ML_HEADS, :] + bi_ref[...]
            lf = _log_sigmoid(gates[ML_HEADS:, :] + bf_ref[...])
            lane_g = lax.broadcasted_iota(jnp.int32, lf.shape, 1)
            step = 1
            while step < L:
                lf = lf + jnp.where(lane_g >= step, pltpu.roll(lf, step, 1), 0.0)
                step *= 2
            li_all.append(li)
            b_all.append(lf)
            x = jnp.concatenate([li - lf, jnp.zeros((L - ML_HEADS, L), F32)], axis=0).T
            spread = jnp.zeros((L, ML_HEADS * L), F32)
            for _ in range(3):
                part = x.astype(BF16)
                spread = spread + _dot(part, pick)
                x = x - part.astype(F32)
            x_col.append(spread)

        st = []
        for n, (b, hh) in enumerate(streams):
            b_row = b_all[b][hh:hh + 1, :]
            li_row = li_all[b][hh:hh + 1, :]
            m_prev = m_scr[n:n + 1, 0:1]
            d_t = jnp.where(causal, x_col[b][:, hh * L:(hh + 1) * L] + b_row, NEG)
            inter = b_row + m_prev
            m_t = jnp.maximum(inter, jnp.max(d_t, axis=0, keepdims=True))
            b_last = b_row[:, L - 1:L]
            g = b_last - b_row + li_row
            m_new = jnp.maximum(b_last + m_prev, jnp.max(g, axis=1, keepdims=True))
            kf = qk[b][:, (ML_HEADS + hh) * ML_DK:(ML_HEADS + hh + 1) * ML_DK] * ML_DK ** -0.5
            vt = vt_ref[b, hh * ML_VT_ROWS:(hh + 1) * ML_VT_ROWS, rows_t]
            st.append(dict(
                qh=qk[b][:, hh * ML_DK:(hh + 1) * ML_DK].astype(BF16),
                kh=kf.astype(BF16), vt=vt,
                vw=(vt.astype(F32) * jnp.exp(g - m_new)).astype(BF16),
                c_t=c_scr[n],
                w_intra=jnp.exp(d_t - m_t), w_inter=jnp.exp(inter - m_t), floor=jnp.exp(-m_t),
                decay=jnp.exp(b_last + m_prev - m_new), m_new=m_new))

        for s_ in st:
            s_["sraw"] = _dot_nt(s_["kh"], s_["qh"])
            s_["cross"] = _dot_nt(s_["c_t"].astype(BF16), s_["qh"])
        for s_ in st:
            s_["s"] = s_["sraw"] * s_["w_intra"]
        for s_ in st:
            s_["intra"] = _dot(s_["vt"], s_["s"].astype(BF16))
            s_["kv"] = _dot(s_["vw"], s_["kh"])
        h_t = []
        for n, s_ in enumerate(st):
            num = s_["intra"][:ML_DV, :] + s_["w_inter"] * s_["cross"][:ML_DV, :]
            nq = jnp.sum(s_["s"], axis=0, keepdims=True) + s_["w_inter"] * s_["cross"][ML_DV:ML_DV + 1, :]
            h_t.append(num / jnp.maximum(jnp.abs(nq), s_["floor"]))
            c_scr[n] = s_["decay"] * s_["c_t"] + s_["kv"]
            m_scr[n:n + 1, :] = jnp.broadcast_to(s_["m_new"], (1, LANES))
        for b in range(nb):
            hout = jnp.concatenate(h_t[b * ML_HEADS:(b + 1) * ML_HEADS], axis=0).T
            y_ref[b, rows_t, :] = (jax.nn.sigmoid(o_ref[b, rows_t, :]) * hout).astype(BF16)

    for sub in range(qkc_ref.shape[1] // L):
        chunk(sub)


def _mlstm(qk_pre, vt, o_pre, gates_t, conv_w, conv_b, b_i, b_f):
    batch, seq, wqk = qk_pre.shape
    L = ML_CHUNK
    step = ML_CHUNKS_PER_STEP * L
    full = lambda a: pl.BlockSpec(a.shape, lambda c: (0,) * a.ndim)
    chunk = lambda w_: pl.BlockSpec((batch, step, w_), lambda c: (0, c, 0))
    lanes = lambda r: pl.BlockSpec((batch, r, step), lambda c: (0, 0, c))
    return pl.pallas_call(
        _mlstm_kernel,
        grid=(seq // step,),
        in_specs=[chunk(wqk),
                  pl.BlockSpec((batch, L, wqk), lambda c: (0, jnp.maximum(ML_CHUNKS_PER_STEP * c - 1, 0), 0)),
                  lanes(ML_HEADS * ML_VT_ROWS), chunk(ML_W), lanes(2 * ML_HEADS),
                  full(conv_w), full(conv_b), full(b_i), full(b_f)],
        out_specs=chunk(ML_W),
        out_shape=jax.ShapeDtypeStruct((batch, seq, ML_W), BF16),
        scratch_shapes=[pltpu.VMEM((batch * ML_HEADS, ML_VT_ROWS, ML_DK), F32),
                        pltpu.VMEM((batch * ML_HEADS, LANES), F32)],
        compiler_params=_params("arbitrary"),
        name="mlstm",
    )(qk_pre, qk_pre, vt, o_pre, gates_t, conv_w, conv_b, b_i, b_f)


def _swa_kernel(q_ref, k_ref, v_ref, y_ref, kk_scr, vv_scr, o_scr, l_scr):
    blk, tile = SW_BLOCK, SWA_TILE
    n_slab = SW_W // LANES
    n = pl.program_id(1)

    @pl.when(n == 0)
    def _():
        kk_scr[:, 0:tile, :] = jnp.zeros((n_slab, tile, LANES), F32)
        vv_scr[:, 0:tile, :] = jnp.zeros((n_slab, tile, LANES), F32)

    @pl.when(n > 0)
    def _():
        kk_scr[:, 0:tile, :] = kk_scr[:, tile:2 * tile, :]
        vv_scr[:, 0:tile, :] = vv_scr[:, tile:2 * tile, :]

    kk_scr[:, tile:2 * tile, :] = k_ref[...]
    vv_scr[:, tile:2 * tile, :] = v_ref[...]

    key = lax.broadcasted_iota(jnp.int32, (2 * blk, blk), 0)
    qry = lax.broadcasted_iota(jnp.int32, (2 * blk, blk), 1)
    dim =lax.broadcasted_iota(jnp.int32, (LANES, blk), 0)
    head_rows = [dim < SW_HD, dim >= SW_HD]
    ones_rows = jnp.ones((16, 2 * blk), BF16)

    for g, dil in enumerate(SW_DILATIONS):
        per_res = tile // (blk * dil)
        shift = per_res.bit_length() - 1

        def block_pair(it, carry, g=g, dil=dil, per_res=per_res, shift=shift):
            def rows(start):
                return pl.ds(start, blk, stride=dil) if dil > 1 else pl.ds(pl.multiple_of(start, blk), blk)

            work = []
            for u in range(SWA_UNROLL):
                bi = SWA_UNROLL * it + u
                r = lax.shift_right_logical(bi, shift)
                j = bi & (per_res - 1)
                q0 = r + j * (blk * dil)
                lowest = jnp.where((n == 0) & (j == 0), blk, 0)
                ok = (key >= jnp.maximum(qry, lowest)) & (key <= qry + blk)
                for sl in range(n_slab):
                    prev, cur = rows(tile + q0 - blk * dil), rows(tile + q0)
                    kb = jnp.concatenate([kk_scr[sl, prev, :], kk_scr[sl, cur, :]], axis=0)
                    vb = jnp.concatenate([vv_scr[sl, prev, :], vv_scr[sl, cur, :]], axis=0)
                    work.append(dict(q0=q0, sl=sl, ok=ok, qt=q_ref[sl, rows(q0), :].T,
                                     kb=kb.astype(BF16), vt=vb.T.astype(BF16)))
            for w in work:
                w["s"] = [_dot(w["kb"], jnp.where(head_rows[hh], w["qt"], 0.0).astype(BF16)) for hh in range(2)]
            for w in work:
                w["e"], w["cmax"] = [], []
                for hh in range(2):
                    s = jnp.where(w["ok"], w["s"][hh], NEG)
                    cmax = jnp.max(s, axis=0, keepdims=True)
                    w["e"].append(jnp.exp2(s - cmax).astype(BF16))
                    w["cmax"].append(cmax)
            for w in work:
                ot, lt = [], []
                for hh in range(2):
                    vt_ext = jnp.concatenate([w["vt"][hh * SW_HD:(hh + 1) * SW_HD, :], ones_rows], axis=0)
                    prod = _dot(vt_ext, w["e"][hh])
                    den = prod[SW_HD:SW_HD + 1, :]
                    ot.append(prod[:SW_HD, :] / den)
                    lt.append(jnp.broadcast_to(w["cmax"][hh] + jnp.log2(den), (SW_HD, blk)))
                o_scr[g, w["sl"], rows(w["q0"]), :] = jnp.concatenate(ot, axis=0).T
                l_scr[g, w["sl"], rows(w["q0"]), :] = jnp.concatenate(lt, axis=0).T
            return carry

        lax.fori_loop(0, tile // blk // SWA_UNROLL, block_pair, 0)

    for sl in range(n_slab):
        ls = [l_scr[g, sl] for g in range(len(SW_DILATIONS))]
        mx = functools.reduce(jnp.maximum, ls)
        ws = [jnp.exp2(l - mx) for l in ls]
        num = functools.reduce(lambda a, b: a + b, [w * o_scr[g, sl] for g, w in enumerate(ws)])
        y_ref[:, sl * LANES:(sl + 1) * LANES] = (num / functools.reduce(lambda a, b: a + b, ws)).astype(BF16)


def _swa(q, k, v, batch, seq):
    tile = SWA_TILE
    n_slab = SW_W // LANES
    per_batch = seq // tile
    spec = pl.BlockSpec((n_slab, tile, LANES), lambda bb, n: (0, bb * per_batch + n, 0))
    n_br = len(SW_DILATIONS)
    return pl.pallas_call(
        _swa_kernel,
        grid=(batch, per_batch),
        in_specs=[spec, spec, spec],
        out_specs=pl.BlockSpec((tile, SW_W), lambda bb, n: (bb * per_batch + n, 0)),
        out_shape=jax.ShapeDtypeStruct((batch * seq, SW_W), BF16),
        scratch_shapes=[pltpu.VMEM((n_slab, 2 * tile, LANES), F32), pltpu.VMEM((n_slab, 2 * tile, LANES), F32),
                        pltpu.VMEM((n_br, n_slab, tile, LANES), F32), pltpu.VMEM((n_br, n_slab, tile, LANES), F32)],
        compiler_params=_params("parallel", "arbitrary"),
        name="swa",
    )(q, k, v)


def _memkv_kernel(mem_ref, g_ref, w_ref, kv_ref):
    kv_ref[...] = _dot(_rms(mem_ref[...], g_ref[...]).astype(BF16), w_ref[...]).astype(BF16)


def _memkv(mem, g, w):
    n = mem.shape[0]
    depth = w.shape[0]
    return pl.pallas_call(
        _memkv_kernel,
        grid=(depth,),
        in_specs=[pl.BlockSpec(mem.shape, lambda l: (0, 0)),
                  pl.BlockSpec((None, 1, D_MODEL), lambda l: (l, 0, 0)),
                  pl.BlockSpec((None, D_MODEL, 2 * D_MODEL), lambda l: (l, 0, 0))],
        out_specs=pl.BlockSpec((None, n, 2 * D_MODEL), lambda l: (l, 0, 0)),
        out_shape=jax.ShapeDtypeStruct((depth, n, 2 * D_MODEL), BF16),
        compiler_params=_params("parallel"),
        name="memkv",
    )(mem, g, w)


def _mixout_xattn_kernel(x_ref, ya_ref, yb_ref, yc_ref, wa_ref, wb_ref, wc_ref, gmix_ref,
                         gpre_ref, wq_ref, kv_ref, wo_ref, gpost_ref, out_ref, o_scr):
    ya = jnp.concatenate([ya_ref[hh] for hh in range(MLA_HEADS)], axis=1)
    y = _dot(ya, wa_ref[...]) + _dot(yb_ref[...], wb_ref[...]) + _dot(yc_ref[...], wc_ref[...])
    x = x_ref[...] + _rms(y, gmix_ref[...])
    h = _rms(x, gpre_ref[...]).astype(BF16)
    q = (_dot(h, wq_ref[...]) * X_HD ** -0.5).astype(BF16)
    for hh in range(X_HEADS):
        hs = slice(hh * X_HD, (hh + 1) * X_HD)
        s = _dot_nt(q[:, hs], kv_ref[:, hs])
        e = jnp.exp(s - jnp.max(s, axis=1, keepdims=True))
        den = jnp.sum(e, axis=1, keepdims=True)
        o_scr[:, hs] = (_dot(e.astype(BF16), kv_ref[:, D_MODEL + hh * X_HD:D_MODEL + (hh + 1) * X_HD]) / den).astype(BF16)
    out_ref[...] = x + _rms(_dot(o_scr[...], wo_ref[...]), gpost_ref[...])


def _mixout_xattn(x, ya, yb, yc, wa, wb, wc, gmix, gpre, wq, kv, wo, gpost, seq):
    t_all = x.shape[0]
    tm = WIDE_TILE
    per_batch = seq // tm
    tok = lambda w_: pl.BlockSpec((tm, w_), lambda i: (i, 0))
    full = lambda a: pl.BlockSpec(a.shape, lambda i: (0,) * a.ndim)
    return pl.pallas_call(
        _mixout_xattn_kernel,
        grid=(t_all // tm,),
        in_specs=[tok(D_MODEL), pl.BlockSpec((MLA_HEADS, tm, LANES), lambda i: (0, i, 0)), tok(ML_W), tok(SW_W),
                  full(wa), full(wb), full(wc), full(gmix), full(gpre), full(wq),
                  pl.BlockSpec((N_MEM, 2 * D_MODEL), lambda i: (i // per_batch, 0)), full(wo), full(gpost)],
        out_specs=tok(D_MODEL),
        out_shape=jax.ShapeDtypeStruct((t_all, D_MODEL), F32),
        scratch_shapes=[pltpu.VMEM((tm, D_MODEL), BF16)],
        compiler_params=_params("parallel"),
        name="mixout_xattn",
    )(x, ya, yb, yc, wa, wb, wc, gmix, gpre, wq, kv, wo, gpost)


def _ffn_kernel(x_ref, gpre_ref, wgu_ref, wd_ref, gpost_ref, out_ref):
    x = x_ref[...]
    h = _rms(x, gpre_ref[...]).astype(BF16)
    y = None
    for lo, hi in zip(FFN_SPLITS[:-1], FFN_SPLITS[1:]):
        gate = _dot(h, wgu_ref[:, lo:hi])
        up = _dot(h, wgu_ref[:, D_FF + lo:D_FF + hi])
        part = _dot((gate * jax.nn.sigmoid(gate) * up).astype(BF16), wd_ref[lo:hi, :])
        y = part if y is None else y + part
    out_ref[...] = x + _rms(y, gpost_ref[...])


def _ffn(x, gpre, w_gate_up, w_down, gpost):
    t_all = x.shape[0]
    tm = TOK_TILE
    tok = pl.BlockSpec((tm, D_MODEL), lambda i: (i, 0))
    full = lambda a: pl.BlockSpec(a.shape, lambda i: (0,) * a.ndim)
    return pl.pallas_call(
        _ffn_kernel,
        grid=(t_all // tm,),
        in_specs=[tok, full(gpre), full(w_gate_up), full(w_down), full(gpost)],
        out_specs=tok,
        out_shape=jax.ShapeDtypeStruct((t_all, D_MODEL), F32),
        compiler_params=_params("parallel"),
        name="ffn",
    )(x, gpre, w_gate_up, w_down, gpost)


def _rope_tables(seq):
    pos = jnp.arange(seq, dtype=F32)[:, None]

    def cos_sin(d_rot):
        inv = ROPE_THETA ** (-jnp.arange(0, d_rot, 2, dtype=F32) / d_rot)
        ang = pos * inv[None, :]
        return jnp.cos(ang), jnp.sin(ang)

    c, s = cos_sin(MLA_ROPE)
    ones = jnp.ones((seq, MLA_NOPE), F32)
    tail = jnp.ones((seq, LANES - MLA_NOPE - MLA_ROPE), F32)
    cm = jnp.concatenate([ones, c, c, tail], axis=1)
    sm = jnp.concatenate([0 * ones, -s, s, 0 * tail], axis=1)
    c, s = cos_sin(SW_ROT)
    rest = jnp.ones((seq, SW_HD - SW_ROT), F32)
    cs = jnp.tile(jnp.concatenate([c, c, rest], axis=1), (1, LANES // SW_HD))
    ss = jnp.tile(jnp.concatenate([-s, s, 0 * rest], axis=1), (1, LANES // SW_HD))
    return cm, sm, cs, ss


def _prep_layer(w_in, w_uq, w_ukv, w_out):
    o = IN_OFFS
    col = lambda i: w_in[:, o[i]:o[i + 1]]
    z = lambda n: jnp.zeros((D_MODEL, n), F32)
    w_p = jnp.concatenate([
        col(0), col(1),
        z(MLA_NOPE), col(2), z(LANES - MLA_NOPE - MLA_ROPE),
        col(3), col(4), col(6), col(9), col(10), col(11)], axis=1)
    assert w_p.shape[1] == P_TOTAL
    mlv = jnp.pad(col(5).reshape(D_MODEL, ML_HEADS, ML_DV), ((0, 0), (0, 0), (0, ML_VT_ROWS - ML_DV)))
    w_gates_t = jnp.concatenate([mlv.reshape(D_MODEL, ML_HEADS * ML_VT_ROWS), col(7), col(8),
                                 z(ML_T_ROWS - ML_HEADS * ML_VT_ROWS - 2 * ML_HEADS)], axis=1).T
    uq = w_uq.reshape(MLA_Q_LORA, MLA_HEADS, MLA_NOPE + MLA_ROPE)
    uq = jnp.pad(uq, ((0, 0), (0, 0), (0, LANES - MLA_NOPE - MLA_ROPE))).reshape(MLA_Q_LORA, MLA_PAD_W)
    ukv = w_ukv.reshape(MLA_KV_LORA, MLA_HEADS, MLA_NOPE + MLA_V)
    pad_half = lambda a: jnp.pad(a, ((0, 0), (0, 0), (0, LANES - a.shape[2]))).reshape(MLA_KV_LORA, MLA_PAD_W)
    wk = pad_half(ukv[:, :, :MLA_NOPE])
    wv = jnp.pad(ukv[:, :, MLA_NOPE:], ((0, 0), (0, 0), (0, MLA_VT_ROWS - MLA_V)))
    wv = wv.reshape(MLA_KV_LORA, MLA_HEADS * MLA_VT_ROWS).T
    wa = w_out[:MLA_W].reshape(MLA_HEADS, MLA_V, D_MODEL)
    wa = jnp.pad(wa, ((0, 0), (0, LANES - MLA_V), (0, 0))).reshape(MLA_PAD_W, D_MODEL)
    wb, wc = w_out[MLA_W:MLA_W + ML_W], w_out[MLA_W + ML_W:]
    bf = lambda a: a.astype(BF16)
    return bf(w_p), bf(w_gates_t), bf(uq), bf(wk), bf(wv), bf(wa), bf(wb), bf(wc)


def kernel(x, mem, w_in, mla_q_norm, mla_w_uq, mla_kv_norm, mla_w_ukv, ml_conv_w, ml_conv_b, ml_b_i, ml_b_f,
           w_out, x_w_q, x_w_kv, x_w_o, w_gate_up, w_down, norm_mix_pre, norm_mix_post, norm_mem,
           norm_x_pre, norm_x_post, norm_ffn_pre, norm_ffn_post):
    batch, seq, _ = x.shape
    depth = w_in.shape[0]
    t_all = batch * seq
    assert seq % SWA_TILE == 0 and seq % TOK_TILE == 0 and seq % WIDE_TILE == 0
    assert WIDE_TILE % FLASH_KEY_TILE == 0
    assert SWA_TILE == SW_DILATIONS[-1] * SW_BLOCK
    cm, sm, cs, ss = _rope_tables(seq)
    row = lambda a: a.reshape(1, -1)
    colv = lambda a: a.reshape(-1, 1)
    xt = x.reshape(t_all, D_MODEL)
    mem2 = mem.reshape(batch * N_MEM, D_MODEL)
    w_p_all, w_gt_all, uq_all, wk_all, wv_all, wa_all, wb_all, wc_all = jax.vmap(_prep_layer)(
        w_in, mla_w_uq, mla_w_ukv, w_out)
    x_w_q, x_w_o, w_gate_up, w_down = (a.astype(BF16) for a in (x_w_q, x_w_o, w_gate_up, w_down))
    kv_all = _memkv(mem2, norm_mem.reshape(depth, 1, D_MODEL), x_w_kv.astype(BF16))
    for l in range(depth):
        w_p, w_gt, uq, wk, wv, wa, wb, wc = (a[l] for a in (w_p_all, w_gt_all, uq_all, wk_all, wv_all,
                                                            wa_all, wb_all, wc_all))
        q, k, v, mqk, mv, mo, gates_t, sq, sk, sv = _inproj(
            xt, row(norm_mix_pre[l]), w_p, w_gt, row(mla_q_norm[l]), uq, row(mla_kv_norm[l]), wk, wv,
            cm, sm, cs, ss, seq)
        b3 = lambda a: a.reshape(batch, seq, a.shape[-1])
        ya = _flash(q, k, v, batch, seq)
        yb = _mlstm(b3(mqk), mv, b3(mo), gates_t, ml_conv_w[l], row(ml_conv_b[l]), colv(ml_b_i[l]),
                    colv(ml_b_f[l])).reshape(t_all, ML_W)
        yc = _swa(sq, sk, sv, batch, seq)
        xt = _mixout_xattn(xt, ya, yb, yc, wa, wb, wc, row(norm_mix_post[l]),
                           row(norm_x_pre[l]), x_w_q[l], kv_all[l], x_w_o[l], row(norm_x_post[l]), seq)
        xt = _ffn(xt, row(norm_ffn_pre[l]), w_gate_up[l], w_down[l], row(norm_ffn_post[l]))
    return xt.reshape(batch, seq, D_MODEL)
```

```python
import functools

import jax
import jax.numpy as jnp
import numpy as np
from jax import lax
from jax.experimental import pallas as pl
from jax.experimental.pallas import tpu as pltpu

F32 = jnp.float32
BF16 = jnp.bfloat16

D_MODEL = 1024
N_MEM = 256
ROPE_THETA = 500000.0
EPS = 1e-6
NEG = -1e30
LOG2_E = 1.4426950408889634
LANES = 128

MLA_HEADS, MLA_NOPE, MLA_ROPE, MLA_V = 8, 64, 32, 64
MLA_Q_LORA, MLA_KV_LORA = 256, 128
ML_HEADS, ML_DK, ML_DV, ML_CONV, ML_CHUNK = 4, 64, 64, 4, 128
SW_HEADS, SW_HD = 4, 64
SW_ROT = SW_HD // 4
SW_DILATIONS = (1, 4, 16)
SW_BLOCK = 128
X_HEADS = 4
X_HD = D_MODEL // X_HEADS
D_FF = 2816

MLA_W = MLA_HEADS * MLA_V
ML_W = ML_HEADS * ML_DV
SW_W = SW_HEADS * SW_HD
MLA_PAD_W = MLA_HEADS * LANES

IN_SIZES = (MLA_Q_LORA, MLA_KV_LORA, MLA_ROPE, ML_HEADS * ML_DK, ML_HEADS * ML_DK, ML_W, ML_W,
            ML_HEADS, ML_HEADS, SW_W, SW_W, SW_W)
IN_OFFS = tuple(int(v) for v in np.cumsum((0,) + IN_SIZES))

P_CQ = 0
P_CKV = 256
P_KROPE = 384
P_MLQK = 512
P_MLO = 1024
P_SWQ = 1280
P_SWK = 1536
P_SWV = 1792
P_TOTAL = 2048
ML_VT_ROWS = 80
ML_T_ROWS = ML_HEADS * ML_VT_ROWS + 16

TOK_TILE = 512
WIDE_TILE = 1024
FLASH_TILE = 1024
FLASH_KEY_TILE = 256
MLA_VT_ROWS = 80
SWA_TILE = 2048
SWA_UNROLL = 8
FFN_SPLITS = (0, 1536, D_FF)
ML_CHUNKS_PER_STEP = 8
VMEM_LIMIT = 56 * 1024 * 1024


def _rms(x, g):
    return x * lax.rsqrt(jnp.mean(x * x, axis=-1, keepdims=True) + EPS) * g


def _dot(a, b):
    return jnp.dot(a, b, preferred_element_type=F32)


def _dot_nt(a, b):
    return lax.dot_general(a, b, (((1,), (1,)), ((), ())), preferred_element_type=F32)


def _params(*sem):
    return pltpu.CompilerParams(dimension_semantics=sem, vmem_limit_bytes=VMEM_LIMIT)


def _inproj_kernel(x_ref, g_ref, w_ref, wgt_ref, qg_ref, wuq_ref, kvg_ref, wk_ref, wvt_ref,
                   cm_ref, sm_ref, cs_ref, ss_ref,
                   q_ref, k_ref, vt_ref, mqk_ref, mv_ref, mo_ref, gr_ref, sq_ref, sk_ref, sv_ref):
    tm = x_ref.shape[0]
    h = _rms(x_ref[...], g_ref[...]).astype(BF16)

    def proj(a, b):
        return _dot(h, w_ref[:, a:b])

    lane = lax.broadcasted_iota(jnp.int32, (tm, LANES), 1)
    cm, sm = cm_ref[...], sm_ref[...]
    mla_first = lane < MLA_NOPE + MLA_ROPE // 2

    def rope_mla(t):
        rot = jnp.where(mla_first, pltpu.roll(t, LANES - MLA_ROPE // 2, 1), pltpu.roll(t, MLA_ROPE // 2, 1))
        return t * cm + rot * sm

    cs, ss = cs_ref[...], ss_ref[...]
    sw_first = (lane % SW_HD) < SW_ROT // 2

    def rope_sw(t):
        rot = jnp.where(sw_first, pltpu.roll(t, LANES - SW_ROT // 2, 1), pltpu.roll(t, SW_ROT // 2, 1))
        return t * cs + rot * ss

    cqn = _rms(proj(P_CQ, P_CQ + MLA_Q_LORA), qg_ref[...]).astype(BF16)
    q_scale = (MLA_NOPE + MLA_ROPE) ** -0.5 * LOG2_E
    for pair in range(MLA_HEADS // 2):
        two = _dot(cqn, wuq_ref[:, 2 * pair * LANES:2 * (pair + 1) * LANES])
        for sub in range(2):
            q_ref[2 * pair + sub] = (rope_mla(two[:, sub * LANES:(sub + 1) * LANES]) * q_scale).astype(BF16)
    ckv_kr = proj(P_CKV, P_KROPE + LANES)
    kvn = _rms(ckv_kr[:, :MLA_KV_LORA], kvg_ref[...]).astype(BF16)
    kpe = rope_mla(ckv_kr[:, MLA_KV_LORA:])
    one_col = (lax.broadcasted_iota(jnp.int32, (MLA_VT_ROWS, 1), 0) == MLA_V).astype(F32)
    for pair in range(MLA_HEADS // 2):
        two = _dot(kvn, wk_ref[:, 2 * pair * LANES:2 * (pair + 1) * LANES])
        for sub in range(2):
            k_ref[2 * pair + sub] = (two[:, sub * LANES:(sub + 1) * LANES] + kpe).astype(BF16)
    for hh in range(MLA_HEADS):
        rows = slice(hh * MLA_VT_ROWS, (hh + 1) * MLA_VT_ROWS)
        for u in range(tm // FLASH_KEY_TILE):
            toks = slice(u * FLASH_KEY_TILE, (u + 1) * FLASH_KEY_TILE)
            vt_ref[hh, u] = (_dot_nt(wvt_ref[rows, :], kvn[toks, :]) + one_col).astype(BF16)

    mqk_ref[...] = proj(P_MLQK, P_MLQK + 2 * ML_HEADS * ML_DK)
    mo_ref[...] = proj(P_MLO, P_MLO + ML_W)
    tr = _dot_nt(wgt_ref[...], h)
    n_v = ML_HEADS * ML_VT_ROWS
    v_row = lax.broadcasted_iota(jnp.int32, (n_v, 1), 0)
    mv_ref[...] = (tr[0:n_v, :] + ((v_row % ML_VT_ROWS) == ML_DV).astype(F32)).astype(BF16)
    gr_ref[...] = tr[n_v:n_v + 2 * ML_HEADS, :]

    sw_q, sw_k, sw_v = (proj(p0, p0 + SW_W) for p0 in (P_SWQ, P_SWK, P_SWV))
    for half in range(SW_W // LANES):
        sl = slice(half * LANES, (half + 1) * LANES)
        sq_ref[half] = rope_sw(sw_q[:, sl]) * (SW_HD ** -0.5 * LOG2_E)
        sk_ref[half] = rope_sw(sw_k[:, sl])
        sv_ref[half] = sw_v[:, sl]


def _inproj(x, g, w, wgt, qg, wuq, kvg, wk, wv, cm, sm, cs, ss, seq):
    t_all = x.shape[0]
    tm = WIDE_TILE
    pos_tiles = seq // tm
    tok = lambda w_: pl.BlockSpec((tm, w_), lambda i: (i, 0))
    full = lambda a: pl.BlockSpec(a.shape, lambda i: (0,) * a.ndim)
    pos = pl.BlockSpec((tm, LANES), lambda i: (i % pos_tiles, 0))
    slabs = pl.BlockSpec((SW_W // LANES, tm, LANES), lambda i: (0, i, 0))
    out_shapes = (
        jax.ShapeDtypeStruct((MLA_HEADS, t_all, LANES), BF16),
        jax.ShapeDtypeStruct((MLA_HEADS, t_all, LANES), BF16),
        jax.ShapeDtypeStruct((MLA_HEADS, t_all // FLASH_KEY_TILE, MLA_VT_ROWS, FLASH_KEY_TILE), BF16),
        jax.ShapeDtypeStruct((t_all, 2 * ML_HEADS * ML_DK), F32),
        jax.ShapeDtypeStruct((t_all // seq, ML_HEADS * ML_VT_ROWS, seq), BF16),
        jax.ShapeDtypeStruct((t_all, ML_W), F32),
        jax.ShapeDtypeStruct((t_all // seq, 2 * ML_HEADS, seq), F32),
        jax.ShapeDtypeStruct((SW_W // LANES, t_all, LANES), F32),
        jax.ShapeDtypeStruct((SW_W // LANES, t_all, LANES), F32),
        jax.ShapeDtypeStruct((SW_W // LANES, t_all, LANES), F32),
    )
    heads = pl.BlockSpec((MLA_HEADS, tm, LANES), lambda i: (0, i, 0))
    out_specs = (heads, heads,
                 pl.BlockSpec((MLA_HEADS, tm // FLASH_KEY_TILE, MLA_VT_ROWS, FLASH_KEY_TILE), lambda i: (0, i, 0, 0)),
                 tok(2 * ML_HEADS * ML_DK),
                 pl.BlockSpec((None, ML_HEADS * ML_VT_ROWS, tm), lambda i: (i // pos_tiles, 0, i % pos_tiles)),
                 tok(ML_W), pl.BlockSpec((None, 2 * ML_HEADS, tm), lambda i: (i // pos_tiles, 0, i % pos_tiles)),
                 slabs, slabs, slabs)
    return pl.pallas_call(
        _inproj_kernel,
        grid=(t_all // tm,),
        in_specs=[tok(D_MODEL), full(g), full(w), full(wgt), full(qg), full(wuq), full(kvg), full(wk), full(wv),
                  pos, pos, pos, pos],
        out_specs=out_specs,
        out_shape=out_shapes,
        compiler_params=_params("parallel"),
        name="inproj",
    )(x, g, w, wgt, qg, wuq, kvg, wk, wv, cm, sm, cs, ss)


def _flash_kernel(q_ref, k_ref, vt_ref, o_ref, m_scr, acc_scr, s_scr, cmax_scr, qt_scr):
    tq, tk = FLASH_TILE, FLASH_KEY_TILE
    per_q = tq // tk
    n_q = q_ref.shape[0] // tq
    assert per_q == 4 and n_q % 2 == 0
    key_l = lax.broadcasted_iota(jnp.int32, (tk, tk), 0)
    qry_l = lax.broadcasted_iota(jnp.int32, (tk, tk), 1)
    tail = [(None, None), (None, 0), (1, 2)]
    plain = [(None, None)] * 2

    def query_tile(i):
        n_tiles = (i + 1) * per_q
        rows = pl.ds(i * tq if isinstance(i, int) else pl.multiple_of(i * tq, tq), tq)

        def start():
            qt_scr[...] = q_ref[rows, :].astype(F32).T.astype(BF16)
            m_scr[...] = jnp.full(m_scr.shape, NEG, F32)
            acc_scr[...] = jnp.zeros(acc_scr.shape, F32)

        def scores(j, buf, d):
            start_k = pl.multiple_of(j * tk, tk)
            c0 = 0 if d is None else d * tk
            st = _dot(k_ref[pl.ds(start_k, tk), :], qt_scr[:, c0:])
            if d is not None:
                tri = jnp.where(key_l <= qry_l, st[:, :tk], NEG)
                st = tri if st.shape[1] == tk else jnp.concatenate([tri, st[:, tk:]], axis=1)
            s_scr[buf, :, c0:] = st
            cmax_scr[buf, :, c0:] = jnp.max(st, axis=0, keepdims=True)

        def accumulate(j, buf, d):
            cols = slice(0 if d is None else d * tk, None)
            m_old = m_scr[:, cols]
            m_new = jnp.maximum(m_old, cmax_scr[buf, :, cols])
            p = jnp.exp2(s_scr[buf, :, cols] - m_new).astype(BF16)
            acc_scr[:, cols] = acc_scr[:, cols] * jnp.exp2(m_old - m_new) + _dot(vt_ref[j], p)
            m_scr[:, cols] = m_new

        def pairs(pp, offsets, held0):
            held = [held0, None]
            for c, (d_a, d_b) in enumerate(offsets):
                scores(2 * (pp + c) + 1, 1, d_a)
                held[1] = d_a
                accumulate(2 * (pp + c), 0, held[0])
                scores(2 * (pp + c) + 2, 0, d_b)
                held[0] = d_b
                accumulate(2 * (pp + c) + 1, 1, held[1])
            return held[0]

        def plain_loop(trips):
            def body(t, carry):
                pairs(4 * t, plain + plain, None)
                return carry
            lax.fori_loop(0, trips, body, 0)

        def finish(held0):
            scores(n_tiles - 1, 1, per_q - 1)
            accumulate(n_tiles - 2, 0, held0)
            accumulate(n_tiles - 1, 1, per_q - 1)
            acc = acc_scr[...]
            out_t = acc[0:MLA_V, :] / acc[MLA_V:MLA_V + 1, :]
            o_ref[rows, :] = jnp.concatenate([out_t, jnp.zeros_like(out_t)], axis=0).T.astype(BF16)

        return start, scores, pairs, plain_loop, finish

    def fill(i):
        start, scores, _, _, _ = query_tile(i)
        start()
        scores(0, 0, None)

    def second_of_pair(g):
        _, _, pairs, loop, finish = query_tile(2 * g + 1)
        loop(g)
        finish(pairs(4 * g, tail, None))

    start, scores, pairs, _, finish = query_tile(0)
    start()
    scores(0, 0, 0)
    finish(pairs(0, [(1, 2)], 0))
    fill(1)
    second_of_pair(0)
    fill(min(2, n_q - 1))

    def pair_body(g, carry):
        _, _, pairs, loop, finish = query_tile(2 * g)
        loop(g - 1)
        finish(pairs(4 * (g - 1), plain + tail, None))
        fill(2 * g + 1)
        second_of_pair(g)
        fill(jnp.minimum(2 * g + 2, n_q - 1))
        return carry

    lax.fori_loop(1, n_q // 2, pair_body, 0)


def _flash(q, k, vt, b, s):
    tq, tk = FLASH_TILE, FLASH_KEY_TILE
    assert s % (2 * tq) == 0 and tq % (2 * tk) == 0
    rows = pl.BlockSpec((None, s, LANES), lambda bb, hh: (hh, bb, 0))
    return pl.pallas_call(
        _flash_kernel,
        grid=(b, MLA_HEADS),
        in_specs=[rows, rows, pl.BlockSpec((None, s // tk, MLA_VT_ROWS, tk), lambda bb, hh: (hh, bb, 0, 0))],
        out_specs=rows,
        out_shape=jax.ShapeDtypeStruct((MLA_HEADS, b * s, LANES), BF16),
        scratch_shapes=[pltpu.VMEM((1, tq), F32), pltpu.VMEM((MLA_VT_ROWS, tq), F32),
                        pltpu.VMEM((2, tk, tq), F32), pltpu.VMEM((2, 1, tq), F32), pltpu.VMEM((LANES, tq), BF16)],
        compiler_params=_params("parallel", "parallel"),
        name="mla_flash",
    )(q, k, vt)


def _log_sigmoid(z):
    return jnp.minimum(z, 0.0) - jnp.log1p(jnp.exp(-jnp.abs(z)))


def _mlstm_kernel(qkc_ref, qkp_ref, vt_ref, o_ref, g_ref, cw_ref, cb_ref, bi_ref, bf_ref,
                  y_ref, c_scr, m_scr):
    L = ML_CHUNK
    nb = qkc_ref.shape[0]
    c = pl.program_id(0)

    @pl.when(c == 0)
    def _():
        c_scr[...] = jnp.zeros(c_scr.shape, F32)
        m_scr[...] = jnp.full(m_scr.shape, NEG, F32)

    src = lax.broadcasted_iota(jnp.int32, (L, L), 0)
    tgt = lax.broadcasted_iota(jnp.int32, (L, L), 1)
    causal = src <= tgt
    pick = (lax.broadcasted_iota(jnp.int32, (L, ML_HEADS * L), 0)
            == lax.broadcasted_iota(jnp.int32, (L, ML_HEADS * L), 1) // L).astype(BF16)

    def chunk(sub):
        rows_t = slice(sub * L, (sub + 1) * L)
        streams = [(b, hh) for b in range(nb) for hh in range(ML_HEADS)]
        qk, b_all, li_all, x_col = [], [], [], []
        for b in range(nb):
            cur = qkc_ref[b, rows_t, :]
            prev = jnp.where(c > 0, qkp_ref[b], 0.0) if sub == 0 else qkc_ref[b, (sub - 1) * L:sub * L, :]
            row_w = lax.broadcasted_iota(jnp.int32, cur.shape, 0)
            conv = cb_ref[...] + cw_ref[ML_CONV - 1:ML_CONV, :] * cur
            for sh in range(1, ML_CONV):
                shifted = jnp.where(row_w >= sh, pltpu.roll(cur, sh, 0), pltpu.roll(prev, sh, 0))
                conv = conv + cw_ref[ML_CONV - 1 - sh:ML_CONV - sh, :] * shifted
            qk.append(conv * jax.nn.sigmoid(conv))
            gates = g_ref[b, :, rows_t]
            li = gates[0:ML_HEADS, :] + bi_ref[...]
            lf = _log_sigmoid(gates[ML_HEADS:, :] + bf_ref[...])
            lane_g = lax.broadcasted_iota(jnp.int32, lf.shape, 1)
            step = 1
            while step < L:
                lf = lf + jnp.where(lane_g >= step, pltpu.roll(lf, step, 1), 0.0)
                step *= 2
            li_all.append(li)
            b_all.append(lf)
            x = jnp.concatenate([li - lf, jnp.zeros((L - ML_HEADS, L), F32)], axis=0).T
            spread = jnp.zeros((L, ML_HEADS * L), F32)
            for _ in range(3):
                part = x.astype(BF16)
                spread = spread + _dot(part, pick)
                x = x - part.astype(F32)
            x_col.append(spread)

        st = []
        for n, (b, hh) in enumerate(streams):
            b_row = b_all[b][hh:hh + 1, :]
            li_row = li_all[b][hh:hh + 1, :]
            m_prev = m_scr[n:n + 1, 0:1]
            d_t = jnp.where(causal, x_col[b][:, hh * L:(hh + 1) * L] + b_row, NEG)
            inter = b_row + m_prev
            m_t = jnp.maximum(inter, jnp.max(d_t, axis=0, keepdims=True))
            b_last = b_row[:, L - 1:L]
            g = b_last - b_row + li_row
            m_new = jnp.maximum(b_last + m_prev, jnp.max(g, axis=1, keepdims=True))
            kf = qk[b][:, (ML_HEADS + hh) * ML_DK:(ML_HEADS + hh + 1) * ML_DK] * ML_DK ** -0.5
            vt = vt_ref[b, hh * ML_VT_ROWS:(hh + 1) * ML_VT_ROWS, rows_t]
            st.append(dict(
                qh=qk[b][:, hh * ML_DK:(hh + 1) * ML_DK].astype(BF16),
                kh=kf.astype(BF16), vt=vt,
                vw=(vt.astype(F32) * jnp.exp(g - m_new)).astype(BF16),
                c_t=c_scr[n],
                w_intra=jnp.exp(d_t - m_t), w_inter=jnp.exp(inter - m_t), floor=jnp.exp(-m_t),
                decay=jnp.exp(b_last + m_prev - m_new), m_new=m_new))

        for s_ in st:
            s_["sraw"] = _dot_nt(s_["kh"], s_["qh"])
            s_["cross"] = _dot_nt(s_["c_t"].astype(BF16), s_["qh"])
        for s_ in st:
            s_["s"] = s_["sraw"] * s_["w_intra"]
        for s_ in st:
            s_["intra"] = _dot(s_["vt"], s_["s"].astype(BF16))
            s_["kv"] = _dot(s_["vw"], s_["kh"])
        h_t = []
        for n, s_ in enumerate(st):
            num = s_["intra"][:ML_DV, :] + s_["w_inter"] * s_["cross"][:ML_DV, :]
            nq = jnp.sum(s_["s"], axis=0, keepdims=True) + s_["w_inter"] * s_["cross"][ML_DV:ML_DV + 1, :]
            h_t.append(num / jnp.maximum(jnp.abs(nq), s_["floor"]))
            c_scr[n] = s_["decay"] * s_["c_t"] + s_["kv"]
            m_scr[n:n + 1, :] = jnp.broadcast_to(s_["m_new"], (1, LANES))
        for b in range(nb):
            hout = jnp.concatenate(h_t[b * ML_HEADS:(b + 1) * ML_HEADS], axis=0).T
            y_ref[b, rows_t, :] = (jax.nn.sigmoid(o_ref[b, rows_t, :]) * hout).astype(BF16)

    for sub in range(qkc_ref.shape[1] // L):
        chunk(sub)


def _mlstm(qk_pre, vt, o_pre, gates_t, conv_w, conv_b, b_i, b_f):
    batch, seq, wqk = qk_pre.shape
    L = ML_CHUNK
    step = ML_CHUNKS_PER_STEP * L
    full = lambda a: pl.BlockSpec(a.shape, lambda c: (0,) * a.ndim)
    chunk = lambda w_: pl.BlockSpec((batch, step, w_), lambda c: (0, c, 0))
    lanes = lambda r: pl.BlockSpec((batch, r, step), lambda c: (0, 0, c))
    return pl.pallas_call(
        _mlstm_kernel,
        grid=(seq // step,),
        in_specs=[chunk(wqk),
                  pl.BlockSpec((batch, L, wqk), lambda c: (0, jnp.maximum(ML_CHUNKS_PER_STEP * c - 1, 0), 0)),
                  lanes(ML_HEADS * ML_VT_ROWS), chunk(ML_W), lanes(2 * ML_HEADS),
                  full(conv_w), full(conv_b), full(b_i), full(b_f)],
        out_specs=chunk(ML_W),
        out_shape=jax.ShapeDtypeStruct((batch, seq, ML_W), BF16),
        scratch_shapes=[pltpu.VMEM((batch * ML_HEADS, ML_VT_ROWS, ML_DK), F32),
                        pltpu.VMEM((batch * ML_HEADS, LANES), F32)],
        compiler_params=_params("arbitrary"),
        name="mlstm",
    )(qk_pre, qk_pre, vt, o_pre, gates_t, conv_w, conv_b, b_i, b_f)


def _swa_kernel(q_ref, k_ref, v_ref, y_ref, kk_scr, vv_scr, o_scr, l_scr):
    blk, tile = SW_BLOCK, SWA_TILE
    n_slab = SW_W // LANES
    n = pl.program_id(1)

    @pl.when(n == 0)
    def _():
        kk_scr[:, 0:tile, :] = jnp.zeros((n_slab, tile, LANES), F32)
        vv_scr[:, 0:tile, :] = jnp.zeros((n_slab, tile, LANES), F32)

    @pl.when(n > 0)
    def _():
        kk_scr[:, 0:tile, :] = kk_scr[:, tile:2 * tile, :]
        vv_scr[:, 0:tile, :] = vv_scr[:, tile:2 * tile, :]

    kk_scr[:, tile:2 * tile, :] = k_ref[...]
    vv_scr[:, tile:2 * tile, :] = v_ref[...]

    key = lax.broadcasted_iota(jnp.int32, (2 * blk, blk), 0)
    qry = lax.broadcasted_iota(jnp.int32, (2 * blk, blk), 1)
    dim =lax.broadcasted_iota(jnp.int32, (LANES, blk), 0)
    head_rows = [dim < SW_HD, dim >= SW_HD]
    ones_rows = jnp.ones((16, 2 * blk), BF16)

    for g, dil in enumerate(SW_DILATIONS):
        per_res = tile // (blk * dil)
        shift = per_res.bit_length() - 1

        def block_pair(it, carry, g=g, dil=dil, per_res=per_res, shift=shift):
            def rows(start):
                return pl.ds(start, blk, stride=dil) if dil > 1 else pl.ds(pl.multiple_of(start, blk), blk)

            work = []
            for u in range(SWA_UNROLL):
                bi = SWA_UNROLL * it + u
                r = lax.shift_right_logical(bi, shift)
                j = bi & (per_res - 1)
                q0 = r + j * (blk * dil)
                lowest = jnp.where((n == 0) & (j == 0), blk, 0)
                ok = (key >= jnp.maximum(qry, lowest)) & (key <= qry + blk)
                for sl in range(n_slab):
                    prev, cur = rows(tile + q0 - blk * dil), rows(tile + q0)
                    kb = jnp.concatenate([kk_scr[sl, prev, :], kk_scr[sl, cur, :]], axis=0)
                    vb = jnp.concatenate([vv_scr[sl, prev, :], vv_scr[sl, cur, :]], axis=0)
                    work.append(dict(q0=q0, sl=sl, ok=ok, qt=q_ref[sl, rows(q0), :].T,
                                     kb=kb.astype(BF16), vt=vb.T.astype(BF16)))
            for w in work:
                w["s"] = [_dot(w["kb"], jnp.where(head_rows[hh], w["qt"], 0.0).astype(BF16)) for hh in range(2)]
            for w in work:
                w["e"], w["cmax"] = [], []
                for hh in range(2):
                    s = jnp.where(w["ok"], w["s"][hh], NEG)
                    cmax = jnp.max(s, axis=0, keepdims=True)
                    w["e"].append(jnp.exp2(s - cmax).astype(BF16))
                    w["cmax"].append(cmax)
            for w in work:
                ot, lt = [], []
                for hh in range(2):
                    vt_ext = jnp.concatenate([w["vt"][hh * SW_HD:(hh + 1) * SW_HD, :], ones_rows], axis=0)
                    prod = _dot(vt_ext, w["e"][hh])
                    den = prod[SW_HD:SW_HD + 1, :]
                    ot.append(prod[:SW_HD, :] / den)
                    lt.append(jnp.broadcast_to(w["cmax"][hh] + jnp.log2(den), (SW_HD, blk)))
                o_scr[g, w["sl"], rows(w["q0"]), :] = jnp.concatenate(ot, axis=0).T
                l_scr[g, w["sl"], rows(w["q0"]), :] = jnp.concatenate(lt, axis=0).T
            return carry

        lax.fori_loop(0, tile // blk // SWA_UNROLL, block_pair, 0)

    for sl in range(n_slab):
        ls = [l_scr[g, sl] for g in range(len(SW_DILATIONS))]
        mx = functools.reduce(jnp.maximum, ls)
        ws = [jnp.exp2(l - mx) for l in ls]
        num = functools.reduce(lambda a, b: a + b, [w * o_scr[g, sl] for g, w in enumerate(ws)])
        y_ref[:, sl * LANES:(sl + 1) * LANES] = (num / functools.reduce(lambda a, b: a + b, ws)).astype(BF16)


def _swa(q, k, v, batch, seq):
    tile = SWA_TILE
    n_slab = SW_W // LANES
    per_batch = seq // tile
    spec = pl.BlockSpec((n_slab, tile, LANES), lambda bb, n: (0, bb * per_batch + n, 0))
    n_br = len(SW_DILATIONS)
    return pl.pallas_call(
        _swa_kernel,
        grid=(batch, per_batch),
        in_specs=[spec, spec, spec],
        out_specs=pl.BlockSpec((tile, SW_W), lambda bb, n: (bb * per_batch + n, 0)),
        out_shape=jax.ShapeDtypeStruct((batch * seq, SW_W), BF16),
        scratch_shapes=[pltpu.VMEM((n_slab, 2 * tile, LANES), F32), pltpu.VMEM((n_slab, 2 * tile, LANES), F32),
                        pltpu.VMEM((n_br, n_slab, tile, LANES), F32), pltpu.VMEM((n_br, n_slab, tile, LANES), F32)],
        compiler_params=_params("parallel", "arbitrary"),
        name="swa",
    )(q, k, v)


def _memkv_kernel(mem_ref, g_ref, w_ref, kv_ref):
    kv_ref[...] = _dot(_rms(mem_ref[...], g_ref[...]).astype(BF16), w_ref[...]).astype(BF16)


def _memkv(mem, g, w):
    n = mem.shape[0]
    depth = w.shape[0]
    return pl.pallas_call(
        _memkv_kernel,
        grid=(depth,),
        in_specs=[pl.BlockSpec(mem.shape, lambda l: (0, 0)),
                  pl.BlockSpec((None, 1, D_MODEL), lambda l: (l, 0, 0)),
                  pl.BlockSpec((None, D_MODEL, 2 * D_MODEL), lambda l: (l, 0, 0))],
        out_specs=pl.BlockSpec((None, n, 2 * D_MODEL), lambda l: (l, 0, 0)),
        out_shape=jax.ShapeDtypeStruct((depth, n, 2 * D_MODEL), BF16),
        compiler_params=_params("parallel"),
        name="memkv",
    )(mem, g, w)


def _mixout_xattn_kernel(x_ref, ya_ref, yb_ref, yc_ref, wa_ref, wb_ref, wc_ref, gmix_ref,
                         gpre_ref, wq_ref, kv_ref, wo_ref, gpost_ref, out_ref, o_scr):
    ya = jnp.concatenate([ya_ref[hh] for hh in range(MLA_HEADS)], axis=1)
    y = _dot(ya, wa_ref[...]) + _dot(yb_ref[...], wb_ref[...]) + _dot(yc_ref[...], wc_ref[...])
    x = x_ref[...] + _rms(y, gmix_ref[...])
    h = _rms(x, gpre_ref[...]).astype(BF16)
    q = (_dot(h, wq_ref[...]) * X_HD ** -0.5).astype(BF16)
    for hh in range(X_HEADS):
        hs = slice(hh * X_HD, (hh + 1) * X_HD)
        s = _dot_nt(q[:, hs], kv_ref[:, hs])
        e = jnp.exp(s - jnp.max(s, axis=1, keepdims=True))
        den = jnp.sum(e, axis=1, keepdims=True)
        o_scr[:, hs] = (_dot(e.astype(BF16), kv_ref[:, D_MODEL + hh * X_HD:D_MODEL + (hh + 1) * X_HD]) / den).astype(BF16)
    out_ref[...] = x + _rms(_dot(o_scr[...], wo_ref[...]), gpost_ref[...])


def _mixout_xattn(x, ya, yb, yc, wa, wb, wc, gmix, gpre, wq, kv, wo, gpost, seq):
    t_all = x.shape[0]
    tm = WIDE_TILE
    per_batch = seq // tm
    tok = lambda w_: pl.BlockSpec((tm, w_), lambda i: (i, 0))
    full = lambda a: pl.BlockSpec(a.shape, lambda i: (0,) * a.ndim)
    return pl.pallas_call(
        _mixout_xattn_kernel,
        grid=(t_all // tm,),
        in_specs=[tok(D_MODEL), pl.BlockSpec((MLA_HEADS, tm, LANES), lambda i: (0, i, 0)), tok(ML_W), tok(SW_W),
                  full(wa), full(wb), full(wc), full(gmix), full(gpre), full(wq),
                  pl.BlockSpec((N_MEM, 2 * D_MODEL), lambda i: (i // per_batch, 0)), full(wo), full(gpost)],
        out_specs=tok(D_MODEL),
        out_shape=jax.ShapeDtypeStruct((t_all, D_MODEL), F32),
        scratch_shapes=[pltpu.VMEM((tm, D_MODEL), BF16)],
        compiler_params=_params("parallel"),
        name="mixout_xattn",
    )(x, ya, yb, yc, wa, wb, wc, gmix, gpre, wq, kv, wo, gpost)


def _ffn_kernel(x_ref, gpre_ref, wgu_ref, wd_ref, gpost_ref, out_ref):
    x = x_ref[...]
    h = _rms(x, gpre_ref[...]).astype(BF16)
    y = None
    for lo, hi in zip(FFN_SPLITS[:-1], FFN_SPLITS[1:]):
        gate = _dot(h, wgu_ref[:, lo:hi])
        up = _dot(h, wgu_ref[:, D_FF + lo:D_FF + hi])
        part = _dot((gate * jax.nn.sigmoid(gate) * up).astype(BF16), wd_ref[lo:hi, :])
        y = part if y is None else y + part
    out_ref[...] = x + _rms(y, gpost_ref[...])


def _ffn(x, gpre, w_gate_up, w_down, gpost):
    t_all = x.shape[0]
    tm = TOK_TILE
    tok = pl.BlockSpec((tm, D_MODEL), lambda i: (i, 0))
    full = lambda a: pl.BlockSpec(a.shape, lambda i: (0,) * a.ndim)
    return pl.pallas_call(
        _ffn_kernel,
        grid=(t_all // tm,),
        in_specs=[tok, full(gpre), full(w_gate_up), full(w_down), full(gpost)],
        out_specs=tok,
        out_shape=jax.ShapeDtypeStruct((t_all, D_MODEL), F32),
        compiler_params=_params("parallel"),
        name="ffn",
    )(x, gpre, w_gate_up, w_down, gpost)


def _rope_tables(seq):
    pos = jnp.arange(seq, dtype=F32)[:, None]

    def cos_sin(d_rot):
        inv = ROPE_THETA ** (-jnp.arange(0, d_rot, 2, dtype=F32) / d_rot)
        ang = pos * inv[None, :]
        return jnp.cos(ang), jnp.sin(ang)

    c, s = cos_sin(MLA_ROPE)
    ones = jnp.ones((seq, MLA_NOPE), F32)
    tail = jnp.ones((seq, LANES - MLA_NOPE - MLA_ROPE), F32)
    cm = jnp.concatenate([ones, c, c, tail], axis=1)
    sm = jnp.concatenate([0 * ones, -s, s, 0 * tail], axis=1)
    c, s = cos_sin(SW_ROT)
    rest = jnp.ones((seq, SW_HD - SW_ROT), F32)
    cs = jnp.tile(jnp.concatenate([c, c, rest], axis=1), (1, LANES // SW_HD))
    ss = jnp.tile(jnp.concatenate([-s, s, 0 * rest], axis=1), (1, LANES // SW_HD))
    return cm, sm, cs, ss


def _prep_layer(w_in, w_uq, w_ukv, w_out):
    o = IN_OFFS
    col = lambda i: w_in[:, o[i]:o[i + 1]]
    z = lambda n: jnp.zeros((D_MODEL, n), F32)
    w_p = jnp.concatenate([
        col(0), col(1),
        z(MLA_NOPE), col(2), z(LANES - MLA_NOPE - MLA_ROPE),
        col(3), col(4), col(6), col(9), col(10), col(11)], axis=1)
    assert w_p.shape[1] == P_TOTAL
    mlv = jnp.pad(col(5).reshape(D_MODEL, ML_HEADS, ML_DV), ((0, 0), (0, 0), (0, ML_VT_ROWS - ML_DV)))
    w_gates_t = jnp.concatenate([mlv.reshape(D_MODEL, ML_HEADS * ML_VT_ROWS), col(7), col(8),
                                 z(ML_T_ROWS - ML_HEADS * ML_VT_ROWS - 2 * ML_HEADS)], axis=1).T
    uq = w_uq.reshape(MLA_Q_LORA, MLA_HEADS, MLA_NOPE + MLA_ROPE)
    uq = jnp.pad(uq, ((0, 0), (0, 0), (0, LANES - MLA_NOPE - MLA_ROPE))).reshape(MLA_Q_LORA, MLA_PAD_W)
    ukv = w_ukv.reshape(MLA_KV_LORA, MLA_HEADS, MLA_NOPE + MLA_V)
    pad_half = lambda a: jnp.pad(a, ((0, 0), (0, 0), (0, LANES - a.shape[2]))).reshape(MLA_KV_LORA, MLA_PAD_W)
    wk = pad_half(ukv[:, :, :MLA_NOPE])
    wv = jnp.pad(ukv[:, :, MLA_NOPE:], ((0, 0), (0, 0), (0, MLA_VT_ROWS - MLA_V)))
    wv = wv.reshape(MLA_KV_LORA, MLA_HEADS * MLA_VT_ROWS).T
    wa = w_out[:MLA_W].reshape(MLA_HEADS, MLA_V, D_MODEL)
    wa = jnp.pad(wa, ((0, 0), (0, LANES - MLA_V), (0, 0))).reshape(MLA_PAD_W, D_MODEL)
    wb, wc = w_out[MLA_W:MLA_W + ML_W], w_out[MLA_W + ML_W:]
    bf = lambda a: a.astype(BF16)
    return bf(w_p), bf(w_gates_t), bf(uq), bf(wk), bf(wv), bf(wa), bf(wb), bf(wc)


def kernel(x, mem, w_in, mla_q_norm, mla_w_uq, mla_kv_norm, mla_w_ukv, ml_conv_w, ml_conv_b, ml_b_i, ml_b_f,
           w_out, x_w_q, x_w_kv, x_w_o, w_gate_up, w_down, norm_mix_pre, norm_mix_post, norm_mem,
           norm_x_pre, norm_x_post, norm_ffn_pre, norm_ffn_post):
    batch, seq, _ = x.shape
    depth = w_in.shape[0]
    t_all = batch * seq
    assert seq % SWA_TILE == 0 and seq % TOK_TILE == 0 and seq % WIDE_TILE == 0
    assert WIDE_TILE % FLASH_KEY_TILE == 0
    assert SWA_TILE == SW_DILATIONS[-1] * SW_BLOCK
    cm, sm, cs, ss = _rope_tables(seq)
    row = lambda a: a.reshape(1, -1)
    colv = lambda a: a.reshape(-1, 1)
    xt = x.reshape(t_all, D_MODEL)
    mem2 = mem.reshape(batch * N_MEM, D_MODEL)
    w_p_all, w_gt_all, uq_all, wk_all, wv_all, wa_all, wb_all, wc_all = jax.vmap(_prep_layer)(
        w_in, mla_w_uq, mla_w_ukv, w_out)
    x_w_q, x_w_o, w_gate_up, w_down = (a.astype(BF16) for a in (x_w_q, x_w_o, w_gate_up, w_down))
    kv_all = _memkv(mem2, norm_mem.reshape(depth, 1, D_MODEL), x_w_kv.astype(BF16))
    for l in range(depth):
        w_p, w_gt, uq, wk, wv, wa, wb, wc = (a[l] for a in (w_p_all, w_gt_all, uq_all, wk_all, wv_all,
                                                            wa_all, wb_all, wc_all))
        q, k, v, mqk, mv, mo, gates_t, sq, sk, sv = _inproj(
            xt, row(norm_mix_pre[l]), w_p, w_gt, row(mla_q_norm[l]), uq, row(mla_kv_norm[l]), wk, wv,
            cm, sm, cs, ss, seq)
        b3 = lambda a: a.reshape(batch, seq, a.shape[-1])
        ya = _flash(q, k, v, batch, seq)
        yb = _mlstm(b3(mqk), mv, b3(mo), gates_t, ml_conv_w[l], row(ml_conv_b[l]), colv(ml_b_i[l]),
                    colv(ml_b_f[l])).reshape(t_all, ML_W)
        yc = _swa(sq, sk, sv, batch, seq)
        xt = _mixout_xattn(xt, ya, yb, yc, wa, wb, wc, row(norm_mix_post[l]),
                           row(norm_x_pre[l]), x_w_q[l], kv_all[l], x_w_o[l], row(norm_x_post[l]), seq)
        xt = _ffn(xt, row(norm_ffn_pre[l]), w_gate_up[l], w_down[l], row(norm_ffn_post[l]))
    return xt.reshape(batch, seq, D_MODEL)
```

```python
import functools

import jax
import jax.numpy as jnp
import numpy as np
from jax import lax
from jax.experimental import pallas as pl
from jax.experimental.pallas import tpu as pltpu

F32 = jnp.float32
BF16 = jnp.bfloat16

D_MODEL = 1024
N_MEM = 256
ROPE_THETA = 500000.0
EPS = 1e-6
NEG = -1e30
LOG2_E = 1.4426950408889634
LANES = 128

MLA_HEADS, MLA_NOPE, MLA_ROPE, MLA_V = 8, 64, 32, 64
MLA_Q_LORA, MLA_KV_LORA = 256, 128
ML_HEADS, ML_DK, ML_DV, ML_CONV, ML_CHUNK = 4, 64, 64, 4, 128
SW_HEADS, SW_HD = 4, 64
SW_ROT = SW_HD // 4
SW_DILATIONS = (1, 4, 16)
SW_BLOCK = 128
X_HEADS = 4
X_HD = D_MODEL // X_HEADS
D_FF = 2816

MLA_W = MLA_HEADS * MLA_V
ML_W = ML_HEADS * ML_DV
SW_W = SW_HEADS * SW_HD
MLA_PAD_W = MLA_HEADS * LANES

IN_SIZES = (MLA_Q_LORA, MLA_KV_LORA, MLA_ROPE, ML_HEADS * ML_DK, ML_HEADS * ML_DK, ML_W, ML_W,
            ML_HEADS, ML_HEADS, SW_W, SW_W, SW_W)
IN_OFFS = tuple(int(v) for v in np.cumsum((0,) + IN_SIZES))

P_CQ = 0
P_CKV = 256
P_KROPE = 384
P_MLQK = 512
P_MLO = 1024
P_SWQ = 1280
P_SWK = 1536
P_SWV = 1792
P_TOTAL = 2048
ML_VT_ROWS = 80
ML_T_ROWS = ML_HEADS * ML_VT_ROWS + 16

TOK_TILE = 1024
WIDE_TILE = 1024
FLASH_TILE = 1024
FLASH_KEY_TILE = 256
MLA_VT_ROWS = 80
SWA_TILE = 2048
SWA_UNROLL = 8
FFN_SPLITS = (0, 1536, D_FF)
ML_CHUNKS_PER_STEP = 8
VMEM_LIMIT = 56 * 1024 * 1024


def _rms(x, g):
    return x * lax.rsqrt(jnp.mean(x * x, axis=-1, keepdims=True) + EPS) * g


def _dot(a, b):
    return jnp.dot(a, b, preferred_element_type=F32)


def _dot_nt(a, b):
    return lax.dot_general(a, b, (((1,), (1,)), ((), ())), preferred_element_type=F32)


def _params(*sem):
    return pltpu.CompilerParams(dimension_semantics=sem, vmem_limit_bytes=VMEM_LIMIT)


def _inproj_kernel(x_ref, g_ref, w_ref, wgt_ref, qg_ref, wuq_ref, kvg_ref, wk_ref, wvt_ref,
                   cm_ref, sm_ref, cs_ref, ss_ref,
                   q_ref, k_ref, vt_ref, mqk_ref, mv_ref, mo_ref, gr_ref, sq_ref, sk_ref, sv_ref):
    tm = x_ref.shape[0]
    h = _rms(x_ref[...], g_ref[...]).astype(BF16)

    def proj(a, b):
        return _dot(h, w_ref[:, a:b])

    lane = lax.broadcasted_iota(jnp.int32, (tm, LANES), 1)
    cm, sm = cm_ref[...], sm_ref[...]
    mla_first = lane < MLA_NOPE + MLA_ROPE // 2

    def rope_mla(t):
        rot = jnp.where(mla_first, pltpu.roll(t, LANES - MLA_ROPE // 2, 1), pltpu.roll(t, MLA_ROPE // 2, 1))
        return t * cm + rot * sm

    cs, ss = cs_ref[...], ss_ref[...]
    sw_first = (lane % SW_HD) < SW_ROT // 2

    def rope_sw(t):
        rot = jnp.where(sw_first, pltpu.roll(t, LANES - SW_ROT // 2, 1), pltpu.roll(t, SW_ROT // 2, 1))
        return t * cs + rot * ss

    cqn = _rms(proj(P_CQ, P_CQ + MLA_Q_LORA), qg_ref[...]).astype(BF16)
    q_scale = (MLA_NOPE + MLA_ROPE) ** -0.5 * LOG2_E
    for pair in range(MLA_HEADS // 2):
        two = _dot(cqn, wuq_ref[:, 2 * pair * LANES:2 * (pair + 1) * LANES])
        for sub in range(2):
            q_ref[2 * pair + sub] = (rope_mla(two[:, sub * LANES:(sub + 1) * LANES]) * q_scale).astype(BF16)
    ckv_kr = proj(P_CKV, P_KROPE + LANES)
    kvn = _rms(ckv_kr[:, :MLA_KV_LORA], kvg_ref[...]).astype(BF16)
    kpe = rope_mla(ckv_kr[:, MLA_KV_LORA:])
    one_col = (lax.broadcasted_iota(jnp.int32, (MLA_VT_ROWS, 1), 0) == MLA_V).astype(F32)
    for pair in range(MLA_HEADS // 2):
        two = _dot(kvn, wk_ref[:, 2 * pair * LANES:2 * (pair + 1) * LANES])
        for sub in range(2):
            k_ref[2 * pair + sub] = (two[:, sub * LANES:(sub + 1) * LANES] + kpe).astype(BF16)
    for hh in range(MLA_HEADS):
        rows = slice(hh * MLA_VT_ROWS, (hh + 1) * MLA_VT_ROWS)
        for u in range(tm // FLASH_KEY_TILE):
            toks = slice(u * FLASH_KEY_TILE, (u + 1) * FLASH_KEY_TILE)
            vt_ref[hh, u] = (_dot_nt(wvt_ref[rows, :], kvn[toks, :]) + one_col).astype(BF16)

    mqk_ref[...] = proj(P_MLQK, P_MLQK + 2 * ML_HEADS * ML_DK)
    mo_ref[...] = proj(P_MLO, P_MLO + ML_W)
    tr = _dot_nt(wgt_ref[...], h)
    n_v = ML_HEADS * ML_VT_ROWS
    v_row = lax.broadcasted_iota(jnp.int32, (n_v, 1), 0)
    mv_ref[...] = (tr[0:n_v, :] + ((v_row % ML_VT_ROWS) == ML_DV).astype(F32)).astype(BF16)
    gr_ref[...] = tr[n_v:n_v + 2 * ML_HEADS, :]

    sw_q, sw_k, sw_v = (proj(p0, p0 + SW_W) for p0 in (P_SWQ, P_SWK, P_SWV))
    for half in range(SW_W // LANES):
        sl = slice(half * LANES, (half + 1) * LANES)
        sq_ref[half] = rope_sw(sw_q[:, sl]) * (SW_HD ** -0.5 * LOG2_E)
        sk_ref[half] = rope_sw(sw_k[:, sl])
        sv_ref[half] = sw_v[:, sl]


def _inproj(x, g, w, wgt, qg, wuq, kvg, wk, wv, cm, sm, cs, ss, seq):
    t_all = x.shape[0]
    tm = WIDE_TILE
    pos_tiles = seq // tm
    tok = lambda w_: pl.BlockSpec((tm, w_), lambda i: (i, 0))
    full = lambda a: pl.BlockSpec(a.shape, lambda i: (0,) * a.ndim)
    pos = pl.BlockSpec((tm, LANES), lambda i: (i % pos_tiles, 0))
    slabs = pl.BlockSpec((SW_W // LANES, tm, LANES), lambda i: (0, i, 0))
    out_shapes = (
        jax.ShapeDtypeStruct((MLA_HEADS, t_all, LANES), BF16),
        jax.ShapeDtypeStruct((MLA_HEADS, t_all, LANES), BF16),
        jax.ShapeDtypeStruct((MLA_HEADS, t_all // FLASH_KEY_TILE, MLA_VT_ROWS, FLASH_KEY_TILE), BF16),
        jax.ShapeDtypeStruct((t_all, 2 * ML_HEADS * ML_DK), F32),
        jax.ShapeDtypeStruct((t_all // seq, ML_HEADS * ML_VT_ROWS, seq), BF16),
        jax.ShapeDtypeStruct((t_all, ML_W), F32),
        jax.ShapeDtypeStruct((t_all // seq, 2 * ML_HEADS, seq), F32),
        jax.ShapeDtypeStruct((SW_W // LANES, t_all, LANES), F32),
        jax.ShapeDtypeStruct((SW_W // LANES, t_all, LANES), F32),
        jax.ShapeDtypeStruct((SW_W // LANES, t_all, LANES), F32),
    )
    heads = pl.BlockSpec((MLA_HEADS, tm, LANES), lambda i: (0, i, 0))
    out_specs = (heads, heads,
                 pl.BlockSpec((MLA_HEADS, tm // FLASH_KEY_TILE, MLA_VT_ROWS, FLASH_KEY_TILE), lambda i: (0, i, 0, 0)),
                 tok(2 * ML_HEADS * ML_DK),
                 pl.BlockSpec((None, ML_HEADS * ML_VT_ROWS, tm), lambda i: (i // pos_tiles, 0, i % pos_tiles)),
                 tok(ML_W), pl.BlockSpec((None, 2 * ML_HEADS, tm), lambda i: (i // pos_tiles, 0, i % pos_tiles)),
                 slabs, slabs, slabs)
    return pl.pallas_call(
        _inproj_kernel,
        grid=(t_all // tm,),
        in_specs=[tok(D_MODEL), full(g), full(w), full(wgt), full(qg), full(wuq), full(kvg), full(wk), full(wv),
                  pos, pos, pos, pos],
        out_specs=out_specs,
        out_shape=out_shapes,
        compiler_params=_params("parallel"),
        name="inproj",
    )(x, g, w, wgt, qg, wuq, kvg, wk, wv, cm, sm, cs, ss)


def _flash_kernel(q_ref, k_ref, vt_ref, o_ref, m_scr, acc_scr, s_scr, cmax_scr, qt_scr):
    tq, tk = FLASH_TILE, FLASH_KEY_TILE
    per_q = tq // tk
    n_q = q_ref.shape[0] // tq
    assert per_q == 4 and n_q % 2 == 0
    key_l = lax.broadcasted_iota(jnp.int32, (tk, tk), 0)
    qry_l = lax.broadcasted_iota(jnp.int32, (tk, tk), 1)
    tail = [(None, None), (None, 0), (1, 2)]
    plain = [(None, None)] * 2

    def query_tile(i):
        n_tiles = (i + 1) * per_q
        rows = pl.ds(i * tq if isinstance(i, int) else pl.multiple_of(i * tq, tq), tq)

        def start():
            qt_scr[...] = q_ref[rows, :].astype(F32).T.astype(BF16)
            m_scr[...] = jnp.full(m_scr.shape, NEG, F32)
            acc_scr[...] = jnp.zeros(acc_scr.shape, F32)

        def scores(j, buf, d):
            start_k = pl.multiple_of(j * tk, tk)
            c0 = 0 if d is None else d * tk
            st = _dot(k_ref[pl.ds(start_k, tk), :], qt_scr[:, c0:])
            if d is not None:
                tri = jnp.where(key_l <= qry_l, st[:, :tk], NEG)
                st = tri if st.shape[1] == tk else jnp.concatenate([tri, st[:, tk:]], axis=1)
            s_scr[buf, :, c0:] = st
            cmax_scr[buf, :, c0:] = jnp.max(st, axis=0, keepdims=True)

        def accumulate(j, buf, d):
            cols = slice(0 if d is None else d * tk, None)
            m_old = m_scr[:, cols]
            m_new = jnp.maximum(m_old, cmax_scr[buf, :, cols])
            p = jnp.exp2(s_scr[buf, :, cols] - m_new).astype(BF16)
            acc_scr[:, cols] = acc_scr[:, cols] * jnp.exp2(m_old - m_new) + _dot(vt_ref[j], p)
            m_scr[:, cols] = m_new

        def pairs(pp, offsets, held0):
            held = [held0, None]
            for c, (d_a, d_b) in enumerate(offsets):
                scores(2 * (pp + c) + 1, 1, d_a)
                held[1] = d_a
                accumulate(2 * (pp + c), 0, held[0])
                scores(2 * (pp + c) + 2, 0, d_b)
                held[0] = d_b
                accumulate(2 * (pp + c) + 1, 1, held[1])
            return held[0]

        def plain_loop(trips):
            def body(t, carry):
                pairs(8 * t, plain * 4, None)
                return carry
            doubles = lax.shift_right_logical(trips, 1)
            lax.fori_loop(0, doubles, body, 0)

            @pl.when((trips & 1) == 1)
            def _():
                pairs(8 * doubles, plain * 2, None)

        def finish(held0):
            scores(n_tiles - 1, 1, per_q - 1)
            accumulate(n_tiles - 2, 0, held0)
            accumulate(n_tiles - 1, 1, per_q - 1)
            acc = acc_scr[...]
            out_t = acc[0:MLA_V, :] / acc[MLA_V:MLA_V + 1, :]
            o_ref[rows, :] = jnp.concatenate([out_t, jnp.zeros_like(out_t)], axis=0).T.astype(BF16)

        return start, scores, pairs, plain_loop, finish

    def fill(i):
        start, scores, _, _, _ = query_tile(i)
        start()
        scores(0, 0, None)

    def second_of_pair(g):
        _, _, pairs, loop, finish = query_tile(2 * g + 1)
        loop(g)
        finish(pairs(4 * g, tail, None))

    start, scores, pairs, _, finish = query_tile(0)
    start()
    scores(0, 0, 0)
    finish(pairs(0, [(1, 2)], 0))
    fill(1)
    second_of_pair(0)
    fill(min(2, n_q - 1))

    def pair_body(g, carry):
        _, _, pairs, loop, finish = query_tile(2 * g)
        loop(g - 1)
        finish(pairs(4 * (g - 1), plain + tail, None))
        fill(2 * g + 1)
        second_of_pair(g)
        fill(jnp.minimum(2 * g + 2, n_q - 1))
        return carry

    lax.fori_loop(1, n_q // 2, pair_body, 0)


def _flash(q, k, vt, b, s):
    tq, tk = FLASH_TILE, FLASH_KEY_TILE
    assert s % (2 * tq) == 0 and tq % (2 * tk) == 0
    rows = pl.BlockSpec((None, s, LANES), lambda bb, hh: (hh, bb, 0))
    return pl.pallas_call(
        _flash_kernel,
        grid=(b, MLA_HEADS),
        in_specs=[rows, rows, pl.BlockSpec((None, s // tk, MLA_VT_ROWS, tk), lambda bb, hh: (hh, bb, 0, 0))],
        out_specs=rows,
        out_shape=jax.ShapeDtypeStruct((MLA_HEADS, b * s, LANES), BF16),
        scratch_shapes=[pltpu.VMEM((1, tq), F32), pltpu.VMEM((MLA_VT_ROWS, tq), F32),
                        pltpu.VMEM((2, tk, tq), F32), pltpu.VMEM((2, 1, tq), F32), pltpu.VMEM((LANES, tq), BF16)],
        compiler_params=_params("parallel", "parallel"),
        name="mla_flash",
    )(q, k, vt)


def _log_sigmoid(z):
    return jnp.minimum(z, 0.0) - jnp.log1p(jnp.exp(-jnp.abs(z)))


def _mlstm_kernel(qkc_ref, qkp_ref, vt_ref, o_ref, g_ref, cw_ref, cb_ref, bi_ref, bf_ref,
                  y_ref, c_scr, m_scr):
    L = ML_CHUNK
    nb = qkc_ref.shape[0]
    c = pl.program_id(0)

    @pl.when(c == 0)
    def _():
        c_scr[...] = jnp.zeros(c_scr.shape, F32)
        m_scr[...] = jnp.full(m_scr.shape, NEG, F32)

    src = lax.broadcasted_iota(jnp.int32, (L, L), 0)
    tgt = lax.broadcasted_iota(jnp.int32, (L, L), 1)
    causal = src <= tgt
    pick = (lax.broadcasted_iota(jnp.int32, (L, ML_HEADS * L), 0)
            == lax.broadcasted_iota(jnp.int32, (L, ML_HEADS * L), 1) // L).astype(BF16)

    def chunk(sub):
        rows_t = slice(sub * L, (sub + 1) * L)
        streams = [(b, hh) for b in range(nb) for hh in range(ML_HEADS)]
        qk, b_all, li_all, x_col = [], [], [], []
        for b in range(nb):
            cur = qkc_ref[b, rows_t, :]
            prev = jnp.where(c > 0, qkp_ref[b], 0.0) if sub == 0 else qkc_ref[b, (sub - 1) * L:sub * L, :]
            row_w = lax.broadcasted_iota(jnp.int32, cur.shape, 0)
            conv = cb_ref[...] + cw_ref[ML_CONV - 1:ML_CONV, :] * cur
            for sh in range(1, ML_CONV):
                shifted = jnp.where(row_w >= sh, pltpu.roll(cur, sh, 0), pltpu.roll(prev, sh, 0))
                conv = conv + cw_ref[ML_CONV - 1 - sh:ML_CONV - sh, :] * shifted
            qk.append(conv * jax.nn.sigmoid(conv))
            gates = g_ref[b, :, rows_t]
            li = gates[0:ML_HEADS, :] + bi_ref[...]
            lf = _log_sigmoid(gates[ML_HEADS:, :] + bf_ref[...])
            lane_g = lax.broadcasted_iota(jnp.int32, lf.shape, 1)
            step = 1
            while step < L:
                lf = lf + jnp.where(lane_g >= step, pltpu.roll(lf, step, 1), 0.0)
                step *= 2
            li_all.append(li)
            b_all.append(lf)
            x = jnp.concatenate([li - lf, jnp.zeros((L - ML_HEADS, L), F32)], axis=0).T
            spread = jnp.zeros((L, ML_HEADS * L), F32)
            for _ in range(3):
                part = x.astype(BF16)
                spread = spread + _dot(part, pick)
                x = x - part.astype(F32)
            x_col.append(spread)

        st = []
        for n, (b, hh) in enumerate(streams):
            b_row = b_all[b][hh:hh + 1, :]
            li_row = li_all[b][hh:hh + 1, :]
            m_prev = m_scr[n:n + 1, 0:1]
            d_t = jnp.where(causal, x_col[b][:, hh * L:(hh + 1) * L] + b_row, NEG)
            inter = b_row + m_prev
            m_t = jnp.maximum(inter, jnp.max(d_t, axis=0, keepdims=True))
            b_last = b_row[:, L - 1:L]
            g = b_last - b_row + li_row
            m_new = jnp.maximum(b_last + m_prev, jnp.max(g, axis=1, keepdims=True))
            kf = qk[b][:, (ML_HEADS + hh) * ML_DK:(ML_HEADS + hh + 1) * ML_DK] * ML_DK ** -0.5
            vt = vt_ref[b, hh * ML_VT_ROWS:(hh + 1) * ML_VT_ROWS, rows_t]
            st.append(dict(
                qh=qk[b][:, hh * ML_DK:(hh + 1) * ML_DK].astype(BF16),
                kh=kf.astype(BF16), vt=vt,
                vw=(vt.astype(F32) * jnp.exp(g - m_new)).astype(BF16),
                c_t=c_scr[n],
                w_intra=jnp.exp(d_t - m_t), w_inter=jnp.exp(inter - m_t), floor=jnp.exp(-m_t),
                decay=jnp.exp(b_last + m_prev - m_new), m_new=m_new))

        for s_ in st:
            s_["sraw"] = _dot_nt(s_["kh"], s_["qh"])
            s_["cross"] = _dot_nt(s_["c_t"].astype(BF16), s_["qh"])
        for s_ in st:
            s_["s"] = s_["sraw"] * s_["w_intra"]
        for s_ in st:
            s_["intra"] = _dot(s_["vt"], s_["s"].astype(BF16))
            s_["kv"] = _dot(s_["vw"], s_["kh"])
        h_t = []
        for n, s_ in enumerate(st):
            num = s_["intra"][:ML_DV, :] + s_["w_inter"] * s_["cross"][:ML_DV, :]
            nq = jnp.sum(s_["s"], axis=0, keepdims=True) + s_["w_inter"] * s_["cross"][ML_DV:ML_DV + 1, :]
            h_t.append(num / jnp.maximum(jnp.abs(nq), s_["floor"]))
            c_scr[n] = s_["decay"] * s_["c_t"] + s_["kv"]
            m_scr[n:n + 1, :] = jnp.broadcast_to(s_["m_new"], (1, LANES))
        for b in range(nb):
            hout = jnp.concatenate(h_t[b * ML_HEADS:(b + 1) * ML_HEADS], axis=0).T
            y_ref[b, rows_t, :] = (jax.nn.sigmoid(o_ref[b, rows_t, :]) * hout).astype(BF16)

    for sub in range(qkc_ref.shape[1] // L):
        chunk(sub)


def _mlstm(qk_pre, vt, o_pre, gates_t, conv_w, conv_b, b_i, b_f):
    batch, seq, wqk = qk_pre.shape
    L = ML_CHUNK
    step = ML_CHUNKS_PER_STEP * L
    full = lambda a: pl.BlockSpec(a.shape, lambda c: (0,) * a.ndim)
    chunk = lambda w_: pl.BlockSpec((batch, step, w_), lambda c: (0, c, 0))
    lanes = lambda r: pl.BlockSpec((batch, r, step), lambda c: (0, 0, c))
    return pl.pallas_call(
        _mlstm_kernel,
        grid=(seq // step,),
        in_specs=[chunk(wqk),
                  pl.BlockSpec((batch, L, wqk), lambda c: (0, jnp.maximum(ML_CHUNKS_PER_STEP * c - 1, 0), 0)),
                  lanes(ML_HEADS * ML_VT_ROWS), chunk(ML_W), lanes(2 * ML_HEADS),
                  full(conv_w), full(conv_b), full(b_i), full(b_f)],
        out_specs=chunk(ML_W),
        out_shape=jax.ShapeDtypeStruct((batch, seq, ML_W), BF16),
        scratch_shapes=[pltpu.VMEM((batch * ML_HEADS, ML_VT_ROWS, ML_DK), F32),
                        pltpu.VMEM((batch * ML_HEADS, LANES), F32)],
        compiler_params=_params("arbitrary"),
        name="mlstm",
    )(qk_pre, qk_pre, vt, o_pre, gates_t, conv_w, conv_b, b_i, b_f)


def _swa_kernel(q_ref, k_ref, v_ref, y_ref, kk_scr, vv_scr, o_scr, l_scr):
    blk, tile = SW_BLOCK, SWA_TILE
    n_slab = SW_W // LANES
    n = pl.program_id(1)

    @pl.when(n == 0)
    def _():
        kk_scr[:, 0:tile, :] = jnp.zeros((n_slab, tile, LANES), F32)
        vv_scr[:, 0:tile, :] = jnp.zeros((n_slab, tile, LANES), F32)

    @pl.when(n > 0)
    def _():
        kk_scr[:, 0:tile, :] = kk_scr[:, tile:2 * tile, :]
        vv_scr[:, 0:tile, :] = vv_scr[:, tile:2 * tile, :]

    kk_scr[:, tile:2 * tile, :] = k_ref[...]
    vv_scr[:, tile:2 * tile, :] = v_ref[...]

    key = lax.broadcasted_iota(jnp.int32, (2 * blk, blk), 0)
    qry = lax.broadcasted_iota(jnp.int32, (2 * blk, blk), 1)
    dim =lax.broadcasted_iota(jnp.int32, (LANES, blk), 0)
    head_rows = [dim < SW_HD, dim >= SW_HD]
    ones_rows = jnp.ones((16, 2 * blk), BF16)

    for g, dil in enumerate(SW_DILATIONS):
        per_res = tile // (blk * dil)
        shift = per_res.bit_length() - 1

        def block_pair(it, carry, g=g, dil=dil, per_res=per_res, shift=shift):
            def rows(start):
                return pl.ds(start, blk, stride=dil) if dil > 1 else pl.ds(pl.multiple_of(start, blk), blk)

            work = []
            for u in range(SWA_UNROLL):
                bi = SWA_UNROLL * it + u
                r = lax.shift_right_logical(bi, shift)
                j = bi & (per_res - 1)
                q0 = r + j * (blk * dil)
                lowest = jnp.where((n == 0) & (j == 0), blk, 0)
                ok = (key >= jnp.maximum(qry, lowest)) & (key <= qry + blk)
                for sl in range(n_slab):
                    prev, cur = rows(tile + q0 - blk * dil), rows(tile + q0)
                    kb = jnp.concatenate([kk_scr[sl, prev, :], kk_scr[sl, cur, :]], axis=0)
                    vb = jnp.concatenate([vv_scr[sl, prev, :], vv_scr[sl, cur, :]], axis=0)
                    work.append(dict(q0=q0, sl=sl, ok=ok, qt=q_ref[sl, rows(q0), :].T,
                                     kb=kb.astype(BF16), vt=vb.T.astype(BF16)))
            for w in work:
                w["s"] = [_dot(w["kb"], jnp.where(head_rows[hh], w["qt"], 0.0).astype(BF16)) for hh in range(2)]
            for w in work:
                w["e"], w["cmax"] = [], []
                for hh in range(2):
                    s = jnp.where(w["ok"], w["s"][hh], NEG)
                    cmax = jnp.max(s, axis=0, keepdims=True)
                    w["e"].append(jnp.exp2(s - cmax).astype(BF16))
                    w["cmax"].append(cmax)
            for w in work:
                ot, lt = [], []
                for hh in range(2):
                    vt_ext = jnp.concatenate([w["vt"][hh * SW_HD:(hh + 1) * SW_HD, :], ones_rows], axis=0)
                    prod = _dot(vt_ext, w["e"][hh])
                    den = prod[SW_HD:SW_HD + 1, :]
                    ot.append(prod[:SW_HD, :] / den)
                    lt.append(jnp.broadcast_to(w["cmax"][hh] + jnp.log2(den), (SW_HD, blk)))
                o_scr[g, w["sl"], rows(w["q0"]), :] = jnp.concatenate(ot, axis=0).T
                l_scr[g, w["sl"], rows(w["q0"]), :] = jnp.concatenate(lt, axis=0).T
            return carry

        lax.fori_loop(0, tile // blk // SWA_UNROLL, block_pair, 0)

    for sl in range(n_slab):
        ls = [l_scr[g, sl] for g in range(len(SW_DILATIONS))]
        mx = functools.reduce(jnp.maximum, ls)
        ws = [jnp.exp2(l - mx) for l in ls]
        num = functools.reduce(lambda a, b: a + b, [w * o_scr[g, sl] for g, w in enumerate(ws)])
        y_ref[:, sl * LANES:(sl + 1) * LANES] = (num / functools.reduce(lambda a, b: a + b, ws)).astype(BF16)


def _swa(q, k, v, batch, seq):
    tile = SWA_TILE
    n_slab = SW_W // LANES
    per_batch = seq // tile
    spec = pl.BlockSpec((n_slab, tile, LANES), lambda bb, n: (0, bb * per_batch + n, 0))
    n_br = len(SW_DILATIONS)
    return pl.pallas_call(
        _swa_kernel,
        grid=(batch, per_batch),
        in_specs=[spec, spec, spec],
        out_specs=pl.BlockSpec((tile, SW_W), lambda bb, n: (bb * per_batch + n, 0)),
        out_shape=jax.ShapeDtypeStruct((batch * seq, SW_W), BF16),
        scratch_shapes=[pltpu.VMEM((n_slab, 2 * tile, LANES), F32), pltpu.VMEM((n_slab, 2 * tile, LANES), F32),
                        pltpu.VMEM((n_br, n_slab, tile, LANES), F32), pltpu.VMEM((n_br, n_slab, tile, LANES), F32)],
        compiler_params=_params("parallel", "arbitrary"),
        name="swa",
    )(q, k, v)


def _memkv_kernel(mem_ref, g_ref, w_ref, kv_ref):
    kv_ref[...] = _dot(_rms(mem_ref[...], g_ref[...]).astype(BF16), w_ref[...]).astype(BF16)


def _memkv(mem, g, w):
    n = mem.shape[0]
    depth = w.shape[0]
    return pl.pallas_call(
        _memkv_kernel,
        grid=(depth,),
        in_specs=[pl.BlockSpec(mem.shape, lambda l: (0, 0)),
                  pl.BlockSpec((None, 1, D_MODEL), lambda l: (l, 0, 0)),
                  pl.BlockSpec((None, D_MODEL, 2 * D_MODEL), lambda l: (l, 0, 0))],
        out_specs=pl.BlockSpec((None, n, 2 * D_MODEL), lambda l: (l, 0, 0)),
        out_shape=jax.ShapeDtypeStruct((depth, n, 2 * D_MODEL), BF16),
        compiler_params=_params("parallel"),
        name="memkv",
    )(mem, g, w)


def _mixout_xattn_kernel(x_ref, ya_ref, yb_ref, yc_ref, wa_ref, wb_ref, wc_ref, gmix_ref,
                         gpre_ref, wq_ref, kv_ref, wo_ref, gpost_ref, out_ref, o_scr):
    ya = jnp.concatenate([ya_ref[hh] for hh in range(MLA_HEADS)], axis=1)
    y = _dot(ya, wa_ref[...]) + _dot(yb_ref[...], wb_ref[...]) + _dot(yc_ref[...], wc_ref[...])
    x = x_ref[...] + _rms(y, gmix_ref[...])
    h = _rms(x, gpre_ref[...]).astype(BF16)
    q = (_dot(h, wq_ref[...]) * X_HD ** -0.5).astype(BF16)
    for hh in range(X_HEADS):
        hs = slice(hh * X_HD, (hh + 1) * X_HD)
        s = _dot_nt(q[:, hs], kv_ref[:, hs])
        e = jnp.exp(s - jnp.max(s, axis=1, keepdims=True))
        den = jnp.sum(e, axis=1, keepdims=True)
        o_scr[:, hs] = (_dot(e.astype(BF16), kv_ref[:, D_MODEL + hh * X_HD:D_MODEL + (hh + 1) * X_HD]) / den).astype(BF16)
    out_ref[...] = x + _rms(_dot(o_scr[...], wo_ref[...]), gpost_ref[...])


def _mixout_xattn(x, ya, yb, yc, wa, wb, wc, gmix, gpre, wq, kv, wo, gpost, seq):
    t_all = x.shape[0]
    tm = WIDE_TILE
    per_batch = seq // tm
    tok = lambda w_: pl.BlockSpec((tm, w_), lambda i: (i, 0))
    full = lambda a: pl.BlockSpec(a.shape, lambda i: (0,) * a.ndim)
    return pl.pallas_call(
        _mixout_xattn_kernel,
        grid=(t_all // tm,),
        in_specs=[tok(D_MODEL), pl.BlockSpec((MLA_HEADS, tm, LANES), lambda i: (0, i, 0)), tok(ML_W), tok(SW_W),
                  full(wa), full(wb), full(wc), full(gmix), full(gpre), full(wq),
                  pl.BlockSpec((N_MEM, 2 * D_MODEL), lambda i: (i // per_batch, 0)), full(wo), full(gpost)],
        out_specs=tok(D_MODEL),
        out_shape=jax.ShapeDtypeStruct((t_all, D_MODEL), F32),
        scratch_shapes=[pltpu.VMEM((tm, D_MODEL), BF16)],
        compiler_params=_params("parallel"),
        name="mixout_xattn",
    )(x, ya, yb, yc, wa, wb, wc, gmix, gpre, wq, kv, wo, gpost)


def _ffn_kernel(x_ref, gpre_ref, wgu_ref, wd_ref, gpost_ref, out_ref):
    x = x_ref[...]
    h = _rms(x, gpre_ref[...]).astype(BF16)
    y = None
    for lo, hi in zip(FFN_SPLITS[:-1], FFN_SPLITS[1:]):
        gate = _dot(h, wgu_ref[:, lo:hi])
        up = _dot(h, wgu_ref[:, D_FF + lo:D_FF + hi])
        part = _dot((gate * jax.nn.sigmoid(gate) * up).astype(BF16), wd_ref[lo:hi, :])
        y = part if y is None else y + part
    out_ref[...] = x + _rms(y, gpost_ref[...])


def _ffn(x, gpre, w_gate_up, w_down, gpost):
    t_all = x.shape[0]
    tm = TOK_TILE
    tok = pl.BlockSpec((tm, D_MODEL), lambda i: (i, 0))
    full = lambda a: pl.BlockSpec(a.shape, lambda i: (0,) * a.ndim)
    return pl.pallas_call(
        _ffn_kernel,
        grid=(t_all // tm,),
        in_specs=[tok, full(gpre), full(w_gate_up), full(w_down), full(gpost)],
        out_specs=tok,
        out_shape=jax.ShapeDtypeStruct((t_all, D_MODEL), F32),
        compiler_params=_params("parallel"),
        name="ffn",
    )(x, gpre, w_gate_up, w_down, gpost)


def _rope_tables(seq):
    pos = jnp.arange(seq, dtype=F32)[:, None]

    def cos_sin(d_rot):
        inv = ROPE_THETA ** (-jnp.arange(0, d_rot, 2, dtype=F32) / d_rot)
        ang = pos * inv[None, :]
        return jnp.cos(ang), jnp.sin(ang)

    c, s = cos_sin(MLA_ROPE)
    ones = jnp.ones((seq, MLA_NOPE), F32)
    tail = jnp.ones((seq, LANES - MLA_NOPE - MLA_ROPE), F32)
    cm = jnp.concatenate([ones, c, c, tail], axis=1)
    sm = jnp.concatenate([0 * ones, -s, s, 0 * tail], axis=1)
    c, s = cos_sin(SW_ROT)
    rest = jnp.ones((seq, SW_HD - SW_ROT), F32)
    cs = jnp.tile(jnp.concatenate([c, c, rest], axis=1), (1, LANES // SW_HD))
    ss = jnp.tile(jnp.concatenate([-s, s, 0 * rest], axis=1), (1, LANES // SW_HD))
    return cm, sm, cs, ss


def _prep_layer(w_in, w_uq, w_ukv, w_out):
    o = IN_OFFS
    col = lambda i: w_in[:, o[i]:o[i + 1]]
    z = lambda n: jnp.zeros((D_MODEL, n), F32)
    w_p = jnp.concatenate([
        col(0), col(1),
        z(MLA_NOPE), col(2), z(LANES - MLA_NOPE - MLA_ROPE),
        col(3), col(4), col(6), col(9), col(10), col(11)], axis=1)
    assert w_p.shape[1] == P_TOTAL
    mlv = jnp.pad(col(5).reshape(D_MODEL, ML_HEADS, ML_DV), ((0, 0), (0, 0), (0, ML_VT_ROWS - ML_DV)))
    w_gates_t = jnp.concatenate([mlv.reshape(D_MODEL, ML_HEADS * ML_VT_ROWS), col(7), col(8),
                                 z(ML_T_ROWS - ML_HEADS * ML_VT_ROWS - 2 * ML_HEADS)], axis=1).T
    uq = w_uq.reshape(MLA_Q_LORA, MLA_HEADS, MLA_NOPE + MLA_ROPE)
    uq = jnp.pad(uq, ((0, 0), (0, 0), (0, LANES - MLA_NOPE - MLA_ROPE))).reshape(MLA_Q_LORA, MLA_PAD_W)
    ukv = w_ukv.reshape(MLA_KV_LORA, MLA_HEADS, MLA_NOPE + MLA_V)
    pad_half = lambda a: jnp.pad(a, ((0, 0), (0, 0), (0, LANES - a.shape[2]))).reshape(MLA_KV_LORA, MLA_PAD_W)
    wk = pad_half(ukv[:, :, :MLA_NOPE])
    wv = jnp.pad(ukv[:, :, MLA_NOPE:], ((0, 0), (0, 0), (0, MLA_VT_ROWS - MLA_V)))
    wv = wv.reshape(MLA_KV_LORA, MLA_HEADS * MLA_VT_ROWS).T
    wa = w_out[:MLA_W].reshape(MLA_HEADS, MLA_V, D_MODEL)
    wa = jnp.pad(wa, ((0, 0), (0, LANES - MLA_V), (0, 0))).reshape(MLA_PAD_W, D_MODEL)
    wb, wc = w_out[MLA_W:MLA_W + ML_W], w_out[MLA_W + ML_W:]
    bf = lambda a: a.astype(BF16)
    return bf(w_p), bf(w_gates_t), bf(uq), bf(wk), bf(wv), bf(wa), bf(wb), bf(wc)


def kernel(x, mem, w_in, mla_q_norm, mla_w_uq, mla_kv_norm, mla_w_ukv, ml_conv_w, ml_conv_b, ml_b_i, ml_b_f,
           w_out, x_w_q, x_w_kv, x_w_o, w_gate_up, w_down, norm_mix_pre, norm_mix_post, norm_mem,
           norm_x_pre, norm_x_post, norm_ffn_pre, norm_ffn_post):
    batch, seq, _ = x.shape
    depth = w_in.shape[0]
    t_all = batch * seq
    assert seq % SWA_TILE == 0 and seq % TOK_TILE == 0 and seq % WIDE_TILE == 0
    assert WIDE_TILE % FLASH_KEY_TILE == 0
    assert SWA_TILE == SW_DILATIONS[-1] * SW_BLOCK
    cm, sm, cs, ss = _rope_tables(seq)
    row = lambda a: a.reshape(1, -1)
    colv = lambda a: a.reshape(-1, 1)
    xt = x.reshape(t_all, D_MODEL)
    mem2 = mem.reshape(batch * N_MEM, D_MODEL)
    w_p_all, w_gt_all, uq_all, wk_all, wv_all, wa_all, wb_all, wc_all = jax.vmap(_prep_layer)(
        w_in, mla_w_uq, mla_w_ukv, w_out)
    x_w_q, x_w_o, w_gate_up, w_down = (a.astype(BF16) for a in (x_w_q, x_w_o, w_gate_up, w_down))
    kv_all = _memkv(mem2, norm_mem.reshape(depth, 1, D_MODEL), x_w_kv.astype(BF16))
    for l in range(depth):
        w_p, w_gt, uq, wk, wv, wa, wb, wc = (a[l] for a in (w_p_all, w_gt_all, uq_all, wk_all, wv_all,
                                                            wa_all, wb_all, wc_all))
        q, k, v, mqk, mv, mo, gates_t, sq, sk, sv = _inproj(
            xt, row(norm_mix_pre[l]), w_p, w_gt, row(mla_q_norm[l]), uq, row(mla_kv_norm[l]), wk, wv,
            cm, sm, cs, ss, seq)
        b3 = lambda a: a.reshape(batch, seq, a.shape[-1])
        ya = _flash(q, k, v, batch, seq)
        yb = _mlstm(b3(mqk), mv, b3(mo), gates_t, ml_conv_w[l], row(ml_conv_b[l]), colv(ml_b_i[l]),
                    colv(ml_b_f[l])).reshape(t_all, ML_W)
        yc = _swa(sq, sk, sv, batch, seq)
        xt = _mixout_xattn(xt, ya, yb, yc, wa, wb, wc, row(norm_mix_post[l]),
                           row(norm_x_pre[l]), x_w_q[l], kv_all[l], x_w_o[l], row(norm_x_post[l]), seq)
        xt = _ffn(xt, row(norm_ffn_pre[l]), w_gate_up[l], w_down[l], row(norm_ffn_post[l]))
    return xt.reshape(batch, seq, D_MODEL)
```

```python
import functools

import jax
import jax.numpy as jnp
import numpy as np
from jax import lax
from jax.experimental import pallas as pl
from jax.experimental.pallas import tpu as pltpu

F32 = jnp.float32
BF16 = jnp.bfloat16

D_MODEL = 1024
N_MEM = 256
ROPE_THETA = 500000.0
EPS = 1e-6
NEG = -1e30
LOG2_E = 1.4426950408889634
LANES = 128

MLA_HEADS, MLA_NOPE, MLA_ROPE, MLA_V = 8, 64, 32, 64
MLA_Q_LORA, MLA_KV_LORA = 256, 128
ML_HEADS, ML_DK, ML_DV, ML_CONV, ML_CHUNK = 4, 64, 64, 4, 128
SW_HEADS, SW_HD = 4, 64
SW_ROT = SW_HD // 4
SW_DILATIONS = (1, 4, 16)
SW_BLOCK = 128
X_HEADS = 4
X_HD = D_MODEL // X_HEADS
D_FF = 2816

MLA_W = MLA_HEADS * MLA_V
ML_W = ML_HEADS * ML_DV
SW_W = SW_HEADS * SW_HD
MLA_PAD_W = MLA_HEADS * LANES

IN_SIZES = (MLA_Q_LORA, MLA_KV_LORA, MLA_ROPE, ML_HEADS * ML_DK, ML_HEADS * ML_DK, ML_W, ML_W,
            ML_HEADS, ML_HEADS, SW_W, SW_W, SW_W)
IN_OFFS = tuple(int(v) for v in np.cumsum((0,) + IN_SIZES))

P_CQ = 0
P_CKV = 256
P_KROPE = 384
P_MLQK = 512
P_MLO = 1024
P_SWQ = 1280
P_SWK = 1536
P_SWV = 1792
P_TOTAL = 2048
ML_VT_ROWS = 80
ML_T_ROWS = ML_HEADS * ML_VT_ROWS + 16

TOK_TILE = 1024
WIDE_TILE = 1024
FLASH_TILE = 1024
FLASH_KEY_TILE = 256
MLA_VT_ROWS = 80
SWA_TILE = 2048
SWA_UNROLL = 8
FFN_SPLITS = (0, 1536, D_FF)
ML_CHUNKS_PER_STEP = 8
VMEM_LIMIT = 56 * 1024 * 1024


def _rms(x, g):
    return x * lax.rsqrt(jnp.mean(x * x, axis=-1, keepdims=True) + EPS) * g


def _dot(a, b):
    return jnp.dot(a, b, preferred_element_type=F32)


def _dot_nt(a, b):
    return lax.dot_general(a, b, (((1,), (1,)), ((), ())), preferred_element_type=F32)


def _params(*sem):
    return pltpu.CompilerParams(dimension_semantics=sem, vmem_limit_bytes=VMEM_LIMIT)


def _inproj_kernel(x_ref, g_ref, w_ref, wgt_ref, qg_ref, wuq_ref, kvg_ref, wk_ref, wvt_ref,
                   cm_ref, sm_ref, cs_ref, ss_ref,
                   q_ref, k_ref, vt_ref, mqk_ref, mv_ref, mo_ref, gr_ref, sq_ref, sk_ref, sv_ref):
    tm = x_ref.shape[0]
    h = _rms(x_ref[...], g_ref[...]).astype(BF16)

    def proj(a, b):
        return _dot(h, w_ref[:, a:b])

    lane = lax.broadcasted_iota(jnp.int32, (tm, LANES), 1)
    cm, sm = cm_ref[...], sm_ref[...]
    mla_first = lane < MLA_NOPE + MLA_ROPE // 2

    def rope_mla(t):
        rot = jnp.where(mla_first, pltpu.roll(t, LANES - MLA_ROPE // 2, 1), pltpu.roll(t, MLA_ROPE // 2, 1))
        return t * cm + rot * sm

    cs, ss = cs_ref[...], ss_ref[...]
    sw_first = (lane % SW_HD) < SW_ROT // 2

    def rope_sw(t):
        rot = jnp.where(sw_first, pltpu.roll(t, LANES - SW_ROT // 2, 1), pltpu.roll(t, SW_ROT // 2, 1))
        return t * cs + rot * ss

    cqn = _rms(proj(P_CQ, P_CQ + MLA_Q_LORA), qg_ref[...]).astype(BF16)
    q_scale = (MLA_NOPE + MLA_ROPE) ** -0.5 * LOG2_E
    for pair in range(MLA_HEADS // 2):
        two = _dot(cqn, wuq_ref[:, 2 * pair * LANES:2 * (pair + 1) * LANES])
        for sub in range(2):
            q_ref[2 * pair + sub] = (rope_mla(two[:, sub * LANES:(sub + 1) * LANES]) * q_scale).astype(BF16)
    ckv_kr = proj(P_CKV, P_KROPE + LANES)
    kvn = _rms(ckv_kr[:, :MLA_KV_LORA], kvg_ref[...]).astype(BF16)
    kpe = rope_mla(ckv_kr[:, MLA_KV_LORA:])
    one_col = (lax.broadcasted_iota(jnp.int32, (MLA_VT_ROWS, 1), 0) == MLA_V).astype(F32)
    for pair in range(MLA_HEADS // 2):
        two = _dot(kvn, wk_ref[:, 2 * pair * LANES:2 * (pair + 1) * LANES])
        for sub in range(2):
            k_ref[2 * pair + sub] = (two[:, sub * LANES:(sub + 1) * LANES] + kpe).astype(BF16)
    for hh in range(MLA_HEADS):
        rows = slice(hh * MLA_VT_ROWS, (hh + 1) * MLA_VT_ROWS)
        for u in range(tm // FLASH_KEY_TILE):
            toks = slice(u * FLASH_KEY_TILE, (u + 1) * FLASH_KEY_TILE)
            vt_ref[hh, u] = (_dot_nt(wvt_ref[rows, :], kvn[toks, :]) + one_col).astype(BF16)

    mqk_ref[...] = proj(P_MLQK, P_MLQK + 2 * ML_HEADS * ML_DK)
    mo_ref[...] = proj(P_MLO, P_MLO + ML_W)
    tr = _dot_nt(wgt_ref[...], h)
    n_v = ML_HEADS * ML_VT_ROWS
    v_row = lax.broadcasted_iota(jnp.int32, (n_v, 1), 0)
    mv_ref[...] = (tr[0:n_v, :] + ((v_row % ML_VT_ROWS) == ML_DV).astype(F32)).astype(BF16)
    gr_ref[...] = tr[n_v:n_v + 2 * ML_HEADS, :]

    sw_q, sw_k, sw_v = (proj(p0, p0 + SW_W) for p0 in (P_SWQ, P_SWK, P_SWV))
    for half in range(SW_W // LANES):
        sl = slice(half * LANES, (half + 1) * LANES)
        sq_ref[half] = rope_sw(sw_q[:, sl]) * (SW_HD ** -0.5 * LOG2_E)
        sk_ref[half] = rope_sw(sw_k[:, sl])
        sv_ref[half] = sw_v[:, sl]


def _inproj(x, g, w, wgt, qg, wuq, kvg, wk, wv, cm, sm, cs, ss, seq):
    t_all = x.shape[0]
    tm = WIDE_TILE
    pos_tiles = seq // tm
    tok = lambda w_: pl.BlockSpec((tm, w_), lambda i: (i, 0))
    full = lambda a: pl.BlockSpec(a.shape, lambda i: (0,) * a.ndim)
    pos = pl.BlockSpec((tm, LANES), lambda i: (i % pos_tiles, 0))
    slabs = pl.BlockSpec((SW_W // LANES, tm, LANES), lambda i: (0, i, 0))
    out_shapes = (
        jax.ShapeDtypeStruct((MLA_HEADS, t_all, LANES), BF16),
        jax.ShapeDtypeStruct((MLA_HEADS, t_all, LANES), BF16),
        jax.ShapeDtypeStruct((MLA_HEADS, t_all // FLASH_KEY_TILE, MLA_VT_ROWS, FLASH_KEY_TILE), BF16),
        jax.ShapeDtypeStruct((t_all, 2 * ML_HEADS * ML_DK), F32),
        jax.ShapeDtypeStruct((t_all // seq, ML_HEADS * ML_VT_ROWS, seq), BF16),
        jax.ShapeDtypeStruct((t_all, ML_W), F32),
        jax.ShapeDtypeStruct((t_all // seq, 2 * ML_HEADS, seq), F32),
        jax.ShapeDtypeStruct((SW_W // LANES, t_all, LANES), F32),
        jax.ShapeDtypeStruct((SW_W // LANES, t_all, LANES), F32),
        jax.ShapeDtypeStruct((SW_W // LANES, t_all, LANES), F32),
    )
    heads = pl.BlockSpec((MLA_HEADS, tm, LANES), lambda i: (0, i, 0))
    out_specs = (heads, heads,
                 pl.BlockSpec((MLA_HEADS, tm // FLASH_KEY_TILE, MLA_VT_ROWS, FLASH_KEY_TILE), lambda i: (0, i, 0, 0)),
                 tok(2 * ML_HEADS * ML_DK),
                 pl.BlockSpec((None, ML_HEADS * ML_VT_ROWS, tm), lambda i: (i // pos_tiles, 0, i % pos_tiles)),
                 tok(ML_W), pl.BlockSpec((None, 2 * ML_HEADS, tm), lambda i: (i // pos_tiles, 0, i % pos_tiles)),
                 slabs, slabs, slabs)
    return pl.pallas_call(
        _inproj_kernel,
        grid=(t_all // tm,),
        in_specs=[tok(D_MODEL), full(g), full(w), full(wgt), full(qg), full(wuq), full(kvg), full(wk), full(wv),
                  pos, pos, pos, pos],
        out_specs=out_specs,
        out_shape=out_shapes,
        compiler_params=_params("parallel"),
        name="inproj",
    )(x, g, w, wgt, qg, wuq, kvg, wk, wv, cm, sm, cs, ss)


def _flash_kernel(q_ref, k_ref, vt_ref, o_ref, m_scr, acc_scr, s_scr, cmax_scr, qt_scr):
    tq, tk = FLASH_TILE, FLASH_KEY_TILE
    per_q = tq // tk
    n_q = q_ref.shape[0] // tq
    assert per_q == 4 and n_q % 2 == 0
    key_l = lax.broadcasted_iota(jnp.int32, (tk, tk), 0)
    qry_l = lax.broadcasted_iota(jnp.int32, (tk, tk), 1)
    tail = [(None, None), (None, 0), (1, 2)]
    plain = [(None, None)] * 2

    def query_tile(i):
        n_tiles = (i + 1) * per_q
        rows = pl.ds(i * tq if isinstance(i, int) else pl.multiple_of(i * tq, tq), tq)

        def start():
            qt_scr[...] = q_ref[rows, :].astype(F32).T.astype(BF16)
            m_scr[...] = jnp.full(m_scr.shape, NEG, F32)
            acc_scr[...] = jnp.zeros(acc_scr.shape, F32)

        def scores(j, buf, d):
            start_k = pl.multiple_of(j * tk, tk)
            c0 = 0 if d is None else d * tk
            st = _dot(k_ref[pl.ds(start_k, tk), :], qt_scr[:, c0:])
            if d is not None:
                tri = jnp.where(key_l <= qry_l, st[:, :tk], NEG)
                st = tri if st.shape[1] == tk else jnp.concatenate([tri, st[:, tk:]], axis=1)
            s_scr[buf, :, c0:] = st
            cmax_scr[buf, :, c0:] = jnp.max(st, axis=0, keepdims=True)

        def accumulate(j, buf, d):
            cols = slice(0 if d is None else d * tk, None)
            m_old = m_scr[:, cols]
            m_new = jnp.maximum(m_old, cmax_scr[buf, :, cols])
            p = jnp.exp2(s_scr[buf, :, cols] - m_new).astype(BF16)
            acc_scr[:, cols] = acc_scr[:, cols] * jnp.exp2(m_old - m_new) + _dot(vt_ref[j], p)
            m_scr[:, cols] = m_new

        def pairs(pp, offsets, held0):
            held = [held0, None]
            for c, (d_a, d_b) in enumerate(offsets):
                scores(2 * (pp + c) + 1, 1, d_a)
                held[1] = d_a
                accumulate(2 * (pp + c), 0, held[0])
                scores(2 * (pp + c) + 2, 0, d_b)
                held[0] = d_b
                accumulate(2 * (pp + c) + 1, 1, held[1])
            return held[0]

        def plain_loop(trips):
            def body(t, carry):
                pairs(8 * t, plain * 4, None)
                return carry
            doubles = lax.shift_right_logical(trips, 1)
            lax.fori_loop(0, doubles, body, 0)

            @pl.when((trips & 1) == 1)
            def _():
                pairs(8 * doubles, plain * 2, None)

        def finish(held0):
            scores(n_tiles - 1, 1, per_q - 1)
            accumulate(n_tiles - 2, 0, held0)
            accumulate(n_tiles - 1, 1, per_q - 1)
            acc = acc_scr[...]
            out_t = acc[0:MLA_V, :] / acc[MLA_V:MLA_V + 1, :]
            o_ref[rows, :] = jnp.concatenate([out_t, jnp.zeros_like(out_t)], axis=0).T.astype(BF16)

        return start, scores, pairs, plain_loop, finish

    def fill(i):
        start, scores, _, _, _ = query_tile(i)
        start()
        scores(0, 0, None)

    def second_of_pair(g):
        _, _, pairs, loop, finish = query_tile(2 * g + 1)
        loop(g)
        finish(pairs(4 * g, tail, None))

    start, scores, pairs, _, finish = query_tile(0)
    start()
    scores(0, 0, 0)
    finish(pairs(0, [(1, 2)], 0))
    fill(1)
    second_of_pair(0)
    fill(min(2, n_q - 1))

    def pair_body(g, carry):
        _, _, pairs, loop, finish = query_tile(2 * g)
        loop(g - 1)
        finish(pairs(4 * (g - 1), plain + tail, None))
        fill(2 * g + 1)
        second_of_pair(g)
        fill(jnp.minimum(2 * g + 2, n_q - 1))
        return carry

    lax.fori_loop(1, n_q // 2, pair_body, 0)


def _flash(q, k, vt, b, s):
    tq, tk = FLASH_TILE, FLASH_KEY_TILE
    assert s % (2 * tq) == 0 and tq % (2 * tk) == 0
    rows = pl.BlockSpec((None, s, LANES), lambda bb, hh: (hh, bb, 0))
    return pl.pallas_call(
        _flash_kernel,
        grid=(b, MLA_HEADS),
        in_specs=[rows, rows, pl.BlockSpec((None, s // tk, MLA_VT_ROWS, tk), lambda bb, hh: (hh, bb, 0, 0))],
        out_specs=rows,
        out_shape=jax.ShapeDtypeStruct((MLA_HEADS, b * s, LANES), BF16),
        scratch_shapes=[pltpu.VMEM((1, tq), F32), pltpu.VMEM((MLA_VT_ROWS, tq), F32),
                        pltpu.VMEM((2, tk, tq), F32), pltpu.VMEM((2, 1, tq), F32), pltpu.VMEM((LANES, tq), BF16)],
        compiler_params=_params("parallel", "parallel"),
        name="mla_flash",
    )(q, k, vt)


def _log_sigmoid(z):
    return jnp.minimum(z, 0.0) - jnp.log1p(jnp.exp(-jnp.abs(z)))


def _mlstm_kernel(qkc_ref, qkp_ref, vt_ref, o_ref, g_ref, cw_ref, cb_ref, bi_ref, bf_ref,
                  y_ref, c_scr, m_scr):
    L = ML_CHUNK
    nb = qkc_ref.shape[0]
    c = pl.program_id(0)

    @pl.when(c == 0)
    def _():
        c_scr[...] = jnp.zeros(c_scr.shape, F32)
        m_scr[...] = jnp.full(m_scr.shape, NEG, F32)

    src = lax.broadcasted_iota(jnp.int32, (L, L), 0)
    tgt = lax.broadcasted_iota(jnp.int32, (L, L), 1)
    causal = src <= tgt
    pick = (lax.broadcasted_iota(jnp.int32, (L, ML_HEADS * L), 0)
            == lax.broadcasted_iota(jnp.int32, (L, ML_HEADS * L), 1) // L).astype(BF16)

    def chunk(sub):
        rows_t = slice(sub * L, (sub + 1) * L)
        streams = [(b, hh) for b in range(nb) for hh in range(ML_HEADS)]
        qk, b_all, li_all, x_col = [], [], [], []
        for b in range(nb):
            cur = qkc_ref[b, rows_t, :]
            prev = jnp.where(c > 0, qkp_ref[b], 0.0) if sub == 0 else qkc_ref[b, (sub - 1) * L:sub * L, :]
            row_w = lax.broadcasted_iota(jnp.int32, cur.shape, 0)
            conv = cb_ref[...] + cw_ref[ML_CONV - 1:ML_CONV, :] * cur
            for sh in range(1, ML_CONV):
                shifted = jnp.where(row_w >= sh, pltpu.roll(cur, sh, 0), pltpu.roll(prev, sh, 0))
                conv = conv + cw_ref[ML_CONV - 1 - sh:ML_CONV - sh, :] * shifted
            qk.append(conv * jax.nn.sigmoid(conv))
            gates = g_ref[b, :, rows_t]
            li = gates[0:ML_HEADS, :] + bi_ref[...]
            lf = _log_sigmoid(gates[ML_HEADS:, :] + bf_ref[...])
            lane_g = lax.broadcasted_iota(jnp.int32, lf.shape, 1)
            step = 1
            while step < L:
                lf = lf + jnp.where(lane_g >= step, pltpu.roll(lf, step, 1), 0.0)
                step *= 2
            li_all.append(li)
            b_all.append(lf)
            x = jnp.concatenate([li - lf, jnp.zeros((L - ML_HEADS, L), F32)], axis=0).T
            spread = jnp.zeros((L, ML_HEADS * L), F32)
            for _ in range(3):
                part = x.astype(BF16)
                spread = spread + _dot(part, pick)
                x = x - part.astype(F32)
            x_col.append(spread)

        st = []
        for n, (b, hh) in enumerate(streams):
            b_row = b_all[b][hh:hh + 1, :]
            li_row = li_all[b][hh:hh + 1, :]
            m_prev = m_scr[n:n + 1, 0:1]
            d_t = jnp.where(causal, x_col[b][:, hh * L:(hh + 1) * L] + b_row, NEG)
            inter = b_row + m_prev
            m_t = jnp.maximum(inter, jnp.max(d_t, axis=0, keepdims=True))
            b_last = b_row[:, L - 1:L]
            g = b_last - b_row + li_row
            m_new = jnp.maximum(b_last + m_prev, jnp.max(g, axis=1, keepdims=True))
            kf = qk[b][:, (ML_HEADS + hh) * ML_DK:(ML_HEADS + hh + 1) * ML_DK] * ML_DK ** -0.5
            vt = vt_ref[b, hh * ML_VT_ROWS:(hh + 1) * ML_VT_ROWS, rows_t]
            st.append(dict(
                qh=qk[b][:, hh * ML_DK:(hh + 1) * ML_DK].astype(BF16),
                kh=kf.astype(BF16), vt=vt,
                vw=(vt.astype(F32) * jnp.exp(g - m_new)).astype(BF16),
                c_t=c_scr[n],
                w_intra=jnp.exp(d_t - m_t), w_inter=jnp.exp(inter - m_t), floor=jnp.exp(-m_t),
                decay=jnp.exp(b_last + m_prev - m_new), m_new=m_new))

        for s_ in st:
            s_["sraw"] = _dot_nt(s_["kh"], s_["qh"])
            s_["cross"] = _dot_nt(s_["c_t"].astype(BF16), s_["qh"])
        for s_ in st:
            s_["s"] = s_["sraw"] * s_["w_intra"]
        for s_ in st:
            s_["intra"] = _dot(s_["vt"], s_["s"].astype(BF16))
            s_["kv"] = _dot(s_["vw"], s_["kh"])
        h_t = []
        for n, s_ in enumerate(st):
            num = s_["intra"][:ML_DV, :] + s_["w_inter"] * s_["cross"][:ML_DV, :]
            nq = jnp.sum(s_["s"], axis=0, keepdims=True) + s_["w_inter"] * s_["cross"][ML_DV:ML_DV + 1, :]
            h_t.append(num / jnp.maximum(jnp.abs(nq), s_["floor"]))
            c_scr[n] = s_["decay"] * s_["c_t"] + s_["kv"]
            m_scr[n:n + 1, :] = jnp.broadcast_to(s_["m_new"], (1, LANES))
        for b in range(nb):
            hout = jnp.concatenate(h_t[b * ML_HEADS:(b + 1) * ML_HEADS], axis=0).T
            y_ref[b, rows_t, :] = (jax.nn.sigmoid(o_ref[b, rows_t, :]) * hout).astype(BF16)

    for sub in range(qkc_ref.shape[1] // L):
        chunk(sub)


def _mlstm(qk_pre, vt, o_pre, gates_t, conv_w, conv_b, b_i, b_f):
    batch, seq, wqk = qk_pre.shape
    L = ML_CHUNK
    step = ML_CHUNKS_PER_STEP * L
    full = lambda a: pl.BlockSpec(a.shape, lambda c: (0,) * a.ndim)
    chunk = lambda w_: pl.BlockSpec((batch, step, w_), lambda c: (0, c, 0))
    lanes = lambda r: pl.BlockSpec((batch, r, step), lambda c: (0, 0, c))
    return pl.pallas_call(
        _mlstm_kernel,
        grid=(seq // step,),
        in_specs=[chunk(wqk),
                  pl.BlockSpec((batch, L, wqk), lambda c: (0, jnp.maximum(ML_CHUNKS_PER_STEP * c - 1, 0), 0)),
                  lanes(ML_HEADS * ML_VT_ROWS), chunk(ML_W), lanes(2 * ML_HEADS),
                  full(conv_w), full(conv_b), full(b_i), full(b_f)],
        out_specs=chunk(ML_W),
        out_shape=jax.ShapeDtypeStruct((batch, seq, ML_W), BF16),
        scratch_shapes=[pltpu.VMEM((batch * ML_HEADS, ML_VT_ROWS, ML_DK), F32),
                        pltpu.VMEM((batch * ML_HEADS, LANES), F32)],
        compiler_params=_params("arbitrary"),
        name="mlstm",
    )(qk_pre, qk_pre, vt, o_pre, gates_t, conv_w, conv_b, b_i, b_f)


def _swa_kernel(q_ref, k_ref, v_ref, y_ref, kk_scr, vv_scr, o_scr, l_scr):
    blk, tile = SW_BLOCK, SWA_TILE
    n_slab = SW_W // LANES
    n = pl.program_id(1)

    @pl.when(n == 0)
    def _():
        kk_scr[:, 0:tile, :] = jnp.zeros((n_slab, tile, LANES), F32)
        vv_scr[:, 0:tile, :] = jnp.zeros((n_slab, tile, LANES), F32)

    @pl.when(n > 0)
    def _():
        kk_scr[:, 0:tile, :] = kk_scr[:, tile:2 * tile, :]
        vv_scr[:, 0:tile, :] = vv_scr[:, tile:2 * tile, :]

    kk_scr[:, tile:2 * tile, :] = k_ref[...]
    vv_scr[:, tile:2 * tile, :] = v_ref[...]

    key = lax.broadcasted_iota(jnp.int32, (2 * blk, blk), 0)
    qry = lax.broadcasted_iota(jnp.int32, (2 * blk, blk), 1)
    dim =lax.broadcasted_iota(jnp.int32, (LANES, blk), 0)
    head_rows = [dim < SW_HD, dim >= SW_HD]
    ones_rows = jnp.ones((16, 2 * blk), BF16)

    for g, dil in enumerate(SW_DILATIONS):
        per_res = tile // (blk * dil)
        shift = per_res.bit_length() - 1

        def block_pair(it, carry, g=g, dil=dil, per_res=per_res, shift=shift):
            def rows(start):
                return pl.ds(start, blk, stride=dil) if dil > 1 else pl.ds(pl.multiple_of(start, blk), blk)

            work = []
            for u in range(SWA_UNROLL):
                bi = SWA_UNROLL * it + u
                r = lax.shift_right_logical(bi, shift)
                j = bi & (per_res - 1)
                q0 = r + j * (blk * dil)
                lowest = jnp.where((n == 0) & (j == 0), blk, 0)
                ok = (key >= jnp.maximum(qry, lowest)) & (key <= qry + blk)
                for sl in range(n_slab):
                    prev, cur = rows(tile + q0 - blk * dil), rows(tile + q0)
                    kb = jnp.concatenate([kk_scr[sl, prev, :], kk_scr[sl, cur, :]], axis=0)
                    vb = jnp.concatenate([vv_scr[sl, prev, :], vv_scr[sl, cur, :]], axis=0)
                    work.append(dict(q0=q0, sl=sl, ok=ok, qt=q_ref[sl, rows(q0), :].T,
                                     kb=kb.astype(BF16), vt=vb.T.astype(BF16)))
            for w in work:
                w["s"] = [_dot(w["kb"], jnp.where(head_rows[hh], w["qt"], 0.0).astype(BF16)) for hh in range(2)]
            for w in work:
                w["e"], w["cmax"] = [], []
                for hh in range(2):
                    s = jnp.where(w["ok"], w["s"][hh], NEG)
                    cmax = jnp.max(s, axis=0, keepdims=True)
                    w["e"].append(jnp.exp2(s - cmax).astype(BF16))
                    w["cmax"].append(cmax)
            for w in work:
                ot, lt = [], []
                for hh in range(2):
                    vt_ext = jnp.concatenate([w["vt"][hh * SW_HD:(hh + 1) * SW_HD, :], ones_rows], axis=0)
                    prod = _dot(vt_ext, w["e"][hh])
                    den = prod[SW_HD:SW_HD + 1, :]
                    ot.append(prod[:SW_HD, :] / den)
                    lt.append(jnp.broadcast_to(w["cmax"][hh] + jnp.log2(den), (SW_HD, blk)))
                o_scr[g, w["sl"], rows(w["q0"]), :] = jnp.concatenate(ot, axis=0).T
                l_scr[g, w["sl"], rows(w["q0"]), :] = jnp.concatenate(lt, axis=0).T
            return carry

        lax.fori_loop(0, tile // blk // SWA_UNROLL, block_pair, 0)

    for sl in range(n_slab):
        ls = [l_scr[g, sl] for g in range(len(SW_DILATIONS))]
        mx = functools.reduce(jnp.maximum, ls)
        ws = [jnp.exp2(l - mx) for l in ls]
        num = functools.reduce(lambda a, b: a + b, [w * o_scr[g, sl] for g, w in enumerate(ws)])
        y_ref[:, sl * LANES:(sl + 1) * LANES] = (num / functools.reduce(lambda a, b: a + b, ws)).astype(BF16)


def _swa(q, k, v, batch, seq):
    tile = SWA_TILE
    n_slab = SW_W // LANES
    per_batch = seq // tile
    spec = pl.BlockSpec((n_slab, tile, LANES), lambda bb, n: (0, bb * per_batch + n, 0))
    n_br = len(SW_DILATIONS)
    return pl.pallas_call(
        _swa_kernel,
        grid=(batch, per_batch),
        in_specs=[spec, spec, spec],
        out_specs=pl.BlockSpec((tile, SW_W), lambda bb, n: (bb * per_batch + n, 0)),
        out_shape=jax.ShapeDtypeStruct((batch * seq, SW_W), BF16),
        scratch_shapes=[pltpu.VMEM((n_slab, 2 * tile, LANES), F32), pltpu.VMEM((n_slab, 2 * tile, LANES), F32),
                        pltpu.VMEM((n_br, n_slab, tile, LANES), F32), pltpu.VMEM((n_br, n_slab, tile, LANES), F32)],
        compiler_params=_params("parallel", "arbitrary"),
        name="swa",
    )(q, k, v)


def _memkv_kernel(mem_ref, g_ref, w_ref, kv_ref):
    kv_ref[...] = _dot(_rms(mem_ref[...], g_ref[...]).astype(BF16), w_ref[...]).astype(BF16)


def _memkv(mem, g, w):
    n = mem.shape[0]
    depth = w.shape[0]
    return pl.pallas_call(
        _memkv_kernel,
        grid=(depth,),
        in_specs=[pl.BlockSpec(mem.shape, lambda l: (0, 0)),
                  pl.BlockSpec((None, 1, D_MODEL), lambda l: (l, 0, 0)),
                  pl.BlockSpec((None, D_MODEL, 2 * D_MODEL), lambda l: (l, 0, 0))],
        out_specs=pl.BlockSpec((None, n, 2 * D_MODEL), lambda l: (l, 0, 0)),
        out_shape=jax.ShapeDtypeStruct((depth, n, 2 * D_MODEL), BF16),
        compiler_params=_params("parallel"),
        name="memkv",
    )(mem, g, w)


def _mixout_xattn_kernel(x_ref, ya_ref, yb_ref, yc_ref, wa_ref, wb_ref, wc_ref, gmix_ref,
                         gpre_ref, wq_ref, kv_ref, wo_ref, gpost_ref, out_ref, o_scr):
    ya = jnp.concatenate([ya_ref[hh] for hh in range(MLA_HEADS)], axis=1)
    y = _dot(ya, wa_ref[...]) + _dot(yb_ref[...], wb_ref[...]) + _dot(yc_ref[...], wc_ref[...])
    x = x_ref[...] + _rms(y, gmix_ref[...])
    h = _rms(x, gpre_ref[...]).astype(BF16)
    q = (_dot(h, wq_ref[...]) * X_HD ** -0.5).astype(BF16)
    for hh in range(X_HEADS):
        hs = slice(hh * X_HD, (hh + 1) * X_HD)
        s = _dot_nt(q[:, hs], kv_ref[:, hs])
        e = jnp.exp(s - jnp.max(s, axis=1, keepdims=True)).astype(BF16)
        den = _dot(e, jnp.ones((N_MEM, LANES), BF16))[:, 0:1]
        o_scr[:, hs] = (_dot(e, kv_ref[:, D_MODEL + hh * X_HD:D_MODEL + (hh + 1) * X_HD]) / den).astype(BF16)
    out_ref[...] = x + _rms(_dot(o_scr[...], wo_ref[...]), gpost_ref[...])


def _mixout_xattn(x, ya, yb, yc, wa, wb, wc, gmix, gpre, wq, kv, wo, gpost, seq):
    t_all = x.shape[0]
    tm = WIDE_TILE
    per_batch = seq // tm
    tok = lambda w_: pl.BlockSpec((tm, w_), lambda i: (i, 0))
    full = lambda a: pl.BlockSpec(a.shape, lambda i: (0,) * a.ndim)
    return pl.pallas_call(
        _mixout_xattn_kernel,
        grid=(t_all // tm,),
        in_specs=[tok(D_MODEL), pl.BlockSpec((MLA_HEADS, tm, LANES), lambda i: (0, i, 0)), tok(ML_W), tok(SW_W),
                  full(wa), full(wb), full(wc), full(gmix), full(gpre), full(wq),
                  pl.BlockSpec((N_MEM, 2 * D_MODEL), lambda i: (i // per_batch, 0)), full(wo), full(gpost)],
        out_specs=tok(D_MODEL),
        out_shape=jax.ShapeDtypeStruct((t_all, D_MODEL), F32),
        scratch_shapes=[pltpu.VMEM((tm, D_MODEL), BF16)],
        compiler_params=_params("parallel"),
        name="mixout_xattn",
    )(x, ya, yb, yc, wa, wb, wc, gmix, gpre, wq, kv, wo, gpost)


def _ffn_kernel(x_ref, gpre_ref, wgu_ref, wd_ref, gpost_ref, out_ref):
    x = x_ref[...]
    h = _rms(x, gpre_ref[...]).astype(BF16)
    y = None
    for lo, hi in zip(FFN_SPLITS[:-1], FFN_SPLITS[1:]):
        gate = _dot(h, wgu_ref[:, lo:hi])
        up = _dot(h, wgu_ref[:, D_FF + lo:D_FF + hi])
        part = _dot((gate * jax.nn.sigmoid(gate) * up).astype(BF16), wd_ref[lo:hi, :])
        y = part if y is None else y + part
    out_ref[...] = x + _rms(y, gpost_ref[...])


def _ffn(x, gpre, w_gate_up, w_down, gpost):
    t_all = x.shape[0]
    tm = TOK_TILE
    tok = pl.BlockSpec((tm, D_MODEL), lambda i: (i, 0))
    full = lambda a: pl.BlockSpec(a.shape, lambda i: (0,) * a.ndim)
    return pl.pallas_call(
        _ffn_kernel,
        grid=(t_all // tm,),
        in_specs=[tok, full(gpre), full(w_gate_up), full(w_down), full(gpost)],
        out_specs=tok,
        out_shape=jax.ShapeDtypeStruct((t_all, D_MODEL), F32),
        compiler_params=_params("parallel"),
        name="ffn",
    )(x, gpre, w_gate_up, w_down, gpost)


def _rope_tables(seq):
    pos = jnp.arange(seq, dtype=F32)[:, None]

    def cos_sin(d_rot):
        inv = ROPE_THETA ** (-jnp.arange(0, d_rot, 2, dtype=F32) / d_rot)
        ang = pos * inv[None, :]
        return jnp.cos(ang), jnp.sin(ang)

    c, s = cos_sin(MLA_ROPE)
    ones = jnp.ones((seq, MLA_NOPE), F32)
    tail = jnp.ones((seq, LANES - MLA_NOPE - MLA_ROPE), F32)
    cm = jnp.concatenate([ones, c, c, tail], axis=1)
    sm = jnp.concatenate([0 * ones, -s, s, 0 * tail], axis=1)
    c, s = cos_sin(SW_ROT)
    rest = jnp.ones((seq, SW_HD - SW_ROT), F32)
    cs = jnp.tile(jnp.concatenate([c, c, rest], axis=1), (1, LANES // SW_HD))
    ss = jnp.tile(jnp.concatenate([-s, s, 0 * rest], axis=1), (1, LANES // SW_HD))
    return cm, sm, cs, ss


def _prep_layer(w_in, w_uq, w_ukv, w_out):
    o = IN_OFFS
    col = lambda i: w_in[:, o[i]:o[i + 1]]
    z = lambda n: jnp.zeros((D_MODEL, n), F32)
    w_p = jnp.concatenate([
        col(0), col(1),
        z(MLA_NOPE), col(2), z(LANES - MLA_NOPE - MLA_ROPE),
        col(3), col(4), col(6), col(9), col(10), col(11)], axis=1)
    assert w_p.shape[1] == P_TOTAL
    mlv = jnp.pad(col(5).reshape(D_MODEL, ML_HEADS, ML_DV), ((0, 0), (0, 0), (0, ML_VT_ROWS - ML_DV)))
    w_gates_t = jnp.concatenate([mlv.reshape(D_MODEL, ML_HEADS * ML_VT_ROWS), col(7), col(8),
                                 z(ML_T_ROWS - ML_HEADS * ML_VT_ROWS - 2 * ML_HEADS)], axis=1).T
    uq = w_uq.reshape(MLA_Q_LORA, MLA_HEADS, MLA_NOPE + MLA_ROPE)
    uq = jnp.pad(uq, ((0, 0), (0, 0), (0, LANES - MLA_NOPE - MLA_ROPE))).reshape(MLA_Q_LORA, MLA_PAD_W)
    ukv = w_ukv.reshape(MLA_KV_LORA, MLA_HEADS, MLA_NOPE + MLA_V)
    pad_half = lambda a: jnp.pad(a, ((0, 0), (0, 0), (0, LANES - a.shape[2]))).reshape(MLA_KV_LORA, MLA_PAD_W)
    wk = pad_half(ukv[:, :, :MLA_NOPE])
    wv = jnp.pad(ukv[:, :, MLA_NOPE:], ((0, 0), (0, 0), (0, MLA_VT_ROWS - MLA_V)))
    wv = wv.reshape(MLA_KV_LORA, MLA_HEADS * MLA_VT_ROWS).T
    wa = w_out[:MLA_W].reshape(MLA_HEADS, MLA_V, D_MODEL)
    wa = jnp.pad(wa, ((0, 0), (0, LANES - MLA_V), (0, 0))).reshape(MLA_PAD_W, D_MODEL)
    wb, wc = w_out[MLA_W:MLA_W + ML_W], w_out[MLA_W + ML_W:]
    bf = lambda a: a.astype(BF16)
    return bf(w_p), bf(w_gates_t), bf(uq), bf(wk), bf(wv), bf(wa), bf(wb), bf(wc)


def kernel(x, mem, w_in, mla_q_norm, mla_w_uq, mla_kv_norm, mla_w_ukv, ml_conv_w, ml_conv_b, ml_b_i, ml_b_f,
           w_out, x_w_q, x_w_kv, x_w_o, w_gate_up, w_down, norm_mix_pre, norm_mix_post, norm_mem,
           norm_x_pre, norm_x_post, norm_ffn_pre, norm_ffn_post):
    batch, seq, _ = x.shape
    depth = w_in.shape[0]
    t_all = batch * seq
    assert seq % SWA_TILE == 0 and seq % TOK_TILE == 0 and seq % WIDE_TILE == 0
    assert WIDE_TILE % FLASH_KEY_TILE == 0
    assert SWA_TILE == SW_DILATIONS[-1] * SW_BLOCK
    cm, sm, cs, ss = _rope_tables(seq)
    row = lambda a: a.reshape(1, -1)
    colv = lambda a: a.reshape(-1, 1)
    xt = x.reshape(t_all, D_MODEL)
    mem2 = mem.reshape(batch * N_MEM, D_MODEL)
    w_p_all, w_gt_all, uq_all, wk_all, wv_all, wa_all, wb_all, wc_all = jax.vmap(_prep_layer)(
        w_in, mla_w_uq, mla_w_ukv, w_out)
    x_w_q, x_w_o, w_gate_up, w_down = (a.astype(BF16) for a in (x_w_q, x_w_o, w_gate_up, w_down))
    kv_all = _memkv(mem2, norm_mem.reshape(depth, 1, D_MODEL), x_w_kv.astype(BF16))
    for l in range(depth):
        w_p, w_gt, uq, wk, wv, wa, wb, wc = (a[l] for a in (w_p_all, w_gt_all, uq_all, wk_all, wv_all,
                                                            wa_all, wb_all, wc_all))
        q, k, v, mqk, mv, mo, gates_t, sq, sk, sv = _inproj(
            xt, row(norm_mix_pre[l]), w_p, w_gt, row(mla_q_norm[l]), uq, row(mla_kv_norm[l]), wk, wv,
            cm, sm, cs, ss, seq)
        b3 = lambda a: a.reshape(batch, seq, a.shape[-1])
        ya = _flash(q, k, v, batch, seq)
        yb = _mlstm(b3(mqk), mv, b3(mo), gates_t, ml_conv_w[l], row(ml_conv_b[l]), colv(ml_b_i[l]),
                    colv(ml_b_f[l])).reshape(t_all, ML_W)
        yc = _swa(sq, sk, sv, batch, seq)
        xt = _mixout_xattn(xt, ya, yb, yc, wa, wb, wc, row(norm_mix_post[l]),
                           row(norm_x_pre[l]), x_w_q[l], kv_all[l], x_w_o[l], row(norm_x_post[l]), seq)
        xt = _ffn(xt, row(norm_ffn_pre[l]), w_gate_up[l], w_down[l], row(norm_ffn_post[l]))
    return xt.reshape(batch, seq, D_MODEL)
```
